```python
import math
import jax, jax.numpy as jnp
from jax import lax
import numpy as np

D_MODEL = 1024
BATCH = 4
SEQ = 4096
DEPTH = 2

D_MIX = D_MODEL
N_MIXERS = 4
D_GROUP = D_MIX // N_MIXERS
HEADS_PER_MIXER = 4
HEAD_DIM = D_GROUP // HEADS_PER_MIXER
CHUNK = 128
CONF_KERNEL = 31
POOL_WINDOWS = (2, 4, 8, 16)
SHORT_KERNEL = 3
N_EXPERTS = 16
N_EXPERT_GROUPS = 4
EXPERTS_PER_GROUP = N_EXPERTS // N_EXPERT_GROUPS
TOP_K = 2
D_EXPERT = 512
D_IN_GMLP = 2 * D_GROUP
D_IN_CONF = 2 * D_GROUP
D_IN_POOL = D_GROUP
D_IN_SCONV = 3 * D_GROUP
D_IN = D_IN_GMLP + D_IN_CONF + D_IN_POOL + D_IN_SCONV
ALPHA = (2 * DEPTH) ** 0.25
BETA = (8 * DEPTH) ** -0.25
LN_EPS = 1e-5
RMS_EPS = 1e-6

kernel_name = "hybrid_gmlp_conformer_pool_shortconv_sharedrouter_moe"


def layer_norm(x, g, b):
    xf = x.astype(jnp.float32)
    mu = jnp.mean(xf, -1, keepdims=True)
    var = jnp.mean(jnp.square(xf - mu), -1, keepdims=True)
    y = (xf - mu) * lax.rsqrt(var + LN_EPS)
    return (y * g.astype(jnp.float32) + b.astype(jnp.float32)).astype(x.dtype)


def group_rms_norm(y, g):
    B_, S_, _ = y.shape
    yf = y.astype(jnp.float32).reshape(B_, S_, N_MIXERS, D_GROUP)
    yf = yf * lax.rsqrt(jnp.mean(jnp.square(yf), -1, keepdims=True) + RMS_EPS)
    return (yf.reshape(B_, S_, D_MIX) * g.astype(jnp.float32)).astype(y.dtype)


def causal_depthwise_conv(x, w):
    K, C = w.shape
    return lax.conv_general_dilated(
        x, w[:, None, :].astype(x.dtype), window_strides=(1,),
        padding=[(K - 1, 0)], dimension_numbers=('NWC', 'WIO', 'NWC'),
        feature_group_count=C)


def spatial_gating_mixer(z, ln_g, ln_b, w_s, b_s):
    u, v = jnp.split(z, 2, axis=-1)
    v = layer_norm(v, ln_g, ln_b)
    B_, S_, _ = v.shape
    v = v.reshape(B_, S_ // CHUNK, CHUNK, HEADS_PER_MIXER, HEAD_DIM)
    mask = jnp.tril(jnp.ones((CHUNK, CHUNK), dtype=bool))
    w = jnp.where(mask, w_s, 0).astype(v.dtype)
    s = jnp.einsum('hts,bnshd->bnthd', w, v) + b_s.T[:, :, None].astype(v.dtype)
    return u * s.reshape(B_, S_, D_GROUP)


def conformer_conv_mixer(z, dw_w, dw_b, ln_g, ln_b, pw):
    a, g = jnp.split(z, 2, axis=-1)
    h = a * jax.nn.sigmoid(g)
    h = causal_depthwise_conv(h, dw_w) + dw_b.astype(h.dtype)
    h = jax.nn.silu(layer_norm(h, ln_g, ln_b))
    return h @ pw


def multiscale_pool_mixer(z, w_grp, scale):
    B_, S_, _ = z.shape
    G = len(POOL_WINDOWS)
    zg = z.astype(jnp.float32).reshape(B_, S_, G, HEAD_DIM)
    csum = jnp.pad(jnp.cumsum(zg, axis=1), ((0, 0), (1, 0), (0, 0), (0, 0)))
    windows = jnp.array(POOL_WINDOWS, dtype=jnp.int32)
    t = jnp.arange(S_, dtype=jnp.int32)[:, None]
    lo_idx = jnp.maximum(t + 1 - windows, 0)
    window_sum = csum[:, 1:] - csum[:, lo_idx, jnp.arange(G)]
    count = jnp.minimum(t + 1, windows).astype(jnp.float32)
    pooled = (window_sum / count[None, :, :, None] - zg).astype(z.dtype)
    y = jnp.einsum('bsgd,gde->bsge', pooled, w_grp).reshape(B_, S_, D_GROUP)
    return y * scale


def short_gated_conv_mixer(z, conv_w):
    b_gate, c_gate, h = jnp.split(z, 3, axis=-1)
    return b_gate * causal_depthwise_conv(c_gate * h, conv_w)


def hybrid_mixer(x, w_in, gm_ln_g, gm_ln_b, gm_w_s, gm_b_s, cf_dw_w, cf_dw_b,
                 cf_ln_g, cf_ln_b, cf_pw, pool_w, pool_scale, sc_w, mix_norm_g, w_o):
    z = x @ w_in
    o1 = D_IN_GMLP
    o2 = o1 + D_IN_CONF
    o3 = o2 + D_IN_POOL
    z_a, z_b, z_c, z_d = z[..., :o1], z[..., o1:o2], z[..., o2:o3], z[..., o3:]
    y = jnp.concatenate([
        spatial_gating_mixer(z_a, gm_ln_g, gm_ln_b, gm_w_s, gm_b_s),
        conformer_conv_mixer(z_b, cf_dw_w, cf_dw_b, cf_ln_g, cf_ln_b, cf_pw),
        multiscale_pool_mixer(z_c, pool_w, pool_scale),
        short_gated_conv_mixer(z_d, sc_w),
    ], axis=-1)
    return group_rms_norm(y, mix_norm_g) @ w_o


def grouped_moe(x, router_w, router_b, w_gate, w_up, w_down):
    B_, S_, D = x.shape
    xt = x.reshape(-1, D)
    logits = (xt @ router_w).astype(jnp.float32)
    sel = logits + router_b.astype(jnp.float32)
    grp = sel.reshape(-1, N_EXPERT_GROUPS, EXPERTS_PER_GROUP)
    grp_score = lax.top_k(grp, TOP_K)[0].sum(-1)
    best = jnp.argmax(grp_score, axis=-1)
    in_group = (jnp.arange(N_EXPERTS) // EXPERTS_PER_GROUP)[None, :] == best[:, None]
    masked = jnp.where(in_group, sel, -jnp.inf)
    _, idx = lax.top_k(masked, TOP_K)
    gates = jax.nn.softmax(jnp.take_along_axis(logits, idx, axis=-1), axis=-1)
    dense_gates = (jax.nn.one_hot(idx, N_EXPERTS, dtype=jnp.float32) * gates[..., None]).sum(1)
    y = jnp.zeros(xt.shape, jnp.float32)
    for e in range(N_EXPERTS):
        h = jax.nn.silu(xt @ w_gate[e]) * (xt @ w_up[e])
        y = y + dense_gates[:, e:e + 1] * (h @ w_down[e]).astype(jnp.float32)
    return y.astype(x.dtype).reshape(B_, S_, D)


def setup_inputs(seed: int = 0) -> dict:
    key = jax.random.key(seed)
    ks = jax.random.split(key, 24)
    L = DEPTH

    def nrm(k, shape, scale):
        return jax.random.normal(k, shape, jnp.float32) * scale

    return {
        "x": nrm(ks[0], (BATCH, SEQ, D_MODEL), 1.0),
        "w_in": nrm(ks[1], (L, D_MODEL, D_IN), D_MODEL ** -0.5),
        "gm_ln_g": 1.0 + nrm(ks[2], (L, D_GROUP), 0.02),
        "gm_ln_b": nrm(ks[3], (L, D_GROUP), 0.02),
        "gm_w_s": nrm(ks[4], (L, HEADS_PER_MIXER, CHUNK, CHUNK), CHUNK ** -0.5),
        "gm_b_s": 1.0 + nrm(ks[5], (L, HEADS_PER_MIXER, CHUNK), 0.02),
        "cf_dw_w": nrm(ks[6], (L, CONF_KERNEL, D_GROUP), CONF_KERNEL ** -0.5),
        "cf_dw_b": nrm(ks[7], (L, D_GROUP), 0.02),
        "cf_ln_g": 1.0 + nrm(ks[8], (L, D_GROUP), 0.02),
        "cf_ln_b": nrm(ks[9], (L, D_GROUP), 0.02),
        "cf_pw": nrm(ks[10], (L, D_GROUP, D_GROUP), D_GROUP ** -0.5),
        "pool_w": nrm(ks[11], (L, len(POOL_WINDOWS), HEAD_DIM, HEAD_DIM), HEAD_DIM ** -0.5),
        "pool_scale": 1.0 + nrm(ks[12], (L, D_GROUP), 0.02),
        "sc_w": nrm(ks[13], (L, SHORT_KERNEL, D_GROUP), SHORT_KERNEL ** -0.5),
        "mix_norm_g": 1.0 + nrm(ks[14], (L, D_MIX), 0.02),
        "w_o": nrm(ks[15], (L, D_MIX, D_MODEL), BETA * D_MIX ** -0.5),
        "ln1_g": 1.0 + nrm(ks[16], (L, D_MODEL), 0.02),
        "ln1_b": nrm(ks[17], (L, D_MODEL), 0.02),
        "router_w": nrm(ks[18], (D_MODEL, N_EXPERTS), D_MODEL ** -0.5),
        "router_b": nrm(ks[19], (N_EXPERTS,), 0.01),
        "exp_w_gate": nrm(ks[20], (L, N_EXPERTS, D_MODEL, D_EXPERT), D_MODEL ** -0.5),
        "exp_w_up": nrm(ks[21], (L, N_EXPERTS, D_MODEL, D_EXPERT), D_MODEL ** -0.5),
        "exp_w_down": nrm(ks[22], (L, N_EXPERTS, D_EXPERT, D_MODEL), BETA * D_EXPERT ** -0.5),
        "ln2_g": 1.0 + nrm(ks[23], (L, D_MODEL), 0.02),
        "ln2_b": nrm(jax.random.fold_in(ks[23], 1), (L, D_MODEL), 0.02),
    }


def reference(x, w_in, gm_ln_g, gm_ln_b, gm_w_s, gm_b_s, cf_dw_w, cf_dw_b, cf_ln_g,
              cf_ln_b, cf_pw, pool_w, pool_scale, sc_w, mix_norm_g, w_o, ln1_g, ln1_b,
              router_w, router_b, exp_w_gate, exp_w_up, exp_w_down, ln2_g, ln2_b):
    for l in range(DEPTH):
        m = hybrid_mixer(x, w_in[l], gm_ln_g[l], gm_ln_b[l], gm_w_s[l], gm_b_s[l],
                         cf_dw_w[l], cf_dw_b[l], cf_ln_g[l], cf_ln_b[l], cf_pw[l],
                         pool_w[l], pool_scale[l], sc_w[l], mix_norm_g[l], w_o[l])
        x = layer_norm(ALPHA * x + m, ln1_g[l], ln1_b[l])
        f = grouped_moe(x, router_w, router_b, exp_w_gate[l], exp_w_up[l], exp_w_down[l])
        x = layer_norm(ALPHA * x + f, ln2_g[l], ln2_b[l])
    return x
```

```python
import functools

import jax
import jax.numpy as jnp
from jax import lax
from jax.experimental import pallas as pl
from jax.experimental.pallas import tpu as pltpu

D_MODEL = 1024
DEPTH = 2
D_GROUP = 256
N_HEADS = 4
HEAD_DIM = 64
CHUNK = 128
CONF_K = 31
SHORT_K = 3
N_EXPERTS = 16
N_GROUPS = 4
GROUP_SIZE = 4
D_EXPERT = 512
ALPHA = (2 * DEPTH) ** 0.25
LN_EPS = 1e-5
RMS_EPS = 1e-6

LANES = 128
SUBLANES = 8
HALO = 32
SC_HALO = SUBLANES
TS = 512
CONV_ROWS = 64
TM = 1024
VMEM_LIMIT = 56 * 1024 * 1024

BF16 = jnp.bfloat16
F32 = jnp.float32


def _dot(a, b):
    return jnp.dot(a, b, preferred_element_type=F32)


def _layer_norm(x, g, b):
    mu = jnp.mean(x, axis=-1, keepdims=True)
    xc = x - mu
    var = jnp.mean(xc * xc, axis=-1, keepdims=True)
    return xc * lax.rsqrt(var + LN_EPS) * g + b


def _rms_norm(y, g):
    ms = jnp.mean(y * y, axis=-1, keepdims=True)
    return y * lax.rsqrt(ms + RMS_EPS) * g


def _sigmoid(x):
    return 1.0 / (1.0 + jnp.exp(-x))


def _route(sel, raw):
    scores = []
    for g in range(N_GROUPS):
        v = sel[g * GROUP_SIZE:(g + 1) * GROUP_SIZE]
        best_pair = None
        for i in range(GROUP_SIZE):
            for j in range(i + 1, GROUP_SIZE):
                p = v[i] + v[j]
                best_pair = p if best_pair is None else jnp.maximum(best_pair, p)
        scores.append(best_pair)
    best = jnp.zeros_like(scores[0], dtype=jnp.int32)
    best_score = scores[0]
    for g in range(1, N_GROUPS):
        better = scores[g] > best_score
        best = jnp.where(better, g, best)
        best_score = jnp.where(better, scores[g], best_score)

    def pick(rows, j):
        out = rows[j]
        for g in range(1, N_GROUPS):
            out = jnp.where(best == g, rows[g * GROUP_SIZE + j], out)
        return out

    v = [pick(sel, j) for j in range(GROUP_SIZE)]
    l = [pick(raw, j) for j in range(GROUP_SIZE)]
    i0 = jnp.zeros_like(best)
    v0 = v[0]
    for j in range(1, GROUP_SIZE):
        better = v[j] > v0
        i0 = jnp.where(better, j, i0)
        v0 = jnp.where(better, v[j], v0)
    neg = jnp.full_like(v0, -jnp.inf)
    w = [jnp.where(i0 == j, neg, v[j]) for j in range(GROUP_SIZE)]
    i1 = jnp.zeros_like(best)
    v1 = w[0]
    for j in range(1, GROUP_SIZE):
        better = w[j] > v1
        i1 = jnp.where(better, j, i1)
        v1 = jnp.where(better, w[j], v1)
    l0 = l[0]
    l1 = l[0]
    for j in range(1, GROUP_SIZE):
        l0 = jnp.where(i0 == j, l[j], l0)
        l1 = jnp.where(i1 == j, l[j], l1)
    m = jnp.maximum(l0, l1)
    e0 = jnp.exp(l0 - m)
    e1 = jnp.exp(l1 - m)
    den = e0 + e1
    g0 = e0 / den
    g1 = e1 / den
    zero = jnp.zeros_like(g0)
    gates = []
    for e in range(N_EXPERTS):
        g, j = divmod(e, GROUP_SIZE)
        val = jnp.where(i0 == j, g0, jnp.where(i1 == j, g1, zero))
        gates.append(jnp.where(best == g, val, zero))
    return gates


def _mixer_kernel(x_ref, w_in_ref, gm_g_ref, gm_b_ref, gm_w_ref, gm_bs_ref,
                  cf_w_ref, cf_b_ref, cf_g_ref, cf_beta_ref, cf_pw_ref,
                  pool_w_ref, pool_scale_ref, sc_w_ref, mixg_ref, w_o_ref,
                  ln_g_ref, ln_b_ref, rw_hi_ref, rw_lo_ref, rb_ref,
                  x1_ref, gates_ref,
                  hbuf, pbuf, s2buf, s4buf, s8buf, cbuf, hn_ref, yn_ref):
    s_idx = pl.program_id(1)

    @pl.when(s_idx == 0)
    def _():
        hbuf[0:HALO, :] = jnp.zeros((HALO, D_GROUP), F32)
        pbuf[0:HALO, :] = jnp.zeros((HALO, D_GROUP), F32)
        cbuf[0:SC_HALO, :] = jnp.zeros((SC_HALO, D_GROUP), F32)

    x = x_ref[0]
    xb = x.astype(BF16)

    z_a = _dot(xb, w_in_ref[:, 0:2 * D_GROUP])
    u = z_a[:, 0:D_GROUP]
    v = _layer_norm(z_a[:, D_GROUP:2 * D_GROUP], gm_g_ref[...], gm_b_ref[...])
    row = lax.broadcasted_iota(jnp.int32, (N_HEADS * CHUNK, CHUNK), 0)
    col = lax.broadcasted_iota(jnp.int32, (N_HEADS * CHUNK, CHUNK), 1)
    w_tril = jnp.where(col <= (row & (CHUNK - 1)), gm_w_ref[...], 0.0).astype(BF16)
    lane = lax.broadcasted_iota(jnp.int32, (CHUNK, D_GROUP), 1)
    bs = gm_bs_ref[...]
    mixg = mixg_ref[...]
    for n in range(TS // CHUNK):
        rows = slice(n * CHUNK, (n + 1) * CHUNK)
        s_all = _dot(w_tril, v[rows, :].astype(BF16))
        s_sel = s_all[3 * CHUNK:4 * CHUNK]
        for h in (2, 1, 0):
            s_sel = jnp.where(lane < (h + 1) * HEAD_DIM, s_all[h * CHUNK:(h + 1) * CHUNK], s_sel)
        y1 = u[rows, :] * (s_sel + bs)
        yn_ref[rows, 0:D_GROUP] = _rms_norm(y1, mixg[:, 0:D_GROUP]).astype(BF16)

    z_b = _dot(xb, w_in_ref[:, 2 * D_GROUP:4 * D_GROUP])
    hbuf[HALO:HALO + TS, :] = z_b[:, 0:D_GROUP] * _sigmoid(z_b[:, D_GROUP:2 * D_GROUP])
    cf_b = cf_b_ref[...]
    cf_g = cf_g_ref[...]
    cf_beta = cf_beta_ref[...]
    for c in range(TS // CONV_ROWS):
        base = c * CONV_ROWS + HALO - (CONF_K - 1)
        acc = jnp.zeros((CONV_ROWS, D_GROUP), F32) + cf_b
        for k in range(CONF_K):
            acc = acc + cf_w_ref[k:k + 1, :] * hbuf[base + k:base + k + CONV_ROWS, :]
        hln = _layer_norm(acc, cf_g, cf_beta)
        hn_ref[c * CONV_ROWS:(c + 1) * CONV_ROWS, :] = (hln * _sigmoid(hln)).astype(BF16)
    y2 = _dot(hn_ref[...], cf_pw_ref[...])
    yn_ref[:, D_GROUP:2 * D_GROUP] = _rms_norm(y2, mixg[:, D_GROUP:2 * D_GROUP]).astype(BF16)

    zc = _dot(xb, w_in_ref[:, 4 * D_GROUP:5 * D_GROUP])
    pbuf[HALO:HALO + TS, :] = zc
    n2 = HALO + TS - 8
    s2buf[8:8 + n2, :] = pbuf[8:8 + n2, :] + pbuf[7:7 + n2, :]
    n4 = HALO + TS - 16
    s4buf[16:16 + n4, :] = s2buf[16:16 + n4, :] + s2buf[14:14 + n4, :]
    n8 = HALO + TS - 24
    s8buf[24:24 + n8, :] = s4buf[24:24 + n8, :] + s4buf[20:20 + n8, :]
    s16 = s8buf[HALO:HALO + TS, :] + s8buf[HALO - 8:HALO - 8 + TS, :]
    lane_t = lax.broadcasted_iota(jnp.int32, (TS, D_GROUP), 1)
    pos1 = lax.broadcasted_iota(jnp.int32, (TS, D_GROUP), 0) + (s_idx * TS + 1)
    wsum = jnp.where(lane_t < HEAD_DIM, s2buf[HALO:HALO + TS, :],
                     jnp.where(lane_t < 2 * HEAD_DIM, s4buf[HALO:HALO + TS, :],
                               jnp.where(lane_t < 3 * HEAD_DIM, s8buf[HALO:HALO + TS, :], s16)))
    win = jnp.where(lane_t < HEAD_DIM, 2,
                    jnp.where(lane_t < 2 * HEAD_DIM, 4, jnp.where(lane_t < 3 * HEAD_DIM, 8, 16)))
    count = jnp.minimum(pos1, win).astype(F32)
    pooled = wsum / count - zc
    y3 = _dot(pooled.astype(BF16), pool_w_ref[...]) * pool_scale_ref[...]
    yn_ref[:, 2 * D_GROUP:3 * D_GROUP] = _rms_norm(y3, mixg[:, 2 * D_GROUP:3 * D_GROUP]).astype(BF16)

    z_d = _dot(xb, w_in_ref[:, 5 * D_GROUP:8 * D_GROUP])
    cbuf[SC_HALO:SC_HALO + TS, :] = z_d[:, D_GROUP:2 * D_GROUP] * z_d[:, 2 * D_GROUP:3 * D_GROUP]
    conv = jnp.zeros((TS, D_GROUP), F32)
    for k in range(SHORT_K):
        off = SC_HALO - (SHORT_K - 1) + k
        conv = conv + sc_w_ref[k:k + 1, :] * cbuf[off:off + TS, :]
    y4 = z_d[:, 0:D_GROUP] * conv
    yn_ref[:, 3 * D_GROUP:4 * D_GROUP] = _rms_norm(y4, mixg[:, 3 * D_GROUP:4 * D_GROUP]).astype(BF16)

    hbuf[0:HALO, :] = hbuf[TS:TS + HALO, :]
    pbuf[0:HALO, :] = pbuf[TS:TS + HALO, :]
    cbuf[0:SC_HALO, :] = cbuf[TS:TS + SC_HALO, :]

    m = _dot(yn_ref[...], w_o_ref[...])
    x1 = _layer_norm(ALPHA * x + m, ln_g_ref[...], ln_b_ref[...])
    x1_ref[0] = x1

    hi = x1.astype(BF16)
    lo = (x1 - hi.astype(F32)).astype(BF16)
    logits = _dot(hi, rw_hi_ref[...]) + _dot(hi, rw_lo_ref[...]) + _dot(lo, rw_hi_ref[...])
    raw_t = logits.T
    sel_t = (logits + rb_ref[...]).T
    raw = [raw_t[e:e + 1, :] for e in range(N_EXPERTS)]
    sel = [sel_t[e:e + 1, :] for e in range(N_EXPERTS)]
    gates = _route(sel, raw)
    gates_t = jnp.concatenate(gates + [jnp.zeros((LANES - N_EXPERTS, TS), F32)], axis=0)
    gates_ref[0] = gates_t.T


def _moe_dense_kernel(x_ref, gates_ref, wg_ref, wu_ref, wd_ref, ln_g_ref, ln_b_ref, out_ref, acc_ref):
    e = pl.program_id(1)

    @pl.when(e == 0)
    def _():
        acc_ref[...] = jnp.zeros_like(acc_ref)

    xb = x_ref[...].astype(BF16)
    g = _dot(xb, wg_ref[0])
    u = _dot(xb, wu_ref[0])
    h = (g * _sigmoid(g)) * u
    y = _dot(h.astype(BF16), wd_ref[0])
    lane = lax.broadcasted_iota(jnp.int32, (TM, LANES), 1)
    gate = jnp.sum(jnp.where(lane == e, gates_ref[...], 0.0), axis=-1, keepdims=True)
    acc_ref[...] += gate * y

    @pl.when(e == N_EXPERTS - 1)
    def _():
        out_ref[...] = _layer_norm(ALPHA * x_ref[...] + acc_ref[...], ln_g_ref[...], ln_b_ref[...])


def _full(shape):
    return pl.BlockSpec(shape, lambda *_: (0,) * len(shape))


def _mixer_call(x, p):
    batch, seq, _ = x.shape
    in_specs = [
        pl.BlockSpec((1, TS, D_MODEL), lambda b, s: (b, s, 0)),
        _full((D_MODEL, 8 * D_GROUP)),
        _full((1, D_GROUP)), _full((1, D_GROUP)),
        _full((N_HEADS * CHUNK, CHUNK)), _full((CHUNK, D_GROUP)),
        _full((HALO, D_GROUP)), _full((1, D_GROUP)), _full((1, D_GROUP)), _full((1, D_GROUP)),
        _full((D_GROUP, D_GROUP)),
        _full((D_GROUP, D_GROUP)), _full((1, D_GROUP)),
        _full((SUBLANES, D_GROUP)),
        _full((1, D_MODEL)), _full((D_MODEL, D_MODEL)),
        _full((1, D_MODEL)), _full((1, D_MODEL)),
        _full((D_MODEL, LANES)), _full((D_MODEL, LANES)), _full((1, LANES)),
    ]
    out_specs = [
        pl.BlockSpec((1, TS, D_MODEL), lambda b, s: (b, s, 0)),
        pl.BlockSpec((1, TS, LANES), lambda b, s: (b, s, 0)),
    ]
    scratch = [
        pltpu.VMEM((HALO + TS, D_GROUP), F32),
        pltpu.VMEM((HALO + TS, D_GROUP), F32),
        pltpu.VMEM((HALO + TS, D_GROUP), F32),
        pltpu.VMEM((HALO + TS, D_GROUP), F32),
        pltpu.VMEM((HALO + TS, D_GROUP), F32),
        pltpu.VMEM((SC_HALO + TS, D_GROUP), F32),
        pltpu.VMEM((TS, D_GROUP), BF16),
        pltpu.VMEM((TS, D_MODEL), BF16),
    ]
    return pl.pallas_call(
        _mixer_kernel,
        grid=(batch, seq // TS),
        in_specs=in_specs,
        out_specs=out_specs,
        out_shape=[jax.ShapeDtypeStruct((batch, seq, D_MODEL), F32),
                   jax.ShapeDtypeStruct((batch, seq, LANES), F32)],
        scratch_shapes=scratch,
        compiler_params=pltpu.CompilerParams(
            dimension_semantics=("arbitrary", "arbitrary"), vmem_limit_bytes=VMEM_LIMIT),
        name="mixer",
    )(x, *p)


def _moe_dense_call(x2d, gates2d, wg, wu, wd, ln_g, ln_b):
    n_tok = x2d.shape[0]
    return pl.pallas_call(
        _moe_dense_kernel,
        grid=(n_tok // TM, N_EXPERTS),
        in_specs=[
            pl.BlockSpec((TM, D_MODEL), lambda i, e: (i, 0)),
            pl.BlockSpec((TM, LANES), lambda i, e: (i, 0)),
            pl.BlockSpec((1, D_MODEL, D_EXPERT), lambda i, e: (e, 0, 0)),
            pl.BlockSpec((1, D_MODEL, D_EXPERT), lambda i, e: (e, 0, 0)),
            pl.BlockSpec((1, D_EXPERT, D_MODEL), lambda i, e: (e, 0, 0)),
            _full((1, D_MODEL)), _full((1, D_MODEL)),
        ],
        out_specs=pl.BlockSpec((TM, D_MODEL), lambda i, e: (i, 0)),
        out_shape=jax.ShapeDtypeStruct((n_tok, D_MODEL), F32),
        scratch_shapes=[pltpu.VMEM((TM, D_MODEL), F32)],
        compiler_params=pltpu.CompilerParams(
            dimension_semantics=("arbitrary", "arbitrary"), vmem_limit_bytes=VMEM_LIMIT),
        name="moe_dense",
    )(x2d, gates2d, wg, wu, wd, ln_g, ln_b)


def _block_diag(w):
    g, d, _ = w.shape
    eye = jnp.eye(g, dtype=w.dtype)
    return (eye[:, None, :, None] * w[:, :, None, :]).reshape(g * d, g * d)


def _pad_rows(w, rows):
    return jnp.pad(w, ((0, rows - w.shape[0]), (0, 0)))


def kernel(x, w_in, gm_ln_g, gm_ln_b, gm_w_s, gm_b_s, cf_dw_w, cf_dw_b, cf_ln_g, cf_ln_b, cf_pw,
           pool_w, pool_scale, sc_w, mix_norm_g, w_o, ln1_g, ln1_b, router_w, router_b,
           exp_w_gate, exp_w_up, exp_w_down, ln2_g, ln2_b):
    batch, seq, _ = x.shape
    assert seq % TS == 0 and (batch * seq) % TM == 0 and TS % CHUNK == 0 and TS % CONV_ROWS == 0
    row = lambda a: a.reshape(1, -1)
    rw = jnp.pad(router_w, ((0, 0), (0, LANES - N_EXPERTS)))
    rw_hi = rw.astype(BF16)
    rw_lo = (rw - rw_hi.astype(F32)).astype(BF16)
    rb = jnp.pad(router_b, (0, LANES - N_EXPERTS)).reshape(1, LANES)
    for l in range(DEPTH):
        params = (
            w_in[l].astype(BF16),
            row(gm_ln_g[l]), row(gm_ln_b[l]),
            gm_w_s[l].reshape(N_HEADS * CHUNK, CHUNK),
            jnp.repeat(gm_b_s[l].T, HEAD_DIM, axis=1),
            _pad_rows(cf_dw_w[l], HALO), row(cf_dw_b[l]), row(cf_ln_g[l]), row(cf_ln_b[l]),
            cf_pw[l].astype(BF16),
            _block_diag(pool_w[l]).astype(BF16), row(pool_scale[l]),
            _pad_rows(sc_w[l], SUBLANES),
            row(mix_norm_g[l]), w_o[l].astype(BF16),
            row(ln1_g[l]), row(ln1_b[l]),
            rw_hi, rw_lo, rb,
        )
        x1, gates = _mixer_call(x, params)
        x2 = _moe_dense_call(
            x1.reshape(batch * seq, D_MODEL), gates.reshape(batch * seq, LANES),
            exp_w_gate[l].astype(BF16), exp_w_up[l].astype(BF16), exp_w_down[l].astype(BF16),
            row(ln2_g[l]), row(ln2_b[l]))
        x = x2.reshape(batch, seq, D_MODEL)
    return x
```

```python
import jax
import jax.numpy as jnp
from jax import lax
from jax.experimental import pallas as pl
from jax.experimental.pallas import tpu as pltpu

D_MODEL = 1024
DEPTH = 2
D_GROUP = 256
N_HEADS = 4
HEAD_DIM = 64
CHUNK = 128
CONF_K = 31
SHORT_K = 3
N_EXPERTS = 16
N_GROUPS = 4
GROUP_SIZE = 4
N_PAIRS = 6
N_CLASSES = N_GROUPS * N_PAIRS
D_EXPERT = 512
ALPHA = (2 * DEPTH) ** 0.25
LN_EPS = 1e-5
RMS_EPS = 1e-6

LANES = 128
SUBLANES = 8
ROW_TILES = D_MODEL // LANES
HALO = 32
SC_HALO = SUBLANES
TS = 512
CONV_ROWS = 64
TMS = 256
TD = 256
VMEM_LIMIT = 56 * 1024 * 1024

BF16 = jnp.bfloat16
F32 = jnp.float32
I32 = jnp.int32


def _dot(a, b):
    return jnp.dot(a, b, preferred_element_type=F32)


def _layer_norm(x, g, b):
    mu = jnp.mean(x, axis=-1, keepdims=True)
    xc = x - mu
    var = jnp.mean(xc * xc, axis=-1, keepdims=True)
    return xc * lax.rsqrt(var + LN_EPS) * g + b


def _rms_norm(y, g):
    ms = jnp.mean(y * y, axis=-1, keepdims=True)
    return y * lax.rsqrt(ms + RMS_EPS) * g


def _sigmoid(x):
    return 1.0 / (1.0 + jnp.exp(-x))


def _route_class(sel):
    scores = []
    for g in range(N_GROUPS):
        v = sel[g * GROUP_SIZE:(g + 1) * GROUP_SIZE]
        best_pair = None
        for i in range(GROUP_SIZE):
            for j in range(i + 1, GROUP_SIZE):
                p = v[i] + v[j]
                best_pair = p if best_pair is None else jnp.maximum(best_pair, p)
        scores.append(best_pair)
    best = jnp.zeros(scores[0].shape, I32)
    best_score = scores[0]
    for g in range(1, N_GROUPS):
        better = scores[g] > best_score
        best = jnp.where(better, g, best)
        best_score = jnp.where(better, scores[g], best_score)
    v = []
    for j in range(GROUP_SIZE):
        out = sel[j]
        for g in range(1, N_GROUPS):
            out = jnp.where(best == g, sel[g * GROUP_SIZE + j], out)
        v.append(out)
    i0 = jnp.zeros_like(best)
    v0 = v[0]
    for j in range(1, GROUP_SIZE):
        better = v[j] > v0
        i0 = jnp.where(better, j, i0)
        v0 = jnp.where(better, v[j], v0)
    neg = jnp.full_like(v0, -jnp.inf)
    w = [jnp.where(i0 == j, neg, v[j]) for j in range(GROUP_SIZE)]
    i1 = jnp.zeros_like(best)
    v1 = w[0]
    for j in range(1, GROUP_SIZE):
        better = w[j] > v1
        i1 = jnp.where(better, j, i1)
        v1 = jnp.where(better, w[j], v1)
    a = jnp.minimum(i0, i1)
    b = jnp.maximum(i0, i1)
    pair = jnp.where(a == 0, b - 1, jnp.where(a == 1, b + 1, N_PAIRS - 1))
    return best * N_PAIRS + pair


def _mixer_kernel(x_ref, w_in_ref, gm_g_ref, gm_b_ref, gm_w_ref, gm_bs_ref,
                  cf_w_ref, cf_b_ref, cf_g_ref, cf_beta_ref, cf_pw_ref,
                  pool_w_ref, pool_scale_ref, sc_w_ref, mixg_ref, w_o_ref,
                  ln_g_ref, ln_b_ref, rw_hi_ref, rw_lo_ref, rb_ref,
                  x1_ref, cls_ref,
                  hbuf, pbuf, s2buf, s4buf, s8buf, cbuf, hn_ref, yn_ref):
    s_idx = pl.program_id(1)

    @pl.when(s_idx == 0)
    def _():
        hbuf[0:HALO, :] = jnp.zeros((HALO, D_GROUP), F32)
        pbuf[0:HALO, :] = jnp.zeros((HALO, D_GROUP), F32)
        cbuf[0:SC_HALO, :] = jnp.zeros((SC_HALO, D_GROUP), F32)

    x = x_ref[0]
    xb = x.astype(BF16)

    z_a = _dot(xb, w_in_ref[:, 0:2 * D_GROUP])
    u = z_a[:, 0:D_GROUP]
    v = _layer_norm(z_a[:, D_GROUP:2 * D_GROUP], gm_g_ref[...], gm_b_ref[...])
    row = lax.broadcasted_iota(I32, (N_HEADS * CHUNK, CHUNK), 0)
    col = lax.broadcasted_iota(I32, (N_HEADS * CHUNK, CHUNK), 1)
    w_tril = jnp.where(col <= (row & (CHUNK - 1)), gm_w_ref[...], 0.0).astype(BF16)
    lane = lax.broadcasted_iota(I32, (CHUNK, D_GROUP), 1)
    bs = gm_bs_ref[...]
    mixg = mixg_ref[...]
    for n in range(TS // CHUNK):
        rows = slice(n * CHUNK, (n + 1) * CHUNK)
        s_all = _dot(w_tril, v[rows, :].astype(BF16))
        s_sel = s_all[3 * CHUNK:4 * CHUNK]
        for h in (2, 1, 0):
            s_sel = jnp.where(lane < (h + 1) * HEAD_DIM, s_all[h * CHUNK:(h + 1) * CHUNK], s_sel)
        y1 = u[rows, :] * (s_sel + bs)
        yn_ref[rows, 0:D_GROUP] = _rms_norm(y1, mixg[:, 0:D_GROUP]).astype(BF16)

    z_b = _dot(xb, w_in_ref[:, 2 * D_GROUP:4 * D_GROUP])
    hbuf[HALO:HALO + TS, :] = z_b[:, 0:D_GROUP] * _sigmoid(z_b[:, D_GROUP:2 * D_GROUP])
    cf_b = cf_b_ref[...]
    cf_g = cf_g_ref[...]
    cf_beta = cf_beta_ref[...]
    for c in range(TS // CONV_ROWS):
        base = c * CONV_ROWS + HALO - (CONF_K - 1)
        acc = jnp.zeros((CONV_ROWS, D_GROUP), F32) + cf_b
        for k in range(CONF_K):
            acc = acc + cf_w_ref[k:k + 1, :] * hbuf[base + k:base + k + CONV_ROWS, :]
        hln = _layer_norm(acc, cf_g, cf_beta)
        hn_ref[c * CONV_ROWS:(c + 1) * CONV_ROWS, :] = (hln * _sigmoid(hln)).astype(BF16)
    y2 = _dot(hn_ref[...], cf_pw_ref[...])
    yn_ref[:, D_GROUP:2 * D_GROUP] = _rms_norm(y2, mixg[:, D_GROUP:2 * D_GROUP]).astype(BF16)

    zc = _dot(xb, w_in_ref[:, 4 * D_GROUP:5 * D_GROUP])
    pbuf[HALO:HALO + TS, :] = zc
    n2 = HALO + TS - 8
    s2buf[8:8 + n2, :] = pbuf[8:8 + n2, :] + pbuf[7:7 + n2, :]
    n4 = HALO + TS - 16
    s4buf[16:16 + n4, :] = s2buf[16:16 + n4, :] + s2buf[14:14 + n4, :]
    n8 = HALO + TS - 24
    s8buf[24:24 + n8, :] = s4buf[24:24 + n8, :] + s4buf[20:20 + n8, :]
    s16 = s8buf[HALO:HALO + TS, :] + s8buf[HALO - 8:HALO - 8 + TS, :]
    lane_t = lax.broadcasted_iota(I32, (TS, D_GROUP), 1)
    pos1 = lax.broadcasted_iota(I32, (TS, D_GROUP), 0) + (s_idx * TS + 1)
    wsum = jnp.where(lane_t < HEAD_DIM, s2buf[HALO:HALO + TS, :],
                     jnp.where(lane_t < 2 * HEAD_DIM, s4buf[HALO:HALO + TS, :],
                               jnp.where(lane_t < 3 * HEAD_DIM, s8buf[HALO:HALO + TS, :], s16)))
    win = jnp.where(lane_t < HEAD_DIM, 2,
                    jnp.where(lane_t < 2 * HEAD_DIM, 4, jnp.where(lane_t < 3 * HEAD_DIM, 8, 16)))
    count = jnp.minimum(pos1, win).astype(F32)
    pooled = wsum / count - zc
    y3 = _dot(pooled.astype(BF16), pool_w_ref[...]) * pool_scale_ref[...]
    yn_ref[:, 2 * D_GROUP:3 * D_GROUP] = _rms_norm(y3, mixg[:, 2 * D_GROUP:3 * D_GROUP]).astype(BF16)

    z_d = _dot(xb, w_in_ref[:, 5 * D_GROUP:8 * D_GROUP])
    cbuf[SC_HALO:SC_HALO + TS, :] = z_d[:, D_GROUP:2 * D_GROUP] * z_d[:, 2 * D_GROUP:3 * D_GROUP]
    conv = jnp.zeros((TS, D_GROUP), F32)
    for k in range(SHORT_K):
        off = SC_HALO - (SHORT_K - 1) + k
        conv = conv + sc_w_ref[k:k + 1, :] * cbuf[off:off + TS, :]
    y4 = z_d[:, 0:D_GROUP] * conv
    yn_ref[:, 3 * D_GROUP:4 * D_GROUP] = _rms_norm(y4, mixg[:, 3 * D_GROUP:4 * D_GROUP]).astype(BF16)

    hbuf[0:HALO, :] = hbuf[TS:TS + HALO, :]
    pbuf[0:HALO, :] = pbuf[TS:TS + HALO, :]
    cbuf[0:SC_HALO, :] = cbuf[TS:TS + SC_HALO, :]

    m = _dot(yn_ref[...], w_o_ref[...])
    x1 = _layer_norm(ALPHA * x + m, ln_g_ref[...], ln_b_ref[...])
    x1_ref[0] = x1

    hi = x1.astype(BF16)
    lo = (x1 - hi.astype(F32)).astype(BF16)
    logits = _dot(hi, rw_hi_ref[...]) + _dot(hi, rw_lo_ref[...]) + _dot(lo, rw_hi_ref[...])
    sel_t = (logits + rb_ref[...]).T
    cls = _route_class([sel_t[e:e + 1, :] for e in range(N_EXPERTS)])
    cls_ref[0] = jnp.concatenate([cls[:, k * LANES:(k + 1) * LANES] for k in range(TS // LANES)], axis=0)


def _plan_kernel(cls_ref, dest_ref, meta_ref):
    n = LANES
    cls = cls_ref[...]
    r = lax.broadcasted_iota(I32, (n, n), 0)
    c = lax.broadcasted_iota(I32, (n, n), 1)
    upper = jnp.where(r < c, 1.0, 0.0).astype(BF16)
    lower = jnp.where(c < r, 1.0, 0.0).astype(BF16)
    ones = jnp.ones((n, n), BF16)
    masks = [cls == k for k in range(N_CLASSES)]
    m_all = jnp.concatenate([jnp.where(mk, 1.0, 0.0) for mk in masks], axis=0).astype(BF16)
    within = _dot(m_all, upper)
    rowsum = _dot(m_all, ones)
    rs = jnp.zeros((n, n), F32)
    for k in range(N_CLASSES):
        rs = jnp.where(c == k, rowsum[k * n:(k + 1) * n], rs)
    rs_b = rs.astype(BF16)
    before = _dot(lower, rs_b)
    total = _dot(ones, rs_b)
    ntile = jnp.floor((total + (TMS - 1)) * (1.0 / TMS))
    tstart = _dot(ntile.astype(BF16), upper)
    base = before + tstart * TMS
    dest = jnp.zeros((n, n), F32)
    for k in range(N_CLASSES):
        dest = jnp.where(masks[k], within[k * n:(k + 1) * n] + base[:, k:k + 1], dest)
    dest_ref[...] = dest.astype(I32)

    tend_t = (tstart + ntile).T
    ntile_t = ntile.T
    tile = c.astype(F32)
    ended = jnp.where((r < N_CLASSES) & (tend_t <= tile), 1.0, 0.0)
    is_last = jnp.where((r < N_CLASSES) & (ntile_t > 0) & (tend_t - 1.0 == tile), 1.0, 0.0)
    tcls = jnp.minimum(jnp.sum(ended, axis=0, keepdims=True), N_CLASSES - 1.0)
    grp = (jnp.where(tcls >= N_PAIRS, 1.0, 0.0) + jnp.where(tcls >= 2 * N_PAIRS, 1.0, 0.0)
           + jnp.where(tcls >= 3 * N_PAIRS, 1.0, 0.0))
    pair = tcls - N_PAIRS * grp
    pa = jnp.where(pair >= 3, 1.0, 0.0) + jnp.where(pair >= 5, 1.0, 0.0)
    pb = jnp.where(pair == 0, 1.0, jnp.where((pair == 1) | (pair == 3), 2.0, 3.0))
    n_tiles = tstart[0:1, N_CLASSES:N_CLASSES + 1]
    tile_row = tile[0:1, :]
    active = jnp.where(tile_row < n_tiles, 1.0, 0.0)
    ragged = jnp.where((jnp.sum(is_last, axis=0, keepdims=True) > 0) | (tile_row >= n_tiles), 1.0, 0.0)
    meta = jnp.concatenate(
        [GROUP_SIZE * grp + pa, GROUP_SIZE * grp + pb, active, ragged, jnp.zeros((4, n), F32)], axis=0)
    meta_ref[...] = meta.astype(I32)


def _row_copy_wait(src, dst, sem):
    pltpu.make_async_copy(src, dst, sem).wait()


def _dispatch_kernel(dest_ref, ragged_ref, x_ref, xs_ref, buf, zbuf, sems, zsem):
    i = pl.program_id(0)
    n_steps = pl.num_programs(0)
    tile_rows = TMS * ROW_TILES
    n_tiles = xs_ref.shape[0] // tile_rows

    @pl.when(i == 0)
    def _():
        zbuf[...] = jnp.zeros(zbuf.shape, F32)
        for k in range(n_tiles):
            @pl.when(ragged_ref[k] == 1)
            def _():
                pltpu.make_async_copy(zbuf, xs_ref.at[pl.ds(k * tile_rows, tile_rows), :], zsem).start()
        for k in range(n_tiles):
            @pl.when(ragged_ref[k] == 1)
            def _():
                _row_copy_wait(zbuf, xs_ref.at[pl.ds(k * tile_rows, tile_rows), :], zsem)

    for s in range(2):
        @pl.when(i > 0)
        def _():
            _row_copy_wait(buf.at[s], xs_ref.at[pl.ds(0, TD * ROW_TILES), :], sems.at[s])

        for j in range(ROW_TILES):
            buf[s, pl.ds(j, TD, stride=ROW_TILES), :] = x_ref[s * TD:(s + 1) * TD, j * LANES:(j + 1) * LANES]
        for t in range(TD):
            d = pl.multiple_of(dest_ref[(i * 2 + s) * TD + t] * ROW_TILES, ROW_TILES)
            pltpu.make_async_copy(buf.at[s, pl.ds(t * ROW_TILES, ROW_TILES), :],
                                  xs_ref.at[pl.ds(d, ROW_TILES), :], sems.at[s]).start()

    @pl.when(i == n_steps - 1)
    def _():
        for s in range(2):
            _row_copy_wait(buf.at[s], xs_ref.at[pl.ds(0, TD * ROW_TILES), :], sems.at[s])


def _combine_kernel(dest_ref, ys_ref, out_ref, buf, sems):
    i = pl.program_id(0)
    n_steps = pl.num_programs(0)

    def issue(step, s):
        for t in range(TD):
            d = pl.multiple_of(dest_ref[(step * 2 + s) * TD + t] * ROW_TILES, ROW_TILES)
            pltpu.make_async_copy(ys_ref.at[pl.ds(d, ROW_TILES), :],
                                  buf.at[s, pl.ds(t * ROW_TILES, ROW_TILES), :], sems.at[s]).start()

    @pl.when(i == 0)
    def _():
        for s in range(2):
            issue(0, s)

    for s in range(2):
        _row_copy_wait(ys_ref.at[pl.ds(0, TD * ROW_TILES), :], buf.at[s], sems.at[s])
        out_ref[s * TD:(s + 1) * TD, :] = jnp.concatenate(
            [buf[s, pl.ds(j, TD, stride=ROW_TILES), :] for j in range(ROW_TILES)], axis=1)

        @pl.when(i + 1 < n_steps)
        def _():
            issue(i + 1, s)


def _moe_kernel(ea_ref, eb_ref, act_ref, xs_ref,
                wga_ref, wua_ref, wda_ref, wgb_ref, wub_ref, wdb_ref,
                rwt_ref, ln_g_ref, ln_b_ref, ys_ref):
    i = pl.program_id(0)

    @pl.when(act_ref[i] == 0)
    def _():
        ys_ref[...] = jnp.zeros(ys_ref.shape, F32)

    @pl.when(act_ref[i] == 1)
    def _():
        x = jnp.concatenate([xs_ref[pl.ds(j, TMS, stride=ROW_TILES), :] for j in range(ROW_TILES)], axis=1)
        la = jnp.sum(x * rwt_ref[pl.ds(ea_ref[i], 1), :], axis=-1, keepdims=True)
        lb = jnp.sum(x * rwt_ref[pl.ds(eb_ref[i], 1), :], axis=-1, keepdims=True)
        m = jnp.maximum(la, lb)
        pa = jnp.exp(la - m)
        pb = jnp.exp(lb - m)
        den = pa + pb
        xb = x.astype(BF16)

        def ffn(wg_ref, wu_ref, wd_ref):
            g = _dot(xb, wg_ref[0])
            u = _dot(xb, wu_ref[0])
            h = (g * _sigmoid(g)) * u
            return _dot(h.astype(BF16), wd_ref[0])

        f = (pa / den) * ffn(wga_ref, wua_ref, wda_ref) + (pb / den) * ffn(wgb_ref, wub_ref, wdb_ref)
        out = _layer_norm(ALPHA * x + f, ln_g_ref[...], ln_b_ref[...])
        for j in range(ROW_TILES):
            ys_ref[pl.ds(j, TMS, stride=ROW_TILES), :] = out[:, j * LANES:(j + 1) * LANES]


def _full(shape):
    return pl.BlockSpec(shape, lambda *_: (0,) * len(shape))


def _mixer_call(x, p):
    batch, seq, _ = x.shape
    steps = seq // TS
    in_specs = [
        pl.BlockSpec((1, TS, D_MODEL), lambda b, s: (b, s, 0)),
        _full((D_MODEL, 8 * D_GROUP)),
        _full((1, D_GROUP)), _full((1, D_GROUP)),
        _full((N_HEADS * CHUNK, CHUNK)), _full((CHUNK, D_GROUP)),
        _full((HALO, D_GROUP)), _full((1, D_GROUP)), _full((1, D_GROUP)), _full((1, D_GROUP)),
        _full((D_GROUP, D_GROUP)),
        _full((D_GROUP, D_GROUP)), _full((1, D_GROUP)),
        _full((SUBLANES, D_GROUP)),
        _full((1, D_MODEL)), _full((D_MODEL, D_MODEL)),
        _full((1, D_MODEL)), _full((1, D_MODEL)),
        _full((D_MODEL, LANES)), _full((D_MODEL, LANES)), _full((1, LANES)),
    ]
    out_specs = [
        pl.BlockSpec((1, TS, D_MODEL), lambda b, s: (b, s, 0)),
        pl.BlockSpec((1, TS // LANES, LANES), lambda b, s: (b * steps + s, 0, 0)),
    ]
    scratch = [
        pltpu.VMEM((HALO + TS, D_GROUP), F32),
        pltpu.VMEM((HALO + TS, D_GROUP), F32),
        pltpu.VMEM((HALO + TS, D_GROUP), F32),
        pltpu.VMEM((HALO + TS, D_GROUP), F32),
        pltpu.VMEM((HALO + TS, D_GROUP), F32),
        pltpu.VMEM((SC_HALO + TS, D_GROUP), F32),
        pltpu.VMEM((TS, D_GROUP), BF16),
        pltpu.VMEM((TS, D_MODEL), BF16),
    ]
    return pl.pallas_call(
        _mixer_kernel,
        grid=(batch, steps),
        in_specs=in_specs,
        out_specs=out_specs,
        out_shape=[jax.ShapeDtypeStruct((batch, seq, D_MODEL), F32),
                   jax.ShapeDtypeStruct((batch * steps, TS // LANES, LANES), I32)],
        scratch_shapes=scratch,
        compiler_params=pltpu.CompilerParams(
            dimension_semantics=("arbitrary", "arbitrary"), vmem_limit_bytes=VMEM_LIMIT),
        name="mixer",
    )(x, *p)


def _plan_call(cls2d):
    return pl.pallas_call(
        _plan_kernel,
        out_shape=[jax.ShapeDtypeStruct((LANES, LANES), I32), jax.ShapeDtypeStruct((SUBLANES, LANES), I32)],
        compiler_params=pltpu.CompilerParams(vmem_limit_bytes=VMEM_LIMIT),
        name="moe_plan",
    )(cls2d)


def _dispatch_call(dest, ragged, x2d, n_sorted):
    n_tok = x2d.shape[0]
    return pl.pallas_call(
        _dispatch_kernel,
        grid_spec=pltpu.PrefetchScalarGridSpec(
            num_scalar_prefetch=2,
            grid=(n_tok // (2 * TD),),
            in_specs=[pl.BlockSpec((2 * TD, D_MODEL), lambda i, *_: (i, 0))],
            out_specs=pl.BlockSpec(memory_space=pl.ANY),
            scratch_shapes=[
                pltpu.VMEM((2, TD * ROW_TILES, LANES), F32),
                pltpu.VMEM((TMS * ROW_TILES, LANES), F32),
                pltpu.SemaphoreType.DMA((2,)),
                pltpu.SemaphoreType.DMA(()),
            ]),
        out_shape=jax.ShapeDtypeStruct((n_sorted * ROW_TILES, LANES), F32),
        compiler_params=pltpu.CompilerParams(
            dimension_semantics=("arbitrary",), vmem_limit_bytes=VMEM_LIMIT),
        name="moe_dispatch",
    )(dest, ragged, x2d)


def _combine_call(dest, ys, n_tok):
    return pl.pallas_call(
        _combine_kernel,
        grid_spec=pltpu.PrefetchScalarGridSpec(
            num_scalar_prefetch=1,
            grid=(n_tok // (2 * TD),),
            in_specs=[pl.BlockSpec(memory_space=pl.ANY)],
            out_specs=pl.BlockSpec((2 * TD, D_MODEL), lambda i, *_: (i, 0)),
            scratch_shapes=[
                pltpu.VMEM((2, TD * ROW_TILES, LANES), F32),
                pltpu.SemaphoreType.DMA((2,)),
            ]),
        out_shape=jax.ShapeDtypeStruct((n_tok, D_MODEL), F32),
        compiler_params=pltpu.CompilerParams(
            dimension_semantics=("arbitrary",), vmem_limit_bytes=VMEM_LIMIT),
        name="moe_combine",
    )(dest, ys)


def _moe_call(ea, eb, act, xs, wg, wu, wd, rwt, ln_g, ln_b, n_tiles):
    rows = TMS * ROW_TILES
    w_in_a = pl.BlockSpec((1, D_MODEL, D_EXPERT), lambda i, ea, eb, act: (ea[i], 0, 0))
    w_out_a = pl.BlockSpec((1, D_EXPERT, D_MODEL), lambda i, ea, eb, act: (ea[i], 0, 0))
    w_in_b = pl.BlockSpec((1, D_MODEL, D_EXPERT), lambda i, ea, eb, act: (eb[i], 0, 0))
    w_out_b = pl.BlockSpec((1, D_EXPERT, D_MODEL), lambda i, ea, eb, act: (eb[i], 0, 0))
    tile = pl.BlockSpec((rows, LANES), lambda i, ea, eb, act: (i, 0))
    return pl.pallas_call(
        _moe_kernel,
        grid_spec=pltpu.PrefetchScalarGridSpec(
            num_scalar_prefetch=3,
            grid=(n_tiles,),
            in_specs=[tile, w_in_a, w_in_a, w_out_a, w_in_b, w_in_b, w_out_b,
                      _full((N_EXPERTS, D_MODEL)), _full((1, D_MODEL)), _full((1, D_MODEL))],
            out_specs=tile),
        out_shape=jax.ShapeDtypeStruct(xs.shape, F32),
        compiler_params=pltpu.CompilerParams(
            dimension_semantics=("arbitrary",), vmem_limit_bytes=VMEM_LIMIT),
        name="moe_experts",
    )(ea, eb, act, xs, wg, wu, wd, wg, wu, wd, rwt, ln_g, ln_b)


def _block_diag(w):
    g, d, _ = w.shape
    eye = jnp.eye(g, dtype=w.dtype)
    return (eye[:, None, :, None] * w[:, :, None, :]).reshape(g * d, g * d)


def _pad_rows(w, rows):
    return jnp.pad(w, ((0, rows - w.shape[0]), (0, 0)))


def kernel(x, w_in, gm_ln_g, gm_ln_b, gm_w_s, gm_b_s, cf_dw_w, cf_dw_b, cf_ln_g, cf_ln_b, cf_pw,
           pool_w, pool_scale, sc_w, mix_norm_g, w_o, ln1_g, ln1_b, router_w, router_b,
           exp_w_gate, exp_w_up, exp_w_down, ln2_g, ln2_b):
    batch, seq, _ = x.shape
    n_tok = batch * seq
    assert n_tok == LANES * LANES, "the routing plan lays tokens out as one (128, 128) tile grid"
    assert seq % TS == 0 and TS % CHUNK == 0 and TS % CONV_ROWS == 0 and n_tok % (2 * TD) == 0
    n_tiles = n_tok // TMS + N_CLASSES
    assert n_tiles <= LANES
    row = lambda a: a.reshape(1, -1)
    rw = jnp.pad(router_w, ((0, 0), (0, LANES - N_EXPERTS)))
    rw_hi = rw.astype(BF16)
    rw_lo = (rw - rw_hi.astype(F32)).astype(BF16)
    rb = jnp.pad(router_b, (0, LANES - N_EXPERTS)).reshape(1, LANES)
    rwt = router_w.T
    for l in range(DEPTH):
        params = (
            w_in[l].astype(BF16),
            row(gm_ln_g[l]), row(gm_ln_b[l]),
            gm_w_s[l].reshape(N_HEADS * CHUNK, CHUNK),
            jnp.repeat(gm_b_s[l].T, HEAD_DIM, axis=1),
            _pad_rows(cf_dw_w[l], HALO), row(cf_dw_b[l]), row(cf_ln_g[l]), row(cf_ln_b[l]),
            cf_pw[l].astype(BF16),
            _block_diag(pool_w[l]).astype(BF16), row(pool_scale[l]),
            _pad_rows(sc_w[l], SUBLANES),
            row(mix_norm_g[l]), w_o[l].astype(BF16),
            row(ln1_g[l]), row(ln1_b[l]),
            rw_hi, rw_lo, rb,
        )
        x1, cls = _mixer_call(x, params)
        dest2d, meta = _plan_call(cls.reshape(LANES, LANES))
        dest = dest2d.reshape(n_tok)
        xs = _dispatch_call(dest, meta[3], x1.reshape(n_tok, D_MODEL), n_tiles * TMS)
        ys = _moe_call(meta[0], meta[1], meta[2], xs,
                       exp_w_gate[l].astype(BF16), exp_w_up[l].astype(BF16), exp_w_down[l].astype(BF16),
                       rwt, row(ln2_g[l]), row(ln2_b[l]), n_tiles)
        x = _combine_call(dest, ys, n_tok).reshape(batch, seq, D_MODEL)
    return x
```

```python
import jax
import jax.numpy as jnp
from jax import lax
from jax.experimental import pallas as pl
from jax.experimental.pallas import tpu as pltpu

D_MODEL = 1024
DEPTH = 2
D_GROUP = 256
N_HEADS = 4
HEAD_DIM = 64
CHUNK = 128
CONF_K = 31
SHORT_K = 3
N_EXPERTS = 16
N_GROUPS = 4
GROUP_SIZE = 4
N_PAIRS = 6
N_CLASSES = N_GROUPS * N_PAIRS
D_EXPERT = 512
ALPHA = (2 * DEPTH) ** 0.25
LN_EPS = 1e-5
RMS_EPS = 1e-6

LANES = 128
SUBLANES = 8
ROW_TILES = D_MODEL // LANES
HALO = 32
SC_HALO = SUBLANES
TS = 512
CONV_ROWS = 64
TMS = 256
TD = 256
VMEM_LIMIT = 56 * 1024 * 1024

BF16 = jnp.bfloat16
F32 = jnp.float32
I32 = jnp.int32


def _dot(a, b):
    return jnp.dot(a, b, preferred_element_type=F32)


def _layer_norm(x, g, b):
    mu = jnp.mean(x, axis=-1, keepdims=True)
    xc = x - mu
    var = jnp.mean(xc * xc, axis=-1, keepdims=True)
    return xc * lax.rsqrt(var + LN_EPS) * g + b


def _rms_norm(y, g):
    ms = jnp.mean(y * y, axis=-1, keepdims=True)
    return y * lax.rsqrt(ms + RMS_EPS) * g


def _sigmoid(x):
    return 1.0 / (1.0 + jnp.exp(-x))


def _route_class(sel):
    scores = []
    for g in range(N_GROUPS):
        v = sel[g * GROUP_SIZE:(g + 1) * GROUP_SIZE]
        best_pair = None
        for i in range(GROUP_SIZE):
            for j in range(i + 1, GROUP_SIZE):
                p = v[i] + v[j]
                best_pair = p if best_pair is None else jnp.maximum(best_pair, p)
        scores.append(best_pair)
    best = jnp.zeros(scores[0].shape, I32)
    best_score = scores[0]
    for g in range(1, N_GROUPS):
        better = scores[g] > best_score
        best = jnp.where(better, g, best)
        best_score = jnp.where(better, scores[g], best_score)
    v = []
    for j in range(GROUP_SIZE):
        out = sel[j]
        for g in range(1, N_GROUPS):
            out = jnp.where(best == g, sel[g * GROUP_SIZE + j], out)
        v.append(out)
    i0 = jnp.zeros_like(best)
    v0 = v[0]
    for j in range(1, GROUP_SIZE):
        better = v[j] > v0
        i0 = jnp.where(better, j, i0)
        v0 = jnp.where(better, v[j], v0)
    neg = jnp.full_like(v0, -jnp.inf)
    w = [jnp.where(i0 == j, neg, v[j]) for j in range(GROUP_SIZE)]
    i1 = jnp.zeros_like(best)
    v1 = w[0]
    for j in range(1, GROUP_SIZE):
        better = w[j] > v1
        i1 = jnp.where(better, j, i1)
        v1 = jnp.where(better, w[j], v1)
    a = jnp.minimum(i0, i1)
    b = jnp.maximum(i0, i1)
    pair = jnp.where(a == 0, b - 1, jnp.where(a == 1, b + 1, N_PAIRS - 1))
    return best * N_PAIRS + pair


def _mixer_kernel(x_ref, w_in_ref, gm_g_ref, gm_b_ref, gm_w_ref, gm_bs_ref,
                  cf_w_ref, cf_b_ref, cf_g_ref, cf_beta_ref, cf_pw_ref,
                  pool_w_ref, pool_scale_ref, sc_w_ref, mixg_ref, w_o_ref,
                  ln_g_ref, ln_b_ref, rw_hi_ref, rw_lo_ref, rb_ref,
                  x1_ref, cls_ref,
                  hbuf0, hbuf1, cbo0, cbo1, pbuf, s2buf, s4buf, s8buf, cbuf, yn_ref):
    s_idx = pl.program_id(1)

    @pl.when(s_idx == 0)
    def _():
        hbuf0[0:HALO, :] = jnp.zeros((HALO, LANES), F32)
        hbuf1[0:HALO, :] = jnp.zeros((HALO, LANES), F32)
        pbuf[0:HALO, :] = jnp.zeros((HALO, D_GROUP), F32)
        cbuf[0:SC_HALO, :] = jnp.zeros((SC_HALO, D_GROUP), F32)

    x = x_ref[0]
    xb = x.astype(BF16)

    z_a = _dot(xb, w_in_ref[:, 0:2 * D_GROUP])
    u = z_a[:, 0:D_GROUP]
    v = _layer_norm(z_a[:, D_GROUP:2 * D_GROUP], gm_g_ref[...], gm_b_ref[...])
    row = lax.broadcasted_iota(I32, (N_HEADS * CHUNK, CHUNK), 0)
    col = lax.broadcasted_iota(I32, (N_HEADS * CHUNK, CHUNK), 1)
    w_tril = jnp.where(col <= (row & (CHUNK - 1)), gm_w_ref[...], 0.0).astype(BF16)
    lane = lax.broadcasted_iota(I32, (CHUNK, D_GROUP), 1)
    bs = gm_bs_ref[...]
    mixg = mixg_ref[...]
    for n in range(TS // CHUNK):
        rows = slice(n * CHUNK, (n + 1) * CHUNK)
        s_all = _dot(w_tril, v[rows, :].astype(BF16))
        s_sel = s_all[3 * CHUNK:4 * CHUNK]
        for h in (2, 1, 0):
            s_sel = jnp.where(lane < (h + 1) * HEAD_DIM, s_all[h * CHUNK:(h + 1) * CHUNK], s_sel)
        y1 = u[rows, :] * (s_sel + bs)
        yn_ref[rows, 0:D_GROUP] = _rms_norm(y1, mixg[:, 0:D_GROUP]).astype(BF16)

    z_b = _dot(xb, w_in_ref[:, 2 * D_GROUP:4 * D_GROUP])
    glu = z_b[:, 0:D_GROUP] * _sigmoid(z_b[:, D_GROUP:2 * D_GROUP])
    for half, (hb, co) in enumerate(((hbuf0, cbo0), (hbuf1, cbo1))):
        lanes = slice(half * LANES, (half + 1) * LANES)
        hb[HALO:HALO + TS, :] = glu[:, lanes]
        bias = cf_b_ref[:, lanes]
        for q in range(TS // (2 * CONV_ROWS)):
            for parity in range(2):
                base = q * 2 * CONV_ROWS + HALO - (CONF_K - 1) + parity
                acc = jnp.zeros((CONV_ROWS, LANES), F32) + bias
                for k in range(CONF_K):
                    acc = acc + cf_w_ref[k:k + 1, lanes] * hb[pl.ds(base + k, CONV_ROWS, stride=2), :]
                co[pl.ds(q * 2 * CONV_ROWS + parity, CONV_ROWS, stride=2), :] = acc
    hln = _layer_norm(jnp.concatenate([cbo0[...], cbo1[...]], axis=1), cf_g_ref[...], cf_beta_ref[...])
    y2 = _dot((hln * _sigmoid(hln)).astype(BF16), cf_pw_ref[...])
    yn_ref[:, D_GROUP:2 * D_GROUP] = _rms_norm(y2, mixg[:, D_GROUP:2 * D_GROUP]).astype(BF16)

    zc = _dot(xb, w_in_ref[:, 4 * D_GROUP:5 * D_GROUP])
    pbuf[HALO:HALO + TS, :] = zc
    n2 = HALO + TS - 8
    s2buf[8:8 + n2, :] = pbuf[8:8 + n2, :] + pbuf[7:7 + n2, :]
    n4 = HALO + TS - 16
    s4buf[16:16 + n4, :] = s2buf[16:16 + n4, :] + s2buf[14:14 + n4, :]
    n8 = HALO + TS - 24
    s8buf[24:24 + n8, :] = s4buf[24:24 + n8, :] + s4buf[20:20 + n8, :]
    s16 = s8buf[HALO:HALO + TS, :] + s8buf[HALO - 8:HALO - 8 + TS, :]
    lane_t = lax.broadcasted_iota(I32, (TS, D_GROUP), 1)
    pos1 = lax.broadcasted_iota(I32, (TS, D_GROUP), 0) + (s_idx * TS + 1)
    wsum = jnp.where(lane_t < HEAD_DIM, s2buf[HALO:HALO + TS, :],
                     jnp.where(lane_t < 2 * HEAD_DIM, s4buf[HALO:HALO + TS, :],
                               jnp.where(lane_t < 3 * HEAD_DIM, s8buf[HALO:HALO + TS, :], s16)))
    win = jnp.where(lane_t < HEAD_DIM, 2,
                    jnp.where(lane_t < 2 * HEAD_DIM, 4, jnp.where(lane_t < 3 * HEAD_DIM, 8, 16)))
    count = jnp.minimum(pos1, win).astype(F32)
    pooled = wsum / count - zc
    y3 = _dot(pooled.astype(BF16), pool_w_ref[...]) * pool_scale_ref[...]
    yn_ref[:, 2 * D_GROUP:3 * D_GROUP] = _rms_norm(y3, mixg[:, 2 * D_GROUP:3 * D_GROUP]).astype(BF16)

    z_d = _dot(xb, w_in_ref[:, 5 * D_GROUP:8 * D_GROUP])
    cbuf[SC_HALO:SC_HALO + TS, :] = z_d[:, D_GROUP:2 * D_GROUP] * z_d[:, 2 * D_GROUP:3 * D_GROUP]
    conv = jnp.zeros((TS, D_GROUP), F32)
    for k in range(SHORT_K):
        off = SC_HALO - (SHORT_K - 1) + k
        conv = conv + sc_w_ref[k:k + 1, :] * cbuf[off:off + TS, :]
    y4 = z_d[:, 0:D_GROUP] * conv
    yn_ref[:, 3 * D_GROUP:4 * D_GROUP] = _rms_norm(y4, mixg[:, 3 * D_GROUP:4 * D_GROUP]).astype(BF16)

    hbuf0[0:HALO, :] = hbuf0[TS:TS + HALO, :]
    hbuf1[0:HALO, :] = hbuf1[TS:TS + HALO, :]
    pbuf[0:HALO, :] = pbuf[TS:TS + HALO, :]
    cbuf[0:SC_HALO, :] = cbuf[TS:TS + SC_HALO, :]

    m = _dot(yn_ref[...], w_o_ref[...])
    x1 = _layer_norm(ALPHA * x + m, ln_g_ref[...], ln_b_ref[...])
    x1_ref[0] = x1

    hi = x1.astype(BF16)
    lo = (x1 - hi.astype(F32)).astype(BF16)
    logits = _dot(hi, rw_hi_ref[...]) + _dot(hi, rw_lo_ref[...]) + _dot(lo, rw_hi_ref[...])
    sel_t = (logits + rb_ref[...]).T
    cls = _route_class([sel_t[e:e + 1, :] for e in range(N_EXPERTS)])
    cls_ref[0] = jnp.concatenate([cls[:, k * LANES:(k + 1) * LANES] for k in range(TS // LANES)], axis=0)


def _plan_kernel(cls_ref, dest_ref, meta_ref):
    n = LANES
    cls = cls_ref[...]
    r = lax.broadcasted_iota(I32, (n, n), 0)
    c = lax.broadcasted_iota(I32, (n, n), 1)
    upper = jnp.where(r < c, 1.0, 0.0).astype(BF16)
    lower = jnp.where(c < r, 1.0, 0.0).astype(BF16)
    ones = jnp.ones((n, n), BF16)
    masks = [cls == k for k in range(N_CLASSES)]
    m_all = jnp.concatenate([jnp.where(mk, 1.0, 0.0) for mk in masks], axis=0).astype(BF16)
    within = _dot(m_all, upper)
    rowsum = _dot(m_all, ones)
    rs = jnp.zeros((n, n), F32)
    for k in range(N_CLASSES):
        rs = jnp.where(c == k, rowsum[k * n:(k + 1) * n], rs)
    rs_b = rs.astype(BF16)
    before = _dot(lower, rs_b)
    total = _dot(ones, rs_b)
    ntile = jnp.floor((total + (TMS - 1)) * (1.0 / TMS))
    tstart = _dot(ntile.astype(BF16), upper)
    base = before + tstart * TMS
    dest = jnp.zeros((n, n), F32)
    for k in range(N_CLASSES):
        dest = jnp.where(masks[k], within[k * n:(k + 1) * n] + base[:, k:k + 1], dest)
    dest_ref[...] = dest.astype(I32)

    tend_t = (tstart + ntile).T
    ntile_t = ntile.T
    tile = c.astype(F32)
    ended = jnp.where((r < N_CLASSES) & (tend_t <= tile), 1.0, 0.0)
    is_last = jnp.where((r < N_CLASSES) & (ntile_t > 0) & (tend_t - 1.0 == tile), 1.0, 0.0)
    tcls = jnp.minimum(jnp.sum(ended, axis=0, keepdims=True), N_CLASSES - 1.0)
    grp = (jnp.where(tcls >= N_PAIRS, 1.0, 0.0) + jnp.where(tcls >= 2 * N_PAIRS, 1.0, 0.0)
           + jnp.where(tcls >= 3 * N_PAIRS, 1.0, 0.0))
    pair = tcls - N_PAIRS * grp
    pa = jnp.where(pair >= 3, 1.0, 0.0) + jnp.where(pair >= 5, 1.0, 0.0)
    pb = jnp.where(pair == 0, 1.0, jnp.where((pair == 1) | (pair == 3), 2.0, 3.0))
    n_tiles = tstart[0:1, N_CLASSES:N_CLASSES + 1]
    tile_row = tile[0:1, :]
    active = jnp.where(tile_row < n_tiles, 1.0, 0.0)
    ragged = jnp.where((jnp.sum(is_last, axis=0, keepdims=True) > 0) | (tile_row >= n_tiles), 1.0, 0.0)
    meta = jnp.concatenate(
        [GROUP_SIZE * grp + pa, GROUP_SIZE * grp + pb, active, ragged, jnp.zeros((4, n), F32)], axis=0)
    meta_ref[...] = meta.astype(I32)


def _row_copy_wait(src, dst, sem):
    pltpu.make_async_copy(src, dst, sem).wait()


def _dispatch_kernel(dest_ref, ragged_ref, x_ref, xs_ref, buf, zbuf, sems, zsem):
    i = pl.program_id(0)
    n_steps = pl.num_programs(0)
    tile_rows = TMS * ROW_TILES
    n_tiles = xs_ref.shape[0] // tile_rows

    @pl.when(i == 0)
    def _():
        zbuf[...] = jnp.zeros(zbuf.shape, F32)
        for k in range(n_tiles):
            @pl.when(ragged_ref[k] == 1)
            def _():
                pltpu.make_async_copy(zbuf, xs_ref.at[pl.ds(k * tile_rows, tile_rows), :], zsem).start()
        for k in range(n_tiles):
            @pl.when(ragged_ref[k] == 1)
            def _():
                _row_copy_wait(zbuf, xs_ref.at[pl.ds(k * tile_rows, tile_rows), :], zsem)

    for s in range(2):
        @pl.when(i > 0)
        def _():
            _row_copy_wait(buf.at[s], xs_ref.at[pl.ds(0, TD * ROW_TILES), :], sems.at[s])

        for j in range(ROW_TILES):
            buf[s, pl.ds(j, TD, stride=ROW_TILES), :] = x_ref[s * TD:(s + 1) * TD, j * LANES:(j + 1) * LANES]
        for t in range(TD):
            d = pl.multiple_of(dest_ref[(i * 2 + s) * TD + t] * ROW_TILES, ROW_TILES)
            pltpu.make_async_copy(buf.at[s, pl.ds(t * ROW_TILES, ROW_TILES), :],
                                  xs_ref.at[pl.ds(d, ROW_TILES), :], sems.at[s]).start(priority=t % 2)

    @pl.when(i == n_steps - 1)
    def _():
        for s in range(2):
            _row_copy_wait(buf.at[s], xs_ref.at[pl.ds(0, TD * ROW_TILES), :], sems.at[s])


def _combine_kernel(dest_ref, ys_ref, out_ref, buf, sems):
    i = pl.program_id(0)
    n_steps = pl.num_programs(0)

    def issue(step, s):
        for t in range(TD):
            d = pl.multiple_of(dest_ref[(step * 2 + s) * TD + t] * ROW_TILES, ROW_TILES)
            pltpu.make_async_copy(ys_ref.at[pl.ds(d, ROW_TILES), :],
                                  buf.at[s, pl.ds(t * ROW_TILES, ROW_TILES), :], sems.at[s]).start(priority=t % 2)

    @pl.when(i == 0)
    def _():
        for s in range(2):
            issue(0, s)

    for s in range(2):
        _row_copy_wait(ys_ref.at[pl.ds(0, TD * ROW_TILES), :], buf.at[s], sems.at[s])
        out_ref[s * TD:(s + 1) * TD, :] = jnp.concatenate(
            [buf[s, pl.ds(j, TD, stride=ROW_TILES), :] for j in range(ROW_TILES)], axis=1)

        @pl.when(i + 1 < n_steps)
        def _():
            issue(i + 1, s)


def _moe_kernel(ea_ref, eb_ref, act_ref, xs_ref,
                wga_ref, wua_ref, wda_ref, wgb_ref, wub_ref, wdb_ref,
                rwt_ref, ln_g_ref, ln_b_ref, ys_ref):
    i = pl.program_id(0)

    @pl.when(act_ref[i] == 0)
    def _():
        ys_ref[...] = jnp.zeros(ys_ref.shape, F32)

    @pl.when(act_ref[i] == 1)
    def _():
        x = jnp.concatenate([xs_ref[pl.ds(j, TMS, stride=ROW_TILES), :] for j in range(ROW_TILES)], axis=1)
        la = jnp.sum(x * rwt_ref[pl.ds(ea_ref[i], 1), :], axis=-1, keepdims=True)
        lb = jnp.sum(x * rwt_ref[pl.ds(eb_ref[i], 1), :], axis=-1, keepdims=True)
        m = jnp.maximum(la, lb)
        pa = jnp.exp(la - m)
        pb = jnp.exp(lb - m)
        den = pa + pb
        xb = x.astype(BF16)

        def ffn(wg_ref, wu_ref, wd_ref):
            g = _dot(xb, wg_ref[0])
            u = _dot(xb, wu_ref[0])
            h = (g * _sigmoid(g)) * u
            return _dot(h.astype(BF16), wd_ref[0])

        f = (pa / den) * ffn(wga_ref, wua_ref, wda_ref) + (pb / den) * ffn(wgb_ref, wub_ref, wdb_ref)
        out = _layer_norm(ALPHA * x + f, ln_g_ref[...], ln_b_ref[...])
        for j in range(ROW_TILES):
            ys_ref[pl.ds(j, TMS, stride=ROW_TILES), :] = out[:, j * LANES:(j + 1) * LANES]


def _full(shape):
    return pl.BlockSpec(shape, lambda *_: (0,) * len(shape))


def _mixer_call(x, p):
    batch, seq, _ = x.shape
    steps = seq // TS
    in_specs = [
        pl.BlockSpec((1, TS, D_MODEL), lambda b, s: (b, s, 0)),
        _full((D_MODEL, 8 * D_GROUP)),
        _full((1, D_GROUP)), _full((1, D_GROUP)),
        _full((N_HEADS * CHUNK, CHUNK)), _full((CHUNK, D_GROUP)),
        _full((HALO, D_GROUP)), _full((1, D_GROUP)), _full((1, D_GROUP)), _full((1, D_GROUP)),
        _full((D_GROUP, D_GROUP)),
        _full((D_GROUP, D_GROUP)), _full((1, D_GROUP)),
        _full((SUBLANES, D_GROUP)),
        _full((1, D_MODEL)), _full((D_MODEL, D_MODEL)),
        _full((1, D_MODEL)), _full((1, D_MODEL)),
        _full((D_MODEL, LANES)), _full((D_MODEL, LANES)), _full((1, LANES)),
    ]
    out_specs = [
        pl.BlockSpec((1, TS, D_MODEL), lambda b, s: (b, s, 0)),
        pl.BlockSpec((1, TS // LANES, LANES), lambda b, s: (b * steps + s, 0, 0)),
    ]
    scratch = [
        pltpu.VMEM((HALO + TS, LANES), F32),
        pltpu.VMEM((HALO + TS, LANES), F32),
        pltpu.VMEM((TS, LANES), F32),
        pltpu.VMEM((TS, LANES), F32),
        pltpu.VMEM((HALO + TS, D_GROUP), F32),
        pltpu.VMEM((HALO + TS, D_GROUP), F32),
        pltpu.VMEM((HALO + TS, D_GROUP), F32),
        pltpu.VMEM((HALO + TS, D_GROUP), F32),
        pltpu.VMEM((SC_HALO + TS, D_GROUP), F32),
        pltpu.VMEM((TS, D_MODEL), BF16),
    ]
    return pl.pallas_call(
        _mixer_kernel,
        grid=(batch, steps),
        in_specs=in_specs,
        out_specs=out_specs,
        out_shape=[jax.ShapeDtypeStruct((batch, seq, D_MODEL), F32),
                   jax.ShapeDtypeStruct((batch * steps, TS // LANES, LANES), I32)],
        scratch_shapes=scratch,
        compiler_params=pltpu.CompilerParams(
            dimension_semantics=("arbitrary", "arbitrary"), vmem_limit_bytes=VMEM_LIMIT),
        name="mixer",
    )(x, *p)


def _plan_call(cls2d):
    return pl.pallas_call(
        _plan_kernel,
        out_shape=[jax.ShapeDtypeStruct((LANES, LANES), I32), jax.ShapeDtypeStruct((SUBLANES, LANES), I32)],
        compiler_params=pltpu.CompilerParams(vmem_limit_bytes=VMEM_LIMIT),
        name="moe_plan",
    )(cls2d)


def _dispatch_call(dest, ragged, x2d, n_sorted):
    n_tok = x2d.shape[0]
    return pl.pallas_call(
        _dispatch_kernel,
        grid_spec=pltpu.PrefetchScalarGridSpec(
            num_scalar_prefetch=2,
            grid=(n_tok // (2 * TD),),
            in_specs=[pl.BlockSpec((2 * TD, D_MODEL), lambda i, *_: (i, 0))],
            out_specs=pl.BlockSpec(memory_space=pl.ANY),
            scratch_shapes=[
                pltpu.VMEM((2, TD * ROW_TILES, LANES), F32),
                pltpu.VMEM((TMS * ROW_TILES, LANES), F32),
                pltpu.SemaphoreType.DMA((2,)),
                pltpu.SemaphoreType.DMA(()),
            ]),
        out_shape=jax.ShapeDtypeStruct((n_sorted * ROW_TILES, LANES), F32),
        compiler_params=pltpu.CompilerParams(
            dimension_semantics=("arbitrary",), vmem_limit_bytes=VMEM_LIMIT),
        name="moe_dispatch",
    )(dest, ragged, x2d)


def _combine_call(dest, ys, n_tok):
    return pl.pallas_call(
        _combine_kernel,
        grid_spec=pltpu.PrefetchScalarGridSpec(
            num_scalar_prefetch=1,
            grid=(n_tok // (2 * TD),),
            in_specs=[pl.BlockSpec(memory_space=pl.ANY)],
            out_specs=pl.BlockSpec((2 * TD, D_MODEL), lambda i, *_: (i, 0)),
            scratch_shapes=[
                pltpu.VMEM((2, TD * ROW_TILES, LANES), F32),
                pltpu.SemaphoreType.DMA((2,)),
            ]),
        out_shape=jax.ShapeDtypeStruct((n_tok, D_MODEL), F32),
        compiler_params=pltpu.CompilerParams(
            dimension_semantics=("arbitrary",), vmem_limit_bytes=VMEM_LIMIT),
        name="moe_combine",
    )(dest, ys)


def _moe_call(ea, eb, act, xs, wg, wu, wd, rwt, ln_g, ln_b, n_tiles):
    rows = TMS * ROW_TILES
    w_in_a = pl.BlockSpec((1, D_MODEL, D_EXPERT), lambda i, ea, eb, act: (ea[i], 0, 0))
    w_out_a = pl.BlockSpec((1, D_EXPERT, D_MODEL), lambda i, ea, eb, act: (ea[i], 0, 0))
    w_in_b = pl.BlockSpec((1, D_MODEL, D_EXPERT), lambda i, ea, eb, act: (eb[i], 0, 0))
    w_out_b = pl.BlockSpec((1, D_EXPERT, D_MODEL), lambda i, ea, eb, act: (eb[i], 0, 0))
    tile = pl.BlockSpec((rows, LANES), lambda i, ea, eb, act: (i, 0))
    return pl.pallas_call(
        _moe_kernel,
        grid_spec=pltpu.PrefetchScalarGridSpec(
            num_scalar_prefetch=3,
            grid=(n_tiles,),
            in_specs=[tile, w_in_a, w_in_a, w_out_a, w_in_b, w_in_b, w_out_b,
                      _full((N_EXPERTS, D_MODEL)), _full((1, D_MODEL)), _full((1, D_MODEL))],
            out_specs=tile),
        out_shape=jax.ShapeDtypeStruct(xs.shape, F32),
        compiler_params=pltpu.CompilerParams(
            dimension_semantics=("arbitrary",), vmem_limit_bytes=VMEM_LIMIT),
        name="moe_experts",
    )(ea, eb, act, xs, wg, wu, wd, wg, wu, wd, rwt, ln_g, ln_b)


def _block_diag(w):
    g, d, _ = w.shape
    eye = jnp.eye(g, dtype=w.dtype)
    return (eye[:, None, :, None] * w[:, :, None, :]).reshape(g * d, g * d)


def _pad_rows(w, rows):
    return jnp.pad(w, ((0, rows - w.shape[0]), (0, 0)))


def kernel(x, w_in, gm_ln_g, gm_ln_b, gm_w_s, gm_b_s, cf_dw_w, cf_dw_b, cf_ln_g, cf_ln_b, cf_pw,
           pool_w, pool_scale, sc_w, mix_norm_g, w_o, ln1_g, ln1_b, router_w, router_b,
           exp_w_gate, exp_w_up, exp_w_down, ln2_g, ln2_b):
    batch, seq, _ = x.shape
    n_tok = batch * seq
    assert n_tok == LANES * LANES, "the routing plan lays tokens out as one (128, 128) tile grid"
    assert seq % TS == 0 and TS % CHUNK == 0 and TS % (2 * CONV_ROWS) == 0 and n_tok % (2 * TD) == 0
    n_tiles = n_tok // TMS + N_CLASSES
    assert n_tiles <= LANES
    row = lambda a: a.reshape(1, -1)
    rw = jnp.pad(router_w, ((0, 0), (0, LANES - N_EXPERTS)))
    rw_hi = rw.astype(BF16)
    rw_lo = (rw - rw_hi.astype(F32)).astype(BF16)
    rb = jnp.pad(router_b, (0, LANES - N_EXPERTS)).reshape(1, LANES)
    rwt = router_w.T
    for l in range(DEPTH):
        params = (
            w_in[l].astype(BF16),
            row(gm_ln_g[l]), row(gm_ln_b[l]),
            gm_w_s[l].reshape(N_HEADS * CHUNK, CHUNK),
            jnp.repeat(gm_b_s[l].T, HEAD_DIM, axis=1),
            _pad_rows(cf_dw_w[l], HALO), row(cf_dw_b[l]), row(cf_ln_g[l]), row(cf_ln_b[l]),
            cf_pw[l].astype(BF16),
            _block_diag(pool_w[l]).astype(BF16), row(pool_scale[l]),
            _pad_rows(sc_w[l], SUBLANES),
            row(mix_norm_g[l]), w_o[l].astype(BF16),
            row(ln1_g[l]), row(ln1_b[l]),
            rw_hi, rw_lo, rb,
        )
        x1, cls = _mixer_call(x, params)
        dest2d, meta = _plan_call(cls.reshape(LANES, LANES))
        dest = dest2d.reshape(n_tok)
        xs = _dispatch_call(dest, meta[3], x1.reshape(n_tok, D_MODEL), n_tiles * TMS)
        ys = _moe_call(meta[0], meta[1], meta[2], xs,
                       exp_w_gate[l].astype(BF16), exp_w_up[l].astype(BF16), exp_w_down[l].astype(BF16),
                       rwt, row(ln2_g[l]), row(ln2_b[l]), n_tiles)
        x = _combine_call(dest, ys, n_tok).reshape(batch, seq, D_MODEL)
    return x
```

```python
import jax
import jax.numpy as jnp
from jax import lax
from jax.experimental import pallas as pl
from jax.experimental.pallas import tpu as pltpu

D_MODEL = 1024
DEPTH = 2
D_GROUP = 256
N_HEADS = 4
HEAD_DIM = 64
CHUNK = 128
CONF_K = 31
SHORT_K = 3
N_EXPERTS = 16
N_GROUPS = 4
GROUP_SIZE = 4
N_PAIRS = 6
N_CLASSES = N_GROUPS * N_PAIRS
D_EXPERT = 512
ALPHA = (2 * DEPTH) ** 0.25
LN_EPS = 1e-5
RMS_EPS = 1e-6

LANES = 128
SUBLANES = 8
ROW_TILES = D_MODEL // LANES
HALO = 32
SC_HALO = SUBLANES
TS = 512
CONV_ROWS = 64
TMS = 256
TILES_PER_STEP = 2
TD = 256
VMEM_LIMIT = 56 * 1024 * 1024

BF16 = jnp.bfloat16
F32 = jnp.float32
I32 = jnp.int32


def _dot(a, b):
    return jnp.dot(a, b, preferred_element_type=F32)


def _layer_norm(x, g, b):
    mu = jnp.mean(x, axis=-1, keepdims=True)
    xc = x - mu
    var = jnp.mean(xc * xc, axis=-1, keepdims=True)
    return xc * lax.rsqrt(var + LN_EPS) * g + b


def _rms_norm(y, g):
    ms = jnp.mean(y * y, axis=-1, keepdims=True)
    return y * lax.rsqrt(ms + RMS_EPS) * g


def _sigmoid(x):
    return 1.0 / (1.0 + jnp.exp(-x))


def _route_class(sel):
    scores = []
    for g in range(N_GROUPS):
        v = sel[g * GROUP_SIZE:(g + 1) * GROUP_SIZE]
        best_pair = None
        for i in range(GROUP_SIZE):
            for j in range(i + 1, GROUP_SIZE):
                p = v[i] + v[j]
                best_pair = p if best_pair is None else jnp.maximum(best_pair, p)
        scores.append(best_pair)
    best = jnp.zeros(scores[0].shape, I32)
    best_score = scores[0]
    for g in range(1, N_GROUPS):
        better = scores[g] > best_score
        best = jnp.where(better, g, best)
        best_score = jnp.where(better, scores[g], best_score)
    v = []
    for j in range(GROUP_SIZE):
        out = sel[j]
        for g in range(1, N_GROUPS):
            out = jnp.where(best == g, sel[g * GROUP_SIZE + j], out)
        v.append(out)
    i0 = jnp.zeros_like(best)
    v0 = v[0]
    for j in range(1, GROUP_SIZE):
        better = v[j] > v0
        i0 = jnp.where(better, j, i0)
        v0 = jnp.where(better, v[j], v0)
    neg = jnp.full_like(v0, -jnp.inf)
    w = [jnp.where(i0 == j, neg, v[j]) for j in range(GROUP_SIZE)]
    i1 = jnp.zeros_like(best)
    v1 = w[0]
    for j in range(1, GROUP_SIZE):
        better = w[j] > v1
        i1 = jnp.where(better, j, i1)
        v1 = jnp.where(better, w[j], v1)
    a = jnp.minimum(i0, i1)
    b = jnp.maximum(i0, i1)
    pair = jnp.where(a == 0, b - 1, jnp.where(a == 1, b + 1, N_PAIRS - 1))
    return best * N_PAIRS + pair


def _mixer_kernel(x_ref, w_in_ref, gm_g_ref, gm_b_ref, gm_w_ref, gm_bs_ref,
                  cf_w_ref, cf_b_ref, cf_g_ref, cf_beta_ref, cf_pw_ref,
                  pool_w_ref, pool_scale_ref, sc_w_ref, mixg_ref, w_o_ref,
                  ln_g_ref, ln_b_ref, rw_hi_ref, rw_lo_ref, rb_ref,
                  x1_ref, cls_ref,
                  hbuf0, hbuf1, cbo0, cbo1, pbuf, s2buf, s4buf, s8buf, cbuf, yn_ref):
    s_idx = pl.program_id(1)

    @pl.when(s_idx == 0)
    def _():
        hbuf0[0:HALO, :] = jnp.zeros((HALO, LANES), F32)
        hbuf1[0:HALO, :] = jnp.zeros((HALO, LANES), F32)
        pbuf[0:HALO, :] = jnp.zeros((HALO, D_GROUP), F32)
        cbuf[0:SC_HALO, :] = jnp.zeros((SC_HALO, D_GROUP), F32)

    x = x_ref[0]
    xb = x.astype(BF16)

    z_a = _dot(xb, w_in_ref[0, :, 0:2 * D_GROUP])
    u = z_a[:, 0:D_GROUP]
    v = _layer_norm(z_a[:, D_GROUP:2 * D_GROUP], gm_g_ref[...], gm_b_ref[...])
    row = lax.broadcasted_iota(I32, (N_HEADS * CHUNK, CHUNK), 0)
    col = lax.broadcasted_iota(I32, (N_HEADS * CHUNK, CHUNK), 1)
    w_tril = jnp.where(col <= (row & (CHUNK - 1)), gm_w_ref[...], 0.0).astype(BF16)
    lane = lax.broadcasted_iota(I32, (CHUNK, D_GROUP), 1)
    bs = gm_bs_ref[...]
    mixg = mixg_ref[...]
    for n in range(TS // CHUNK):
        rows = slice(n * CHUNK, (n + 1) * CHUNK)
        s_all = _dot(w_tril, v[rows, :].astype(BF16))
        s_sel = s_all[3 * CHUNK:4 * CHUNK]
        for h in (2, 1, 0):
            s_sel = jnp.where(lane < (h + 1) * HEAD_DIM, s_all[h * CHUNK:(h + 1) * CHUNK], s_sel)
        y1 = u[rows, :] * (s_sel + bs)
        yn_ref[rows, 0:D_GROUP] = _rms_norm(y1, mixg[:, 0:D_GROUP]).astype(BF16)

    z_b = _dot(xb, w_in_ref[0, :, 2 * D_GROUP:4 * D_GROUP])
    glu = z_b[:, 0:D_GROUP] * _sigmoid(z_b[:, D_GROUP:2 * D_GROUP])
    for half, (hb, co) in enumerate(((hbuf0, cbo0), (hbuf1, cbo1))):
        lanes = slice(half * LANES, (half + 1) * LANES)
        hb[HALO:HALO + TS, :] = glu[:, lanes]
        bias = cf_b_ref[:, lanes]
        for q in range(TS // (2 * CONV_ROWS)):
            for parity in range(2):
                base = q * 2 * CONV_ROWS + HALO - (CONF_K - 1) + parity
                acc = jnp.zeros((CONV_ROWS, LANES), F32) + bias
                for k in range(CONF_K):
                    acc = acc + cf_w_ref[k:k + 1, lanes] * hb[pl.ds(base + k, CONV_ROWS, stride=2), :]
                co[pl.ds(q * 2 * CONV_ROWS + parity, CONV_ROWS, stride=2), :] = acc
    hln = _layer_norm(jnp.concatenate([cbo0[...], cbo1[...]], axis=1), cf_g_ref[...], cf_beta_ref[...])
    y2 = _dot((hln * _sigmoid(hln)).astype(BF16), cf_pw_ref[0])
    yn_ref[:, D_GROUP:2 * D_GROUP] = _rms_norm(y2, mixg[:, D_GROUP:2 * D_GROUP]).astype(BF16)

    zc = _dot(xb, w_in_ref[0, :, 4 * D_GROUP:5 * D_GROUP])
    pbuf[HALO:HALO + TS, :] = zc
    n2 = HALO + TS - 8
    s2buf[8:8 + n2, :] = pbuf[8:8 + n2, :] + pbuf[7:7 + n2, :]
    n4 = HALO + TS - 16
    s4buf[16:16 + n4, :] = s2buf[16:16 + n4, :] + s2buf[14:14 + n4, :]
    n8 = HALO + TS - 24
    s8buf[24:24 + n8, :] = s4buf[24:24 + n8, :] + s4buf[20:20 + n8, :]
    s16 = s8buf[HALO:HALO + TS, :] + s8buf[HALO - 8:HALO - 8 + TS, :]
    lane_t = lax.broadcasted_iota(I32, (TS, D_GROUP), 1)
    pos1 = lax.broadcasted_iota(I32, (TS, D_GROUP), 0) + (s_idx * TS + 1)
    wsum = jnp.where(lane_t < HEAD_DIM, s2buf[HALO:HALO + TS, :],
                     jnp.where(lane_t < 2 * HEAD_DIM, s4buf[HALO:HALO + TS, :],
                               jnp.where(lane_t < 3 * HEAD_DIM, s8buf[HALO:HALO + TS, :], s16)))
    win = jnp.where(lane_t < HEAD_DIM, 2,
                    jnp.where(lane_t < 2 * HEAD_DIM, 4, jnp.where(lane_t < 3 * HEAD_DIM, 8, 16)))
    count = jnp.minimum(pos1, win).astype(F32)
    pooled = wsum / count - zc
    y3 = _dot(pooled.astype(BF16), pool_w_ref[0]) * pool_scale_ref[...]
    yn_ref[:, 2 * D_GROUP:3 * D_GROUP] = _rms_norm(y3, mixg[:, 2 * D_GROUP:3 * D_GROUP]).astype(BF16)

    z_d = _dot(xb, w_in_ref[0, :, 5 * D_GROUP:8 * D_GROUP])
    cbuf[SC_HALO:SC_HALO + TS, :] = z_d[:, D_GROUP:2 * D_GROUP] * z_d[:, 2 * D_GROUP:3 * D_GROUP]
    conv = jnp.zeros((TS, D_GROUP), F32)
    for k in range(SHORT_K):
        off = SC_HALO - (SHORT_K - 1) + k
        conv = conv + sc_w_ref[k:k + 1, :] * cbuf[off:off + TS, :]
    y4 = z_d[:, 0:D_GROUP] * conv
    yn_ref[:, 3 * D_GROUP:4 * D_GROUP] = _rms_norm(y4, mixg[:, 3 * D_GROUP:4 * D_GROUP]).astype(BF16)

    hbuf0[0:HALO, :] = hbuf0[TS:TS + HALO, :]
    hbuf1[0:HALO, :] = hbuf1[TS:TS + HALO, :]
    pbuf[0:HALO, :] = pbuf[TS:TS + HALO, :]
    cbuf[0:SC_HALO, :] = cbuf[TS:TS + SC_HALO, :]

    m = _dot(yn_ref[...], w_o_ref[0])
    x1 = _layer_norm(ALPHA * x + m, ln_g_ref[...], ln_b_ref[...])
    x1_ref[0] = x1

    hi = x1.astype(BF16)
    lo = (x1 - hi.astype(F32)).astype(BF16)
    logits = _dot(hi, rw_hi_ref[...]) + _dot(hi, rw_lo_ref[...]) + _dot(lo, rw_hi_ref[...])
    sel_t = (logits + rb_ref[...]).T
    cls = _route_class([sel_t[e:e + 1, :] for e in range(N_EXPERTS)])
    cls_ref[0] = jnp.concatenate([cls[:, k * LANES:(k + 1) * LANES] for k in range(TS // LANES)], axis=0)


def _plan_kernel(cls_ref, dest_ref, meta_ref):
    n = LANES
    cls = cls_ref[...]
    r = lax.broadcasted_iota(I32, (n, n), 0)
    c = lax.broadcasted_iota(I32, (n, n), 1)
    upper = jnp.where(r < c, 1.0, 0.0).astype(BF16)
    lower = jnp.where(c < r, 1.0, 0.0).astype(BF16)
    ones = jnp.ones((n, n), BF16)
    masks = [cls == k for k in range(N_CLASSES)]
    m_all = jnp.concatenate([jnp.where(mk, 1.0, 0.0) for mk in masks], axis=0).astype(BF16)
    within = _dot(m_all, upper)
    rowsum = _dot(m_all, ones)
    rs = jnp.zeros((n, n), F32)
    for k in range(N_CLASSES):
        rs = jnp.where(c == k, rowsum[k * n:(k + 1) * n], rs)
    rs_b = rs.astype(BF16)
    before = _dot(lower, rs_b)
    total = _dot(ones, rs_b)
    ntile = jnp.floor((total + (TMS - 1)) * (1.0 / TMS))
    tstart = _dot(ntile.astype(BF16), upper)
    base = before + tstart * TMS
    dest = jnp.zeros((n, n), F32)
    for k in range(N_CLASSES):
        dest = jnp.where(masks[k], within[k * n:(k + 1) * n] + base[:, k:k + 1], dest)
    dest_ref[...] = dest.astype(I32)

    tend_t = (tstart + ntile).T
    ntile_t = ntile.T
    tile = c.astype(F32)
    ended = jnp.where((r < N_CLASSES) & (tend_t <= tile), 1.0, 0.0)
    is_last = jnp.where((r < N_CLASSES) & (ntile_t > 0) & (tend_t - 1.0 == tile), 1.0, 0.0)
    tcls = jnp.minimum(jnp.sum(ended, axis=0, keepdims=True), N_CLASSES - 1.0)
    grp = (jnp.where(tcls >= N_PAIRS, 1.0, 0.0) + jnp.where(tcls >= 2 * N_PAIRS, 1.0, 0.0)
           + jnp.where(tcls >= 3 * N_PAIRS, 1.0, 0.0))
    pair = tcls - N_PAIRS * grp
    pa = jnp.where(pair >= 3, 1.0, 0.0) + jnp.where(pair >= 5, 1.0, 0.0)
    pb = jnp.where(pair == 0, 1.0, jnp.where((pair == 1) | (pair == 3), 2.0, 3.0))
    n_tiles = tstart[0:1, N_CLASSES:N_CLASSES + 1]
    tile_row = tile[0:1, :]
    active = jnp.where(tile_row < n_tiles, 1.0, 0.0)
    ragged = jnp.where((jnp.sum(is_last, axis=0, keepdims=True) > 0) | (tile_row >= n_tiles), 1.0, 0.0)
    meta = jnp.concatenate(
        [GROUP_SIZE * grp + pa, GROUP_SIZE * grp + pb, active, ragged, jnp.zeros((4, n), F32)], axis=0)
    meta_ref[...] = meta.astype(I32)


def _row_copy_wait(src, dst, sem):
    pltpu.make_async_copy(src, dst, sem).wait()


def _dispatch_kernel(dest_ref, ragged_ref, x_ref, xs_ref, buf, zbuf, sems, zsem):
    i = pl.program_id(0)
    n_steps = pl.num_programs(0)
    tile_rows = TMS * ROW_TILES
    n_tiles = xs_ref.shape[0] // tile_rows

    @pl.when(i == 0)
    def _():
        zbuf[...] = jnp.zeros(zbuf.shape, F32)
        for k in range(n_tiles):
            @pl.when(ragged_ref[k] == 1)
            def _():
                pltpu.make_async_copy(zbuf, xs_ref.at[pl.ds(k * tile_rows, tile_rows), :], zsem).start()
        for k in range(n_tiles):
            @pl.when(ragged_ref[k] == 1)
            def _():
                _row_copy_wait(zbuf, xs_ref.at[pl.ds(k * tile_rows, tile_rows), :], zsem)

    for s in range(2):
        @pl.when(i > 0)
        def _():
            _row_copy_wait(buf.at[s], xs_ref.at[pl.ds(0, TD * ROW_TILES), :], sems.at[s])

        for j in range(ROW_TILES):
            buf[s, pl.ds(j, TD, stride=ROW_TILES), :] = x_ref[s * TD:(s + 1) * TD, j * LANES:(j + 1) * LANES]
        for t in range(TD):
            d = pl.multiple_of(dest_ref[(i * 2 + s) * TD + t] * ROW_TILES, ROW_TILES)
            pltpu.make_async_copy(buf.at[s, pl.ds(t * ROW_TILES, ROW_TILES), :],
                                  xs_ref.at[pl.ds(d, ROW_TILES), :], sems.at[s]).start(priority=t % 2)

    @pl.when(i == n_steps - 1)
    def _():
        for s in range(2):
            _row_copy_wait(buf.at[s], xs_ref.at[pl.ds(0, TD * ROW_TILES), :], sems.at[s])


def _combine_kernel(dest_ref, ys_ref, x_ref, ln_g_ref, ln_b_ref, out_ref, buf, sems):
    i = pl.program_id(0)
    n_steps = pl.num_programs(0)

    def issue(step, s):
        for t in range(TD):
            d = pl.multiple_of(dest_ref[(step * 2 + s) * TD + t] * ROW_TILES, ROW_TILES)
            pltpu.make_async_copy(ys_ref.at[pl.ds(d, ROW_TILES), :],
                                  buf.at[s, pl.ds(t * ROW_TILES, ROW_TILES), :], sems.at[s]).start(priority=t % 2)

    @pl.when(i == 0)
    def _():
        for s in range(2):
            issue(0, s)

    for s in range(2):
        _row_copy_wait(ys_ref.at[pl.ds(0, TD * ROW_TILES), :], buf.at[s], sems.at[s])
        f = jnp.concatenate([buf[s, pl.ds(j, TD, stride=ROW_TILES), :] for j in range(ROW_TILES)], axis=1)
        rows = slice(s * TD, (s + 1) * TD)
        out_ref[rows, :] = _layer_norm(ALPHA * x_ref[rows, :] + f, ln_g_ref[...], ln_b_ref[...])

        @pl.when(i + 1 < n_steps)
        def _():
            issue(i + 1, s)


def _moe_kernel(ea_ref, eb_ref, act_ref, xs_ref, *refs):
    i = pl.program_id(0)
    w_refs, rwt_ref, ys_ref = refs[:-2], refs[-2], refs[-1]
    rows = TMS * ROW_TILES

    @pl.when(act_ref[i * TILES_PER_STEP] == 0)
    def _():
        ys_ref[...] = jnp.zeros(ys_ref.shape, F32)

    @pl.when(act_ref[i * TILES_PER_STEP] == 1)
    def _():
        for s in range(TILES_PER_STEP):
            tile = i * TILES_PER_STEP + s
            wga_ref, wua_ref, wda_ref, wgb_ref, wub_ref, wdb_ref = w_refs[6 * s:6 * s + 6]
            x = jnp.concatenate(
                [xs_ref[pl.ds(s * rows + j, TMS, stride=ROW_TILES), :] for j in range(ROW_TILES)], axis=1)
            la = jnp.sum(x * rwt_ref[pl.ds(ea_ref[tile], 1), :], axis=-1, keepdims=True)
            lb = jnp.sum(x * rwt_ref[pl.ds(eb_ref[tile], 1), :], axis=-1, keepdims=True)
            m = jnp.maximum(la, lb)
            pa = jnp.exp(la - m)
            pb = jnp.exp(lb - m)
            den = pa + pb
            xb = x.astype(BF16)

            def ffn(wg_ref, wu_ref, wd_ref):
                g = _dot(xb, wg_ref[0, 0])
                u = _dot(xb, wu_ref[0, 0])
                h = (g * _sigmoid(g)) * u
                return _dot(h.astype(BF16), wd_ref[0, 0])

            f = (pa / den) * ffn(wga_ref, wua_ref, wda_ref) + (pb / den) * ffn(wgb_ref, wub_ref, wdb_ref)
            for j in range(ROW_TILES):
                ys_ref[pl.ds(s * rows + j, TMS, stride=ROW_TILES), :] = f[:, j * LANES:(j + 1) * LANES]


def _full(shape):
    return pl.BlockSpec(shape, lambda *_: (0,) * len(shape))


def _layer_block(shape, layer):
    return pl.BlockSpec((1,) + shape, lambda *_: (layer,) + (0,) * len(shape))


def _mixer_call(x, p, layer):
    batch, seq, _ = x.shape
    steps = seq // TS
    in_specs = [
        pl.BlockSpec((1, TS, D_MODEL), lambda b, s: (b, s, 0)),
        _layer_block((D_MODEL, 8 * D_GROUP), layer),
        _full((1, D_GROUP)), _full((1, D_GROUP)),
        _full((N_HEADS * CHUNK, CHUNK)), _full((CHUNK, D_GROUP)),
        _full((HALO, D_GROUP)), _full((1, D_GROUP)), _full((1, D_GROUP)), _full((1, D_GROUP)),
        _layer_block((D_GROUP, D_GROUP), layer),
        _layer_block((D_GROUP, D_GROUP), layer), _full((1, D_GROUP)),
        _full((SUBLANES, D_GROUP)),
        _full((1, D_MODEL)), _layer_block((D_MODEL, D_MODEL), layer),
        _full((1, D_MODEL)), _full((1, D_MODEL)),
        _full((D_MODEL, LANES)), _full((D_MODEL, LANES)), _full((1, LANES)),
    ]
    out_specs = [
        pl.BlockSpec((1, TS, D_MODEL), lambda b, s: (b, s, 0)),
        pl.BlockSpec((1, TS // LANES, LANES), lambda b, s: (b * steps + s, 0, 0)),
    ]
    scratch = [
        pltpu.VMEM((HALO + TS, LANES), F32),
        pltpu.VMEM((HALO + TS, LANES), F32),
        pltpu.VMEM((TS, LANES), F32),
        pltpu.VMEM((TS, LANES), F32),
        pltpu.VMEM((HALO + TS, D_GROUP), F32),
        pltpu.VMEM((HALO + TS, D_GROUP), F32),
        pltpu.VMEM((HALO + TS, D_GROUP), F32),
        pltpu.VMEM((HALO + TS, D_GROUP), F32),
        pltpu.VMEM((SC_HALO + TS, D_GROUP), F32),
        pltpu.VMEM((TS, D_MODEL), BF16),
    ]
    return pl.pallas_call(
        _mixer_kernel,
        grid=(batch, steps),
        in_specs=in_specs,
        out_specs=out_specs,
        out_shape=[jax.ShapeDtypeStruct((batch, seq, D_MODEL), F32),
                   jax.ShapeDtypeStruct((batch * steps, TS // LANES, LANES), I32)],
        scratch_shapes=scratch,
        compiler_params=pltpu.CompilerParams(
            dimension_semantics=("arbitrary", "arbitrary"), vmem_limit_bytes=VMEM_LIMIT),
        name="mixer",
    )(x, *p)


def _plan_call(cls2d):
    return pl.pallas_call(
        _plan_kernel,
        out_shape=[jax.ShapeDtypeStruct((LANES, LANES), I32), jax.ShapeDtypeStruct((SUBLANES, LANES), I32)],
        compiler_params=pltpu.CompilerParams(vmem_limit_bytes=VMEM_LIMIT),
        name="moe_plan",
    )(cls2d)


def _dispatch_call(dest, ragged, x2d, n_sorted):
    n_tok = x2d.shape[0]
    return pl.pallas_call(
        _dispatch_kernel,
        grid_spec=pltpu.PrefetchScalarGridSpec(
            num_scalar_prefetch=2,
            grid=(n_tok // (2 * TD),),
            in_specs=[pl.BlockSpec((2 * TD, D_MODEL), lambda i, *_: (i, 0))],
            out_specs=pl.BlockSpec(memory_space=pl.ANY),
            scratch_shapes=[
                pltpu.VMEM((2, TD * ROW_TILES, LANES), F32),
                pltpu.VMEM((TMS * ROW_TILES, LANES), F32),
                pltpu.SemaphoreType.DMA((2,)),
                pltpu.SemaphoreType.DMA(()),
            ]),
        out_shape=jax.ShapeDtypeStruct((n_sorted * ROW_TILES, LANES), F32),
        compiler_params=pltpu.CompilerParams(
            dimension_semantics=("arbitrary",), vmem_limit_bytes=VMEM_LIMIT),
        name="moe_dispatch",
    )(dest, ragged, x2d)


def _combine_call(dest, ys, x2d, ln_g, ln_b):
    n_tok = x2d.shape[0]
    return pl.pallas_call(
        _combine_kernel,
        grid_spec=pltpu.PrefetchScalarGridSpec(
            num_scalar_prefetch=1,
            grid=(n_tok // (2 * TD),),
            in_specs=[pl.BlockSpec(memory_space=pl.ANY),
                      pl.BlockSpec((2 * TD, D_MODEL), lambda i, *_: (i, 0)),
                      _full((1, D_MODEL)), _full((1, D_MODEL))],
            out_specs=pl.BlockSpec((2 * TD, D_MODEL), lambda i, *_: (i, 0)),
            scratch_shapes=[
                pltpu.VMEM((2, TD * ROW_TILES, LANES), F32),
                pltpu.SemaphoreType.DMA((2,)),
            ]),
        out_shape=jax.ShapeDtypeStruct((n_tok, D_MODEL), F32),
        compiler_params=pltpu.CompilerParams(
            dimension_semantics=("arbitrary",), vmem_limit_bytes=VMEM_LIMIT),
        name="moe_combine",
    )(dest, ys, x2d, ln_g, ln_b)


def _moe_call(ea, eb, act, xs, wg, wu, wd, rwt, n_tiles, layer):
    rows = TILES_PER_STEP * TMS * ROW_TILES
    w_specs, w_args = [], []
    for s in range(TILES_PER_STEP):
        for sel in (0, 1):
            def expert(i, ea, eb, act, s=s, sel=sel):
                return (layer, (ea, eb)[sel][i * TILES_PER_STEP + s], 0, 0)
            w_specs += [pl.BlockSpec((1, 1, D_MODEL, D_EXPERT), expert),
                        pl.BlockSpec((1, 1, D_MODEL, D_EXPERT), expert),
                        pl.BlockSpec((1, 1, D_EXPERT, D_MODEL), expert)]
            w_args += [wg, wu, wd]
    tile = pl.BlockSpec((rows, LANES), lambda i, ea, eb, act: (i, 0))
    return pl.pallas_call(
        _moe_kernel,
        grid_spec=pltpu.PrefetchScalarGridSpec(
            num_scalar_prefetch=3,
            grid=(n_tiles // TILES_PER_STEP,),
            in_specs=[tile] + w_specs + [_full((N_EXPERTS, D_MODEL))],
            out_specs=tile),
        out_shape=jax.ShapeDtypeStruct(xs.shape, F32),
        compiler_params=pltpu.CompilerParams(
            dimension_semantics=("arbitrary",), vmem_limit_bytes=VMEM_LIMIT),
        name="moe_experts",
    )(ea, eb, act, xs, *w_args, rwt)


def _block_diag(w):
    g, d, _ = w.shape
    eye = jnp.eye(g, dtype=w.dtype)
    return (eye[:, None, :, None] * w[:, :, None, :]).reshape(g * d, g * d)


def _pad_rows(w, rows):
    return jnp.pad(w, ((0, rows - w.shape[0]), (0, 0)))


def kernel(x, w_in, gm_ln_g, gm_ln_b, gm_w_s, gm_b_s, cf_dw_w, cf_dw_b, cf_ln_g, cf_ln_b, cf_pw,
           pool_w, pool_scale, sc_w, mix_norm_g, w_o, ln1_g, ln1_b, router_w, router_b,
           exp_w_gate, exp_w_up, exp_w_down, ln2_g, ln2_b):
    batch, seq, _ = x.shape
    n_tok = batch * seq
    assert n_tok == LANES * LANES, "the routing plan lays tokens out as one (128, 128) tile grid"
    assert seq % TS == 0 and TS % CHUNK == 0 and TS % (2 * CONV_ROWS) == 0 and n_tok % (2 * TD) == 0
    n_tiles = n_tok // TMS + N_CLASSES
    assert n_tiles <= LANES and n_tiles % TILES_PER_STEP == 0
    row = lambda a: a.reshape(1, -1)
    rw = jnp.pad(router_w, ((0, 0), (0, LANES - N_EXPERTS)))
    rw_hi = rw.astype(BF16)
    rw_lo = (rw - rw_hi.astype(F32)).astype(BF16)
    rb = jnp.pad(router_b, (0, LANES - N_EXPERTS)).reshape(1, LANES)
    rwt = router_w.T
    w_in_b, w_o_b, cf_pw_b = w_in.astype(BF16), w_o.astype(BF16), cf_pw.astype(BF16)
    pool_b = jax.vmap(_block_diag)(pool_w).astype(BF16)
    wg_b, wu_b, wd_b = exp_w_gate.astype(BF16), exp_w_up.astype(BF16), exp_w_down.astype(BF16)
    for l in range(DEPTH):
        params = (
            w_in_b,
            row(gm_ln_g[l]), row(gm_ln_b[l]),
            gm_w_s[l].reshape(N_HEADS * CHUNK, CHUNK),
            jnp.repeat(gm_b_s[l].T, HEAD_DIM, axis=1),
            _pad_rows(cf_dw_w[l], HALO), row(cf_dw_b[l]), row(cf_ln_g[l]), row(cf_ln_b[l]),
            cf_pw_b,
            pool_b, row(pool_scale[l]),
            _pad_rows(sc_w[l], SUBLANES),
            row(mix_norm_g[l]), w_o_b,
            row(ln1_g[l]), row(ln1_b[l]),
            rw_hi, rw_lo, rb,
        )
        x1, cls = _mixer_call(x, params, l)
        x1 = x1.reshape(n_tok, D_MODEL)
        dest2d, meta = _plan_call(cls.reshape(LANES, LANES))
        dest = dest2d.reshape(n_tok)
        xs = _dispatch_call(dest, meta[3], x1, n_tiles * TMS)
        ys = _moe_call(meta[0], meta[1], meta[2], xs, wg_b, wu_b, wd_b, rwt, n_tiles, l)
        x = _combine_call(dest, ys, x1, row(ln2_g[l]), row(ln2_b[l])).reshape(batch, seq, D_MODEL)
    return x
```

```python
import functools

import jax
import jax.numpy as jnp
from jax import lax
from jax.experimental import pallas as pl
from jax.experimental.pallas import tpu as pltpu
from jax.experimental.pallas import tpu_sc as plsc

D_MODEL = 1024
DEPTH = 2
D_GROUP = 256
N_HEADS = 4
HEAD_DIM = 64
CHUNK = 128
CONF_K = 31
SHORT_K = 3
N_EXPERTS = 16
N_GROUPS = 4
GROUP_SIZE = 4
N_PAIRS = 6
N_CLASSES = N_GROUPS * N_PAIRS
D_EXPERT = 512
ALPHA = (2 * DEPTH) ** 0.25
LN_EPS = 1e-5
RMS_EPS = 1e-6

LANES = 128
SUBLANES = 8
ROW_TILES = D_MODEL // LANES
HALO = 32
SC_HALO = SUBLANES
TS = 512
SUB_ROWS = 512
CONV_ROWS = 64
TMS = 256
TILES_PER_STEP = 2
TD = 256
SC_CORES = 2
SC_WORKERS = 32
SC_CHUNK = 32
VMEM_LIMIT = 56 * 1024 * 1024

BF16 = jnp.bfloat16
F32 = jnp.float32
I32 = jnp.int32


def _dot(a, b):
    return jnp.dot(a, b, preferred_element_type=F32)


def _layer_norm(x, g, b):
    mu = jnp.mean(x, axis=-1, keepdims=True)
    xc = x - mu
    var = jnp.mean(xc * xc, axis=-1, keepdims=True)
    return xc * lax.rsqrt(var + LN_EPS) * g + b


def _rms_norm(y, g):
    ms = jnp.mean(y * y, axis=-1, keepdims=True)
    return y * lax.rsqrt(ms + RMS_EPS) * g


def _sigmoid(x):
    return 1.0 / (1.0 + jnp.exp(-x))


def _route_class(sel):
    scores = []
    for g in range(N_GROUPS):
        v = sel[g * GROUP_SIZE:(g + 1) * GROUP_SIZE]
        best_pair = None
        for i in range(GROUP_SIZE):
            for j in range(i + 1, GROUP_SIZE):
                p = v[i] + v[j]
                best_pair = p if best_pair is None else jnp.maximum(best_pair, p)
        scores.append(best_pair)
    best = jnp.zeros(scores[0].shape, I32)
    best_score = scores[0]
    for g in range(1, N_GROUPS):
        better = scores[g] > best_score
        best = jnp.where(better, g, best)
        best_score = jnp.where(better, scores[g], best_score)
    v = []
    for j in range(GROUP_SIZE):
        out = sel[j]
        for g in range(1, N_GROUPS):
            out = jnp.where(best == g, sel[g * GROUP_SIZE + j], out)
        v.append(out)
    i0 = jnp.zeros_like(best)
    v0 = v[0]
    for j in range(1, GROUP_SIZE):
        better = v[j] > v0
        i0 = jnp.where(better, j, i0)
        v0 = jnp.where(better, v[j], v0)
    neg = jnp.full_like(v0, -jnp.inf)
    w = [jnp.where(i0 == j, neg, v[j]) for j in range(GROUP_SIZE)]
    i1 = jnp.zeros_like(best)
    v1 = w[0]
    for j in range(1, GROUP_SIZE):
        better = w[j] > v1
        i1 = jnp.where(better, j, i1)
        v1 = jnp.where(better, w[j], v1)
    a = jnp.minimum(i0, i1)
    b = jnp.maximum(i0, i1)
    pair = jnp.where(a == 0, b - 1, jnp.where(a == 1, b + 1, N_PAIRS - 1))
    return best * N_PAIRS + pair


def _mixer_kernel(x_ref, w_in_ref, gm_g_ref, gm_b_ref, gm_w_ref, gm_bs_ref,
                  cf_w_ref, cf_b_ref, cf_g_ref, cf_beta_ref, cf_pw_ref,
                  pool_w_ref, pool_scale_ref, sc_w_ref, mixg_ref, w_o_ref,
                  ln_g_ref, ln_b_ref, rw_hi_ref, rw_lo_ref, rb_ref,
                  x1_ref, cls_ref,
                  hbuf0, hbuf1, cbo0, cbo1, pbuf, s2buf, s4buf, s8buf, cbuf, yn_ref):
    s_idx = pl.program_id(1)

    @pl.when(s_idx == 0)
    def _():
        hbuf0[0:HALO, :] = jnp.zeros((HALO, LANES), F32)
        hbuf1[0:HALO, :] = jnp.zeros((HALO, LANES), F32)
        pbuf[0:HALO, :] = jnp.zeros((HALO, D_GROUP), F32)
        cbuf[0:SC_HALO, :] = jnp.zeros((SC_HALO, D_GROUP), F32)

    row = lax.broadcasted_iota(I32, (N_HEADS * CHUNK, CHUNK), 0)
    col = lax.broadcasted_iota(I32, (N_HEADS * CHUNK, CHUNK), 1)
    w_tril = jnp.where(col <= (row & (CHUNK - 1)), gm_w_ref[...], 0.0).astype(BF16)
    lane = lax.broadcasted_iota(I32, (CHUNK, D_GROUP), 1)
    lane_t = lax.broadcasted_iota(I32, (SUB_ROWS, D_GROUP), 1)
    win = jnp.where(lane_t < HEAD_DIM, 2,
                    jnp.where(lane_t < 2 * HEAD_DIM, 4, jnp.where(lane_t < 3 * HEAD_DIM, 8, 16)))
    bs = gm_bs_ref[...]
    mixg = mixg_ref[...]

    for r0 in range(0, TS, SUB_ROWS):
        rows = slice(r0, r0 + SUB_ROWS)
        x = x_ref[0, rows, :]
        xb = x.astype(BF16)

        z_a = _dot(xb, w_in_ref[0, :, 0:2 * D_GROUP])
        u = z_a[:, 0:D_GROUP]
        v = _layer_norm(z_a[:, D_GROUP:2 * D_GROUP], gm_g_ref[...], gm_b_ref[...])
        for n in range(SUB_ROWS // CHUNK):
            crows = slice(n * CHUNK, (n + 1) * CHUNK)
            s_all = _dot(w_tril, v[crows, :].astype(BF16))
            s_sel = s_all[3 * CHUNK:4 * CHUNK]
            for h in (2, 1, 0):
                s_sel = jnp.where(lane < (h + 1) * HEAD_DIM, s_all[h * CHUNK:(h + 1) * CHUNK], s_sel)
            y1 = u[crows, :] * (s_sel + bs)
            yn_ref[r0 + n * CHUNK:r0 + (n + 1) * CHUNK, 0:D_GROUP] = _rms_norm(y1, mixg[:, 0:D_GROUP]).astype(BF16)

        z_b = _dot(xb, w_in_ref[0, :, 2 * D_GROUP:4 * D_GROUP])
        glu = z_b[:, 0:D_GROUP] * _sigmoid(z_b[:, D_GROUP:2 * D_GROUP])
        for half, (hb, co) in enumerate(((hbuf0, cbo0), (hbuf1, cbo1))):
            lanes = slice(half * LANES, (half + 1) * LANES)
            hb[HALO + r0:HALO + r0 + SUB_ROWS, :] = glu[:, lanes]
            bias = cf_b_ref[:, lanes]
            for q in range(SUB_ROWS // (2 * CONV_ROWS)):
                for parity in range(2):
                    out0 = r0 + q * 2 * CONV_ROWS + parity
                    base = out0 + HALO - (CONF_K - 1)
                    acc = jnp.zeros((CONV_ROWS, LANES), F32) + bias
                    for k in range(CONF_K):
                        acc = acc + cf_w_ref[k:k + 1, lanes] * hb[pl.ds(base + k, CONV_ROWS, stride=2), :]
                    co[pl.ds(out0, CONV_ROWS, stride=2), :] = acc
        hln = _layer_norm(jnp.concatenate([cbo0[rows, :], cbo1[rows, :]], axis=1), cf_g_ref[...], cf_beta_ref[...])
        y2 = _dot((hln * _sigmoid(hln)).astype(BF16), cf_pw_ref[0])
        yn_ref[rows, D_GROUP:2 * D_GROUP] = _rms_norm(y2, mixg[:, D_GROUP:2 * D_GROUP]).astype(BF16)

        zc = _dot(xb, w_in_ref[0, :, 4 * D_GROUP:5 * D_GROUP])
        pbuf[HALO + r0:HALO + r0 + SUB_ROWS, :] = zc
        n2 = HALO + SUB_ROWS - 8
        s2buf[r0 + 8:r0 + 8 + n2, :] = pbuf[r0 + 8:r0 + 8 + n2, :] + pbuf[r0 + 7:r0 + 7 + n2, :]
        n4 = HALO + SUB_ROWS - 16
        s4buf[r0 + 16:r0 + 16 + n4, :] = s2buf[r0 + 16:r0 + 16 + n4, :] + s2buf[r0 + 14:r0 + 14 + n4, :]
        n8 = HALO + SUB_ROWS - 24
        s8buf[r0 + 24:r0 + 24 + n8, :] = s4buf[r0 + 24:r0 + 24 + n8, :] + s4buf[r0 + 20:r0 + 20 + n8, :]
        cur = slice(HALO + r0, HALO + r0 + SUB_ROWS)
        s16 = s8buf[cur, :] + s8buf[HALO + r0 - 8:HALO + r0 - 8 + SUB_ROWS, :]
        pos1 = lax.broadcasted_iota(I32, (SUB_ROWS, D_GROUP), 0) + (s_idx * TS + r0 + 1)
        wsum = jnp.where(lane_t < HEAD_DIM, s2buf[cur, :],
                         jnp.where(lane_t < 2 * HEAD_DIM, s4buf[cur, :],
                                   jnp.where(lane_t < 3 * HEAD_DIM, s8buf[cur, :], s16)))
        count = jnp.minimum(pos1, win).astype(F32)
        pooled = wsum / count - zc
        y3 = _dot(pooled.astype(BF16), pool_w_ref[0]) * pool_scale_ref[...]
        yn_ref[rows, 2 * D_GROUP:3 * D_GROUP] = _rms_norm(y3, mixg[:, 2 * D_GROUP:3 * D_GROUP]).astype(BF16)

        z_d = _dot(xb, w_in_ref[0, :, 5 * D_GROUP:8 * D_GROUP])
        cbuf[SC_HALO + r0:SC_HALO + r0 + SUB_ROWS, :] = z_d[:, D_GROUP:2 * D_GROUP] * z_d[:, 2 * D_GROUP:3 * D_GROUP]
        conv = jnp.zeros((SUB_ROWS, D_GROUP), F32)
        for k in range(SHORT_K):
            off = r0 + SC_HALO - (SHORT_K - 1) + k
            conv = conv + sc_w_ref[k:k + 1, :] * cbuf[off:off + SUB_ROWS, :]
        y4 = z_d[:, 0:D_GROUP] * conv
        yn_ref[rows, 3 * D_GROUP:4 * D_GROUP] = _rms_norm(y4, mixg[:, 3 * D_GROUP:4 * D_GROUP]).astype(BF16)

        m = _dot(yn_ref[rows, :], w_o_ref[0])
        x1 = _layer_norm(ALPHA * x + m, ln_g_ref[...], ln_b_ref[...])
        x1_ref[0, rows, :] = x1

        hi = x1.astype(BF16)
        lo = (x1 - hi.astype(F32)).astype(BF16)
        logits = _dot(hi, rw_hi_ref[...]) + _dot(hi, rw_lo_ref[...]) + _dot(lo, rw_hi_ref[...])
        sel_t = (logits + rb_ref[...]).T
        cls = _route_class([sel_t[e:e + 1, :] for e in range(N_EXPERTS)])
        cls_ref[0, r0 // LANES:(r0 + SUB_ROWS) // LANES, :] = jnp.concatenate(
            [cls[:, k * LANES:(k + 1) * LANES] for k in range(SUB_ROWS // LANES)], axis=0)

    hbuf0[0:HALO, :] = hbuf0[TS:TS + HALO, :]
    hbuf1[0:HALO, :] = hbuf1[TS:TS + HALO, :]
    pbuf[0:HALO, :] = pbuf[TS:TS + HALO, :]
    cbuf[0:SC_HALO, :] = cbuf[TS:TS + SC_HALO, :]


def _plan_kernel(cls_ref, dest_ref, meta_ref):
    n = LANES
    cls = cls_ref[...]
    r = lax.broadcasted_iota(I32, (n, n), 0)
    c = lax.broadcasted_iota(I32, (n, n), 1)
    upper = jnp.where(r < c, 1.0, 0.0).astype(BF16)
    lower = jnp.where(c < r, 1.0, 0.0).astype(BF16)
    ones = jnp.ones((n, n), BF16)
    masks = [cls == k for k in range(N_CLASSES)]
    m_all = jnp.concatenate([jnp.where(mk, 1.0, 0.0) for mk in masks], axis=0).astype(BF16)
    within = _dot(m_all, upper)
    rowsum = _dot(m_all, ones)
    rs = jnp.zeros((n, n), F32)
    for k in range(N_CLASSES):
        rs = jnp.where(c == k, rowsum[k * n:(k + 1) * n], rs)
    rs_b = rs.astype(BF16)
    before = _dot(lower, rs_b)
    total = _dot(ones, rs_b)
    ntile = jnp.floor((total + (TMS - 1)) * (1.0 / TMS))
    tstart = _dot(ntile.astype(BF16), upper)
    base = before + tstart * TMS
    dest = jnp.zeros((n, n), F32)
    for k in range(N_CLASSES):
        dest = jnp.where(masks[k], within[k * n:(k + 1) * n] + base[:, k:k + 1], dest)
    dest_ref[...] = dest.astype(I32)

    tend_t = (tstart + ntile).T
    ntile_t = ntile.T
    tile = c.astype(F32)
    ended = jnp.where((r < N_CLASSES) & (tend_t <= tile), 1.0, 0.0)
    is_last = jnp.where((r < N_CLASSES) & (ntile_t > 0) & (tend_t - 1.0 == tile), 1.0, 0.0)
    tcls = jnp.minimum(jnp.sum(ended, axis=0, keepdims=True), N_CLASSES - 1.0)
    grp = (jnp.where(tcls >= N_PAIRS, 1.0, 0.0) + jnp.where(tcls >= 2 * N_PAIRS, 1.0, 0.0)
           + jnp.where(tcls >= 3 * N_PAIRS, 1.0, 0.0))
    pair = tcls - N_PAIRS * grp
    pa = jnp.where(pair >= 3, 1.0, 0.0) + jnp.where(pair >= 5, 1.0, 0.0)
    pb = jnp.where(pair == 0, 1.0, jnp.where((pair == 1) | (pair == 3), 2.0, 3.0))
    n_tiles = tstart[0:1, N_CLASSES:N_CLASSES + 1]
    tile_row = tile[0:1, :]
    active = jnp.where(tile_row < n_tiles, 1.0, 0.0)
    ragged = jnp.where((jnp.sum(is_last, axis=0, keepdims=True) > 0) | (tile_row >= n_tiles), 1.0, 0.0)
    meta = jnp.concatenate(
        [GROUP_SIZE * grp + pa, GROUP_SIZE * grp + pb, active, ragged, jnp.zeros((4, n), F32)], axis=0)
    meta_ref[...] = meta.astype(I32)


def _row_copy_wait(src, dst, sem):
    pltpu.make_async_copy(src, dst, sem).wait()


def _dispatch_kernel(dest_ref, ragged_ref, x_ref, xs_ref, buf, zbuf, sems, zsem):
    i = pl.program_id(0)
    n_steps = pl.num_programs(0)
    tile_rows = TMS * ROW_TILES
    n_tiles = xs_ref.shape[0] // tile_rows

    @pl.when(i == 0)
    def _():
        zbuf[...] = jnp.zeros(zbuf.shape, F32)
        for k in range(n_tiles):
            @pl.when(ragged_ref[k] == 1)
            def _():
                pltpu.make_async_copy(zbuf, xs_ref.at[pl.ds(k * tile_rows, tile_rows), :], zsem).start()
        for k in range(n_tiles):
            @pl.when(ragged_ref[k] == 1)
            def _():
                _row_copy_wait(zbuf, xs_ref.at[pl.ds(k * tile_rows, tile_rows), :], zsem)

    for s in range(2):
        @pl.when(i > 0)
        def _():
            _row_copy_wait(buf.at[s], xs_ref.at[pl.ds(0, TD * ROW_TILES), :], sems.at[s])

        for j in range(ROW_TILES):
            buf[s, pl.ds(j, TD, stride=ROW_TILES), :] = x_ref[s * TD:(s + 1) * TD, j * LANES:(j + 1) * LANES]
        for t in range(TD):
            d = pl.multiple_of(dest_ref[(i * 2 + s) * TD + t] * ROW_TILES, ROW_TILES)
            pltpu.make_async_copy(buf.at[s, pl.ds(t * ROW_TILES, ROW_TILES), :],
                                  xs_ref.at[pl.ds(d, ROW_TILES), :], sems.at[s]).start(priority=t % 2)

    @pl.when(i == n_steps - 1)
    def _():
        for s in range(2):
            _row_copy_wait(buf.at[s], xs_ref.at[pl.ds(0, TD * ROW_TILES), :], sems.at[s])


def _combine_kernel(dest_ref, ys_ref, x_ref, ln_g_ref, ln_b_ref, out_ref, buf, sems):
    i = pl.program_id(0)
    n_steps = pl.num_programs(0)

    def issue(step, s):
        for t in range(TD):
            d = pl.multiple_of(dest_ref[(step * 2 + s) * TD + t] * ROW_TILES, ROW_TILES)
            pltpu.make_async_copy(ys_ref.at[pl.ds(d, ROW_TILES), :],
                                  buf.at[s, pl.ds(t * ROW_TILES, ROW_TILES), :], sems.at[s]).start(priority=t % 2)

    @pl.when(i == 0)
    def _():
        for s in range(2):
            issue(0, s)

    for s in range(2):
        _row_copy_wait(ys_ref.at[pl.ds(0, TD * ROW_TILES), :], buf.at[s], sems.at[s])
        f = jnp.concatenate([buf[s, pl.ds(j, TD, stride=ROW_TILES), :] for j in range(ROW_TILES)], axis=1)
        rows = slice(s * TD, (s + 1) * TD)
        out_ref[rows, :] = _layer_norm(ALPHA * x_ref[rows, :] + f, ln_g_ref[...], ln_b_ref[...])

        @pl.when(i + 1 < n_steps)
        def _():
            issue(i + 1, s)


def _moe_kernel(ea_ref, eb_ref, act_ref, xs_ref, *refs):
    i = pl.program_id(0)
    w_refs, rwt_ref, ys_ref = refs[:-2], refs[-2], refs[-1]
    rows = TMS * ROW_TILES

    @pl.when(act_ref[i * TILES_PER_STEP] == 0)
    def _():
        ys_ref[...] = jnp.zeros(ys_ref.shape, F32)

    @pl.when(act_ref[i * TILES_PER_STEP] == 1)
    def _():
        for s in range(TILES_PER_STEP):
            tile = i * TILES_PER_STEP + s
            wga_ref, wua_ref, wda_ref, wgb_ref, wub_ref, wdb_ref = w_refs[6 * s:6 * s + 6]
            x = jnp.concatenate(
                [xs_ref[pl.ds(s * rows + j, TMS, stride=ROW_TILES), :] for j in range(ROW_TILES)], axis=1)
            la = jnp.sum(x * rwt_ref[pl.ds(ea_ref[tile], 1), :], axis=-1, keepdims=True)
            lb = jnp.sum(x * rwt_ref[pl.ds(eb_ref[tile], 1), :], axis=-1, keepdims=True)
            m = jnp.maximum(la, lb)
            pa = jnp.exp(la - m)
            pb = jnp.exp(lb - m)
            den = pa + pb
            xb = x.astype(BF16)

            def ffn(wg_ref, wu_ref, wd_ref):
                g = _dot(xb, wg_ref[0, 0])
                u = _dot(xb, wu_ref[0, 0])
                h = (g * _sigmoid(g)) * u
                return _dot(h.astype(BF16), wd_ref[0, 0])

            f = (pa / den) * ffn(wga_ref, wua_ref, wda_ref) + (pb / den) * ffn(wgb_ref, wub_ref, wdb_ref)
            for j in range(ROW_TILES):
                ys_ref[pl.ds(s * rows + j, TMS, stride=ROW_TILES), :] = f[:, j * LANES:(j + 1) * LANES]


def _full(shape):
    return pl.BlockSpec(shape, lambda *_: (0,) * len(shape))


def _layer_block(shape, layer):
    return pl.BlockSpec((1,) + shape, lambda *_: (layer,) + (0,) * len(shape))


def _mixer_call(x, p, layer):
    batch, seq, _ = x.shape
    steps = seq // TS
    in_specs = [
        pl.BlockSpec((1, TS, D_MODEL), lambda b, s: (b, s, 0)),
        _layer_block((D_MODEL, 8 * D_GROUP), layer),
        _full((1, D_GROUP)), _full((1, D_GROUP)),
        _full((N_HEADS * CHUNK, CHUNK)), _full((CHUNK, D_GROUP)),
        _full((HALO, D_GROUP)), _full((1, D_GROUP)), _full((1, D_GROUP)), _full((1, D_GROUP)),
        _layer_block((D_GROUP, D_GROUP), layer),
        _layer_block((D_GROUP, D_GROUP), layer), _full((1, D_GROUP)),
        _full((SUBLANES, D_GROUP)),
        _full((1, D_MODEL)), _layer_block((D_MODEL, D_MODEL), layer),
        _full((1, D_MODEL)), _full((1, D_MODEL)),
        _full((D_MODEL, LANES)), _full((D_MODEL, LANES)), _full((1, LANES)),
    ]
    out_specs = [
        pl.BlockSpec((1, TS, D_MODEL), lambda b, s: (b, s, 0)),
        pl.BlockSpec((1, TS // LANES, LANES), lambda b, s: (b * steps + s, 0, 0)),
    ]
    scratch = [
        pltpu.VMEM((HALO + TS, LANES), F32),
        pltpu.VMEM((HALO + TS, LANES), F32),
        pltpu.VMEM((TS, LANES), F32),
        pltpu.VMEM((TS, LANES), F32),
        pltpu.VMEM((HALO + TS, D_GROUP), F32),
        pltpu.VMEM((HALO + TS, D_GROUP), F32),
        pltpu.VMEM((HALO + TS, D_GROUP), F32),
        pltpu.VMEM((HALO + TS, D_GROUP), F32),
        pltpu.VMEM((SC_HALO + TS, D_GROUP), F32),
        pltpu.VMEM((TS, D_MODEL), BF16),
    ]
    return pl.pallas_call(
        _mixer_kernel,
        grid=(batch, steps),
        in_specs=in_specs,
        out_specs=out_specs,
        out_shape=[jax.ShapeDtypeStruct((batch, seq, D_MODEL), F32),
                   jax.ShapeDtypeStruct((batch * steps, TS // LANES, LANES), I32)],
        scratch_shapes=scratch,
        compiler_params=pltpu.CompilerParams(
            dimension_semantics=("arbitrary", "arbitrary"), vmem_limit_bytes=VMEM_LIMIT),
        name="mixer",
    )(x, *p)


def _plan_call(cls2d):
    return pl.pallas_call(
        _plan_kernel,
        out_shape=[jax.ShapeDtypeStruct((LANES, LANES), I32), jax.ShapeDtypeStruct((SUBLANES, LANES), I32)],
        compiler_params=pltpu.CompilerParams(vmem_limit_bytes=VMEM_LIMIT),
        name="moe_plan",
    )(cls2d)


def _dispatch_call(dest, ragged, x2d, n_sorted):
    n_tok = x2d.shape[0]
    return pl.pallas_call(
        _dispatch_kernel,
        grid_spec=pltpu.PrefetchScalarGridSpec(
            num_scalar_prefetch=2,
            grid=(n_tok // (2 * TD),),
            in_specs=[pl.BlockSpec((2 * TD, D_MODEL), lambda i, *_: (i, 0))],
            out_specs=pl.BlockSpec(memory_space=pl.ANY),
            scratch_shapes=[
                pltpu.VMEM((2, TD * ROW_TILES, LANES), F32),
                pltpu.VMEM((TMS * ROW_TILES, LANES), F32),
                pltpu.SemaphoreType.DMA((2,)),
                pltpu.SemaphoreType.DMA(()),
            ]),
        out_shape=jax.ShapeDtypeStruct((n_sorted * ROW_TILES, LANES), F32),
        compiler_params=pltpu.CompilerParams(
            dimension_semantics=("arbitrary",), vmem_limit_bytes=VMEM_LIMIT),
        name="moe_dispatch",
    )(dest, ragged, x2d)


def _combine_call(dest, ys, x2d, ln_g, ln_b):
    n_tok = x2d.shape[0]
    return pl.pallas_call(
        _combine_kernel,
        grid_spec=pltpu.PrefetchScalarGridSpec(
            num_scalar_prefetch=1,
            grid=(n_tok // (2 * TD),),
            in_specs=[pl.BlockSpec(memory_space=pl.ANY),
                      pl.BlockSpec((2 * TD, D_MODEL), lambda i, *_: (i, 0)),
                      _full((1, D_MODEL)), _full((1, D_MODEL))],
            out_specs=pl.BlockSpec((2 * TD, D_MODEL), lambda i, *_: (i, 0)),
            scratch_shapes=[
                pltpu.VMEM((2, TD * ROW_TILES, LANES), F32),
                pltpu.SemaphoreType.DMA((2,)),
            ]),
        out_shape=jax.ShapeDtypeStruct((n_tok, D_MODEL), F32),
        compiler_params=pltpu.CompilerParams(
            dimension_semantics=("arbitrary",), vmem_limit_bytes=VMEM_LIMIT),
        name="moe_combine",
    )(dest, ys, x2d, ln_g, ln_b)


def _sc_gather_rows(src3, idx):
    n_out = idx.shape[0]
    per_worker = n_out // SC_WORKERS
    mesh = plsc.VectorSubcoreMesh(core_axis_name="c", subcore_axis_name="s",
                                  num_cores=SC_CORES, num_subcores=SC_WORKERS // SC_CORES)

    @functools.partial(
        pl.kernel, mesh=mesh,
        out_type=jax.ShapeDtypeStruct((n_out, ROW_TILES, LANES), F32),
        scratch_types=[pltpu.VMEM((per_worker,), I32),
                       pltpu.VMEM((SC_CHUNK, ROW_TILES, LANES), F32),
                       pltpu.SemaphoreType.DMA],
        name="sc_gather_rows")
    def gather(src_hbm, idx_hbm, out_hbm, idx_v, rows_v, sem):
        wid = lax.axis_index("s") * SC_CORES + lax.axis_index("c")
        base = wid * per_worker
        pltpu.sync_copy(idx_hbm.at[pl.ds(base, per_worker)], idx_v)

        @pl.loop(0, per_worker // SC_CHUNK)
        def _(ch):
            off = pl.multiple_of(ch * SC_CHUNK, SC_CHUNK)
            pltpu.async_copy(src_hbm.at[idx_v.at[pl.ds(off, SC_CHUNK)]], rows_v, sem).wait()
            pltpu.sync_copy(rows_v, out_hbm.at[pl.ds(base + off, SC_CHUNK)])

    return gather(src3, idx)


def _post_norm_kernel(f_ref, x_ref, ln_g_ref, ln_b_ref, out_ref):
    rows = x_ref.shape[0]
    f = jnp.concatenate([f_ref[pl.ds(j, rows, stride=ROW_TILES), :] for j in range(ROW_TILES)], axis=1)
    out_ref[...] = _layer_norm(ALPHA * x_ref[...] + f, ln_g_ref[...], ln_b_ref[...])


def _post_norm_call(f_rows, x2d, ln_g, ln_b):
    n_tok = x2d.shape[0]
    return pl.pallas_call(
        _post_norm_kernel,
        grid=(n_tok // (2 * TD),),
        in_specs=[pl.BlockSpec((2 * TD * ROW_TILES, LANES), lambda i: (i, 0)),
                  pl.BlockSpec((2 * TD, D_MODEL), lambda i: (i, 0)),
                  _full((1, D_MODEL)), _full((1, D_MODEL))],
        out_specs=pl.BlockSpec((2 * TD, D_MODEL), lambda i: (i, 0)),
        out_shape=jax.ShapeDtypeStruct((n_tok, D_MODEL), F32),
        compiler_params=pltpu.CompilerParams(
            dimension_semantics=("arbitrary",), vmem_limit_bytes=VMEM_LIMIT),
        name="post_norm",
    )(f_rows, x2d, ln_g, ln_b)


def _moe_call(ea, eb, act, xs, wg, wu, wd, rwt, n_tiles, layer):
    rows = TILES_PER_STEP * TMS * ROW_TILES
    w_specs, w_args = [], []
    for s in range(TILES_PER_STEP):
        for sel in (0, 1):
            def expert(i, ea, eb, act, s=s, sel=sel):
                return (layer, (ea, eb)[sel][i * TILES_PER_STEP + s], 0, 0)
            w_specs += [pl.BlockSpec((1, 1, D_MODEL, D_EXPERT), expert),
                        pl.BlockSpec((1, 1, D_MODEL, D_EXPERT), expert),
                        pl.BlockSpec((1, 1, D_EXPERT, D_MODEL), expert)]
            w_args += [wg, wu, wd]
    tile = pl.BlockSpec((rows, LANES), lambda i, ea, eb, act: (i, 0))
    return pl.pallas_call(
        _moe_kernel,
        grid_spec=pltpu.PrefetchScalarGridSpec(
            num_scalar_prefetch=3,
            grid=(n_tiles // TILES_PER_STEP,),
            in_specs=[tile] + w_specs + [_full((N_EXPERTS, D_MODEL))],
            out_specs=tile),
        out_shape=jax.ShapeDtypeStruct(xs.shape, F32),
        compiler_params=pltpu.CompilerParams(
            dimension_semantics=("arbitrary",), vmem_limit_bytes=VMEM_LIMIT),
        name="moe_experts",
    )(ea, eb, act, xs, *w_args, rwt)


def _block_diag(w):
    g, d, _ = w.shape
    eye = jnp.eye(g, dtype=w.dtype)
    return (eye[:, None, :, None] * w[:, :, None, :]).reshape(g * d, g * d)


def _pad_rows(w, rows):
    return jnp.pad(w, ((0, rows - w.shape[0]), (0, 0)))


def kernel(x, w_in, gm_ln_g, gm_ln_b, gm_w_s, gm_b_s, cf_dw_w, cf_dw_b, cf_ln_g, cf_ln_b, cf_pw,
           pool_w, pool_scale, sc_w, mix_norm_g, w_o, ln1_g, ln1_b, router_w, router_b,
           exp_w_gate, exp_w_up, exp_w_down, ln2_g, ln2_b):
    batch, seq, _ = x.shape
    n_tok = batch * seq
    assert n_tok == LANES * LANES, "the routing plan lays tokens out as one (128, 128) tile grid"
    assert seq % TS == 0 and TS % SUB_ROWS == 0 and SUB_ROWS % (2 * CONV_ROWS) == 0 and n_tok % (2 * TD) == 0
    n_tiles = n_tok // TMS + N_CLASSES
    assert n_tiles <= LANES and n_tiles % TILES_PER_STEP == 0
    row = lambda a: a.reshape(1, -1)
    rw = jnp.pad(router_w, ((0, 0), (0, LANES - N_EXPERTS)))
    rw_hi = rw.astype(BF16)
    rw_lo = (rw - rw_hi.astype(F32)).astype(BF16)
    rb = jnp.pad(router_b, (0, LANES - N_EXPERTS)).reshape(1, LANES)
    rwt = router_w.T
    w_in_b, w_o_b, cf_pw_b = w_in.astype(BF16), w_o.astype(BF16), cf_pw.astype(BF16)
    pool_b = jax.vmap(_block_diag)(pool_w).astype(BF16)
    wg_b, wu_b, wd_b = exp_w_gate.astype(BF16), exp_w_up.astype(BF16), exp_w_down.astype(BF16)
    for l in range(DEPTH):
        params = (
            w_in_b,
            row(gm_ln_g[l]), row(gm_ln_b[l]),
            gm_w_s[l].reshape(N_HEADS * CHUNK, CHUNK),
            jnp.repeat(gm_b_s[l].T, HEAD_DIM, axis=1),
            _pad_rows(cf_dw_w[l], HALO), row(cf_dw_b[l]), row(cf_ln_g[l]), row(cf_ln_b[l]),
            cf_pw_b,
            pool_b, row(pool_scale[l]),
            _pad_rows(sc_w[l], SUBLANES),
            row(mix_norm_g[l]), w_o_b,
            row(ln1_g[l]), row(ln1_b[l]),
            rw_hi, rw_lo, rb,
        )
        x1, cls = _mixer_call(x, params, l)
        x1 = x1.reshape(n_tok, D_MODEL)
        dest2d, meta = _plan_call(cls.reshape(LANES, LANES))
        dest = dest2d.reshape(n_tok)
        xs = _dispatch_call(dest, meta[3], x1, n_tiles * TMS)
        ys = _moe_call(meta[0], meta[1], meta[2], xs, wg_b, wu_b, wd_b, rwt, n_tiles, l)
        f_tok = _sc_gather_rows(ys.reshape(n_tiles * TMS, ROW_TILES, LANES), dest)
        x = _post_norm_call(f_tok.reshape(n_tok * ROW_TILES, LANES), x1, row(ln2_g[l]), row(ln2_b[l]))
        x = x.reshape(batch, seq, D_MODEL)
    return x
```

```python
import jax
import jax.numpy as jnp
from jax import lax
from jax.experimental import pallas as pl
from jax.experimental.pallas import tpu as pltpu

D_MODEL = 1024
DEPTH = 2
D_GROUP = 256
N_HEADS = 4
HEAD_DIM = 64
CHUNK = 128
CONF_K = 31
SHORT_K = 3
N_EXPERTS = 16
N_GROUPS = 4
GROUP_SIZE = 4
N_PAIRS = 6
N_CLASSES = N_GROUPS * N_PAIRS
D_EXPERT = 512
ALPHA = (2 * DEPTH) ** 0.25
LN_EPS = 1e-5
RMS_EPS = 1e-6

LANES = 128
SUBLANES = 8
ROW_TILES = D_MODEL // LANES
HALO = 32
SC_HALO = SUBLANES
TS = 512
SUB_ROWS = 256
CONV_ROWS = 64
TMS = 256
TILES_PER_STEP = 2
TD = 256
VMEM_LIMIT = 56 * 1024 * 1024

BF16 = jnp.bfloat16
F32 = jnp.float32
I32 = jnp.int32


def _dot(a, b):
    return jnp.dot(a, b, preferred_element_type=F32)


def _layer_norm(x, g, b):
    mu = jnp.mean(x, axis=-1, keepdims=True)
    xc = x - mu
    var = jnp.mean(xc * xc, axis=-1, keepdims=True)
    return xc * lax.rsqrt(var + LN_EPS) * g + b


def _rms_norm(y, g):
    ms = jnp.mean(y * y, axis=-1, keepdims=True)
    return y * lax.rsqrt(ms + RMS_EPS) * g


def _sigmoid(x):
    return 1.0 / (1.0 + jnp.exp(-x))


def _route_class(sel):
    scores = []
    for g in range(N_GROUPS):
        v = sel[g * GROUP_SIZE:(g + 1) * GROUP_SIZE]
        best_pair = None
        for i in range(GROUP_SIZE):
            for j in range(i + 1, GROUP_SIZE):
                p = v[i] + v[j]
                best_pair = p if best_pair is None else jnp.maximum(best_pair, p)
        scores.append(best_pair)
    best = jnp.zeros(scores[0].shape, I32)
    best_score = scores[0]
    for g in range(1, N_GROUPS):
        better = scores[g] > best_score
        best = jnp.where(better, g, best)
        best_score = jnp.where(better, scores[g], best_score)
    v = []
    for j in range(GROUP_SIZE):
        out = sel[j]
        for g in range(1, N_GROUPS):
            out = jnp.where(best == g, sel[g * GROUP_SIZE + j], out)
        v.append(out)
    i0 = jnp.zeros_like(best)
    v0 = v[0]
    for j in range(1, GROUP_SIZE):
        better = v[j] > v0
        i0 = jnp.where(better, j, i0)
        v0 = jnp.where(better, v[j], v0)
    neg = jnp.full_like(v0, -jnp.inf)
    w = [jnp.where(i0 == j, neg, v[j]) for j in range(GROUP_SIZE)]
    i1 = jnp.zeros_like(best)
    v1 = w[0]
    for j in range(1, GROUP_SIZE):
        better = w[j] > v1
        i1 = jnp.where(better, j, i1)
        v1 = jnp.where(better, w[j], v1)
    a = jnp.minimum(i0, i1)
    b = jnp.maximum(i0, i1)
    pair = jnp.where(a == 0, b - 1, jnp.where(a == 1, b + 1, N_PAIRS - 1))
    return best * N_PAIRS + pair


def _mixer_kernel(x_ref, w_in_ref, gm_g_ref, gm_b_ref, gm_w_ref, gm_bs_ref,
                  cf_w_ref, cf_b_ref, cf_g_ref, cf_beta_ref, cf_pw_ref,
                  pool_w_ref, pool_scale_ref, sc_w_ref, mixg_ref, w_o_ref,
                  ln_g_ref, ln_b_ref, rw_hi_ref, rw_lo_ref, rb_ref,
                  x1_ref, cls_ref, *scratch):
    s_idx = pl.program_id(1)

    row = lax.broadcasted_iota(I32, (N_HEADS * CHUNK, CHUNK), 0)
    col = lax.broadcasted_iota(I32, (N_HEADS * CHUNK, CHUNK), 1)
    w_tril = jnp.where(col <= (row & (CHUNK - 1)), gm_w_ref[...], 0.0).astype(BF16)
    lane = lax.broadcasted_iota(I32, (CHUNK, D_GROUP), 1)
    lane_t = lax.broadcasted_iota(I32, (SUB_ROWS, D_GROUP), 1)
    win = jnp.where(lane_t < HEAD_DIM, 2,
                    jnp.where(lane_t < 2 * HEAD_DIM, 4, jnp.where(lane_t < 3 * HEAD_DIM, 8, 16)))
    bs = gm_bs_ref[...]
    mixg = mixg_ref[...]

    n_blocks = TS // SUB_ROWS
    per_block = len(scratch) // n_blocks
    blocks = [scratch[i * per_block:(i + 1) * per_block] for i in range(n_blocks)]

    @pl.when(s_idx == 0)
    def _():
        hbuf0, hbuf1, _, _, pbuf, _, _, _, cbuf, _ = blocks[0]
        hbuf0[0:HALO, :] = jnp.zeros((HALO, LANES), F32)
        hbuf1[0:HALO, :] = jnp.zeros((HALO, LANES), F32)
        pbuf[0:HALO, :] = jnp.zeros((HALO, D_GROUP), F32)
        cbuf[0:SC_HALO, :] = jnp.zeros((SC_HALO, D_GROUP), F32)

    def carry_halo(src, dst, which):
        for i in which:
            rows = SC_HALO if i == 8 else HALO
            dst[i][0:rows, :] = src[i][SUB_ROWS:SUB_ROWS + rows, :]

    st = [dict() for _ in range(n_blocks)]

    def load(bi):
        x = x_ref[0, bi * SUB_ROWS:(bi + 1) * SUB_ROWS, :]
        st[bi]["x"] = x
        st[bi]["xb"] = x.astype(BF16)

    def in_proj(bi, lo, hi):
        z = _dot(st[bi]["xb"], w_in_ref[0, :, lo * D_GROUP:hi * D_GROUP])
        for j in range(lo, hi):
            st[bi][j] = z[:, (j - lo) * D_GROUP:(j - lo + 1) * D_GROUP]

    def gating_mlp(bi):
        yn_ref = blocks[bi][9]
        u = st[bi].pop(0)
        v = _layer_norm(st[bi].pop(1), gm_g_ref[...], gm_b_ref[...])
        for n in range(SUB_ROWS // CHUNK):
            crows = slice(n * CHUNK, (n + 1) * CHUNK)
            s_all = _dot(w_tril, v[crows, :].astype(BF16))
            s_sel = s_all[3 * CHUNK:4 * CHUNK]
            for h in (2, 1, 0):
                s_sel = jnp.where(lane < (h + 1) * HEAD_DIM, s_all[h * CHUNK:(h + 1) * CHUNK], s_sel)
            y1 = u[crows, :] * (s_sel + bs)
            yn_ref[crows, 0:D_GROUP] = _rms_norm(y1, mixg[:, 0:D_GROUP]).astype(BF16)

    def conformer_glu(bi):
        if bi > 0:
            carry_halo(blocks[bi - 1], blocks[bi], (0, 1))
        glu = st[bi].pop(2) * _sigmoid(st[bi].pop(3))
        for half in range(2):
            blocks[bi][half][HALO:HALO + SUB_ROWS, :] = glu[:, half * LANES:(half + 1) * LANES]

    def conformer_conv(bi, half):
        hb, co = blocks[bi][half], blocks[bi][2 + half]
        lanes = slice(half * LANES, (half + 1) * LANES)
        bias = cf_b_ref[:, lanes]
        for q in range(SUB_ROWS // (2 * CONV_ROWS)):
            for parity in range(2):
                out0 = q * 2 * CONV_ROWS + parity
                base = out0 + HALO - (CONF_K - 1)
                acc = jnp.zeros((CONV_ROWS, LANES), F32) + bias
                for k in range(CONF_K):
                    acc = acc + cf_w_ref[k:k + 1, lanes] * hb[pl.ds(base + k, CONV_ROWS, stride=2), :]
                co[pl.ds(out0, CONV_ROWS, stride=2), :] = acc

    def conformer_out(bi):
        cbo0, cbo1, yn_ref = blocks[bi][2], blocks[bi][3], blocks[bi][9]
        hln = _layer_norm(jnp.concatenate([cbo0[...], cbo1[...]], axis=1), cf_g_ref[...], cf_beta_ref[...])
        y2 = _dot((hln * _sigmoid(hln)).astype(BF16), cf_pw_ref[0])
        yn_ref[:, D_GROUP:2 * D_GROUP] = _rms_norm(y2, mixg[:, D_GROUP:2 * D_GROUP]).astype(BF16)

    def pooling(bi):
        pbuf, s2buf, s4buf, s8buf = blocks[bi][4:8]
        yn_ref = blocks[bi][9]
        if bi > 0:
            carry_halo(blocks[bi - 1], blocks[bi], (4,))
        zc = st[bi].pop(4)
        pbuf[HALO:HALO + SUB_ROWS, :] = zc
        n2 = HALO + SUB_ROWS - 8
        s2buf[8:8 + n2, :] = pbuf[8:8 + n2, :] + pbuf[7:7 + n2, :]
        n4 = HALO + SUB_ROWS - 16
        s4buf[16:16 + n4, :] = s2buf[16:16 + n4, :] + s2buf[14:14 + n4, :]
        n8 = HALO + SUB_ROWS - 24
        s8buf[24:24 + n8, :] = s4buf[24:24 + n8, :] + s4buf[20:20 + n8, :]
        cur = slice(HALO, HALO + SUB_ROWS)
        s16 = s8buf[cur, :] + s8buf[HALO - 8:HALO - 8 + SUB_ROWS, :]
        pos1 = lax.broadcasted_iota(I32, (SUB_ROWS, D_GROUP), 0) + (s_idx * TS + bi * SUB_ROWS + 1)
        wsum = jnp.where(lane_t < HEAD_DIM, s2buf[cur, :],
                         jnp.where(lane_t < 2 * HEAD_DIM, s4buf[cur, :],
                                   jnp.where(lane_t < 3 * HEAD_DIM, s8buf[cur, :], s16)))
        count = jnp.minimum(pos1, win).astype(F32)
        pooled = wsum / count - zc
        y3 = _dot(pooled.astype(BF16), pool_w_ref[0]) * pool_scale_ref[...]
        yn_ref[:, 2 * D_GROUP:3 * D_GROUP] = _rms_norm(y3, mixg[:, 2 * D_GROUP:3 * D_GROUP]).astype(BF16)

    def short_conv(bi):
        cbuf, yn_ref = blocks[bi][8], blocks[bi][9]
        if bi > 0:
            carry_halo(blocks[bi - 1], blocks[bi], (8,))
        cbuf[SC_HALO:SC_HALO + SUB_ROWS, :] = st[bi].pop(6) * st[bi].pop(7)
        conv = jnp.zeros((SUB_ROWS, D_GROUP), F32)
        for k in range(SHORT_K):
            off = SC_HALO - (SHORT_K - 1) + k
            conv = conv + sc_w_ref[k:k + 1, :] * cbuf[off:off + SUB_ROWS, :]
        y4 = st[bi].pop(5) * conv
        yn_ref[:, 3 * D_GROUP:4 * D_GROUP] = _rms_norm(y4, mixg[:, 3 * D_GROUP:4 * D_GROUP]).astype(BF16)

    def out_proj(bi):
        m = _dot(blocks[bi][9][...], w_o_ref[0])
        x1 = _layer_norm(ALPHA * st[bi].pop("x") + m, ln_g_ref[...], ln_b_ref[...])
        x1_ref[0, bi * SUB_ROWS:(bi + 1) * SUB_ROWS, :] = x1
        st[bi]["x1"] = x1

    def router(bi):
        x1 = st[bi].pop("x1")
        hi = x1.astype(BF16)
        lo = (x1 - hi.astype(F32)).astype(BF16)
        logits = _dot(hi, rw_hi_ref[...]) + _dot(hi, rw_lo_ref[...]) + _dot(lo, rw_hi_ref[...])
        st[bi]["sel"] = logits + rb_ref[...]

    def route(bi):
        sel_t = st[bi].pop("sel").T
        cls = _route_class([sel_t[e:e + 1, :] for e in range(N_EXPERTS)])
        r0 = bi * SUB_ROWS
        cls_ref[0, r0 // LANES:(r0 + SUB_ROWS) // LANES, :] = jnp.concatenate(
            [cls[:, k * LANES:(k + 1) * LANES] for k in range(SUB_ROWS // LANES)], axis=0)

    def conformer(b):
        conformer_glu(b)
        conformer_conv(b, 0)
        conformer_conv(b, 1)
        conformer_out(b)

    def stage1(b):
        return [lambda: load(b), lambda: in_proj(b, 0, 2), lambda: in_proj(b, 2, 4), lambda: in_proj(b, 4, 5),
                lambda: in_proj(b, 5, 8)]

    def stage2(b):
        return [lambda: gating_mlp(b), lambda: conformer(b), lambda: pooling(b), lambda: short_conv(b)]

    def stage3(b):
        return [lambda: out_proj(b), lambda: router(b), lambda: route(b)]

    for t in range(n_blocks + 2):
        stages = [stage(t - lag) for lag, stage in ((2, stage3), (0, stage1), (1, stage2)) if 0 <= t - lag < n_blocks]
        for k in range(max(len(stage) for stage in stages)):
            for stage in stages:
                if k < len(stage):
                    stage[k]()

    carry_halo(blocks[-1], blocks[0], (0, 1, 4, 8))


def _plan_kernel(cls_ref, dest_ref, meta_ref):
    n = LANES
    cls = cls_ref[...]
    r = lax.broadcasted_iota(I32, (n, n), 0)
    c = lax.broadcasted_iota(I32, (n, n), 1)
    upper = jnp.where(r < c, 1.0, 0.0).astype(BF16)
    lower = jnp.where(c < r, 1.0, 0.0).astype(BF16)
    ones = jnp.ones((n, n), BF16)
    masks = [cls == k for k in range(N_CLASSES)]
    m_all = jnp.concatenate([jnp.where(mk, 1.0, 0.0) for mk in masks], axis=0).astype(BF16)
    within = _dot(m_all, upper)
    rowsum = _dot(m_all, ones)
    rs = jnp.zeros((n, n), F32)
    for k in range(N_CLASSES):
        rs = jnp.where(c == k, rowsum[k * n:(k + 1) * n], rs)
    rs_b = rs.astype(BF16)
    before = _dot(lower, rs_b)
    total = _dot(ones, rs_b)
    ntile = jnp.floor((total + (TMS - 1)) * (1.0 / TMS))
    tstart = _dot(ntile.astype(BF16), upper)
    base = before + tstart * TMS
    dest = jnp.zeros((n, n), F32)
    for k in range(N_CLASSES):
        dest = jnp.where(masks[k], within[k * n:(k + 1) * n] + base[:, k:k + 1], dest)
    dest_ref[...] = dest.astype(I32)

    tend_t = (tstart + ntile).T
    ntile_t = ntile.T
    tile = c.astype(F32)
    ended = jnp.where((r < N_CLASSES) & (tend_t <= tile), 1.0, 0.0)
    is_last = jnp.where((r < N_CLASSES) & (ntile_t > 0) & (tend_t - 1.0 == tile), 1.0, 0.0)
    tcls = jnp.minimum(jnp.sum(ended, axis=0, keepdims=True), N_CLASSES - 1.0)
    grp = (jnp.where(tcls >= N_PAIRS, 1.0, 0.0) + jnp.where(tcls >= 2 * N_PAIRS, 1.0, 0.0)
           + jnp.where(tcls >= 3 * N_PAIRS, 1.0, 0.0))
    pair = tcls - N_PAIRS * grp
    pa = jnp.where(pair >= 3, 1.0, 0.0) + jnp.where(pair >= 5, 1.0, 0.0)
    pb = jnp.where(pair == 0, 1.0, jnp.where((pair == 1) | (pair == 3), 2.0, 3.0))
    n_tiles = tstart[0:1, N_CLASSES:N_CLASSES + 1]
    tile_row = tile[0:1, :]
    active = jnp.where(tile_row < n_tiles, 1.0, 0.0)
    ragged = jnp.where((jnp.sum(is_last, axis=0, keepdims=True) > 0) | (tile_row >= n_tiles), 1.0, 0.0)
    meta = jnp.concatenate(
        [GROUP_SIZE * grp + pa, GROUP_SIZE * grp + pb, active, ragged, jnp.zeros((4, n), F32)], axis=0)
    meta_ref[...] = meta.astype(I32)


def _row_copy_wait(src, dst, sem):
    pltpu.make_async_copy(src, dst, sem).wait()


def _dispatch_kernel(dest_ref, ragged_ref, x_ref, xs_ref, buf, zbuf, sems, zsem):
    i = pl.program_id(0)
    n_steps = pl.num_programs(0)
    tile_rows = TMS * ROW_TILES
    n_tiles = xs_ref.shape[0] // tile_rows

    @pl.when(i == 0)
    def _():
        zbuf[...] = jnp.zeros(zbuf.shape, F32)
        for k in range(n_tiles):
            @pl.when(ragged_ref[k] == 1)
            def _():
                pltpu.make_async_copy(zbuf, xs_ref.at[pl.ds(k * tile_rows, tile_rows), :], zsem).start()
        for k in range(n_tiles):
            @pl.when(ragged_ref[k] == 1)
            def _():
                _row_copy_wait(zbuf, xs_ref.at[pl.ds(k * tile_rows, tile_rows), :], zsem)

    for s in range(2):
        @pl.when(i > 0)
        def _():
            _row_copy_wait(buf.at[s], xs_ref.at[pl.ds(0, TD * ROW_TILES), :], sems.at[s])

        for j in range(ROW_TILES):
            buf[s, pl.ds(j, TD, stride=ROW_TILES), :] = x_ref[s * TD:(s + 1) * TD, j * LANES:(j + 1) * LANES]
        for t in range(TD):
            d = pl.multiple_of(dest_ref[(i * 2 + s) * TD + t] * ROW_TILES, ROW_TILES)
            pltpu.make_async_copy(buf.at[s, pl.ds(t * ROW_TILES, ROW_TILES), :],
                                  xs_ref.at[pl.ds(d, ROW_TILES), :], sems.at[s]).start(priority=t % 2)

    @pl.when(i == n_steps - 1)
    def _():
        for s in range(2):
            _row_copy_wait(buf.at[s], xs_ref.at[pl.ds(0, TD * ROW_TILES), :], sems.at[s])


def _combine_kernel(dest_ref, ys_ref, x_ref, ln_g_ref, ln_b_ref, out_ref, buf, sems):
    i = pl.program_id(0)
    n_steps = pl.num_programs(0)

    def issue(step, s):
        for t in range(TD):
            d = pl.multiple_of(dest_ref[(step * 2 + s) * TD + t] * ROW_TILES, ROW_TILES)
            pltpu.make_async_copy(ys_ref.at[pl.ds(d, ROW_TILES), :],
                                  buf.at[s, pl.ds(t * ROW_TILES, ROW_TILES), :], sems.at[s]).start(priority=t % 2)

    @pl.when(i == 0)
    def _():
        for s in range(2):
            issue(0, s)

    for s in range(2):
        _row_copy_wait(ys_ref.at[pl.ds(0, TD * ROW_TILES), :], buf.at[s], sems.at[s])
        f = jnp.concatenate([buf[s, pl.ds(j, TD, stride=ROW_TILES), :] for j in range(ROW_TILES)], axis=1)
        rows = slice(s * TD, (s + 1) * TD)
        out_ref[rows, :] = _layer_norm(ALPHA * x_ref[rows, :] + f, ln_g_ref[...], ln_b_ref[...])

        @pl.when(i + 1 < n_steps)
        def _():
            issue(i + 1, s)


def _moe_kernel(ea_ref, eb_ref, act_ref, xs_ref, *refs):
    i = pl.program_id(0)
    w_refs, rwt_ref, ys_ref = refs[:-2], refs[-2], refs[-1]
    rows = TMS * ROW_TILES

    @pl.when(act_ref[i * TILES_PER_STEP] == 0)
    def _():
        ys_ref[...] = jnp.zeros(ys_ref.shape, F32)

    @pl.when(act_ref[i * TILES_PER_STEP] == 1)
    def _():
        for s in range(TILES_PER_STEP):
            tile = i * TILES_PER_STEP + s
            wga_ref, wua_ref, wda_ref, wgb_ref, wub_ref, wdb_ref = w_refs[6 * s:6 * s + 6]
            x = jnp.concatenate(
                [xs_ref[pl.ds(s * rows + j, TMS, stride=ROW_TILES), :] for j in range(ROW_TILES)], axis=1)
            la = jnp.sum(x * rwt_ref[pl.ds(ea_ref[tile], 1), :], axis=-1, keepdims=True)
            lb = jnp.sum(x * rwt_ref[pl.ds(eb_ref[tile], 1), :], axis=-1, keepdims=True)
            m = jnp.maximum(la, lb)
            pa = jnp.exp(la - m)
            pb = jnp.exp(lb - m)
            den = pa + pb
            xb = x.astype(BF16)

            def ffn(wg_ref, wu_ref, wd_ref):
                g = _dot(xb, wg_ref[0, 0])
                u = _dot(xb, wu_ref[0, 0])
                h = (g * _sigmoid(g)) * u
                return _dot(h.astype(BF16), wd_ref[0, 0])

            f = (pa / den) * ffn(wga_ref, wua_ref, wda_ref) + (pb / den) * ffn(wgb_ref, wub_ref, wdb_ref)
            for j in range(ROW_TILES):
                ys_ref[pl.ds(s * rows + j, TMS, stride=ROW_TILES), :] = f[:, j * LANES:(j + 1) * LANES]


def _full(shape):
    return pl.BlockSpec(shape, lambda *_: (0,) * len(shape))


def _layer_block(shape, layer):
    return pl.BlockSpec((1,) + shape, lambda *_: (layer,) + (0,) * len(shape))


def _mixer_call(x, p, layer):
    batch, seq, _ = x.shape
    steps = seq // TS
    in_specs = [
        pl.BlockSpec((1, TS, D_MODEL), lambda b, s: (b, s, 0)),
        _layer_block((D_MODEL, 8 * D_GROUP), layer),
        _full((1, D_GROUP)), _full((1, D_GROUP)),
        _full((N_HEADS * CHUNK, CHUNK)), _full((CHUNK, D_GROUP)),
        _full((HALO, D_GROUP)), _full((1, D_GROUP)), _full((1, D_GROUP)), _full((1, D_GROUP)),
        _layer_block((D_GROUP, D_GROUP), layer),
        _layer_block((D_GROUP, D_GROUP), layer), _full((1, D_GROUP)),
        _full((SUBLANES, D_GROUP)),
        _full((1, D_MODEL)), _layer_block((D_MODEL, D_MODEL), layer),
        _full((1, D_MODEL)), _full((1, D_MODEL)),
        _full((D_MODEL, LANES)), _full((D_MODEL, LANES)), _full((1, LANES)),
    ]
    out_specs = [
        pl.BlockSpec((1, TS, D_MODEL), lambda b, s: (b, s, 0)),
        pl.BlockSpec((1, TS // LANES, LANES), lambda b, s: (b * steps + s, 0, 0)),
    ]
    block_scratch = [
        pltpu.VMEM((HALO + SUB_ROWS, LANES), F32),
        pltpu.VMEM((HALO + SUB_ROWS, LANES), F32),
        pltpu.VMEM((SUB_ROWS, LANES), F32),
        pltpu.VMEM((SUB_ROWS, LANES), F32),
        pltpu.VMEM((HALO + SUB_ROWS, D_GROUP), F32),
        pltpu.VMEM((HALO + SUB_ROWS, D_GROUP), F32),
        pltpu.VMEM((HALO + SUB_ROWS, D_GROUP), F32),
        pltpu.VMEM((HALO + SUB_ROWS, D_GROUP), F32),
        pltpu.VMEM((SC_HALO + SUB_ROWS, D_GROUP), F32),
        pltpu.VMEM((SUB_ROWS, D_MODEL), BF16),
    ]
    scratch = block_scratch * (TS // SUB_ROWS)
    return pl.pallas_call(
        _mixer_kernel,
        grid=(batch, steps),
        in_specs=in_specs,
        out_specs=out_specs,
        out_shape=[jax.ShapeDtypeStruct((batch, seq, D_MODEL), F32),
                   jax.ShapeDtypeStruct((batch * steps, TS // LANES, LANES), I32)],
        scratch_shapes=scratch,
        compiler_params=pltpu.CompilerParams(
            dimension_semantics=("arbitrary", "arbitrary"), vmem_limit_bytes=VMEM_LIMIT),
        name="mixer",
    )(x, *p)


def _plan_call(cls2d):
    return pl.pallas_call(
        _plan_kernel,
        out_shape=[jax.ShapeDtypeStruct((LANES, LANES), I32), jax.ShapeDtypeStruct((SUBLANES, LANES), I32)],
        compiler_params=pltpu.CompilerParams(vmem_limit_bytes=VMEM_LIMIT),
        name="moe_plan",
    )(cls2d)


def _dispatch_call(dest, ragged, x2d, n_sorted):
    n_tok = x2d.shape[0]
    return pl.pallas_call(
        _dispatch_kernel,
        grid_spec=pltpu.PrefetchScalarGridSpec(
            num_scalar_prefetch=2,
            grid=(n_tok // (2 * TD),),
            in_specs=[pl.BlockSpec((2 * TD, D_MODEL), lambda i, *_: (i, 0))],
            out_specs=pl.BlockSpec(memory_space=pl.ANY),
            scratch_shapes=[
                pltpu.VMEM((2, TD * ROW_TILES, LANES), F32),
                pltpu.VMEM((TMS * ROW_TILES, LANES), F32),
                pltpu.SemaphoreType.DMA((2,)),
                pltpu.SemaphoreType.DMA(()),
            ]),
        out_shape=jax.ShapeDtypeStruct((n_sorted * ROW_TILES, LANES), F32),
        compiler_params=pltpu.CompilerParams(
            dimension_semantics=("arbitrary",), vmem_limit_bytes=VMEM_LIMIT),
        name="moe_dispatch",
    )(dest, ragged, x2d)


def _combine_call(dest, ys, x2d, ln_g, ln_b):
    n_tok = x2d.shape[0]
    return pl.pallas_call(
        _combine_kernel,
        grid_spec=pltpu.PrefetchScalarGridSpec(
            num_scalar_prefetch=1,
            grid=(n_tok // (2 * TD),),
            in_specs=[pl.BlockSpec(memory_space=pl.ANY),
                      pl.BlockSpec((2 * TD, D_MODEL), lambda i, *_: (i, 0)),
                      _full((1, D_MODEL)), _full((1, D_MODEL))],
            out_specs=pl.BlockSpec((2 * TD, D_MODEL), lambda i, *_: (i, 0)),
            scratch_shapes=[
                pltpu.VMEM((2, TD * ROW_TILES, LANES), F32),
                pltpu.SemaphoreType.DMA((2,)),
            ]),
        out_shape=jax.ShapeDtypeStruct((n_tok, D_MODEL), F32),
        compiler_params=pltpu.CompilerParams(
            dimension_semantics=("arbitrary",), vmem_limit_bytes=VMEM_LIMIT),
        name="moe_combine",
    )(dest, ys, x2d, ln_g, ln_b)


def _moe_call(ea, eb, act, xs, wg, wu, wd, rwt, n_tiles, layer):
    rows = TILES_PER_STEP * TMS * ROW_TILES
    w_specs, w_args = [], []
    for s in range(TILES_PER_STEP):
        for sel in (0, 1):
            def expert(i, ea, eb, act, s=s, sel=sel):
                return (layer, (ea, eb)[sel][i * TILES_PER_STEP + s], 0, 0)
            w_specs += [pl.BlockSpec((1, 1, D_MODEL, D_EXPERT), expert),
                        pl.BlockSpec((1, 1, D_MODEL, D_EXPERT), expert),
                        pl.BlockSpec((1, 1, D_EXPERT, D_MODEL), expert)]
            w_args += [wg, wu, wd]
    tile = pl.BlockSpec((rows, LANES), lambda i, ea, eb, act: (i, 0))
    return pl.pallas_call(
        _moe_kernel,
        grid_spec=pltpu.PrefetchScalarGridSpec(
            num_scalar_prefetch=3,
            grid=(n_tiles // TILES_PER_STEP,),
            in_specs=[tile] + w_specs + [_full((N_EXPERTS, D_MODEL))],
            out_specs=tile),
        out_shape=jax.ShapeDtypeStruct(xs.shape, F32),
        compiler_params=pltpu.CompilerParams(
            dimension_semantics=("arbitrary",), vmem_limit_bytes=VMEM_LIMIT),
        name="moe_experts",
    )(ea, eb, act, xs, *w_args, rwt)


def _block_diag(w):
    g, d, _ = w.shape
    eye = jnp.eye(g, dtype=w.dtype)
    return (eye[:, None, :, None] * w[:, :, None, :]).reshape(g * d, g * d)


def _pad_rows(w, rows):
    return jnp.pad(w, ((0, rows - w.shape[0]), (0, 0)))


def kernel(x, w_in, gm_ln_g, gm_ln_b, gm_w_s, gm_b_s, cf_dw_w, cf_dw_b, cf_ln_g, cf_ln_b, cf_pw,
           pool_w, pool_scale, sc_w, mix_norm_g, w_o, ln1_g, ln1_b, router_w, router_b,
           exp_w_gate, exp_w_up, exp_w_down, ln2_g, ln2_b):
    batch, seq, _ = x.shape
    n_tok = batch * seq
    assert n_tok == LANES * LANES, "the routing plan lays tokens out as one (128, 128) tile grid"
    assert seq % TS == 0 and TS % SUB_ROWS == 0 and SUB_ROWS % (2 * CONV_ROWS) == 0 and n_tok % (2 * TD) == 0
    n_tiles = n_tok // TMS + N_CLASSES
    assert n_tiles <= LANES and n_tiles % TILES_PER_STEP == 0
    row = lambda a: a.reshape(1, -1)
    rw = jnp.pad(router_w, ((0, 0), (0, LANES - N_EXPERTS)))
    rw_hi = rw.astype(BF16)
    rw_lo = (rw - rw_hi.astype(F32)).astype(BF16)
    rb = jnp.pad(router_b, (0, LANES - N_EXPERTS)).reshape(1, LANES)
    rwt = router_w.T
    w_in_b, w_o_b, cf_pw_b = w_in.astype(BF16), w_o.astype(BF16), cf_pw.astype(BF16)
    pool_b = jax.vmap(_block_diag)(pool_w).astype(BF16)
    wg_b, wu_b, wd_b = exp_w_gate.astype(BF16), exp_w_up.astype(BF16), exp_w_down.astype(BF16)
    for l in range(DEPTH):
        params = (
            w_in_b,
            row(gm_ln_g[l]), row(gm_ln_b[l]),
            gm_w_s[l].reshape(N_HEADS * CHUNK, CHUNK),
            jnp.repeat(gm_b_s[l].T, HEAD_DIM, axis=1),
            _pad_rows(cf_dw_w[l], HALO), row(cf_dw_b[l]), row(cf_ln_g[l]), row(cf_ln_b[l]),
            cf_pw_b,
            pool_b, row(pool_scale[l]),
            _pad_rows(sc_w[l], SUBLANES),
            row(mix_norm_g[l]), w_o_b,
            row(ln1_g[l]), row(ln1_b[l]),
            rw_hi, rw_lo, rb,
        )
        x1, cls = _mixer_call(x, params, l)
        x1 = x1.reshape(n_tok, D_MODEL)
        dest2d, meta = _plan_call(cls.reshape(LANES, LANES))
        dest = dest2d.reshape(n_tok)
        xs = _dispatch_call(dest, meta[3], x1, n_tiles * TMS)
        ys = _moe_call(meta[0], meta[1], meta[2], xs, wg_b, wu_b, wd_b, rwt, n_tiles, l)
        x = _combine_call(dest, ys, x1, row(ln2_g[l]), row(ln2_b[l])).reshape(batch, seq, D_MODEL)
    return x
```

```python
import functools

import jax
import jax.numpy as jnp
from jax import lax
from jax.experimental import pallas as pl
from jax.experimental.pallas import tpu as pltpu
from jax.experimental.pallas import tpu_sc as plsc

D_MODEL = 1024
DEPTH = 2
D_GROUP = 256
N_HEADS = 4
HEAD_DIM = 64
CHUNK = 128
CONF_K = 31
SHORT_K = 3
N_EXPERTS = 16
N_GROUPS = 4
GROUP_SIZE = 4
N_PAIRS = 6
N_CLASSES = N_GROUPS * N_PAIRS
D_EXPERT = 512
ALPHA = (2 * DEPTH) ** 0.25
LN_EPS = 1e-5
RMS_EPS = 1e-6

LANES = 128
SUBLANES = 8
ROW_TILES = D_MODEL // LANES
HALO = 32
SC_HALO = SUBLANES
TS = 512
SUB_ROWS = 256
CONV_ROWS = 64
TMS = 256
TILES_PER_STEP = 2
TD = 256
SC_CORES = 2
SC_WORKERS = 32
SC_CHUNK = 32
VMEM_LIMIT = 56 * 1024 * 1024

BF16 = jnp.bfloat16
F32 = jnp.float32
I32 = jnp.int32


def _dot(a, b):
    return jnp.dot(a, b, preferred_element_type=F32)


def _layer_norm(x, g, b):
    mu = jnp.mean(x, axis=-1, keepdims=True)
    xc = x - mu
    var = jnp.mean(xc * xc, axis=-1, keepdims=True)
    return xc * lax.rsqrt(var + LN_EPS) * g + b


def _rms_norm(y, g):
    ms = jnp.mean(y * y, axis=-1, keepdims=True)
    return y * lax.rsqrt(ms + RMS_EPS) * g


def _sigmoid(x):
    return 1.0 / (1.0 + jnp.exp(-x))


def _route_class(sel):
    scores = []
    for g in range(N_GROUPS):
        v = sel[g * GROUP_SIZE:(g + 1) * GROUP_SIZE]
        best_pair = None
        for i in range(GROUP_SIZE):
            for j in range(i + 1, GROUP_SIZE):
                p = v[i] + v[j]
                best_pair = p if best_pair is None else jnp.maximum(best_pair, p)
        scores.append(best_pair)
    best = jnp.zeros(scores[0].shape, I32)
    best_score = scores[0]
    for g in range(1, N_GROUPS):
        better = scores[g] > best_score
        best = jnp.where(better, g, best)
        best_score = jnp.where(better, scores[g], best_score)
    v = []
    for j in range(GROUP_SIZE):
        out = sel[j]
        for g in range(1, N_GROUPS):
            out = jnp.where(best == g, sel[g * GROUP_SIZE + j], out)
        v.append(out)
    i0 = jnp.zeros_like(best)
    v0 = v[0]
    for j in range(1, GROUP_SIZE):
        better = v[j] > v0
        i0 = jnp.where(better, j, i0)
        v0 = jnp.where(better, v[j], v0)
    neg = jnp.full_like(v0, -jnp.inf)
    w = [jnp.where(i0 == j, neg, v[j]) for j in range(GROUP_SIZE)]
    i1 = jnp.zeros_like(best)
    v1 = w[0]
    for j in range(1, GROUP_SIZE):
        better = w[j] > v1
        i1 = jnp.where(better, j, i1)
        v1 = jnp.where(better, w[j], v1)
    a = jnp.minimum(i0, i1)
    b = jnp.maximum(i0, i1)
    pair = jnp.where(a == 0, b - 1, jnp.where(a == 1, b + 1, N_PAIRS - 1))
    return best * N_PAIRS + pair


def _mixer_kernel(x_ref, w_in_ref, gm_g_ref, gm_b_ref, gm_w_ref, gm_bs_ref,
                  cf_w_ref, cf_b_ref, cf_g_ref, cf_beta_ref, cf_pw_ref,
                  pool_w_ref, pool_scale_ref, sc_w_ref, mixg_ref, w_o_ref,
                  ln_g_ref, ln_b_ref, rw_hi_ref, rw_lo_ref, rb_ref,
                  x1_ref, cls_ref, *scratch):
    s_idx = pl.program_id(1)

    row = lax.broadcasted_iota(I32, (N_HEADS * CHUNK, CHUNK), 0)
    col = lax.broadcasted_iota(I32, (N_HEADS * CHUNK, CHUNK), 1)
    w_tril = jnp.where(col <= (row & (CHUNK - 1)), gm_w_ref[...], 0.0).astype(BF16)
    lane = lax.broadcasted_iota(I32, (CHUNK, D_GROUP), 1)
    lane_t = lax.broadcasted_iota(I32, (SUB_ROWS, D_GROUP), 1)
    win = jnp.where(lane_t < HEAD_DIM, 2,
                    jnp.where(lane_t < 2 * HEAD_DIM, 4, jnp.where(lane_t < 3 * HEAD_DIM, 8, 16)))
    bs = gm_bs_ref[...]
    mixg = mixg_ref[...]

    n_blocks = TS // SUB_ROWS
    per_block = len(scratch) // n_blocks
    blocks = [scratch[i * per_block:(i + 1) * per_block] for i in range(n_blocks)]

    @pl.when(s_idx == 0)
    def _():
        hbuf0, hbuf1, _, _, pbuf, _, _, _, cbuf, _ = blocks[0]
        hbuf0[0:HALO, :] = jnp.zeros((HALO, LANES), F32)
        hbuf1[0:HALO, :] = jnp.zeros((HALO, LANES), F32)
        pbuf[0:HALO, :] = jnp.zeros((HALO, D_GROUP), F32)
        cbuf[0:SC_HALO, :] = jnp.zeros((SC_HALO, D_GROUP), F32)

    def carry_halo(src, dst, which):
        for i in which:
            rows = SC_HALO if i == 8 else HALO
            dst[i][0:rows, :] = src[i][SUB_ROWS:SUB_ROWS + rows, :]

    st = [dict() for _ in range(n_blocks)]

    def load(bi):
        x = x_ref[0, bi * SUB_ROWS:(bi + 1) * SUB_ROWS, :]
        st[bi]["x"] = x
        st[bi]["xb"] = x.astype(BF16)

    def in_proj(bi, lo, hi):
        z = _dot(st[bi]["xb"], w_in_ref[0, :, lo * D_GROUP:hi * D_GROUP])
        for j in range(lo, hi):
            st[bi][j] = z[:, (j - lo) * D_GROUP:(j - lo + 1) * D_GROUP]

    def gating_mlp(bi):
        yn_ref = blocks[bi][9]
        u = st[bi].pop(0)
        v = _layer_norm(st[bi].pop(1), gm_g_ref[...], gm_b_ref[...])
        for n in range(SUB_ROWS // CHUNK):
            crows = slice(n * CHUNK, (n + 1) * CHUNK)
            s_all = _dot(w_tril, v[crows, :].astype(BF16))
            s_sel = s_all[3 * CHUNK:4 * CHUNK]
            for h in (2, 1, 0):
                s_sel = jnp.where(lane < (h + 1) * HEAD_DIM, s_all[h * CHUNK:(h + 1) * CHUNK], s_sel)
            y1 = u[crows, :] * (s_sel + bs)
            yn_ref[crows, 0:D_GROUP] = _rms_norm(y1, mixg[:, 0:D_GROUP]).astype(BF16)

    def conformer_glu(bi):
        if bi > 0:
            carry_halo(blocks[bi - 1], blocks[bi], (0, 1))
        glu = st[bi].pop(2) * _sigmoid(st[bi].pop(3))
        for half in range(2):
            blocks[bi][half][HALO:HALO + SUB_ROWS, :] = glu[:, half * LANES:(half + 1) * LANES]

    def conformer_conv(bi, half):
        hb, co = blocks[bi][half], blocks[bi][2 + half]
        lanes = slice(half * LANES, (half + 1) * LANES)
        bias = cf_b_ref[:, lanes]
        for q in range(SUB_ROWS // (2 * CONV_ROWS)):
            for parity in range(2):
                out0 = q * 2 * CONV_ROWS + parity
                base = out0 + HALO - (CONF_K - 1)
                acc = jnp.zeros((CONV_ROWS, LANES), F32) + bias
                for k in range(CONF_K):
                    acc = acc + cf_w_ref[k:k + 1, lanes] * hb[pl.ds(base + k, CONV_ROWS, stride=2), :]
                co[pl.ds(out0, CONV_ROWS, stride=2), :] = acc

    def conformer_out(bi):
        cbo0, cbo1, yn_ref = blocks[bi][2], blocks[bi][3], blocks[bi][9]
        hln = _layer_norm(jnp.concatenate([cbo0[...], cbo1[...]], axis=1), cf_g_ref[...], cf_beta_ref[...])
        y2 = _dot((hln * _sigmoid(hln)).astype(BF16), cf_pw_ref[0])
        yn_ref[:, D_GROUP:2 * D_GROUP] = _rms_norm(y2, mixg[:, D_GROUP:2 * D_GROUP]).astype(BF16)

    def pooling(bi):
        pbuf, s2buf, s4buf, s8buf = blocks[bi][4:8]
        yn_ref = blocks[bi][9]
        if bi > 0:
            carry_halo(blocks[bi - 1], blocks[bi], (4,))
        zc = st[bi].pop(4)
        pbuf[HALO:HALO + SUB_ROWS, :] = zc
        n2 = HALO + SUB_ROWS - 8
        s2buf[8:8 + n2, :] = pbuf[8:8 + n2, :] + pbuf[7:7 + n2, :]
        n4 = HALO + SUB_ROWS - 16
        s4buf[16:16 + n4, :] = s2buf[16:16 + n4, :] + s2buf[14:14 + n4, :]
        n8 = HALO + SUB_ROWS - 24
        s8buf[24:24 + n8, :] = s4buf[24:24 + n8, :] + s4buf[20:20 + n8, :]
        cur = slice(HALO, HALO + SUB_ROWS)
        s16 = s8buf[cur, :] + s8buf[HALO - 8:HALO - 8 + SUB_ROWS, :]
        pos1 = lax.broadcasted_iota(I32, (SUB_ROWS, D_GROUP), 0) + (s_idx * TS + bi * SUB_ROWS + 1)
        wsum = jnp.where(lane_t < HEAD_DIM, s2buf[cur, :],
                         jnp.where(lane_t < 2 * HEAD_DIM, s4buf[cur, :],
                                   jnp.where(lane_t < 3 * HEAD_DIM, s8buf[cur, :], s16)))
        count = jnp.minimum(pos1, win).astype(F32)
        pooled = wsum / count - zc
        y3 = _dot(pooled.astype(BF16), pool_w_ref[0]) * pool_scale_ref[...]
        yn_ref[:, 2 * D_GROUP:3 * D_GROUP] = _rms_norm(y3, mixg[:, 2 * D_GROUP:3 * D_GROUP]).astype(BF16)

    def short_conv(bi):
        cbuf, yn_ref = blocks[bi][8], blocks[bi][9]
        if bi > 0:
            carry_halo(blocks[bi - 1], blocks[bi], (8,))
        cbuf[SC_HALO:SC_HALO + SUB_ROWS, :] = st[bi].pop(6) * st[bi].pop(7)
        conv = jnp.zeros((SUB_ROWS, D_GROUP), F32)
        for k in range(SHORT_K):
            off = SC_HALO - (SHORT_K - 1) + k
            conv = conv + sc_w_ref[k:k + 1, :] * cbuf[off:off + SUB_ROWS, :]
        y4 = st[bi].pop(5) * conv
        yn_ref[:, 3 * D_GROUP:4 * D_GROUP] = _rms_norm(y4, mixg[:, 3 * D_GROUP:4 * D_GROUP]).astype(BF16)

    def out_proj(bi):
        m = _dot(blocks[bi][9][...], w_o_ref[0])
        x1 = _layer_norm(ALPHA * st[bi].pop("x") + m, ln_g_ref[...], ln_b_ref[...])
        x1_ref[0, bi * SUB_ROWS:(bi + 1) * SUB_ROWS, :] = x1
        st[bi]["x1"] = x1

    def router(bi):
        x1 = st[bi].pop("x1")
        hi = x1.astype(BF16)
        lo = (x1 - hi.astype(F32)).astype(BF16)
        logits = _dot(hi, rw_hi_ref[...]) + _dot(hi, rw_lo_ref[...]) + _dot(lo, rw_hi_ref[...])
        st[bi]["sel"] = logits + rb_ref[...]

    def route(bi):
        sel_t = st[bi].pop("sel").T
        cls = _route_class([sel_t[e:e + 1, :] for e in range(N_EXPERTS)])
        r0 = bi * SUB_ROWS
        cls_ref[0, r0 // LANES:(r0 + SUB_ROWS) // LANES, :] = jnp.concatenate(
            [cls[:, k * LANES:(k + 1) * LANES] for k in range(SUB_ROWS // LANES)], axis=0)

    def conformer(b):
        conformer_glu(b)
        conformer_conv(b, 0)
        conformer_conv(b, 1)
        conformer_out(b)

    def stage1(b):
        return [lambda: load(b), lambda: in_proj(b, 0, 2), lambda: in_proj(b, 2, 4), lambda: in_proj(b, 4, 5),
                lambda: in_proj(b, 5, 8)]

    def stage2(b):
        return [lambda: gating_mlp(b), lambda: conformer(b), lambda: pooling(b), lambda: short_conv(b)]

    def stage3(b):
        return [lambda: out_proj(b), lambda: router(b), lambda: route(b)]

    for t in range(n_blocks + 2):
        stages = [stage(t - lag) for lag, stage in ((2, stage3), (0, stage1), (1, stage2)) if 0 <= t - lag < n_blocks]
        for k in range(max(len(stage) for stage in stages)):
            for stage in stages:
                if k < len(stage):
                    stage[k]()

    carry_halo(blocks[-1], blocks[0], (0, 1, 4, 8))


def _plan_kernel(cls_ref, dest_ref, meta_ref):
    n = LANES
    cls = cls_ref[...]
    r = lax.broadcasted_iota(I32, (n, n), 0)
    c = lax.broadcasted_iota(I32, (n, n), 1)
    upper = jnp.where(r < c, 1.0, 0.0).astype(BF16)
    lower = jnp.where(c < r, 1.0, 0.0).astype(BF16)
    ones = jnp.ones((n, n), BF16)
    masks = [cls == k for k in range(N_CLASSES)]
    m_all = jnp.concatenate([jnp.where(mk, 1.0, 0.0) for mk in masks], axis=0).astype(BF16)
    within = _dot(m_all, upper)
    rowsum = _dot(m_all, ones)
    rs = jnp.zeros((n, n), F32)
    for k in range(N_CLASSES):
        rs = jnp.where(c == k, rowsum[k * n:(k + 1) * n], rs)
    rs_b = rs.astype(BF16)
    before = _dot(lower, rs_b)
    total = _dot(ones, rs_b)
    ntile = jnp.floor((total + (TMS - 1)) * (1.0 / TMS))
    tstart = _dot(ntile.astype(BF16), upper)
    base = before + tstart * TMS
    dest = jnp.zeros((n, n), F32)
    for k in range(N_CLASSES):
        dest = jnp.where(masks[k], within[k * n:(k + 1) * n] + base[:, k:k + 1], dest)
    dest_ref[...] = dest.astype(I32)

    tend_t = (tstart + ntile).T
    ntile_t = ntile.T
    tile = c.astype(F32)
    ended = jnp.where((r < N_CLASSES) & (tend_t <= tile), 1.0, 0.0)
    is_last = jnp.where((r < N_CLASSES) & (ntile_t > 0) & (tend_t - 1.0 == tile), 1.0, 0.0)
    tcls = jnp.minimum(jnp.sum(ended, axis=0, keepdims=True), N_CLASSES - 1.0)
    grp = (jnp.where(tcls >= N_PAIRS, 1.0, 0.0) + jnp.where(tcls >= 2 * N_PAIRS, 1.0, 0.0)
           + jnp.where(tcls >= 3 * N_PAIRS, 1.0, 0.0))
    pair = tcls - N_PAIRS * grp
    pa = jnp.where(pair >= 3, 1.0, 0.0) + jnp.where(pair >= 5, 1.0, 0.0)
    pb = jnp.where(pair == 0, 1.0, jnp.where((pair == 1) | (pair == 3), 2.0, 3.0))
    n_tiles = tstart[0:1, N_CLASSES:N_CLASSES + 1]
    tile_row = tile[0:1, :]
    active = jnp.where(tile_row < n_tiles, 1.0, 0.0)
    ragged = jnp.where((jnp.sum(is_last, axis=0, keepdims=True) > 0) | (tile_row >= n_tiles), 1.0, 0.0)
    meta = jnp.concatenate(
        [GROUP_SIZE * grp + pa, GROUP_SIZE * grp + pb, active, ragged, jnp.zeros((4, n), F32)], axis=0)
    meta_ref[...] = meta.astype(I32)


def _row_copy_wait(src, dst, sem):
    pltpu.make_async_copy(src, dst, sem).wait()


def _dispatch_kernel(dest_ref, ragged_ref, x_ref, xs_ref, buf, zbuf, sems, zsem):
    i = pl.program_id(0)
    n_steps = pl.num_programs(0)
    tile_rows = TMS * ROW_TILES
    n_tiles = xs_ref.shape[0] // tile_rows

    @pl.when(i == 0)
    def _():
        zbuf[...] = jnp.zeros(zbuf.shape, F32)
        for k in range(n_tiles):
            @pl.when(ragged_ref[k] == 1)
            def _():
                pltpu.make_async_copy(zbuf, xs_ref.at[pl.ds(k * tile_rows, tile_rows), :], zsem).start()
        for k in range(n_tiles):
            @pl.when(ragged_ref[k] == 1)
            def _():
                _row_copy_wait(zbuf, xs_ref.at[pl.ds(k * tile_rows, tile_rows), :], zsem)

    for s in range(2):
        @pl.when(i > 0)
        def _():
            _row_copy_wait(buf.at[s], xs_ref.at[pl.ds(0, TD * ROW_TILES), :], sems.at[s])

        for j in range(ROW_TILES):
            buf[s, pl.ds(j, TD, stride=ROW_TILES), :] = x_ref[s * TD:(s + 1) * TD, j * LANES:(j + 1) * LANES]
        for t in range(TD):
            d = pl.multiple_of(dest_ref[(i * 2 + s) * TD + t] * ROW_TILES, ROW_TILES)
            pltpu.make_async_copy(buf.at[s, pl.ds(t * ROW_TILES, ROW_TILES), :],
                                  xs_ref.at[pl.ds(d, ROW_TILES), :], sems.at[s]).start(priority=t % 2)

    @pl.when(i == n_steps - 1)
    def _():
        for s in range(2):
            _row_copy_wait(buf.at[s], xs_ref.at[pl.ds(0, TD * ROW_TILES), :], sems.at[s])


def _combine_kernel(dest_ref, ys_ref, x_ref, ln_g_ref, ln_b_ref, out_ref, buf, sems):
    i = pl.program_id(0)
    n_steps = pl.num_programs(0)

    def issue(step, s):
        for t in range(TD):
            d = pl.multiple_of(dest_ref[(step * 2 + s) * TD + t] * ROW_TILES, ROW_TILES)
            pltpu.make_async_copy(ys_ref.at[pl.ds(d, ROW_TILES), :],
                                  buf.at[s, pl.ds(t * ROW_TILES, ROW_TILES), :], sems.at[s]).start(priority=t % 2)

    @pl.when(i == 0)
    def _():
        for s in range(2):
            issue(0, s)

    for s in range(2):
        _row_copy_wait(ys_ref.at[pl.ds(0, TD * ROW_TILES), :], buf.at[s], sems.at[s])
        f = jnp.concatenate([buf[s, pl.ds(j, TD, stride=ROW_TILES), :] for j in range(ROW_TILES)], axis=1)
        rows = slice(s * TD, (s + 1) * TD)
        out_ref[rows, :] = _layer_norm(ALPHA * x_ref[rows, :] + f, ln_g_ref[...], ln_b_ref[...])

        @pl.when(i + 1 < n_steps)
        def _():
            issue(i + 1, s)


def _moe_kernel(ea_ref, eb_ref, act_ref, xs_ref, *refs):
    i = pl.program_id(0)
    w_refs, rwt_ref, ys_ref = refs[:-2], refs[-2], refs[-1]
    rows = TMS * ROW_TILES

    @pl.when(act_ref[i * TILES_PER_STEP] == 0)
    def _():
        ys_ref[...] = jnp.zeros(ys_ref.shape, F32)

    @pl.when(act_ref[i * TILES_PER_STEP] == 1)
    def _():
        for s in range(TILES_PER_STEP):
            tile = i * TILES_PER_STEP + s
            wga_ref, wua_ref, wda_ref, wgb_ref, wub_ref, wdb_ref = w_refs[6 * s:6 * s + 6]
            x = jnp.concatenate(
                [xs_ref[pl.ds(s * rows + j, TMS, stride=ROW_TILES), :] for j in range(ROW_TILES)], axis=1)
            la = jnp.sum(x * rwt_ref[pl.ds(ea_ref[tile], 1), :], axis=-1, keepdims=True)
            lb = jnp.sum(x * rwt_ref[pl.ds(eb_ref[tile], 1), :], axis=-1, keepdims=True)
            m = jnp.maximum(la, lb)
            pa = jnp.exp(la - m)
            pb = jnp.exp(lb - m)
            den = pa + pb
            xb = x.astype(BF16)

            def ffn(wg_ref, wu_ref, wd_ref):
                g = _dot(xb, wg_ref[0, 0])
                u = _dot(xb, wu_ref[0, 0])
                h = (g * _sigmoid(g)) * u
                return _dot(h.astype(BF16), wd_ref[0, 0])

            f = (pa / den) * ffn(wga_ref, wua_ref, wda_ref) + (pb / den) * ffn(wgb_ref, wub_ref, wdb_ref)
            for j in range(ROW_TILES):
                ys_ref[pl.ds(s * rows + j, TMS, stride=ROW_TILES), :] = f[:, j * LANES:(j + 1) * LANES]


def _full(shape):
    return pl.BlockSpec(shape, lambda *_: (0,) * len(shape))


def _layer_block(shape, layer):
    return pl.BlockSpec((1,) + shape, lambda *_: (layer,) + (0,) * len(shape))


def _mixer_call(x, p, layer):
    batch, seq, _ = x.shape
    steps = seq // TS
    in_specs = [
        pl.BlockSpec((1, TS, D_MODEL), lambda b, s: (b, s, 0)),
        _layer_block((D_MODEL, 8 * D_GROUP), layer),
        _full((1, D_GROUP)), _full((1, D_GROUP)),
        _full((N_HEADS * CHUNK, CHUNK)), _full((CHUNK, D_GROUP)),
        _full((HALO, D_GROUP)), _full((1, D_GROUP)), _full((1, D_GROUP)), _full((1, D_GROUP)),
        _layer_block((D_GROUP, D_GROUP), layer),
        _layer_block((D_GROUP, D_GROUP), layer), _full((1, D_GROUP)),
        _full((SUBLANES, D_GROUP)),
        _full((1, D_MODEL)), _layer_block((D_MODEL, D_MODEL), layer),
        _full((1, D_MODEL)), _full((1, D_MODEL)),
        _full((D_MODEL, LANES)), _full((D_MODEL, LANES)), _full((1, LANES)),
    ]
    out_specs = [
        pl.BlockSpec((1, TS, D_MODEL), lambda b, s: (b, s, 0)),
        pl.BlockSpec((1, TS // LANES, LANES), lambda b, s: (b * steps + s, 0, 0)),
    ]
    block_scratch = [
        pltpu.VMEM((HALO + SUB_ROWS, LANES), F32),
        pltpu.VMEM((HALO + SUB_ROWS, LANES), F32),
        pltpu.VMEM((SUB_ROWS, LANES), F32),
        pltpu.VMEM((SUB_ROWS, LANES), F32),
        pltpu.VMEM((HALO + SUB_ROWS, D_GROUP), F32),
        pltpu.VMEM((HALO + SUB_ROWS, D_GROUP), F32),
        pltpu.VMEM((HALO + SUB_ROWS, D_GROUP), F32),
        pltpu.VMEM((HALO + SUB_ROWS, D_GROUP), F32),
        pltpu.VMEM((SC_HALO + SUB_ROWS, D_GROUP), F32),
        pltpu.VMEM((SUB_ROWS, D_MODEL), BF16),
    ]
    scratch = block_scratch * (TS // SUB_ROWS)
    return pl.pallas_call(
        _mixer_kernel,
        grid=(batch, steps),
        in_specs=in_specs,
        out_specs=out_specs,
        out_shape=[jax.ShapeDtypeStruct((batch, seq, D_MODEL), F32),
                   jax.ShapeDtypeStruct((batch * steps, TS // LANES, LANES), I32)],
        scratch_shapes=scratch,
        compiler_params=pltpu.CompilerParams(
            dimension_semantics=("arbitrary", "arbitrary"), vmem_limit_bytes=VMEM_LIMIT),
        name="mixer",
    )(x, *p)


def _plan_call(cls2d):
    return pl.pallas_call(
        _plan_kernel,
        out_shape=[jax.ShapeDtypeStruct((LANES, LANES), I32), jax.ShapeDtypeStruct((SUBLANES, LANES), I32)],
        compiler_params=pltpu.CompilerParams(vmem_limit_bytes=VMEM_LIMIT),
        name="moe_plan",
    )(cls2d)


def _dispatch_call(dest, ragged, x2d, n_sorted):
    n_tok = x2d.shape[0]
    return pl.pallas_call(
        _dispatch_kernel,
        grid_spec=pltpu.PrefetchScalarGridSpec(
            num_scalar_prefetch=2,
            grid=(n_tok // (2 * TD),),
            in_specs=[pl.BlockSpec((2 * TD, D_MODEL), lambda i, *_: (i, 0))],
            out_specs=pl.BlockSpec(memory_space=pl.ANY),
            scratch_shapes=[
                pltpu.VMEM((2, TD * ROW_TILES, LANES), F32),
                pltpu.VMEM((TMS * ROW_TILES, LANES), F32),
                pltpu.SemaphoreType.DMA((2,)),
                pltpu.SemaphoreType.DMA(()),
            ]),
        out_shape=jax.ShapeDtypeStruct((n_sorted * ROW_TILES, LANES), F32),
        compiler_params=pltpu.CompilerParams(
            dimension_semantics=("arbitrary",), vmem_limit_bytes=VMEM_LIMIT),
        name="moe_dispatch",
    )(dest, ragged, x2d)


def _combine_call(dest, ys, x2d, ln_g, ln_b):
    n_tok = x2d.shape[0]
    return pl.pallas_call(
        _combine_kernel,
        grid_spec=pltpu.PrefetchScalarGridSpec(
            num_scalar_prefetch=1,
            grid=(n_tok // (2 * TD),),
            in_specs=[pl.BlockSpec(memory_space=pl.ANY),
                      pl.BlockSpec((2 * TD, D_MODEL), lambda i, *_: (i, 0)),
                      _full((1, D_MODEL)), _full((1, D_MODEL))],
            out_specs=pl.BlockSpec((2 * TD, D_MODEL), lambda i, *_: (i, 0)),
            scratch_shapes=[
                pltpu.VMEM((2, TD * ROW_TILES, LANES), F32),
                pltpu.SemaphoreType.DMA((2,)),
            ]),
        out_shape=jax.ShapeDtypeStruct((n_tok, D_MODEL), F32),
        compiler_params=pltpu.CompilerParams(
            dimension_semantics=("arbitrary",), vmem_limit_bytes=VMEM_LIMIT),
        name="moe_combine",
    )(dest, ys, x2d, ln_g, ln_b)


def _sc_gather_rows(src3, idx):
    n_out = idx.shape[0]
    per_worker = n_out // SC_WORKERS
    mesh = plsc.VectorSubcoreMesh(core_axis_name="c", subcore_axis_name="s",
                                  num_cores=SC_CORES, num_subcores=SC_WORKERS // SC_CORES)

    @functools.partial(
        pl.kernel, mesh=mesh,
        out_type=jax.ShapeDtypeStruct((n_out, ROW_TILES, LANES), F32),
        scratch_types=[pltpu.VMEM((per_worker,), I32),
                       pltpu.VMEM((SC_CHUNK, ROW_TILES, LANES), F32),
                       pltpu.SemaphoreType.DMA],
        name="sc_gather_rows")
    def gather(src_hbm, idx_hbm, out_hbm, idx_v, rows_v, sem):
        wid = lax.axis_index("s") * SC_CORES + lax.axis_index("c")
        base = wid * per_worker
        pltpu.sync_copy(idx_hbm.at[pl.ds(base, per_worker)], idx_v)

        @pl.loop(0, per_worker // SC_CHUNK)
        def _(ch):
            off = pl.multiple_of(ch * SC_CHUNK, SC_CHUNK)
            pltpu.async_copy(src_hbm.at[idx_v.at[pl.ds(off, SC_CHUNK)]], rows_v, sem).wait()
            pltpu.sync_copy(rows_v, out_hbm.at[pl.ds(base + off, SC_CHUNK)])

    return gather(src3, idx)


def _post_norm_kernel(f_ref, x_ref, ln_g_ref, ln_b_ref, out_ref):
    rows = x_ref.shape[0]
    f = jnp.concatenate([f_ref[pl.ds(j, rows, stride=ROW_TILES), :] for j in range(ROW_TILES)], axis=1)
    out_ref[...] = _layer_norm(ALPHA * x_ref[...] + f, ln_g_ref[...], ln_b_ref[...])


def _post_norm_call(f_rows, x2d, ln_g, ln_b):
    n_tok = x2d.shape[0]
    return pl.pallas_call(
        _post_norm_kernel,
        grid=(n_tok // (2 * TD),),
        in_specs=[pl.BlockSpec((2 * TD * ROW_TILES, LANES), lambda i: (i, 0)),
                  pl.BlockSpec((2 * TD, D_MODEL), lambda i: (i, 0)),
                  _full((1, D_MODEL)), _full((1, D_MODEL))],
        out_specs=pl.BlockSpec((2 * TD, D_MODEL), lambda i: (i, 0)),
        out_shape=jax.ShapeDtypeStruct((n_tok, D_MODEL), F32),
        compiler_params=pltpu.CompilerParams(
            dimension_semantics=("arbitrary",), vmem_limit_bytes=VMEM_LIMIT),
        name="post_norm",
    )(f_rows, x2d, ln_g, ln_b)


def _moe_call(ea, eb, act, xs, wg, wu, wd, rwt, n_tiles, layer):
    rows = TILES_PER_STEP * TMS * ROW_TILES
    w_specs, w_args = [], []
    for s in range(TILES_PER_STEP):
        for sel in (0, 1):
            def expert(i, ea, eb, act, s=s, sel=sel):
                return (layer, (ea, eb)[sel][i * TILES_PER_STEP + s], 0, 0)
            w_specs += [pl.BlockSpec((1, 1, D_MODEL, D_EXPERT), expert),
                        pl.BlockSpec((1, 1, D_MODEL, D_EXPERT), expert),
                        pl.BlockSpec((1, 1, D_EXPERT, D_MODEL), expert)]
            w_args += [wg, wu, wd]
    tile = pl.BlockSpec((rows, LANES), lambda i, ea, eb, act: (i, 0))
    return pl.pallas_call(
        _moe_kernel,
        grid_spec=pltpu.PrefetchScalarGridSpec(
            num_scalar_prefetch=3,
            grid=(n_tiles // TILES_PER_STEP,),
            in_specs=[tile] + w_specs + [_full((N_EXPERTS, D_MODEL))],
            out_specs=tile),
        out_shape=jax.ShapeDtypeStruct(xs.shape, F32),
        compiler_params=pltpu.CompilerParams(
            dimension_semantics=("arbitrary",), vmem_limit_bytes=VMEM_LIMIT),
        name="moe_experts",
    )(ea, eb, act, xs, *w_args, rwt)


def _block_diag(w):
    g, d, _ = w.shape
    eye = jnp.eye(g, dtype=w.dtype)
    return (eye[:, None, :, None] * w[:, :, None, :]).reshape(g * d, g * d)


def _pad_rows(w, rows):
    return jnp.pad(w, ((0, rows - w.shape[0]), (0, 0)))


def kernel(x, w_in, gm_ln_g, gm_ln_b, gm_w_s, gm_b_s, cf_dw_w, cf_dw_b, cf_ln_g, cf_ln_b, cf_pw,
           pool_w, pool_scale, sc_w, mix_norm_g, w_o, ln1_g, ln1_b, router_w, router_b,
           exp_w_gate, exp_w_up, exp_w_down, ln2_g, ln2_b):
    batch, seq, _ = x.shape
    n_tok = batch * seq
    assert n_tok == LANES * LANES, "the routing plan lays tokens out as one (128, 128) tile grid"
    assert seq % TS == 0 and TS % SUB_ROWS == 0 and SUB_ROWS % (2 * CONV_ROWS) == 0 and n_tok % (2 * TD) == 0
    n_tiles = n_tok // TMS + N_CLASSES
    assert n_tiles <= LANES and n_tiles % TILES_PER_STEP == 0
    row = lambda a: a.reshape(1, -1)
    rw = jnp.pad(router_w, ((0, 0), (0, LANES - N_EXPERTS)))
    rw_hi = rw.astype(BF16)
    rw_lo = (rw - rw_hi.astype(F32)).astype(BF16)
    rb = jnp.pad(router_b, (0, LANES - N_EXPERTS)).reshape(1, LANES)
    rwt = router_w.T
    w_in_b, w_o_b, cf_pw_b = w_in.astype(BF16), w_o.astype(BF16), cf_pw.astype(BF16)
    pool_b = jax.vmap(_block_diag)(pool_w).astype(BF16)
    wg_b, wu_b, wd_b = exp_w_gate.astype(BF16), exp_w_up.astype(BF16), exp_w_down.astype(BF16)
    for l in range(DEPTH):
        params = (
            w_in_b,
            row(gm_ln_g[l]), row(gm_ln_b[l]),
            gm_w_s[l].reshape(N_HEADS * CHUNK, CHUNK),
            jnp.repeat(gm_b_s[l].T, HEAD_DIM, axis=1),
            _pad_rows(cf_dw_w[l], HALO), row(cf_dw_b[l]), row(cf_ln_g[l]), row(cf_ln_b[l]),
            cf_pw_b,
            pool_b, row(pool_scale[l]),
            _pad_rows(sc_w[l], SUBLANES),
            row(mix_norm_g[l]), w_o_b,
            row(ln1_g[l]), row(ln1_b[l]),
            rw_hi, rw_lo, rb,
        )
        x1, cls = _mixer_call(x, params, l)
        x1 = x1.reshape(n_tok, D_MODEL)
        dest2d, meta = _plan_call(cls.reshape(LANES, LANES))
        dest = dest2d.reshape(n_tok)
        xs = _dispatch_call(dest, meta[3], x1, n_tiles * TMS)
        ys = _moe_call(meta[0], meta[1], meta[2], xs, wg_b, wu_b, wd_b, rwt, n_tiles, l)
        ys3 = ys.reshape(n_tiles * TMS, ROW_TILES, LANES)
        half = n_tok // 2
        parts = []
        for h in range(2):
            f_tok = _sc_gather_rows(ys3, dest[h * half:(h + 1) * half])
            parts.append(_post_norm_call(f_tok.reshape(half * ROW_TILES, LANES), x1[h * half:(h + 1) * half],
                                         row(ln2_g[l]), row(ln2_b[l])))
        x = jnp.concatenate(parts, axis=0).reshape(batch, seq, D_MODEL)
    return x
```

```python
import jax
import jax.numpy as jnp
from jax import lax
from jax.experimental import pallas as pl
from jax.experimental.pallas import tpu as pltpu

D_MODEL = 1024
DEPTH = 2
D_GROUP = 256
N_HEADS = 4
HEAD_DIM = 64
CHUNK = 128
CONF_K = 31
SHORT_K = 3
N_EXPERTS = 16
N_GROUPS = 4
GROUP_SIZE = 4
N_PAIRS = 6
N_CLASSES = N_GROUPS * N_PAIRS
D_EXPERT = 512
ALPHA = (2 * DEPTH) ** 0.25
LN_EPS = 1e-5
RMS_EPS = 1e-6

LANES = 128
SUBLANES = 8
ROW_TILES = D_MODEL // LANES
HALO = 32
SC_HALO = SUBLANES
TS = 512
SUB_ROWS = 256
CONV_ROWS = 64
TMS = 256
TILES_PER_STEP = 2
TD = 256
VMEM_LIMIT = 56 * 1024 * 1024

BF16 = jnp.bfloat16
F32 = jnp.float32
I32 = jnp.int32


def _dot(a, b):
    return jnp.dot(a, b, preferred_element_type=F32)


def _layer_norm(x, g, b):
    mu = jnp.mean(x, axis=-1, keepdims=True)
    xc = x - mu
    var = jnp.mean(xc * xc, axis=-1, keepdims=True)
    return xc * lax.rsqrt(var + LN_EPS) * g + b


def _rms_norm(y, g):
    ms = jnp.mean(y * y, axis=-1, keepdims=True)
    return y * lax.rsqrt(ms + RMS_EPS) * g


def _sigmoid(x):
    return 1.0 / (1.0 + jnp.exp(-x))


def _route_class(sel):
    scores = []
    for g in range(N_GROUPS):
        v = sel[g * GROUP_SIZE:(g + 1) * GROUP_SIZE]
        best_pair = None
        for i in range(GROUP_SIZE):
            for j in range(i + 1, GROUP_SIZE):
                p = v[i] + v[j]
                best_pair = p if best_pair is None else jnp.maximum(best_pair, p)
        scores.append(best_pair)
    best = jnp.zeros(scores[0].shape, I32)
    best_score = scores[0]
    for g in range(1, N_GROUPS):
        better = scores[g] > best_score
        best = jnp.where(better, g, best)
        best_score = jnp.where(better, scores[g], best_score)
    v = []
    for j in range(GROUP_SIZE):
        out = sel[j]
        for g in range(1, N_GROUPS):
            out = jnp.where(best == g, sel[g * GROUP_SIZE + j], out)
        v.append(out)
    i0 = jnp.zeros_like(best)
    v0 = v[0]
    for j in range(1, GROUP_SIZE):
        better = v[j] > v0
        i0 = jnp.where(better, j, i0)
        v0 = jnp.where(better, v[j], v0)
    neg = jnp.full_like(v0, -jnp.inf)
    w = [jnp.where(i0 == j, neg, v[j]) for j in range(GROUP_SIZE)]
    i1 = jnp.zeros_like(best)
    v1 = w[0]
    for j in range(1, GROUP_SIZE):
        better = w[j] > v1
        i1 = jnp.where(better, j, i1)
        v1 = jnp.where(better, w[j], v1)
    a = jnp.minimum(i0, i1)
    b = jnp.maximum(i0, i1)
    pair = jnp.where(a == 0, b - 1, jnp.where(a == 1, b + 1, N_PAIRS - 1))
    return best * N_PAIRS + pair


def _mixer_kernel(x_ref, w_in_ref, gm_g_ref, gm_b_ref, gm_w_ref, gm_bs_ref,
                  cf_w_ref, cf_b_ref, cf_g_ref, cf_beta_ref, cf_pw_ref,
                  pool_w_ref, pool_scale_ref, sc_w_ref, mixg_ref, w_o_ref,
                  ln_g_ref, ln_b_ref, rw_ref, rb_ref,
                  x1_ref, cls_ref, *scratch):
    s_idx = pl.program_id(1)

    row = lax.broadcasted_iota(I32, (N_HEADS * CHUNK, CHUNK), 0)
    col = lax.broadcasted_iota(I32, (N_HEADS * CHUNK, CHUNK), 1)
    w_tril = jnp.where(col <= (row & (CHUNK - 1)), gm_w_ref[...], 0.0).astype(BF16)
    lane = lax.broadcasted_iota(I32, (CHUNK, D_GROUP), 1)
    lane_t = lax.broadcasted_iota(I32, (SUB_ROWS, D_GROUP), 1)
    win = jnp.where(lane_t < HEAD_DIM, 2,
                    jnp.where(lane_t < 2 * HEAD_DIM, 4, jnp.where(lane_t < 3 * HEAD_DIM, 8, 16)))
    bs = gm_bs_ref[...]
    mixg = mixg_ref[...]

    n_blocks = TS // SUB_ROWS
    per_block = len(scratch) // n_blocks
    blocks = [scratch[i * per_block:(i + 1) * per_block] for i in range(n_blocks)]

    @pl.when(s_idx == 0)
    def _():
        hbuf0, hbuf1, _, _, pbuf, _, _, _, cbuf, _ = blocks[0]
        hbuf0[0:HALO, :] = jnp.zeros((HALO, LANES), F32)
        hbuf1[0:HALO, :] = jnp.zeros((HALO, LANES), F32)
        pbuf[0:HALO, :] = jnp.zeros((HALO, D_GROUP), F32)
        cbuf[0:SC_HALO, :] = jnp.zeros((SC_HALO, D_GROUP), F32)

    def carry_halo(src, dst, which):
        for i in which:
            rows = SC_HALO if i == 8 else HALO
            dst[i][0:rows, :] = src[i][SUB_ROWS:SUB_ROWS + rows, :]

    st = [dict() for _ in range(n_blocks)]

    def load(bi):
        x = x_ref[0, bi * SUB_ROWS:(bi + 1) * SUB_ROWS, :]
        st[bi]["x"] = x
        st[bi]["xb"] = x.astype(BF16)

    def in_proj(bi, lo, hi):
        z = _dot(st[bi]["xb"], w_in_ref[0, :, lo * D_GROUP:hi * D_GROUP])
        for j in range(lo, hi):
            st[bi][j] = z[:, (j - lo) * D_GROUP:(j - lo + 1) * D_GROUP]

    def gating_mlp(bi):
        yn_ref = blocks[bi][9]
        u = st[bi].pop(0)
        v = _layer_norm(st[bi].pop(1), gm_g_ref[...], gm_b_ref[...])
        for n in range(SUB_ROWS // CHUNK):
            crows = slice(n * CHUNK, (n + 1) * CHUNK)
            s_all = _dot(w_tril, v[crows, :].astype(BF16))
            s_sel = s_all[3 * CHUNK:4 * CHUNK]
            for h in (2, 1, 0):
                s_sel = jnp.where(lane < (h + 1) * HEAD_DIM, s_all[h * CHUNK:(h + 1) * CHUNK], s_sel)
            y1 = u[crows, :] * (s_sel + bs)
            yn_ref[crows, 0:D_GROUP] = _rms_norm(y1, mixg[:, 0:D_GROUP]).astype(BF16)

    def conformer_glu(bi):
        if bi > 0:
            carry_halo(blocks[bi - 1], blocks[bi], (0, 1))
        glu = st[bi].pop(2) * _sigmoid(st[bi].pop(3))
        for half in range(2):
            blocks[bi][half][HALO:HALO + SUB_ROWS, :] = glu[:, half * LANES:(half + 1) * LANES]

    def conformer_conv(bi, half):
        hb, co = blocks[bi][half], blocks[bi][2 + half]
        lanes = slice(half * LANES, (half + 1) * LANES)
        bias = cf_b_ref[:, lanes]
        for q in range(SUB_ROWS // (2 * CONV_ROWS)):
            for parity in range(2):
                out0 = q * 2 * CONV_ROWS + parity
                base = out0 + HALO - (CONF_K - 1)
                acc = jnp.zeros((CONV_ROWS, LANES), F32) + bias
                for k in range(CONF_K):
                    acc = acc + cf_w_ref[k:k + 1, lanes] * hb[pl.ds(base + k, CONV_ROWS, stride=2), :]
                co[pl.ds(out0, CONV_ROWS, stride=2), :] = acc

    def conformer_out(bi):
        cbo0, cbo1, yn_ref = blocks[bi][2], blocks[bi][3], blocks[bi][9]
        hln = _layer_norm(jnp.concatenate([cbo0[...], cbo1[...]], axis=1), cf_g_ref[...], cf_beta_ref[...])
        y2 = _dot((hln * _sigmoid(hln)).astype(BF16), cf_pw_ref[0])
        yn_ref[:, D_GROUP:2 * D_GROUP] = _rms_norm(y2, mixg[:, D_GROUP:2 * D_GROUP]).astype(BF16)

    def pooling(bi):
        pbuf, s2buf, s4buf, s8buf = blocks[bi][4:8]
        yn_ref = blocks[bi][9]
        if bi > 0:
            carry_halo(blocks[bi - 1], blocks[bi], (4,))
        zc = st[bi].pop(4)
        pbuf[HALO:HALO + SUB_ROWS, :] = zc
        n2 = HALO + SUB_ROWS - 8
        s2buf[8:8 + n2, :] = pbuf[8:8 + n2, :] + pbuf[7:7 + n2, :]
        n4 = HALO + SUB_ROWS - 16
        s4buf[16:16 + n4, :] = s2buf[16:16 + n4, :] + s2buf[14:14 + n4, :]
        n8 = HALO + SUB_ROWS - 24
        s8buf[24:24 + n8, :] = s4buf[24:24 + n8, :] + s4buf[20:20 + n8, :]
        cur = slice(HALO, HALO + SUB_ROWS)
        s16 = s8buf[cur, :] + s8buf[HALO - 8:HALO - 8 + SUB_ROWS, :]
        pos1 = lax.broadcasted_iota(I32, (SUB_ROWS, D_GROUP), 0) + (s_idx * TS + bi * SUB_ROWS + 1)
        wsum = jnp.where(lane_t < HEAD_DIM, s2buf[cur, :],
                         jnp.where(lane_t < 2 * HEAD_DIM, s4buf[cur, :],
                                   jnp.where(lane_t < 3 * HEAD_DIM, s8buf[cur, :], s16)))
        count = jnp.minimum(pos1, win).astype(F32)
        pooled = wsum / count - zc
        y3 = _dot(pooled.astype(BF16), pool_w_ref[0]) * pool_scale_ref[...]
        yn_ref[:, 2 * D_GROUP:3 * D_GROUP] = _rms_norm(y3, mixg[:, 2 * D_GROUP:3 * D_GROUP]).astype(BF16)

    def short_conv(bi):
        cbuf, yn_ref = blocks[bi][8], blocks[bi][9]
        if bi > 0:
            carry_halo(blocks[bi - 1], blocks[bi], (8,))
        cbuf[SC_HALO:SC_HALO + SUB_ROWS, :] = st[bi].pop(6) * st[bi].pop(7)
        conv = jnp.zeros((SUB_ROWS, D_GROUP), F32)
        for k in range(SHORT_K):
            off = SC_HALO - (SHORT_K - 1) + k
            conv = conv + sc_w_ref[k:k + 1, :] * cbuf[off:off + SUB_ROWS, :]
        y4 = st[bi].pop(5) * conv
        yn_ref[:, 3 * D_GROUP:4 * D_GROUP] = _rms_norm(y4, mixg[:, 3 * D_GROUP:4 * D_GROUP]).astype(BF16)

    def out_proj(bi):
        m = _dot(blocks[bi][9][...], w_o_ref[0])
        x1 = _layer_norm(ALPHA * st[bi].pop("x") + m, ln_g_ref[...], ln_b_ref[...])
        x1_ref[0, bi * SUB_ROWS:(bi + 1) * SUB_ROWS, :] = x1
        st[bi]["x1"] = x1

    def router(bi):
        x1 = st[bi].pop("x1")
        hi = x1.astype(BF16)
        lo = (x1 - hi.astype(F32)).astype(BF16)
        hi_both = _dot(hi, rw_ref[...])
        logits = hi_both[:, 0:LANES] + hi_both[:, LANES:2 * LANES] + _dot(lo, rw_ref[:, 0:LANES])
        st[bi]["sel"] = logits + rb_ref[...]

    def route(bi):
        sel_t = st[bi].pop("sel").T
        cls = _route_class([sel_t[e:e + 1, :] for e in range(N_EXPERTS)])
        r0 = bi * SUB_ROWS
        cls_ref[0, r0 // LANES:(r0 + SUB_ROWS) // LANES, :] = jnp.concatenate(
            [cls[:, k * LANES:(k + 1) * LANES] for k in range(SUB_ROWS // LANES)], axis=0)

    def conformer(b):
        conformer_glu(b)
        conformer_conv(b, 0)
        conformer_conv(b, 1)
        conformer_out(b)

    def stage1(b):
        return [lambda: load(b), lambda: in_proj(b, 0, 2), lambda: in_proj(b, 2, 4), lambda: in_proj(b, 4, 5),
                lambda: in_proj(b, 5, 8)]

    def stage2(b):
        return [lambda: gating_mlp(b), lambda: conformer(b), lambda: pooling(b), lambda: short_conv(b)]

    def stage3(b):
        return [lambda: out_proj(b), lambda: router(b), lambda: route(b)]

    for t in range(n_blocks + 2):
        stages = [stage(t - lag) for lag, stage in ((2, stage3), (0, stage1), (1, stage2)) if 0 <= t - lag < n_blocks]
        for k in range(max(len(stage) for stage in stages)):
            for stage in stages:
                if k < len(stage):
                    stage[k]()

    carry_halo(blocks[-1], blocks[0], (0, 1, 4, 8))


def _plan_kernel(cls_ref, dest_ref, meta_ref):
    n = LANES
    cls = cls_ref[...]
    r = lax.broadcasted_iota(I32, (n, n), 0)
    c = lax.broadcasted_iota(I32, (n, n), 1)
    upper = jnp.where(r < c, 1.0, 0.0).astype(BF16)
    lower = jnp.where(c < r, 1.0, 0.0).astype(BF16)
    ones = jnp.ones((n, n), BF16)
    masks = [cls == k for k in range(N_CLASSES)]
    m_all = jnp.concatenate([jnp.where(mk, 1.0, 0.0) for mk in masks], axis=0).astype(BF16)
    within = _dot(m_all, upper)
    rowsum = _dot(m_all, ones)
    rs = jnp.zeros((n, n), F32)
    for k in range(N_CLASSES):
        rs = jnp.where(c == k, rowsum[k * n:(k + 1) * n], rs)
    rs_b = rs.astype(BF16)
    before = _dot(lower, rs_b)
    total = _dot(ones, rs_b)
    ntile = jnp.floor((total + (TMS - 1)) * (1.0 / TMS))
    tstart = _dot(ntile.astype(BF16), upper)
    base = before + tstart * TMS
    dest = jnp.zeros((n, n), F32)
    for k in range(N_CLASSES):
        dest = jnp.where(masks[k], within[k * n:(k + 1) * n] + base[:, k:k + 1], dest)
    dest_ref[...] = dest.astype(I32)

    tend_t = (tstart + ntile).T
    ntile_t = ntile.T
    tile = c.astype(F32)
    ended = jnp.where((r < N_CLASSES) & (tend_t <= tile), 1.0, 0.0)
    is_last = jnp.where((r < N_CLASSES) & (ntile_t > 0) & (tend_t - 1.0 == tile), 1.0, 0.0)
    tcls = jnp.minimum(jnp.sum(ended, axis=0, keepdims=True), N_CLASSES - 1.0)
    grp = (jnp.where(tcls >= N_PAIRS, 1.0, 0.0) + jnp.where(tcls >= 2 * N_PAIRS, 1.0, 0.0)
           + jnp.where(tcls >= 3 * N_PAIRS, 1.0, 0.0))
    pair = tcls - N_PAIRS * grp
    pa = jnp.where(pair >= 3, 1.0, 0.0) + jnp.where(pair >= 5, 1.0, 0.0)
    pb = jnp.where(pair == 0, 1.0, jnp.where((pair == 1) | (pair == 3), 2.0, 3.0))
    n_tiles = tstart[0:1, N_CLASSES:N_CLASSES + 1]
    tile_row = tile[0:1, :]
    active = jnp.where(tile_row < n_tiles, 1.0, 0.0)
    ragged = jnp.where((jnp.sum(is_last, axis=0, keepdims=True) > 0) | (tile_row >= n_tiles), 1.0, 0.0)
    meta = jnp.concatenate(
        [GROUP_SIZE * grp + pa, GROUP_SIZE * grp + pb, active, ragged, jnp.zeros((4, n), F32)], axis=0)
    meta_ref[...] = meta.astype(I32)


def _row_copy_wait(src, dst, sem):
    pltpu.make_async_copy(src, dst, sem).wait()


def _dispatch_kernel(dest_ref, ragged_ref, x_ref, xs_ref, buf, zbuf, sems, zsem):
    i = pl.program_id(0)
    n_steps = pl.num_programs(0)
    tile_rows = TMS * ROW_TILES
    n_tiles = xs_ref.shape[0] // tile_rows

    @pl.when(i == 0)
    def _():
        zbuf[...] = jnp.zeros(zbuf.shape, F32)
        for k in range(n_tiles):
            @pl.when(ragged_ref[k] == 1)
            def _():
                pltpu.make_async_copy(zbuf, xs_ref.at[pl.ds(k * tile_rows, tile_rows), :], zsem).start()
        for k in range(n_tiles):
            @pl.when(ragged_ref[k] == 1)
            def _():
                _row_copy_wait(zbuf, xs_ref.at[pl.ds(k * tile_rows, tile_rows), :], zsem)

    for s in range(2):
        @pl.when(i > 0)
        def _():
            _row_copy_wait(buf.at[s], xs_ref.at[pl.ds(0, TD * ROW_TILES), :], sems.at[s])

        for j in range(ROW_TILES):
            buf[s, pl.ds(j, TD, stride=ROW_TILES), :] = x_ref[s * TD:(s + 1) * TD, j * LANES:(j + 1) * LANES]
        for t in range(TD):
            d = pl.multiple_of(dest_ref[(i * 2 + s) * TD + t] * ROW_TILES, ROW_TILES)
            pltpu.make_async_copy(buf.at[s, pl.ds(t * ROW_TILES, ROW_TILES), :],
                                  xs_ref.at[pl.ds(d, ROW_TILES), :], sems.at[s]).start(priority=t % 2)

    @pl.when(i == n_steps - 1)
    def _():
        for s in range(2):
            _row_copy_wait(buf.at[s], xs_ref.at[pl.ds(0, TD * ROW_TILES), :], sems.at[s])


def _combine_kernel(dest_ref, ys_ref, x_ref, ln_g_ref, ln_b_ref, out_ref, buf, sems):
    i = pl.program_id(0)
    n_steps = pl.num_programs(0)

    def issue(step, s):
        for t in range(TD):
            d = pl.multiple_of(dest_ref[(step * 2 + s) * TD + t] * ROW_TILES, ROW_TILES)
            pltpu.make_async_copy(ys_ref.at[pl.ds(d, ROW_TILES), :],
                                  buf.at[s, pl.ds(t * ROW_TILES, ROW_TILES), :], sems.at[s]).start(priority=t % 2)

    @pl.when(i == 0)
    def _():
        for s in range(2):
            issue(0, s)

    for s in range(2):
        _row_copy_wait(ys_ref.at[pl.ds(0, TD * ROW_TILES), :], buf.at[s], sems.at[s])
        f = jnp.concatenate([buf[s, pl.ds(j, TD, stride=ROW_TILES), :] for j in range(ROW_TILES)], axis=1)
        rows = slice(s * TD, (s + 1) * TD)
        out_ref[rows, :] = _layer_norm(ALPHA * x_ref[rows, :] + f, ln_g_ref[...], ln_b_ref[...])

        @pl.when(i + 1 < n_steps)
        def _():
            issue(i + 1, s)


def _moe_kernel(ea_ref, eb_ref, act_ref, xs_ref, *refs):
    i = pl.program_id(0)
    w_refs, rwt_ref, ys_ref = refs[:-2], refs[-2], refs[-1]
    rows = TMS * ROW_TILES

    @pl.when(act_ref[i * TILES_PER_STEP] == 0)
    def _():
        ys_ref[...] = jnp.zeros(ys_ref.shape, F32)

    @pl.when(act_ref[i * TILES_PER_STEP] == 1)
    def _():
        st = [dict() for _ in range(TILES_PER_STEP)]

        def load(s):
            x = jnp.concatenate(
                [xs_ref[pl.ds(s * rows + j, TMS, stride=ROW_TILES), :] for j in range(ROW_TILES)], axis=1)
            st[s]["x"] = x
            st[s]["xb"] = x.astype(BF16)

        def up(s, e):
            wg_ref, wu_ref, _ = w_refs[6 * s + 3 * e:6 * s + 3 * e + 3]
            st[s]["g", e] = _dot(st[s]["xb"], wg_ref[0, 0])
            st[s]["u", e] = _dot(st[s]["xb"], wu_ref[0, 0])

        def act(s, e):
            g = st[s].pop(("g", e))
            st[s]["h", e] = ((g * _sigmoid(g)) * st[s].pop(("u", e))).astype(BF16)

        def down(s, e):
            st[s]["y", e] = _dot(st[s].pop(("h", e)), w_refs[6 * s + 3 * e + 2][0, 0])

        def mix(s):
            tile = i * TILES_PER_STEP + s
            x = st[s].pop("x")
            la = jnp.sum(x * rwt_ref[pl.ds(ea_ref[tile], 1), :], axis=-1, keepdims=True)
            lb = jnp.sum(x * rwt_ref[pl.ds(eb_ref[tile], 1), :], axis=-1, keepdims=True)
            m = jnp.maximum(la, lb)
            pa = jnp.exp(la - m)
            pb = jnp.exp(lb - m)
            den = pa + pb
            f = (pa / den) * st[s].pop(("y", 0)) + (pb / den) * st[s].pop(("y", 1))
            for j in range(ROW_TILES):
                ys_ref[pl.ds(s * rows + j, TMS, stride=ROW_TILES), :] = f[:, j * LANES:(j + 1) * LANES]

        units = [(s, e) for s in range(TILES_PER_STEP) for e in range(2)]
        load(0)
        up(*units[0])
        for k, (s, e) in enumerate(units):
            if k + 1 < len(units):
                nxt = units[k + 1]
                if nxt[1] == 0:
                    load(nxt[0])
                up(*nxt)
            act(s, e)
            down(s, e)
            if e == 1:
                mix(s)


def _full(shape):
    return pl.BlockSpec(shape, lambda *_: (0,) * len(shape))


def _layer_block(shape, layer):
    return pl.BlockSpec((1,) + shape, lambda *_: (layer,) + (0,) * len(shape))


def _mixer_call(x, p, layer):
    batch, seq, _ = x.shape
    steps = seq // TS
    in_specs = [
        pl.BlockSpec((1, TS, D_MODEL), lambda b, s: (b, s, 0)),
        _layer_block((D_MODEL, 8 * D_GROUP), layer),
        _full((1, D_GROUP)), _full((1, D_GROUP)),
        _full((N_HEADS * CHUNK, CHUNK)), _full((CHUNK, D_GROUP)),
        _full((HALO, D_GROUP)), _full((1, D_GROUP)), _full((1, D_GROUP)), _full((1, D_GROUP)),
        _layer_block((D_GROUP, D_GROUP), layer),
        _layer_block((D_GROUP, D_GROUP), layer), _full((1, D_GROUP)),
        _full((SUBLANES, D_GROUP)),
        _full((1, D_MODEL)), _layer_block((D_MODEL, D_MODEL), layer),
        _full((1, D_MODEL)), _full((1, D_MODEL)),
        _full((D_MODEL, 2 * LANES)), _full((1, LANES)),
    ]
    out_specs = [
        pl.BlockSpec((1, TS, D_MODEL), lambda b, s: (b, s, 0)),
        pl.BlockSpec((1, TS // LANES, LANES), lambda b, s: (b * steps + s, 0, 0)),
    ]
    block_scratch = [
        pltpu.VMEM((HALO + SUB_ROWS, LANES), F32),
        pltpu.VMEM((HALO + SUB_ROWS, LANES), F32),
        pltpu.VMEM((SUB_ROWS, LANES), F32),
        pltpu.VMEM((SUB_ROWS, LANES), F32),
        pltpu.VMEM((HALO + SUB_ROWS, D_GROUP), F32),
        pltpu.VMEM((HALO + SUB_ROWS, D_GROUP), F32),
        pltpu.VMEM((HALO + SUB_ROWS, D_GROUP), F32),
        pltpu.VMEM((HALO + SUB_ROWS, D_GROUP), F32),
        pltpu.VMEM((SC_HALO + SUB_ROWS, D_GROUP), F32),
        pltpu.VMEM((SUB_ROWS, D_MODEL), BF16),
    ]
    scratch = block_scratch * (TS // SUB_ROWS)
    return pl.pallas_call(
        _mixer_kernel,
        grid=(batch, steps),
        in_specs=in_specs,
        out_specs=out_specs,
        out_shape=[jax.ShapeDtypeStruct((batch, seq, D_MODEL), F32),
                   jax.ShapeDtypeStruct((batch * steps, TS // LANES, LANES), I32)],
        scratch_shapes=scratch,
        compiler_params=pltpu.CompilerParams(
            dimension_semantics=("arbitrary", "arbitrary"), vmem_limit_bytes=VMEM_LIMIT),
        name="mixer",
    )(x, *p)


def _plan_call(cls2d):
    return pl.pallas_call(
        _plan_kernel,
        out_shape=[jax.ShapeDtypeStruct((LANES, LANES), I32), jax.ShapeDtypeStruct((SUBLANES, LANES), I32)],
        compiler_params=pltpu.CompilerParams(vmem_limit_bytes=VMEM_LIMIT),
        name="moe_plan",
    )(cls2d)


def _dispatch_call(dest, ragged, x2d, n_sorted):
    n_tok = x2d.shape[0]
    return pl.pallas_call(
        _dispatch_kernel,
        grid_spec=pltpu.PrefetchScalarGridSpec(
            num_scalar_prefetch=2,
            grid=(n_tok // (2 * TD),),
            in_specs=[pl.BlockSpec((2 * TD, D_MODEL), lambda i, *_: (i, 0))],
            out_specs=pl.BlockSpec(memory_space=pl.ANY),
            scratch_shapes=[
                pltpu.VMEM((2, TD * ROW_TILES, LANES), F32),
                pltpu.VMEM((TMS * ROW_TILES, LANES), F32),
                pltpu.SemaphoreType.DMA((2,)),
                pltpu.SemaphoreType.DMA(()),
            ]),
        out_shape=jax.ShapeDtypeStruct((n_sorted * ROW_TILES, LANES), F32),
        compiler_params=pltpu.CompilerParams(
            dimension_semantics=("arbitrary",), vmem_limit_bytes=VMEM_LIMIT),
        name="moe_dispatch",
    )(dest, ragged, x2d)


def _combine_call(dest, ys, x2d, ln_g, ln_b):
    n_tok = x2d.shape[0]
    return pl.pallas_call(
        _combine_kernel,
        grid_spec=pltpu.PrefetchScalarGridSpec(
            num_scalar_prefetch=1,
            grid=(n_tok // (2 * TD),),
            in_specs=[pl.BlockSpec(memory_space=pl.ANY),
                      pl.BlockSpec((2 * TD, D_MODEL), lambda i, *_: (i, 0)),
                      _full((1, D_MODEL)), _full((1, D_MODEL))],
            out_specs=pl.BlockSpec((2 * TD, D_MODEL), lambda i, *_: (i, 0)),
            scratch_shapes=[
                pltpu.VMEM((2, TD * ROW_TILES, LANES), F32),
                pltpu.SemaphoreType.DMA((2,)),
            ]),
        out_shape=jax.ShapeDtypeStruct((n_tok, D_MODEL), F32),
        compiler_params=pltpu.CompilerParams(
            dimension_semantics=("arbitrary",), vmem_limit_bytes=VMEM_LIMIT),
        name="moe_combine",
    )(dest, ys, x2d, ln_g, ln_b)


def _moe_call(ea, eb, act, xs, wg, wu, wd, rwt, n_tiles, layer):
    rows = TILES_PER_STEP * TMS * ROW_TILES
    w_specs, w_args = [], []
    for s in range(TILES_PER_STEP):
        for sel in (0, 1):
            def expert(i, ea, eb, act, s=s, sel=sel):
                return (layer, (ea, eb)[sel][i * TILES_PER_STEP + s], 0, 0)
            w_specs += [pl.BlockSpec((1, 1, D_MODEL, D_EXPERT), expert),
                        pl.BlockSpec((1, 1, D_MODEL, D_EXPERT), expert),
                        pl.BlockSpec((1, 1, D_EXPERT, D_MODEL), expert)]
            w_args += [wg, wu, wd]
    tile = pl.BlockSpec((rows, LANES), lambda i, ea, eb, act: (i, 0))
    return pl.pallas_call(
        _moe_kernel,
        grid_spec=pltpu.PrefetchScalarGridSpec(
            num_scalar_prefetch=3,
            grid=(n_tiles // TILES_PER_STEP,),
            in_specs=[tile] + w_specs + [_full((N_EXPERTS, D_MODEL))],
            out_specs=tile),
        out_shape=jax.ShapeDtypeStruct(xs.shape, F32),
        compiler_params=pltpu.CompilerParams(
            dimension_semantics=("arbitrary",), vmem_limit_bytes=VMEM_LIMIT),
        name="moe_experts",
    )(ea, eb, act, xs, *w_args, rwt)


def _block_diag(w):
    g, d, _ = w.shape
    eye = jnp.eye(g, dtype=w.dtype)
    return (eye[:, None, :, None] * w[:, :, None, :]).reshape(g * d, g * d)


def _pad_rows(w, rows):
    return jnp.pad(w, ((0, rows - w.shape[0]), (0, 0)))


def kernel(x, w_in, gm_ln_g, gm_ln_b, gm_w_s, gm_b_s, cf_dw_w, cf_dw_b, cf_ln_g, cf_ln_b, cf_pw,
           pool_w, pool_scale, sc_w, mix_norm_g, w_o, ln1_g, ln1_b, router_w, router_b,
           exp_w_gate, exp_w_up, exp_w_down, ln2_g, ln2_b):
    batch, seq, _ = x.shape
    n_tok = batch * seq
    assert n_tok == LANES * LANES, "the routing plan lays tokens out as one (128, 128) tile grid"
    assert seq % TS == 0 and TS % SUB_ROWS == 0 and SUB_ROWS % (2 * CONV_ROWS) == 0 and n_tok % (2 * TD) == 0
    n_tiles = n_tok // TMS + N_CLASSES
    assert n_tiles <= LANES and n_tiles % TILES_PER_STEP == 0
    row = lambda a: a.reshape(1, -1)
    rw = jnp.pad(router_w, ((0, 0), (0, LANES - N_EXPERTS)))
    rw_hi = rw.astype(BF16)
    rw_lo = (rw - rw_hi.astype(F32)).astype(BF16)
    rb = jnp.pad(router_b, (0, LANES - N_EXPERTS)).reshape(1, LANES)
    rwt = router_w.T
    w_in_b, w_o_b, cf_pw_b = w_in.astype(BF16), w_o.astype(BF16), cf_pw.astype(BF16)
    pool_b = jax.vmap(_block_diag)(pool_w).astype(BF16)
    wg_b, wu_b, wd_b = exp_w_gate.astype(BF16), exp_w_up.astype(BF16), exp_w_down.astype(BF16)
    for l in range(DEPTH):
        params = (
            w_in_b,
            row(gm_ln_g[l]), row(gm_ln_b[l]),
            gm_w_s[l].reshape(N_HEADS * CHUNK, CHUNK),
            jnp.repeat(gm_b_s[l].T, HEAD_DIM, axis=1),
            _pad_rows(cf_dw_w[l], HALO), row(cf_dw_b[l]), row(cf_ln_g[l]), row(cf_ln_b[l]),
            cf_pw_b,
            pool_b, row(pool_scale[l]),
            _pad_rows(sc_w[l], SUBLANES),
            row(mix_norm_g[l]), w_o_b,
            row(ln1_g[l]), row(ln1_b[l]),
            jnp.concatenate([rw_hi, rw_lo], axis=1), rb,
        )
        x1, cls = _mixer_call(x, params, l)
        x1 = x1.reshape(n_tok, D_MODEL)
        dest2d, meta = _plan_call(cls.reshape(LANES, LANES))
        dest = dest2d.reshape(n_tok)
        xs = _dispatch_call(dest, meta[3], x1, n_tiles * TMS)
        ys = _moe_call(meta[0], meta[1], meta[2], xs, wg_b, wu_b, wd_b, rwt, n_tiles, l)
        x = _combine_call(dest, ys, x1, row(ln2_g[l]), row(ln2_b[l])).reshape(batch, seq, D_MODEL)
    return x
```

```python
import jax
import jax.numpy as jnp
from jax import lax
from jax.experimental import pallas as pl
from jax.experimental.pallas import tpu as pltpu

D_MODEL = 1024
DEPTH = 2
D_GROUP = 256
N_HEADS = 4
HEAD_DIM = 64
CHUNK = 128
CONF_K = 31
SHORT_K = 3
N_EXPERTS = 16
N_GROUPS = 4
GROUP_SIZE = 4
N_PAIRS = 6
N_CLASSES = N_GROUPS * N_PAIRS
D_EXPERT = 512
ALPHA = (2 * DEPTH) ** 0.25
LN_EPS = 1e-5
RMS_EPS = 1e-6

LANES = 128
SUBLANES = 8
ROW_TILES = D_MODEL // LANES
HALO = 32
SC_HALO = SUBLANES
TS = 512
SUB_ROWS = 256
CONV_ROWS = 64
TMS = 256
TILES_PER_STEP = 2
TD = 256
VMEM_LIMIT = 56 * 1024 * 1024

BF16 = jnp.bfloat16
F32 = jnp.float32
I32 = jnp.int32


def _dot(a, b):
    return jnp.dot(a, b, preferred_element_type=F32)


def _layer_norm(x, g, b):
    mu = jnp.mean(x, axis=-1, keepdims=True)
    xc = x - mu
    var = jnp.mean(xc * xc, axis=-1, keepdims=True)
    return xc * lax.rsqrt(var + LN_EPS) * g + b


def _rms_norm(y, g):
    ms = jnp.mean(y * y, axis=-1, keepdims=True)
    return y * lax.rsqrt(ms + RMS_EPS) * g


def _sigmoid(x):
    return 1.0 / (1.0 + jnp.exp(-x))


def _route_class(sel):
    scores = []
    for g in range(N_GROUPS):
        v = sel[g * GROUP_SIZE:(g + 1) * GROUP_SIZE]
        best_pair = None
        for i in range(GROUP_SIZE):
            for j in range(i + 1, GROUP_SIZE):
                p = v[i] + v[j]
                best_pair = p if best_pair is None else jnp.maximum(best_pair, p)
        scores.append(best_pair)
    best = jnp.zeros(scores[0].shape, I32)
    best_score = scores[0]
    for g in range(1, N_GROUPS):
        better = scores[g] > best_score
        best = jnp.where(better, g, best)
        best_score = jnp.where(better, scores[g], best_score)
    v = []
    for j in range(GROUP_SIZE):
        out = sel[j]
        for g in range(1, N_GROUPS):
            out = jnp.where(best == g, sel[g * GROUP_SIZE + j], out)
        v.append(out)
    i0 = jnp.zeros_like(best)
    v0 = v[0]
    for j in range(1, GROUP_SIZE):
        better = v[j] > v0
        i0 = jnp.where(better, j, i0)
        v0 = jnp.where(better, v[j], v0)
    neg = jnp.full_like(v0, -jnp.inf)
    w = [jnp.where(i0 == j, neg, v[j]) for j in range(GROUP_SIZE)]
    i1 = jnp.zeros_like(best)
    v1 = w[0]
    for j in range(1, GROUP_SIZE):
        better = w[j] > v1
        i1 = jnp.where(better, j, i1)
        v1 = jnp.where(better, w[j], v1)
    a = jnp.minimum(i0, i1)
    b = jnp.maximum(i0, i1)
    pair = jnp.where(a == 0, b - 1, jnp.where(a == 1, b + 1, N_PAIRS - 1))
    return best * N_PAIRS + pair


def _mixer_kernel(x_ref, w_in_ref, gm_g_ref, gm_b_ref, gm_w_ref, gm_bs_ref,
                  cf_w_ref, cf_b_ref, cf_g_ref, cf_beta_ref, cf_pw_ref,
                  pool_w_ref, pool_scale_ref, sc_w_ref, mixg_ref, w_o_ref,
                  ln_g_ref, ln_b_ref, rw_ref, rb_ref,
                  x1_ref, cls_ref, *scratch):
    s_idx = pl.program_id(1)

    row = lax.broadcasted_iota(I32, (N_HEADS * CHUNK, CHUNK), 0)
    col = lax.broadcasted_iota(I32, (N_HEADS * CHUNK, CHUNK), 1)
    w_tril = jnp.where(col <= (row & (CHUNK - 1)), gm_w_ref[...], 0.0).astype(BF16)
    lane = lax.broadcasted_iota(I32, (CHUNK, D_GROUP), 1)
    lane_t = lax.broadcasted_iota(I32, (SUB_ROWS, D_GROUP), 1)
    win = jnp.where(lane_t < HEAD_DIM, 2,
                    jnp.where(lane_t < 2 * HEAD_DIM, 4, jnp.where(lane_t < 3 * HEAD_DIM, 8, 16)))
    bs = gm_bs_ref[...]
    mixg = mixg_ref[...]

    n_blocks = TS // SUB_ROWS
    per_block = len(scratch) // n_blocks
    blocks = [scratch[i * per_block:(i + 1) * per_block] for i in range(n_blocks)]

    @pl.when(s_idx == 0)
    def _():
        hbuf0, hbuf1, _, _, pbuf, _, _, _, cbuf, _ = blocks[0]
        hbuf0[0:HALO, :] = jnp.zeros((HALO, LANES), F32)
        hbuf1[0:HALO, :] = jnp.zeros((HALO, LANES), F32)
        pbuf[0:HALO, :] = jnp.zeros((HALO, D_GROUP), F32)
        cbuf[0:SC_HALO, :] = jnp.zeros((SC_HALO, D_GROUP), F32)

    def carry_halo(src, dst, which):
        for i in which:
            rows = SC_HALO if i == 8 else HALO
            dst[i][0:rows, :] = src[i][SUB_ROWS:SUB_ROWS + rows, :]

    st = [dict() for _ in range(n_blocks)]

    def load(bi):
        x = x_ref[0, bi * SUB_ROWS:(bi + 1) * SUB_ROWS, :]
        st[bi]["x"] = x
        st[bi]["xb"] = x.astype(BF16)

    def in_proj(bi, lo, hi):
        z = _dot(st[bi]["xb"], w_in_ref[0, :, lo * D_GROUP:hi * D_GROUP])
        for j in range(lo, hi):
            st[bi][j] = z[:, (j - lo) * D_GROUP:(j - lo + 1) * D_GROUP]

    def gating_mlp(bi):
        yn_ref = blocks[bi][9]
        u = st[bi].pop(0)
        v = _layer_norm(st[bi].pop(1), gm_g_ref[...], gm_b_ref[...])
        for n in range(SUB_ROWS // CHUNK):
            crows = slice(n * CHUNK, (n + 1) * CHUNK)
            s_all = _dot(w_tril, v[crows, :].astype(BF16))
            s_sel = s_all[3 * CHUNK:4 * CHUNK]
            for h in (2, 1, 0):
                s_sel = jnp.where(lane < (h + 1) * HEAD_DIM, s_all[h * CHUNK:(h + 1) * CHUNK], s_sel)
            y1 = u[crows, :] * (s_sel + bs)
            yn_ref[crows, 0:D_GROUP] = _rms_norm(y1, mixg[:, 0:D_GROUP]).astype(BF16)

    def conformer_glu(bi):
        if bi > 0:
            carry_halo(blocks[bi - 1], blocks[bi], (0, 1))
        glu = st[bi].pop(2) * _sigmoid(st[bi].pop(3))
        for half in range(2):
            blocks[bi][half][HALO:HALO + SUB_ROWS, :] = glu[:, half * LANES:(half + 1) * LANES]

    def conformer_conv(bi, half):
        hb, co = blocks[bi][half], blocks[bi][2 + half]
        lanes = slice(half * LANES, (half + 1) * LANES)
        bias = cf_b_ref[:, lanes]
        for q in range(SUB_ROWS // (2 * CONV_ROWS)):
            for parity in range(2):
                out0 = q * 2 * CONV_ROWS + parity
                base = out0 + HALO - (CONF_K - 1)
                acc = jnp.zeros((CONV_ROWS, LANES), F32) + bias
                for k in range(CONF_K):
                    acc = acc + cf_w_ref[k:k + 1, lanes] * hb[pl.ds(base + k, CONV_ROWS, stride=2), :]
                co[pl.ds(out0, CONV_ROWS, stride=2), :] = acc

    def conformer_out(bi):
        cbo0, cbo1, yn_ref = blocks[bi][2], blocks[bi][3], blocks[bi][9]
        hln = _layer_norm(jnp.concatenate([cbo0[...], cbo1[...]], axis=1), cf_g_ref[...], cf_beta_ref[...])
        y2 = _dot((hln * _sigmoid(hln)).astype(BF16), cf_pw_ref[0])
        yn_ref[:, D_GROUP:2 * D_GROUP] = _rms_norm(y2, mixg[:, D_GROUP:2 * D_GROUP]).astype(BF16)

    def pooling(bi):
        pbuf, s2buf, s4buf, s8buf = blocks[bi][4:8]
        yn_ref = blocks[bi][9]
        if bi > 0:
            carry_halo(blocks[bi - 1], blocks[bi], (4,))
        zc = st[bi].pop(4)
        pbuf[HALO:HALO + SUB_ROWS, :] = zc
        n2 = HALO + SUB_ROWS - 8
        s2buf[8:8 + n2, :] = pbuf[8:8 + n2, :] + pbuf[7:7 + n2, :]
        n4 = HALO + SUB_ROWS - 16
        s4buf[16:16 + n4, :] = s2buf[16:16 + n4, :] + s2buf[14:14 + n4, :]
        n8 = HALO + SUB_ROWS - 24
        s8buf[24:24 + n8, :] = s4buf[24:24 + n8, :] + s4buf[20:20 + n8, :]
        cur = slice(HALO, HALO + SUB_ROWS)
        s16 = s8buf[cur, :] + s8buf[HALO - 8:HALO - 8 + SUB_ROWS, :]
        pos1 = lax.broadcasted_iota(I32, (SUB_ROWS, D_GROUP), 0) + (s_idx * TS + bi * SUB_ROWS + 1)
        wsum = jnp.where(lane_t < HEAD_DIM, s2buf[cur, :],
                         jnp.where(lane_t < 2 * HEAD_DIM, s4buf[cur, :],
                                   jnp.where(lane_t < 3 * HEAD_DIM, s8buf[cur, :], s16)))
        count = jnp.minimum(pos1, win).astype(F32)
        pooled = wsum / count - zc
        y3 = _dot(pooled.astype(BF16), pool_w_ref[0]) * pool_scale_ref[...]
        yn_ref[:, 2 * D_GROUP:3 * D_GROUP] = _rms_norm(y3, mixg[:, 2 * D_GROUP:3 * D_GROUP]).astype(BF16)

    def short_conv(bi):
        cbuf, yn_ref = blocks[bi][8], blocks[bi][9]
        if bi > 0:
            carry_halo(blocks[bi - 1], blocks[bi], (8,))
        cbuf[SC_HALO:SC_HALO + SUB_ROWS, :] = st[bi].pop(6) * st[bi].pop(7)
        conv = jnp.zeros((SUB_ROWS, D_GROUP), F32)
        for k in range(SHORT_K):
            off = SC_HALO - (SHORT_K - 1) + k
            conv = conv + sc_w_ref[k:k + 1, :] * cbuf[off:off + SUB_ROWS, :]
        y4 = st[bi].pop(5) * conv
        yn_ref[:, 3 * D_GROUP:4 * D_GROUP] = _rms_norm(y4, mixg[:, 3 * D_GROUP:4 * D_GROUP]).astype(BF16)

    def out_proj(bi):
        m = _dot(blocks[bi][9][...], w_o_ref[0])
        x1 = _layer_norm(ALPHA * st[bi].pop("x") + m, ln_g_ref[...], ln_b_ref[...])
        x1_ref[0, bi * SUB_ROWS:(bi + 1) * SUB_ROWS, :] = x1
        st[bi]["x1"] = x1

    def router(bi):
        x1 = st[bi].pop("x1")
        hi = x1.astype(BF16)
        lo = (x1 - hi.astype(F32)).astype(BF16)
        hi_both = _dot(hi, rw_ref[...])
        logits = hi_both[:, 0:LANES] + hi_both[:, LANES:2 * LANES] + _dot(lo, rw_ref[:, 0:LANES])
        st[bi]["sel"] = logits + rb_ref[...]

    def route(bi):
        sel_t = st[bi].pop("sel").T
        cls = _route_class([sel_t[e:e + 1, :] for e in range(N_EXPERTS)])
        r0 = bi * SUB_ROWS
        cls_ref[0, r0 // LANES:(r0 + SUB_ROWS) // LANES, :] = jnp.concatenate(
            [cls[:, k * LANES:(k + 1) * LANES] for k in range(SUB_ROWS // LANES)], axis=0)

    def conformer(b):
        conformer_glu(b)
        conformer_conv(b, 0)
        conformer_conv(b, 1)
        conformer_out(b)

    def stage1(b):
        return [lambda: load(b), lambda: in_proj(b, 0, 2), lambda: in_proj(b, 2, 4), lambda: in_proj(b, 4, 5),
                lambda: in_proj(b, 5, 8)]

    def stage2(b):
        return [lambda: gating_mlp(b), lambda: conformer(b), lambda: pooling(b), lambda: short_conv(b)]

    def stage3(b):
        return [lambda: out_proj(b), lambda: router(b), lambda: route(b)]

    for t in range(n_blocks + 2):
        stages = [stage(t - lag) for lag, stage in ((2, stage3), (0, stage1), (1, stage2)) if 0 <= t - lag < n_blocks]
        for k in range(max(len(stage) for stage in stages)):
            for stage in stages:
                if k < len(stage):
                    stage[k]()

    carry_halo(blocks[-1], blocks[0], (0, 1, 4, 8))


def _plan_kernel(cls_ref, dest_ref, meta_ref):
    n = LANES
    cls = cls_ref[...]
    r = lax.broadcasted_iota(I32, (n, n), 0)
    c = lax.broadcasted_iota(I32, (n, n), 1)
    upper = jnp.where(r < c, 1.0, 0.0).astype(BF16)
    lower = jnp.where(c < r, 1.0, 0.0).astype(BF16)
    ones = jnp.ones((n, n), BF16)
    masks = [cls == k for k in range(N_CLASSES)]
    m_all = jnp.concatenate([jnp.where(mk, 1.0, 0.0) for mk in masks], axis=0).astype(BF16)
    within = _dot(m_all, upper)
    rowsum = _dot(m_all, ones)
    rs = jnp.zeros((n, n), F32)
    for k in range(N_CLASSES):
        rs = jnp.where(c == k, rowsum[k * n:(k + 1) * n], rs)
    rs_b = rs.astype(BF16)
    before = _dot(lower, rs_b)
    total = _dot(ones, rs_b)
    ntile = jnp.floor((total + (TMS - 1)) * (1.0 / TMS))
    tstart = _dot(ntile.astype(BF16), upper)
    base = before + tstart * TMS
    dest = jnp.zeros((n, n), F32)
    for k in range(N_CLASSES):
        dest = jnp.where(masks[k], within[k * n:(k + 1) * n] + base[:, k:k + 1], dest)
    dest_ref[...] = dest.astype(I32)

    tend_t = (tstart + ntile).T
    ntile_t = ntile.T
    tile = c.astype(F32)
    ended = jnp.where((r < N_CLASSES) & (tend_t <= tile), 1.0, 0.0)
    is_last = jnp.where((r < N_CLASSES) & (ntile_t > 0) & (tend_t - 1.0 == tile), 1.0, 0.0)
    tcls = jnp.minimum(jnp.sum(ended, axis=0, keepdims=True), N_CLASSES - 1.0)
    grp = (jnp.where(tcls >= N_PAIRS, 1.0, 0.0) + jnp.where(tcls >= 2 * N_PAIRS, 1.0, 0.0)
           + jnp.where(tcls >= 3 * N_PAIRS, 1.0, 0.0))
    pair = tcls - N_PAIRS * grp
    pa = jnp.where(pair >= 3, 1.0, 0.0) + jnp.where(pair >= 5, 1.0, 0.0)
    pb = jnp.where(pair == 0, 1.0, jnp.where((pair == 1) | (pair == 3), 2.0, 3.0))
    n_tiles = tstart[0:1, N_CLASSES:N_CLASSES + 1]
    tile_row = tile[0:1, :]
    active = jnp.where(tile_row < n_tiles, 1.0, 0.0)
    ragged = jnp.where((jnp.sum(is_last, axis=0, keepdims=True) > 0) | (tile_row >= n_tiles), 1.0, 0.0)
    meta = jnp.concatenate(
        [GROUP_SIZE * grp + pa, GROUP_SIZE * grp + pb, active, ragged, jnp.zeros((4, n), F32)], axis=0)
    meta_ref[...] = meta.astype(I32)


def _row_copy_wait(src, dst, sem):
    pltpu.make_async_copy(src, dst, sem).wait()


def _dispatch_kernel(dest_ref, ragged_ref, x_ref, wg_ref, wu_ref, wd_ref,
                     xs_ref, wg_out, wu_out, wd_out, buf, zbuf, sems, zsem):
    i = pl.program_id(0)
    wg_out[0] = wg_ref[0, 0].astype(BF16)
    wu_out[0] = wu_ref[0, 0].astype(BF16)
    wd_out[0] = wd_ref[0, 0].astype(BF16)
    n_steps = pl.num_programs(0)
    tile_rows = TMS * ROW_TILES
    n_tiles = xs_ref.shape[0] // tile_rows

    @pl.when(i == 0)
    def _():
        zbuf[...] = jnp.zeros(zbuf.shape, F32)
        for k in range(n_tiles):
            @pl.when(ragged_ref[k] == 1)
            def _():
                pltpu.make_async_copy(zbuf, xs_ref.at[pl.ds(k * tile_rows, tile_rows), :], zsem).start()
        for k in range(n_tiles):
            @pl.when(ragged_ref[k] == 1)
            def _():
                _row_copy_wait(zbuf, xs_ref.at[pl.ds(k * tile_rows, tile_rows), :], zsem)

    for s in range(2):
        @pl.when(i > 0)
        def _():
            _row_copy_wait(buf.at[s], xs_ref.at[pl.ds(0, TD * ROW_TILES), :], sems.at[s])

        for j in range(ROW_TILES):
            buf[s, pl.ds(j, TD, stride=ROW_TILES), :] = x_ref[s * TD:(s + 1) * TD, j * LANES:(j + 1) * LANES]
        for t in range(TD):
            d = pl.multiple_of(dest_ref[(i * 2 + s) * TD + t] * ROW_TILES, ROW_TILES)
            pltpu.make_async_copy(buf.at[s, pl.ds(t * ROW_TILES, ROW_TILES), :],
                                  xs_ref.at[pl.ds(d, ROW_TILES), :], sems.at[s]).start(priority=t % 2)

    @pl.when(i == n_steps - 1)
    def _():
        for s in range(2):
            _row_copy_wait(buf.at[s], xs_ref.at[pl.ds(0, TD * ROW_TILES), :], sems.at[s])


def _combine_kernel(dest_ref, ys_ref, x_ref, ln_g_ref, ln_b_ref, out_ref, buf, sems):
    i = pl.program_id(0)
    n_steps = pl.num_programs(0)

    def issue(step, s):
        for t in range(TD):
            d = pl.multiple_of(dest_ref[(step * 2 + s) * TD + t] * ROW_TILES, ROW_TILES)
            pltpu.make_async_copy(ys_ref.at[pl.ds(d, ROW_TILES), :],
                                  buf.at[s, pl.ds(t * ROW_TILES, ROW_TILES), :], sems.at[s]).start(priority=t % 2)

    @pl.when(i == 0)
    def _():
        for s in range(2):
            issue(0, s)

    for s in range(2):
        _row_copy_wait(ys_ref.at[pl.ds(0, TD * ROW_TILES), :], buf.at[s], sems.at[s])
        f = jnp.concatenate([buf[s, pl.ds(j, TD, stride=ROW_TILES), :] for j in range(ROW_TILES)], axis=1)
        rows = slice(s * TD, (s + 1) * TD)
        out_ref[rows, :] = _layer_norm(ALPHA * x_ref[rows, :] + f, ln_g_ref[...], ln_b_ref[...])

        @pl.when(i + 1 < n_steps)
        def _():
            issue(i + 1, s)


def _moe_kernel(ea_ref, eb_ref, act_ref, xs_ref, *refs):
    i = pl.program_id(0)
    w_refs, rwt_ref, ys_ref = refs[:-2], refs[-2], refs[-1]
    rows = TMS * ROW_TILES

    @pl.when(act_ref[i * TILES_PER_STEP] == 0)
    def _():
        ys_ref[...] = jnp.zeros(ys_ref.shape, F32)

    @pl.when(act_ref[i * TILES_PER_STEP] == 1)
    def _():
        st = [dict() for _ in range(TILES_PER_STEP)]

        def load(s):
            x = jnp.concatenate(
                [xs_ref[pl.ds(s * rows + j, TMS, stride=ROW_TILES), :] for j in range(ROW_TILES)], axis=1)
            st[s]["x"] = x
            st[s]["xb"] = x.astype(BF16)

        def up(s, e):
            wg_ref, wu_ref, _ = w_refs[6 * s + 3 * e:6 * s + 3 * e + 3]
            st[s]["g", e] = _dot(st[s]["xb"], wg_ref[0])
            st[s]["u", e] = _dot(st[s]["xb"], wu_ref[0])

        def act(s, e):
            g = st[s].pop(("g", e))
            st[s]["h", e] = ((g * _sigmoid(g)) * st[s].pop(("u", e))).astype(BF16)

        def down(s, e):
            st[s]["y", e] = _dot(st[s].pop(("h", e)), w_refs[6 * s + 3 * e + 2][0])

        def mix(s):
            tile = i * TILES_PER_STEP + s
            x = st[s].pop("x")
            la = jnp.sum(x * rwt_ref[pl.ds(ea_ref[tile], 1), :], axis=-1, keepdims=True)
            lb = jnp.sum(x * rwt_ref[pl.ds(eb_ref[tile], 1), :], axis=-1, keepdims=True)
            m = jnp.maximum(la, lb)
            pa = jnp.exp(la - m)
            pb = jnp.exp(lb - m)
            den = pa + pb
            f = (pa / den) * st[s].pop(("y", 0)) + (pb / den) * st[s].pop(("y", 1))
            for j in range(ROW_TILES):
                ys_ref[pl.ds(s * rows + j, TMS, stride=ROW_TILES), :] = f[:, j * LANES:(j + 1) * LANES]

        units = [(s, e) for s in range(TILES_PER_STEP) for e in range(2)]
        load(0)
        up(*units[0])
        for k, (s, e) in enumerate(units):
            if k + 1 < len(units):
                nxt = units[k + 1]
                if nxt[1] == 0:
                    load(nxt[0])
                up(*nxt)
            act(s, e)
            down(s, e)
            if e == 1:
                mix(s)


def _full(shape):
    return pl.BlockSpec(shape, lambda *_: (0,) * len(shape))


def _layer_block(shape, layer):
    return pl.BlockSpec((1,) + shape, lambda *_: (layer,) + (0,) * len(shape))


def _mixer_call(x, p, layer):
    batch, seq, _ = x.shape
    steps = seq // TS
    in_specs = [
        pl.BlockSpec((1, TS, D_MODEL), lambda b, s: (b, s, 0)),
        _layer_block((D_MODEL, 8 * D_GROUP), layer),
        _full((1, D_GROUP)), _full((1, D_GROUP)),
        _full((N_HEADS * CHUNK, CHUNK)), _full((CHUNK, D_GROUP)),
        _full((HALO, D_GROUP)), _full((1, D_GROUP)), _full((1, D_GROUP)), _full((1, D_GROUP)),
        _layer_block((D_GROUP, D_GROUP), layer),
        _layer_block((D_GROUP, D_GROUP), layer), _full((1, D_GROUP)),
        _full((SUBLANES, D_GROUP)),
        _full((1, D_MODEL)), _layer_block((D_MODEL, D_MODEL), layer),
        _full((1, D_MODEL)), _full((1, D_MODEL)),
        _full((D_MODEL, 2 * LANES)), _full((1, LANES)),
    ]
    out_specs = [
        pl.BlockSpec((1, TS, D_MODEL), lambda b, s: (b, s, 0)),
        pl.BlockSpec((1, TS // LANES, LANES), lambda b, s: (b * steps + s, 0, 0)),
    ]
    block_scratch = [
        pltpu.VMEM((HALO + SUB_ROWS, LANES), F32),
        pltpu.VMEM((HALO + SUB_ROWS, LANES), F32),
        pltpu.VMEM((SUB_ROWS, LANES), F32),
        pltpu.VMEM((SUB_ROWS, LANES), F32),
        pltpu.VMEM((HALO + SUB_ROWS, D_GROUP), F32),
        pltpu.VMEM((HALO + SUB_ROWS, D_GROUP), F32),
        pltpu.VMEM((HALO + SUB_ROWS, D_GROUP), F32),
        pltpu.VMEM((HALO + SUB_ROWS, D_GROUP), F32),
        pltpu.VMEM((SC_HALO + SUB_ROWS, D_GROUP), F32),
        pltpu.VMEM((SUB_ROWS, D_MODEL), BF16),
    ]
    scratch = block_scratch * (TS // SUB_ROWS)
    return pl.pallas_call(
        _mixer_kernel,
        grid=(batch, steps),
        in_specs=in_specs,
        out_specs=out_specs,
        out_shape=[jax.ShapeDtypeStruct((batch, seq, D_MODEL), F32),
                   jax.ShapeDtypeStruct((batch * steps, TS // LANES, LANES), I32)],
        scratch_shapes=scratch,
        compiler_params=pltpu.CompilerParams(
            dimension_semantics=("arbitrary", "arbitrary"), vmem_limit_bytes=VMEM_LIMIT),
        name="mixer",
    )(x, *p)


def _plan_call(cls2d):
    return pl.pallas_call(
        _plan_kernel,
        out_shape=[jax.ShapeDtypeStruct((LANES, LANES), I32), jax.ShapeDtypeStruct((SUBLANES, LANES), I32)],
        compiler_params=pltpu.CompilerParams(vmem_limit_bytes=VMEM_LIMIT),
        name="moe_plan",
    )(cls2d)


def _dispatch_call(dest, ragged, x2d, wg, wu, wd, n_sorted, layer):
    n_tok = x2d.shape[0]
    n_steps = n_tok // (2 * TD)
    slabs = n_steps // N_EXPERTS
    assert n_steps == slabs * N_EXPERTS

    def w_in(i, *_):
        return (layer, i // slabs, i % slabs, 0)

    def w_out(i, *_):
        return (i // slabs, i % slabs, 0)

    up_rows, down_rows = D_MODEL // slabs, D_EXPERT // slabs
    return pl.pallas_call(
        _dispatch_kernel,
        grid_spec=pltpu.PrefetchScalarGridSpec(
            num_scalar_prefetch=2,
            grid=(n_steps,),
            in_specs=[pl.BlockSpec((2 * TD, D_MODEL), lambda i, *_: (i, 0)),
                      pl.BlockSpec((1, 1, up_rows, D_EXPERT), w_in),
                      pl.BlockSpec((1, 1, up_rows, D_EXPERT), w_in),
                      pl.BlockSpec((1, 1, down_rows, D_MODEL), w_in)],
            out_specs=[pl.BlockSpec(memory_space=pl.ANY),
                       pl.BlockSpec((1, up_rows, D_EXPERT), w_out),
                       pl.BlockSpec((1, up_rows, D_EXPERT), w_out),
                       pl.BlockSpec((1, down_rows, D_MODEL), w_out)],
            scratch_shapes=[
                pltpu.VMEM((2, TD * ROW_TILES, LANES), F32),
                pltpu.VMEM((TMS * ROW_TILES, LANES), F32),
                pltpu.SemaphoreType.DMA((2,)),
                pltpu.SemaphoreType.DMA(()),
            ]),
        out_shape=[jax.ShapeDtypeStruct((n_sorted * ROW_TILES, LANES), F32),
                   jax.ShapeDtypeStruct((N_EXPERTS, D_MODEL, D_EXPERT), BF16),
                   jax.ShapeDtypeStruct((N_EXPERTS, D_MODEL, D_EXPERT), BF16),
                   jax.ShapeDtypeStruct((N_EXPERTS, D_EXPERT, D_MODEL), BF16)],
        compiler_params=pltpu.CompilerParams(
            dimension_semantics=("arbitrary",), vmem_limit_bytes=VMEM_LIMIT),
        name="moe_dispatch",
    )(dest, ragged, x2d, wg, wu, wd)


def _combine_call(dest, ys, x2d, ln_g, ln_b):
    n_tok = x2d.shape[0]
    return pl.pallas_call(
        _combine_kernel,
        grid_spec=pltpu.PrefetchScalarGridSpec(
            num_scalar_prefetch=1,
            grid=(n_tok // (2 * TD),),
            in_specs=[pl.BlockSpec(memory_space=pl.ANY),
                      pl.BlockSpec((2 * TD, D_MODEL), lambda i, *_: (i, 0)),
                      _full((1, D_MODEL)), _full((1, D_MODEL))],
            out_specs=pl.BlockSpec((2 * TD, D_MODEL), lambda i, *_: (i, 0)),
            scratch_shapes=[
                pltpu.VMEM((2, TD * ROW_TILES, LANES), F32),
                pltpu.SemaphoreType.DMA((2,)),
            ]),
        out_shape=jax.ShapeDtypeStruct((n_tok, D_MODEL), F32),
        compiler_params=pltpu.CompilerParams(
            dimension_semantics=("arbitrary",), vmem_limit_bytes=VMEM_LIMIT),
        name="moe_combine",
    )(dest, ys, x2d, ln_g, ln_b)


def _moe_call(ea, eb, act, xs, wg, wu, wd, rwt, n_tiles):
    rows = TILES_PER_STEP * TMS * ROW_TILES
    w_specs, w_args = [], []
    for s in range(TILES_PER_STEP):
        for sel in (0, 1):
            def expert(i, ea, eb, act, s=s, sel=sel):
                return ((ea, eb)[sel][i * TILES_PER_STEP + s], 0, 0)
            w_specs += [pl.BlockSpec((1, D_MODEL, D_EXPERT), expert),
                        pl.BlockSpec((1, D_MODEL, D_EXPERT), expert),
                        pl.BlockSpec((1, D_EXPERT, D_MODEL), expert)]
            w_args += [wg, wu, wd]
    tile = pl.BlockSpec((rows, LANES), lambda i, ea, eb, act: (i, 0))
    return pl.pallas_call(
        _moe_kernel,
        grid_spec=pltpu.PrefetchScalarGridSpec(
            num_scalar_prefetch=3,
            grid=(n_tiles // TILES_PER_STEP,),
            in_specs=[tile] + w_specs + [_full((N_EXPERTS, D_MODEL))],
            out_specs=tile),
        out_shape=jax.ShapeDtypeStruct(xs.shape, F32),
        compiler_params=pltpu.CompilerParams(
            dimension_semantics=("arbitrary",), vmem_limit_bytes=VMEM_LIMIT),
        name="moe_experts",
    )(ea, eb, act, xs, *w_args, rwt)


def _block_diag(w):
    g, d, _ = w.shape
    eye = jnp.eye(g, dtype=w.dtype)
    return (eye[:, None, :, None] * w[:, :, None, :]).reshape(g * d, g * d)


def _pad_rows(w, rows):
    return jnp.pad(w, ((0, rows - w.shape[0]), (0, 0)))


def kernel(x, w_in, gm_ln_g, gm_ln_b, gm_w_s, gm_b_s, cf_dw_w, cf_dw_b, cf_ln_g, cf_ln_b, cf_pw,
           pool_w, pool_scale, sc_w, mix_norm_g, w_o, ln1_g, ln1_b, router_w, router_b,
           exp_w_gate, exp_w_up, exp_w_down, ln2_g, ln2_b):
    batch, seq, _ = x.shape
    n_tok = batch * seq
    assert n_tok == LANES * LANES, "the routing plan lays tokens out as one (128, 128) tile grid"
    assert seq % TS == 0 and TS % SUB_ROWS == 0 and SUB_ROWS % (2 * CONV_ROWS) == 0 and n_tok % (2 * TD) == 0
    n_tiles = n_tok // TMS + N_CLASSES
    assert n_tiles <= LANES and n_tiles % TILES_PER_STEP == 0
    row = lambda a: a.reshape(1, -1)
    rw = jnp.pad(router_w, ((0, 0), (0, LANES - N_EXPERTS)))
    rw_hi = rw.astype(BF16)
    rw_lo = (rw - rw_hi.astype(F32)).astype(BF16)
    rb = jnp.pad(router_b, (0, LANES - N_EXPERTS)).reshape(1, LANES)
    rwt = router_w.T
    w_in_b, w_o_b, cf_pw_b = w_in.astype(BF16), w_o.astype(BF16), cf_pw.astype(BF16)
    pool_b = jax.vmap(_block_diag)(pool_w).astype(BF16)
    for l in range(DEPTH):
        params = (
            w_in_b,
            row(gm_ln_g[l]), row(gm_ln_b[l]),
            gm_w_s[l].reshape(N_HEADS * CHUNK, CHUNK),
            jnp.repeat(gm_b_s[l].T, HEAD_DIM, axis=1),
            _pad_rows(cf_dw_w[l], HALO), row(cf_dw_b[l]), row(cf_ln_g[l]), row(cf_ln_b[l]),
            cf_pw_b,
            pool_b, row(pool_scale[l]),
            _pad_rows(sc_w[l], SUBLANES),
            row(mix_norm_g[l]), w_o_b,
            row(ln1_g[l]), row(ln1_b[l]),
            jnp.concatenate([rw_hi, rw_lo], axis=1), rb,
        )
        x1, cls = _mixer_call(x, params, l)
        x1 = x1.reshape(n_tok, D_MODEL)
        dest2d, meta = _plan_call(cls.reshape(LANES, LANES))
        dest = dest2d.reshape(n_tok)
        xs, wg_b, wu_b, wd_b = _dispatch_call(dest, meta[3], x1, exp_w_gate, exp_w_up, exp_w_down, n_tiles * TMS, l)
        ys = _moe_call(meta[0], meta[1], meta[2], xs, wg_b, wu_b, wd_b, rwt, n_tiles)
        x = _combine_call(dest, ys, x1, row(ln2_g[l]), row(ln2_b[l])).reshape(batch, seq, D_MODEL)
    return x
```

```python
import jax
import jax.numpy as jnp
from jax import lax
from jax.experimental import pallas as pl
from jax.experimental.pallas import tpu as pltpu

D_MODEL = 1024
DEPTH = 2
D_GROUP = 256
N_HEADS = 4
HEAD_DIM = 64
CHUNK = 128
CONF_K = 31
SHORT_K = 3
N_EXPERTS = 16
N_GROUPS = 4
GROUP_SIZE = 4
N_PAIRS = 6
N_CLASSES = N_GROUPS * N_PAIRS
D_EXPERT = 512
ALPHA = (2 * DEPTH) ** 0.25
LN_EPS = 1e-5
RMS_EPS = 1e-6

LANES = 128
SUBLANES = 8
ROW_TILES = D_MODEL // LANES
HALO = 32
SC_HALO = SUBLANES
TS = 512
SUB_ROWS = 256
CONV_ROWS = 64
TMS = 256
TILES_PER_STEP = 2
TD = 256
VMEM_LIMIT = 56 * 1024 * 1024

BF16 = jnp.bfloat16
F32 = jnp.float32
I32 = jnp.int32


def _dot(a, b):
    return jnp.dot(a, b, preferred_element_type=F32)


def _layer_norm(x, g, b):
    mu = jnp.mean(x, axis=-1, keepdims=True)
    xc = x - mu
    var = jnp.mean(xc * xc, axis=-1, keepdims=True)
    return xc * lax.rsqrt(var + LN_EPS) * g + b


def _rms_norm(y, g):
    ms = jnp.mean(y * y, axis=-1, keepdims=True)
    return y * lax.rsqrt(ms + RMS_EPS) * g


def _sigmoid(x):
    return 1.0 / (1.0 + jnp.exp(-x))


def _route_class(sel):
    scores = []
    for g in range(N_GROUPS):
        v = sel[g * GROUP_SIZE:(g + 1) * GROUP_SIZE]
        best_pair = None
        for i in range(GROUP_SIZE):
            for j in range(i + 1, GROUP_SIZE):
                p = v[i] + v[j]
                best_pair = p if best_pair is None else jnp.maximum(best_pair, p)
        scores.append(best_pair)
    best = jnp.zeros(scores[0].shape, I32)
    best_score = scores[0]
    for g in range(1, N_GROUPS):
        better = scores[g] > best_score
        best = jnp.where(better, g, best)
        best_score = jnp.where(better, scores[g], best_score)
    v = []
    for j in range(GROUP_SIZE):
        out = sel[j]
        for g in range(1, N_GROUPS):
            out = jnp.where(best == g, sel[g * GROUP_SIZE + j], out)
        v.append(out)
    i0 = jnp.zeros_like(best)
    v0 = v[0]
    for j in range(1, GROUP_SIZE):
        better = v[j] > v0
        i0 = jnp.where(better, j, i0)
        v0 = jnp.where(better, v[j], v0)
    neg = jnp.full_like(v0, -jnp.inf)
    w = [jnp.where(i0 == j, neg, v[j]) for j in range(GROUP_SIZE)]
    i1 = jnp.zeros_like(best)
    v1 = w[0]
    for j in range(1, GROUP_SIZE):
        better = w[j] > v1
        i1 = jnp.where(better, j, i1)
        v1 = jnp.where(better, w[j], v1)
    a = jnp.minimum(i0, i1)
    b = jnp.maximum(i0, i1)
    pair = jnp.where(a == 0, b - 1, jnp.where(a == 1, b + 1, N_PAIRS - 1))
    return best * N_PAIRS + pair


def _mixer_kernel(x_ref, w_in_ref, gm_g_ref, gm_b_ref, gm_w_ref, gm_bs_ref,
                  cf_w_ref, cf_b_ref, cf_g_ref, cf_beta_ref, cf_pw_ref,
                  pool_w_ref, pool_scale_ref, sc_w_ref, mixg_ref, w_o_ref,
                  ln_g_ref, ln_b_ref, rw_ref, rb_ref, wg_ref, wu_ref, wd_ref,
                  x1_ref, cls_ref, wg_out, wu_out, wd_out, *scratch):
    s_idx = pl.program_id(1)
    wg_out[0] = wg_ref[0, 0].astype(BF16)
    wu_out[0] = wu_ref[0, 0].astype(BF16)
    wd_out[0] = wd_ref[0, 0].astype(BF16)

    row = lax.broadcasted_iota(I32, (N_HEADS * CHUNK, CHUNK), 0)
    col = lax.broadcasted_iota(I32, (N_HEADS * CHUNK, CHUNK), 1)
    w_tril = jnp.where(col <= (row & (CHUNK - 1)), gm_w_ref[...], 0.0).astype(BF16)
    lane = lax.broadcasted_iota(I32, (CHUNK, D_GROUP), 1)
    lane_t = lax.broadcasted_iota(I32, (SUB_ROWS, D_GROUP), 1)
    win = jnp.where(lane_t < HEAD_DIM, 2,
                    jnp.where(lane_t < 2 * HEAD_DIM, 4, jnp.where(lane_t < 3 * HEAD_DIM, 8, 16)))
    bs = gm_bs_ref[...]
    mixg = mixg_ref[...]

    n_blocks = TS // SUB_ROWS
    per_block = len(scratch) // n_blocks
    blocks = [scratch[i * per_block:(i + 1) * per_block] for i in range(n_blocks)]

    @pl.when(s_idx == 0)
    def _():
        hbuf0, hbuf1, _, _, pbuf, _, _, _, cbuf, _ = blocks[0]
        hbuf0[0:HALO, :] = jnp.zeros((HALO, LANES), F32)
        hbuf1[0:HALO, :] = jnp.zeros((HALO, LANES), F32)
        pbuf[0:HALO, :] = jnp.zeros((HALO, D_GROUP), F32)
        cbuf[0:SC_HALO, :] = jnp.zeros((SC_HALO, D_GROUP), F32)

    def carry_halo(src, dst, which):
        for i in which:
            rows = SC_HALO if i == 8 else HALO
            dst[i][0:rows, :] = src[i][SUB_ROWS:SUB_ROWS + rows, :]

    st = [dict() for _ in range(n_blocks)]

    def load(bi):
        x = x_ref[0, bi * SUB_ROWS:(bi + 1) * SUB_ROWS, :]
        st[bi]["x"] = x
        st[bi]["xb"] = x.astype(BF16)

    def in_proj(bi, lo, hi):
        z = _dot(st[bi]["xb"], w_in_ref[0, :, lo * D_GROUP:hi * D_GROUP])
        for j in range(lo, hi):
            st[bi][j] = z[:, (j - lo) * D_GROUP:(j - lo + 1) * D_GROUP]

    def gating_mlp(bi):
        yn_ref = blocks[bi][9]
        u = st[bi].pop(0)
        v = _layer_norm(st[bi].pop(1), gm_g_ref[...], gm_b_ref[...])
        for n in range(SUB_ROWS // CHUNK):
            crows = slice(n * CHUNK, (n + 1) * CHUNK)
            s_all = _dot(w_tril, v[crows, :].astype(BF16))
            s_sel = s_all[3 * CHUNK:4 * CHUNK]
            for h in (2, 1, 0):
                s_sel = jnp.where(lane < (h + 1) * HEAD_DIM, s_all[h * CHUNK:(h + 1) * CHUNK], s_sel)
            y1 = u[crows, :] * (s_sel + bs)
            yn_ref[crows, 0:D_GROUP] = _rms_norm(y1, mixg[:, 0:D_GROUP]).astype(BF16)

    def conformer_glu(bi):
        if bi > 0:
            carry_halo(blocks[bi - 1], blocks[bi], (0, 1))
        glu = st[bi].pop(2) * _sigmoid(st[bi].pop(3))
        for half in range(2):
            blocks[bi][half][HALO:HALO + SUB_ROWS, :] = glu[:, half * LANES:(half + 1) * LANES]

    def conformer_conv(bi, half):
        hb, co = blocks[bi][half], blocks[bi][2 + half]
        lanes = slice(half * LANES, (half + 1) * LANES)
        bias = cf_b_ref[:, lanes]
        for q in range(SUB_ROWS // (2 * CONV_ROWS)):
            for parity in range(2):
                out0 = q * 2 * CONV_ROWS + parity
                base = out0 + HALO - (CONF_K - 1)
                acc = jnp.zeros((CONV_ROWS, LANES), F32) + bias
                for k in range(CONF_K):
                    acc = acc + cf_w_ref[k:k + 1, lanes] * hb[pl.ds(base + k, CONV_ROWS, stride=2), :]
                co[pl.ds(out0, CONV_ROWS, stride=2), :] = acc

    def conformer_out(bi):
        cbo0, cbo1, yn_ref = blocks[bi][2], blocks[bi][3], blocks[bi][9]
        hln = _layer_norm(jnp.concatenate([cbo0[...], cbo1[...]], axis=1), cf_g_ref[...], cf_beta_ref[...])
        y2 = _dot((hln * _sigmoid(hln)).astype(BF16), cf_pw_ref[0])
        yn_ref[:, D_GROUP:2 * D_GROUP] = _rms_norm(y2, mixg[:, D_GROUP:2 * D_GROUP]).astype(BF16)

    def pooling(bi):
        pbuf, s2buf, s4buf, s8buf = blocks[bi][4:8]
        yn_ref = blocks[bi][9]
        if bi > 0:
            carry_halo(blocks[bi - 1], blocks[bi], (4,))
        zc = st[bi].pop(4)
        pbuf[HALO:HALO + SUB_ROWS, :] = zc
        n2 = HALO + SUB_ROWS - 8
        s2buf[8:8 + n2, :] = pbuf[8:8 + n2, :] + pbuf[7:7 + n2, :]
        n4 = HALO + SUB_ROWS - 16
        s4buf[16:16 + n4, :] = s2buf[16:16 + n4, :] + s2buf[14:14 + n4, :]
        n8 = HALO + SUB_ROWS - 24
        s8buf[24:24 + n8, :] = s4buf[24:24 + n8, :] + s4buf[20:20 + n8, :]
        cur = slice(HALO, HALO + SUB_ROWS)
        s16 = s8buf[cur, :] + s8buf[HALO - 8:HALO - 8 + SUB_ROWS, :]
        pos1 = lax.broadcasted_iota(I32, (SUB_ROWS, D_GROUP), 0) + (s_idx * TS + bi * SUB_ROWS + 1)
        wsum = jnp.where(lane_t < HEAD_DIM, s2buf[cur, :],
                         jnp.where(lane_t < 2 * HEAD_DIM, s4buf[cur, :],
                                   jnp.where(lane_t < 3 * HEAD_DIM, s8buf[cur, :], s16)))
        count = jnp.minimum(pos1, win).astype(F32)
        pooled = wsum / count - zc
        y3 = _dot(pooled.astype(BF16), pool_w_ref[0]) * pool_scale_ref[...]
        yn_ref[:, 2 * D_GROUP:3 * D_GROUP] = _rms_norm(y3, mixg[:, 2 * D_GROUP:3 * D_GROUP]).astype(BF16)

    def short_conv(bi):
        cbuf, yn_ref = blocks[bi][8], blocks[bi][9]
        if bi > 0:
            carry_halo(blocks[bi - 1], blocks[bi], (8,))
        cbuf[SC_HALO:SC_HALO + SUB_ROWS, :] = st[bi].pop(6) * st[bi].pop(7)
        conv = jnp.zeros((SUB_ROWS, D_GROUP), F32)
        for k in range(SHORT_K):
            off = SC_HALO - (SHORT_K - 1) + k
            conv = conv + sc_w_ref[k:k + 1, :] * cbuf[off:off + SUB_ROWS, :]
        y4 = st[bi].pop(5) * conv
        yn_ref[:, 3 * D_GROUP:4 * D_GROUP] = _rms_norm(y4, mixg[:, 3 * D_GROUP:4 * D_GROUP]).astype(BF16)

    def out_proj(bi):
        m = _dot(blocks[bi][9][...], w_o_ref[0])
        x1 = _layer_norm(ALPHA * st[bi].pop("x") + m, ln_g_ref[...], ln_b_ref[...])
        x1_ref[0, bi * SUB_ROWS:(bi + 1) * SUB_ROWS, :] = x1
        st[bi]["x1"] = x1

    def router(bi):
        x1 = st[bi].pop("x1")
        hi = x1.astype(BF16)
        lo = (x1 - hi.astype(F32)).astype(BF16)
        hi_both = _dot(hi, rw_ref[...])
        logits = hi_both[:, 0:LANES] + hi_both[:, LANES:2 * LANES] + _dot(lo, rw_ref[:, 0:LANES])
        st[bi]["sel"] = logits + rb_ref[...]

    def route(bi):
        sel_t = st[bi].pop("sel").T
        cls = _route_class([sel_t[e:e + 1, :] for e in range(N_EXPERTS)])
        r0 = bi * SUB_ROWS
        cls_ref[0, r0 // LANES:(r0 + SUB_ROWS) // LANES, :] = jnp.concatenate(
            [cls[:, k * LANES:(k + 1) * LANES] for k in range(SUB_ROWS // LANES)], axis=0)

    def conformer(b):
        conformer_glu(b)
        conformer_conv(b, 0)
        conformer_conv(b, 1)
        conformer_out(b)

    def stage1(b):
        return [lambda: load(b), lambda: in_proj(b, 0, 2), lambda: in_proj(b, 2, 4), lambda: in_proj(b, 4, 5),
                lambda: in_proj(b, 5, 8)]

    def stage2(b):
        return [lambda: gating_mlp(b), lambda: conformer(b), lambda: pooling(b), lambda: short_conv(b)]

    def stage3(b):
        return [lambda: out_proj(b), lambda: router(b), lambda: route(b)]

    for t in range(n_blocks + 2):
        stages = [stage(t - lag) for lag, stage in ((2, stage3), (0, stage1), (1, stage2)) if 0 <= t - lag < n_blocks]
        for k in range(max(len(stage) for stage in stages)):
            for stage in stages:
                if k < len(stage):
                    stage[k]()

    carry_halo(blocks[-1], blocks[0], (0, 1, 4, 8))


def _plan_kernel(cls_ref, dest_ref, meta_ref):
    n = LANES
    cls = cls_ref[...]
    r = lax.broadcasted_iota(I32, (n, n), 0)
    c = lax.broadcasted_iota(I32, (n, n), 1)
    upper = jnp.where(r < c, 1.0, 0.0).astype(BF16)
    lower = jnp.where(c < r, 1.0, 0.0).astype(BF16)
    ones = jnp.ones((n, n), BF16)
    masks = [cls == k for k in range(N_CLASSES)]
    m_all = jnp.concatenate([jnp.where(mk, 1.0, 0.0) for mk in masks], axis=0).astype(BF16)
    within = _dot(m_all, upper)
    rowsum = _dot(m_all, ones)
    rs = jnp.zeros((n, n), F32)
    for k in range(N_CLASSES):
        rs = jnp.where(c == k, rowsum[k * n:(k + 1) * n], rs)
    rs_b = rs.astype(BF16)
    before = _dot(lower, rs_b)
    total = _dot(ones, rs_b)
    ntile = jnp.floor((total + (TMS - 1)) * (1.0 / TMS))
    tstart = _dot(ntile.astype(BF16), upper)
    base = before + tstart * TMS
    dest = jnp.zeros((n, n), F32)
    for k in range(N_CLASSES):
        dest = jnp.where(masks[k], within[k * n:(k + 1) * n] + base[:, k:k + 1], dest)
    dest_ref[...] = dest.astype(I32)

    tend_t = (tstart + ntile).T
    ntile_t = ntile.T
    tile = c.astype(F32)
    ended = jnp.where((r < N_CLASSES) & (tend_t <= tile), 1.0, 0.0)
    is_last = jnp.where((r < N_CLASSES) & (ntile_t > 0) & (tend_t - 1.0 == tile), 1.0, 0.0)
    tcls = jnp.minimum(jnp.sum(ended, axis=0, keepdims=True), N_CLASSES - 1.0)
    grp = (jnp.where(tcls >= N_PAIRS, 1.0, 0.0) + jnp.where(tcls >= 2 * N_PAIRS, 1.0, 0.0)
           + jnp.where(tcls >= 3 * N_PAIRS, 1.0, 0.0))
    pair = tcls - N_PAIRS * grp
    pa = jnp.where(pair >= 3, 1.0, 0.0) + jnp.where(pair >= 5, 1.0, 0.0)
    pb = jnp.where(pair == 0, 1.0, jnp.where((pair == 1) | (pair == 3), 2.0, 3.0))
    n_tiles = tstart[0:1, N_CLASSES:N_CLASSES + 1]
    tile_row = tile[0:1, :]
    active = jnp.where(tile_row < n_tiles, 1.0, 0.0)
    ragged = jnp.where((jnp.sum(is_last, axis=0, keepdims=True) > 0) | (tile_row >= n_tiles), 1.0, 0.0)
    meta = jnp.concatenate(
        [GROUP_SIZE * grp + pa, GROUP_SIZE * grp + pb, active, ragged, jnp.zeros((4, n), F32)], axis=0)
    meta_ref[...] = meta.astype(I32)


def _row_copy_wait(src, dst, sem):
    pltpu.make_async_copy(src, dst, sem).wait()


def _dispatch_kernel(dest_ref, ragged_ref, x_ref, xs_ref, buf, zbuf, sems, zsem):
    i = pl.program_id(0)
    n_steps = pl.num_programs(0)
    tile_rows = TMS * ROW_TILES
    n_tiles = xs_ref.shape[0] // tile_rows

    @pl.when(i == 0)
    def _():
        zbuf[...] = jnp.zeros(zbuf.shape, F32)
        for k in range(n_tiles):
            @pl.when(ragged_ref[k] == 1)
            def _():
                pltpu.make_async_copy(zbuf, xs_ref.at[pl.ds(k * tile_rows, tile_rows), :], zsem).start()
        for k in range(n_tiles):
            @pl.when(ragged_ref[k] == 1)
            def _():
                _row_copy_wait(zbuf, xs_ref.at[pl.ds(k * tile_rows, tile_rows), :], zsem)

    for s in range(2):
        @pl.when(i > 0)
        def _():
            _row_copy_wait(buf.at[s], xs_ref.at[pl.ds(0, TD * ROW_TILES), :], sems.at[s])

        for j in range(ROW_TILES):
            buf[s, pl.ds(j, TD, stride=ROW_TILES), :] = x_ref[s * TD:(s + 1) * TD, j * LANES:(j + 1) * LANES]
        for t in range(TD):
            d = pl.multiple_of(dest_ref[(i * 2 + s) * TD + t] * ROW_TILES, ROW_TILES)
            pltpu.make_async_copy(buf.at[s, pl.ds(t * ROW_TILES, ROW_TILES), :],
                                  xs_ref.at[pl.ds(d, ROW_TILES), :], sems.at[s]).start(priority=t % 2)

    @pl.when(i == n_steps - 1)
    def _():
        for s in range(2):
            _row_copy_wait(buf.at[s], xs_ref.at[pl.ds(0, TD * ROW_TILES), :], sems.at[s])


def _combine_kernel(dest_ref, ys_ref, x_ref, ln_g_ref, ln_b_ref, out_ref, buf, sems):
    i = pl.program_id(0)
    n_steps = pl.num_programs(0)

    def issue(step, s):
        for t in range(TD):
            d = pl.multiple_of(dest_ref[(step * 2 + s) * TD + t] * ROW_TILES, ROW_TILES)
            pltpu.make_async_copy(ys_ref.at[pl.ds(d, ROW_TILES), :],
                                  buf.at[s, pl.ds(t * ROW_TILES, ROW_TILES), :], sems.at[s]).start(priority=t % 2)

    @pl.when(i == 0)
    def _():
        for s in range(2):
            issue(0, s)

    for s in range(2):
        _row_copy_wait(ys_ref.at[pl.ds(0, TD * ROW_TILES), :], buf.at[s], sems.at[s])
        f = jnp.concatenate([buf[s, pl.ds(j, TD, stride=ROW_TILES), :] for j in range(ROW_TILES)], axis=1)
        rows = slice(s * TD, (s + 1) * TD)
        out_ref[rows, :] = _layer_norm(ALPHA * x_ref[rows, :] + f, ln_g_ref[...], ln_b_ref[...])

        @pl.when(i + 1 < n_steps)
        def _():
            issue(i + 1, s)


def _moe_kernel(ea_ref, eb_ref, act_ref, xs_ref, *refs):
    i = pl.program_id(0)
    w_refs, rwt_ref, ys_ref = refs[:-2], refs[-2], refs[-1]
    rows = TMS * ROW_TILES

    @pl.when(act_ref[i * TILES_PER_STEP] == 0)
    def _():
        ys_ref[...] = jnp.zeros(ys_ref.shape, F32)

    @pl.when(act_ref[i * TILES_PER_STEP] == 1)
    def _():
        st = [dict() for _ in range(TILES_PER_STEP)]

        def load(s):
            x = jnp.concatenate(
                [xs_ref[pl.ds(s * rows + j, TMS, stride=ROW_TILES), :] for j in range(ROW_TILES)], axis=1)
            st[s]["x"] = x
            st[s]["xb"] = x.astype(BF16)

        def up(s, e):
            wg_ref, wu_ref, _ = w_refs[6 * s + 3 * e:6 * s + 3 * e + 3]
            st[s]["g", e] = _dot(st[s]["xb"], wg_ref[0])
            st[s]["u", e] = _dot(st[s]["xb"], wu_ref[0])

        def act(s, e):
            g = st[s].pop(("g", e))
            st[s]["h", e] = ((g * _sigmoid(g)) * st[s].pop(("u", e))).astype(BF16)

        def down(s, e):
            st[s]["y", e] = _dot(st[s].pop(("h", e)), w_refs[6 * s + 3 * e + 2][0])

        def mix(s):
            tile = i * TILES_PER_STEP + s
            x = st[s].pop("x")
            la = jnp.sum(x * rwt_ref[pl.ds(ea_ref[tile], 1), :], axis=-1, keepdims=True)
            lb = jnp.sum(x * rwt_ref[pl.ds(eb_ref[tile], 1), :], axis=-1, keepdims=True)
            m = jnp.maximum(la, lb)
            pa = jnp.exp(la - m)
            pb = jnp.exp(lb - m)
            den = pa + pb
            f = (pa / den) * st[s].pop(("y", 0)) + (pb / den) * st[s].pop(("y", 1))
            for j in range(ROW_TILES):
                ys_ref[pl.ds(s * rows + j, TMS, stride=ROW_TILES), :] = f[:, j * LANES:(j + 1) * LANES]

        units = [(s, e) for s in range(TILES_PER_STEP) for e in range(2)]
        load(0)
        up(*units[0])
        for k, (s, e) in enumerate(units):
            if k + 1 < len(units):
                nxt = units[k + 1]
                if nxt[1] == 0:
                    load(nxt[0])
                up(*nxt)
            act(s, e)
            down(s, e)
            if e == 1:
                mix(s)


def _full(shape):
    return pl.BlockSpec(shape, lambda *_: (0,) * len(shape))


def _layer_block(shape, layer):
    return pl.BlockSpec((1,) + shape, lambda *_: (layer,) + (0,) * len(shape))


def _mixer_call(x, p, expert_w, layer):
    batch, seq, _ = x.shape
    steps = seq // TS
    slabs = batch * steps // N_EXPERTS
    assert batch * steps == slabs * N_EXPERTS

    def w_in(b, s):
        i = b * steps + s
        return (layer, i // slabs, i % slabs, 0)

    def w_out(b, s):
        i = b * steps + s
        return (i // slabs, i % slabs, 0)

    up_rows, down_rows = D_MODEL // slabs, D_EXPERT // slabs
    in_specs = [
        pl.BlockSpec((1, TS, D_MODEL), lambda b, s: (b, s, 0)),
        _layer_block((D_MODEL, 8 * D_GROUP), layer),
        _full((1, D_GROUP)), _full((1, D_GROUP)),
        _full((N_HEADS * CHUNK, CHUNK)), _full((CHUNK, D_GROUP)),
        _full((HALO, D_GROUP)), _full((1, D_GROUP)), _full((1, D_GROUP)), _full((1, D_GROUP)),
        _layer_block((D_GROUP, D_GROUP), layer),
        _layer_block((D_GROUP, D_GROUP), layer), _full((1, D_GROUP)),
        _full((SUBLANES, D_GROUP)),
        _full((1, D_MODEL)), _layer_block((D_MODEL, D_MODEL), layer),
        _full((1, D_MODEL)), _full((1, D_MODEL)),
        _full((D_MODEL, 2 * LANES)), _full((1, LANES)),
        pl.BlockSpec((1, 1, up_rows, D_EXPERT), w_in),
        pl.BlockSpec((1, 1, up_rows, D_EXPERT), w_in),
        pl.BlockSpec((1, 1, down_rows, D_MODEL), w_in),
    ]
    out_specs = [
        pl.BlockSpec((1, TS, D_MODEL), lambda b, s: (b, s, 0)),
        pl.BlockSpec((1, TS // LANES, LANES), lambda b, s: (b * steps + s, 0, 0)),
        pl.BlockSpec((1, up_rows, D_EXPERT), w_out),
        pl.BlockSpec((1, up_rows, D_EXPERT), w_out),
        pl.BlockSpec((1, down_rows, D_MODEL), w_out),
    ]
    block_scratch = [
        pltpu.VMEM((HALO + SUB_ROWS, LANES), F32),
        pltpu.VMEM((HALO + SUB_ROWS, LANES), F32),
        pltpu.VMEM((SUB_ROWS, LANES), F32),
        pltpu.VMEM((SUB_ROWS, LANES), F32),
        pltpu.VMEM((HALO + SUB_ROWS, D_GROUP), F32),
        pltpu.VMEM((HALO + SUB_ROWS, D_GROUP), F32),
        pltpu.VMEM((HALO + SUB_ROWS, D_GROUP), F32),
        pltpu.VMEM((HALO + SUB_ROWS, D_GROUP), F32),
        pltpu.VMEM((SC_HALO + SUB_ROWS, D_GROUP), F32),
        pltpu.VMEM((SUB_ROWS, D_MODEL), BF16),
    ]
    scratch = block_scratch * (TS // SUB_ROWS)
    return pl.pallas_call(
        _mixer_kernel,
        grid=(batch, steps),
        in_specs=in_specs,
        out_specs=out_specs,
        out_shape=[jax.ShapeDtypeStruct((batch, seq, D_MODEL), F32),
                   jax.ShapeDtypeStruct((batch * steps, TS // LANES, LANES), I32),
                   jax.ShapeDtypeStruct((N_EXPERTS, D_MODEL, D_EXPERT), BF16),
                   jax.ShapeDtypeStruct((N_EXPERTS, D_MODEL, D_EXPERT), BF16),
                   jax.ShapeDtypeStruct((N_EXPERTS, D_EXPERT, D_MODEL), BF16)],
        scratch_shapes=scratch,
        compiler_params=pltpu.CompilerParams(
            dimension_semantics=("arbitrary", "arbitrary"), vmem_limit_bytes=VMEM_LIMIT),
        name="mixer",
    )(x, *p, *expert_w)


def _plan_call(cls2d):
    return pl.pallas_call(
        _plan_kernel,
        out_shape=[jax.ShapeDtypeStruct((LANES, LANES), I32), jax.ShapeDtypeStruct((SUBLANES, LANES), I32)],
        compiler_params=pltpu.CompilerParams(vmem_limit_bytes=VMEM_LIMIT),
        name="moe_plan",
    )(cls2d)


def _dispatch_call(dest, ragged, x2d, n_sorted):
    n_tok = x2d.shape[0]
    return pl.pallas_call(
        _dispatch_kernel,
        grid_spec=pltpu.PrefetchScalarGridSpec(
            num_scalar_prefetch=2,
            grid=(n_tok // (2 * TD),),
            in_specs=[pl.BlockSpec((2 * TD, D_MODEL), lambda i, *_: (i, 0))],
            out_specs=pl.BlockSpec(memory_space=pl.ANY),
            scratch_shapes=[
                pltpu.VMEM((2, TD * ROW_TILES, LANES), F32),
                pltpu.VMEM((TMS * ROW_TILES, LANES), F32),
                pltpu.SemaphoreType.DMA((2,)),
                pltpu.SemaphoreType.DMA(()),
            ]),
        out_shape=jax.ShapeDtypeStruct((n_sorted * ROW_TILES, LANES), F32),
        compiler_params=pltpu.CompilerParams(
            dimension_semantics=("arbitrary",), vmem_limit_bytes=VMEM_LIMIT),
        name="moe_dispatch",
    )(dest, ragged, x2d)


def _combine_call(dest, ys, x2d, ln_g, ln_b):
    n_tok = x2d.shape[0]
    return pl.pallas_call(
        _combine_kernel,
        grid_spec=pltpu.PrefetchScalarGridSpec(
            num_scalar_prefetch=1,
            grid=(n_tok // (2 * TD),),
            in_specs=[pl.BlockSpec(memory_space=pl.ANY),
                      pl.BlockSpec((2 * TD, D_MODEL), lambda i, *_: (i, 0)),
                      _full((1, D_MODEL)), _full((1, D_MODEL))],
            out_specs=pl.BlockSpec((2 * TD, D_MODEL), lambda i, *_: (i, 0)),
            scratch_shapes=[
                pltpu.VMEM((2, TD * ROW_TILES, LANES), F32),
                pltpu.SemaphoreType.DMA((2,)),
            ]),
        out_shape=jax.ShapeDtypeStruct((n_tok, D_MODEL), F32),
        compiler_params=pltpu.CompilerParams(
            dimension_semantics=("arbitrary",), vmem_limit_bytes=VMEM_LIMIT),
        name="moe_combine",
    )(dest, ys, x2d, ln_g, ln_b)


def _moe_call(ea, eb, act, xs, wg, wu, wd, rwt, n_tiles):
    rows = TILES_PER_STEP * TMS * ROW_TILES
    w_specs, w_args = [], []
    for s in range(TILES_PER_STEP):
        for sel in (0, 1):
            def expert(i, ea, eb, act, s=s, sel=sel):
                return ((ea, eb)[sel][i * TILES_PER_STEP + s], 0, 0)
            w_specs += [pl.BlockSpec((1, D_MODEL, D_EXPERT), expert),
                        pl.BlockSpec((1, D_MODEL, D_EXPERT), expert),
                        pl.BlockSpec((1, D_EXPERT, D_MODEL), expert)]
            w_args += [wg, wu, wd]
    tile = pl.BlockSpec((rows, LANES), lambda i, ea, eb, act: (i, 0))
    return pl.pallas_call(
        _moe_kernel,
        grid_spec=pltpu.PrefetchScalarGridSpec(
            num_scalar_prefetch=3,
            grid=(n_tiles // TILES_PER_STEP,),
            in_specs=[tile] + w_specs + [_full((N_EXPERTS, D_MODEL))],
            out_specs=tile),
        out_shape=jax.ShapeDtypeStruct(xs.shape, F32),
        compiler_params=pltpu.CompilerParams(
            dimension_semantics=("arbitrary",), vmem_limit_bytes=VMEM_LIMIT),
        name="moe_experts",
    )(ea, eb, act, xs, *w_args, rwt)


def _block_diag(w):
    g, d, _ = w.shape
    eye = jnp.eye(g, dtype=w.dtype)
    return (eye[:, None, :, None] * w[:, :, None, :]).reshape(g * d, g * d)


def _pad_rows(w, rows):
    return jnp.pad(w, ((0, rows - w.shape[0]), (0, 0)))


def kernel(x, w_in, gm_ln_g, gm_ln_b, gm_w_s, gm_b_s, cf_dw_w, cf_dw_b, cf_ln_g, cf_ln_b, cf_pw,
           pool_w, pool_scale, sc_w, mix_norm_g, w_o, ln1_g, ln1_b, router_w, router_b,
           exp_w_gate, exp_w_up, exp_w_down, ln2_g, ln2_b):
    batch, seq, _ = x.shape
    n_tok = batch * seq
    assert n_tok == LANES * LANES, "the routing plan lays tokens out as one (128, 128) tile grid"
    assert seq % TS == 0 and TS % SUB_ROWS == 0 and SUB_ROWS % (2 * CONV_ROWS) == 0 and n_tok % (2 * TD) == 0
    n_tiles = n_tok // TMS + N_CLASSES
    assert n_tiles <= LANES and n_tiles % TILES_PER_STEP == 0
    row = lambda a: a.reshape(1, -1)
    rw = jnp.pad(router_w, ((0, 0), (0, LANES - N_EXPERTS)))
    rw_hi = rw.astype(BF16)
    rw_lo = (rw - rw_hi.astype(F32)).astype(BF16)
    rb = jnp.pad(router_b, (0, LANES - N_EXPERTS)).reshape(1, LANES)
    rwt = router_w.T
    w_in_b, w_o_b, cf_pw_b = w_in.astype(BF16), w_o.astype(BF16), cf_pw.astype(BF16)
    pool_b = jax.vmap(_block_diag)(pool_w).astype(BF16)
    for l in range(DEPTH):
        params = (
            w_in_b,
            row(gm_ln_g[l]), row(gm_ln_b[l]),
            gm_w_s[l].reshape(N_HEADS * CHUNK, CHUNK),
            jnp.repeat(gm_b_s[l].T, HEAD_DIM, axis=1),
            _pad_rows(cf_dw_w[l], HALO), row(cf_dw_b[l]), row(cf_ln_g[l]), row(cf_ln_b[l]),
            cf_pw_b,
            pool_b, row(pool_scale[l]),
            _pad_rows(sc_w[l], SUBLANES),
            row(mix_norm_g[l]), w_o_b,
            row(ln1_g[l]), row(ln1_b[l]),
            jnp.concatenate([rw_hi, rw_lo], axis=1), rb,
        )
        x1, cls, wg_b, wu_b, wd_b = _mixer_call(x, params, (exp_w_gate, exp_w_up, exp_w_down), l)
        x1 = x1.reshape(n_tok, D_MODEL)
        dest2d, meta = _plan_call(cls.reshape(LANES, LANES))
        dest = dest2d.reshape(n_tok)
        xs = _dispatch_call(dest, meta[3], x1, n_tiles * TMS)
        ys = _moe_call(meta[0], meta[1], meta[2], xs, wg_b, wu_b, wd_b, rwt, n_tiles)
        x = _combine_call(dest, ys, x1, row(ln2_g[l]), row(ln2_b[l])).reshape(batch, seq, D_MODEL)
    return x
```

```python
import jax
import jax.numpy as jnp
from jax import lax
from jax.experimental import pallas as pl
from jax.experimental.pallas import tpu as pltpu

D_MODEL = 1024
DEPTH = 2
D_GROUP = 256
N_HEADS = 4
HEAD_DIM = 64
CHUNK = 128
CONF_K = 31
SHORT_K = 3
N_EXPERTS = 16
N_GROUPS = 4
GROUP_SIZE = 4
N_PAIRS = 6
N_CLASSES = N_GROUPS * N_PAIRS
D_EXPERT = 512
ALPHA = (2 * DEPTH) ** 0.25
LN_EPS = 1e-5
RMS_EPS = 1e-6

LANES = 128
SUBLANES = 8
ROW_TILES = D_MODEL // LANES
HALO = 32
SC_HALO = SUBLANES
TS = 512
SUB_ROWS = 256
CONV_ROWS = 64
TMS = 256
TILES_PER_STEP = 2
TD = 256
VMEM_LIMIT = 56 * 1024 * 1024

BF16 = jnp.bfloat16
F32 = jnp.float32
I32 = jnp.int32


def _dot(a, b):
    return jnp.dot(a, b, preferred_element_type=F32)


def _layer_norm(x, g, b):
    mu = jnp.mean(x, axis=-1, keepdims=True)
    xc = x - mu
    var = jnp.mean(xc * xc, axis=-1, keepdims=True)
    return xc * lax.rsqrt(var + LN_EPS) * g + b


def _rms_norm(y, g):
    ms = jnp.mean(y * y, axis=-1, keepdims=True)
    return y * lax.rsqrt(ms + RMS_EPS) * g


def _sigmoid(x):
    return 1.0 / (1.0 + jnp.exp(-x))


def _route_class(sel):
    scores = []
    for g in range(N_GROUPS):
        v = sel[g * GROUP_SIZE:(g + 1) * GROUP_SIZE]
        best_pair = None
        for i in range(GROUP_SIZE):
            for j in range(i + 1, GROUP_SIZE):
                p = v[i] + v[j]
                best_pair = p if best_pair is None else jnp.maximum(best_pair, p)
        scores.append(best_pair)
    best = jnp.zeros(scores[0].shape, I32)
    best_score = scores[0]
    for g in range(1, N_GROUPS):
        better = scores[g] > best_score
        best = jnp.where(better, g, best)
        best_score = jnp.where(better, scores[g], best_score)
    v = []
    for j in range(GROUP_SIZE):
        out = sel[j]
        for g in range(1, N_GROUPS):
            out = jnp.where(best == g, sel[g * GROUP_SIZE + j], out)
        v.append(out)
    i0 = jnp.zeros_like(best)
    v0 = v[0]
    for j in range(1, GROUP_SIZE):
        better = v[j] > v0
        i0 = jnp.where(better, j, i0)
        v0 = jnp.where(better, v[j], v0)
    neg = jnp.full_like(v0, -jnp.inf)
    w = [jnp.where(i0 == j, neg, v[j]) for j in range(GROUP_SIZE)]
    i1 = jnp.zeros_like(best)
    v1 = w[0]
    for j in range(1, GROUP_SIZE):
        better = w[j] > v1
        i1 = jnp.where(better, j, i1)
        v1 = jnp.where(better, w[j], v1)
    a = jnp.minimum(i0, i1)
    b = jnp.maximum(i0, i1)
    pair = jnp.where(a == 0, b - 1, jnp.where(a == 1, b + 1, N_PAIRS - 1))
    return best * N_PAIRS + pair


def _mixer_kernel(x_ref, w_in_ref, gm_g_ref, gm_b_ref, gm_w_ref, gm_bs_ref,
                  cf_w_ref, cf_b_ref, cf_g_ref, cf_beta_ref, cf_pw_ref,
                  pool_w_ref, pool_scale_ref, sc_w_ref, mixg_ref, w_o_ref,
                  ln_g_ref, ln_b_ref, rw_ref, rb_ref, wg_ref, wu_ref, wd_ref,
                  x1_ref, cls_ref, wg_out, wu_out, wd_out, *scratch):
    s_idx = pl.program_id(1)
    wg_out[0] = wg_ref[0, 0].astype(BF16)
    wu_out[0] = wu_ref[0, 0].astype(BF16)
    wd_out[0] = wd_ref[0, 0].astype(BF16)

    row = lax.broadcasted_iota(I32, (N_HEADS * CHUNK, CHUNK), 0)
    col = lax.broadcasted_iota(I32, (N_HEADS * CHUNK, CHUNK), 1)
    w_tril = jnp.where(col <= (row & (CHUNK - 1)), gm_w_ref[0], 0.0).astype(BF16)
    lane = lax.broadcasted_iota(I32, (CHUNK, D_GROUP), 1)
    lane_t = lax.broadcasted_iota(I32, (SUB_ROWS, D_GROUP), 1)
    win = jnp.where(lane_t < HEAD_DIM, 2,
                    jnp.where(lane_t < 2 * HEAD_DIM, 4, jnp.where(lane_t < 3 * HEAD_DIM, 8, 16)))
    bs = gm_bs_ref[0]
    mixg = mixg_ref[0]

    n_blocks = TS // SUB_ROWS
    per_block = len(scratch) // n_blocks
    blocks = [scratch[i * per_block:(i + 1) * per_block] for i in range(n_blocks)]

    @pl.when(s_idx == 0)
    def _():
        hbuf0, hbuf1, _, _, pbuf, _, _, _, cbuf, _ = blocks[0]
        hbuf0[0:HALO, :] = jnp.zeros((HALO, LANES), F32)
        hbuf1[0:HALO, :] = jnp.zeros((HALO, LANES), F32)
        pbuf[0:HALO, :] = jnp.zeros((HALO, D_GROUP), F32)
        cbuf[0:SC_HALO, :] = jnp.zeros((SC_HALO, D_GROUP), F32)

    def carry_halo(src, dst, which):
        for i in which:
            rows = SC_HALO if i == 8 else HALO
            dst[i][0:rows, :] = src[i][SUB_ROWS:SUB_ROWS + rows, :]

    st = [dict() for _ in range(n_blocks)]

    def load(bi):
        x = x_ref[0, bi * SUB_ROWS:(bi + 1) * SUB_ROWS, :]
        st[bi]["x"] = x
        st[bi]["xb"] = x.astype(BF16)

    def in_proj(bi, lo, hi):
        z = _dot(st[bi]["xb"], w_in_ref[0, :, lo * D_GROUP:hi * D_GROUP])
        for j in range(lo, hi):
            st[bi][j] = z[:, (j - lo) * D_GROUP:(j - lo + 1) * D_GROUP]

    def gating_mlp(bi):
        yn_ref = blocks[bi][9]
        u = st[bi].pop(0)
        v = _layer_norm(st[bi].pop(1), gm_g_ref[0], gm_b_ref[0])
        for n in range(SUB_ROWS // CHUNK):
            crows = slice(n * CHUNK, (n + 1) * CHUNK)
            s_all = _dot(w_tril, v[crows, :].astype(BF16))
            s_sel = s_all[3 * CHUNK:4 * CHUNK]
            for h in (2, 1, 0):
                s_sel = jnp.where(lane < (h + 1) * HEAD_DIM, s_all[h * CHUNK:(h + 1) * CHUNK], s_sel)
            y1 = u[crows, :] * (s_sel + bs)
            yn_ref[crows, 0:D_GROUP] = _rms_norm(y1, mixg[:, 0:D_GROUP]).astype(BF16)

    def conformer_glu(bi):
        if bi > 0:
            carry_halo(blocks[bi - 1], blocks[bi], (0, 1))
        glu = st[bi].pop(2) * _sigmoid(st[bi].pop(3))
        for half in range(2):
            blocks[bi][half][HALO:HALO + SUB_ROWS, :] = glu[:, half * LANES:(half + 1) * LANES]

    def conformer_conv(bi, half):
        hb, co = blocks[bi][half], blocks[bi][2 + half]
        lanes = slice(half * LANES, (half + 1) * LANES)
        bias = cf_b_ref[0, :, lanes]
        for q in range(SUB_ROWS // (2 * CONV_ROWS)):
            for parity in range(2):
                out0 = q * 2 * CONV_ROWS + parity
                base = out0 + HALO - (CONF_K - 1)
                acc = jnp.zeros((CONV_ROWS, LANES), F32) + bias
                for k in range(CONF_K):
                    acc = acc + cf_w_ref[0, k:k + 1, lanes] * hb[pl.ds(base + k, CONV_ROWS, stride=2), :]
                co[pl.ds(out0, CONV_ROWS, stride=2), :] = acc

    def conformer_out(bi):
        cbo0, cbo1, yn_ref = blocks[bi][2], blocks[bi][3], blocks[bi][9]
        hln = _layer_norm(jnp.concatenate([cbo0[...], cbo1[...]], axis=1), cf_g_ref[0], cf_beta_ref[0])
        y2 = _dot((hln * _sigmoid(hln)).astype(BF16), cf_pw_ref[0])
        yn_ref[:, D_GROUP:2 * D_GROUP] = _rms_norm(y2, mixg[:, D_GROUP:2 * D_GROUP]).astype(BF16)

    def pooling(bi):
        pbuf, s2buf, s4buf, s8buf = blocks[bi][4:8]
        yn_ref = blocks[bi][9]
        if bi > 0:
            carry_halo(blocks[bi - 1], blocks[bi], (4,))
        zc = st[bi].pop(4)
        pbuf[HALO:HALO + SUB_ROWS, :] = zc
        n2 = HALO + SUB_ROWS - 8
        s2buf[8:8 + n2, :] = pbuf[8:8 + n2, :] + pbuf[7:7 + n2, :]
        n4 = HALO + SUB_ROWS - 16
        s4buf[16:16 + n4, :] = s2buf[16:16 + n4, :] + s2buf[14:14 + n4, :]
        n8 = HALO + SUB_ROWS - 24
        s8buf[24:24 + n8, :] = s4buf[24:24 + n8, :] + s4buf[20:20 + n8, :]
        cur = slice(HALO, HALO + SUB_ROWS)
        s16 = s8buf[cur, :] + s8buf[HALO - 8:HALO - 8 + SUB_ROWS, :]
        pos1 = lax.broadcasted_iota(I32, (SUB_ROWS, D_GROUP), 0) + (s_idx * TS + bi * SUB_ROWS + 1)
        wsum = jnp.where(lane_t < HEAD_DIM, s2buf[cur, :],
                         jnp.where(lane_t < 2 * HEAD_DIM, s4buf[cur, :],
                                   jnp.where(lane_t < 3 * HEAD_DIM, s8buf[cur, :], s16)))
        count = jnp.minimum(pos1, win).astype(F32)
        pooled = wsum / count - zc
        y3 = _dot(pooled.astype(BF16), pool_w_ref[0]) * pool_scale_ref[0]
        yn_ref[:, 2 * D_GROUP:3 * D_GROUP] = _rms_norm(y3, mixg[:, 2 * D_GROUP:3 * D_GROUP]).astype(BF16)

    def short_conv(bi):
        cbuf, yn_ref = blocks[bi][8], blocks[bi][9]
        if bi > 0:
            carry_halo(blocks[bi - 1], blocks[bi], (8,))
        cbuf[SC_HALO:SC_HALO + SUB_ROWS, :] = st[bi].pop(6) * st[bi].pop(7)
        conv = jnp.zeros((SUB_ROWS, D_GROUP), F32)
        for k in range(SHORT_K):
            off = SC_HALO - (SHORT_K - 1) + k
            conv = conv + sc_w_ref[0, k:k + 1, :] * cbuf[off:off + SUB_ROWS, :]
        y4 = st[bi].pop(5) * conv
        yn_ref[:, 3 * D_GROUP:4 * D_GROUP] = _rms_norm(y4, mixg[:, 3 * D_GROUP:4 * D_GROUP]).astype(BF16)

    def out_proj(bi):
        m = _dot(blocks[bi][9][...], w_o_ref[0])
        x1 = _layer_norm(ALPHA * st[bi].pop("x") + m, ln_g_ref[0], ln_b_ref[0])
        x1_ref[0, bi * SUB_ROWS:(bi + 1) * SUB_ROWS, :] = x1
        st[bi]["x1"] = x1

    def router(bi):
        x1 = st[bi].pop("x1")
        hi = x1.astype(BF16)
        lo = (x1 - hi.astype(F32)).astype(BF16)
        hi_both = _dot(hi, rw_ref[...])
        logits = hi_both[:, 0:LANES] + hi_both[:, LANES:2 * LANES] + _dot(lo, rw_ref[:, 0:LANES])
        st[bi]["sel"] = logits + rb_ref[...]

    def route(bi):
        sel_t = st[bi].pop("sel").T
        cls = _route_class([sel_t[e:e + 1, :] for e in range(N_EXPERTS)])
        r0 = bi * SUB_ROWS
        cls_ref[0, r0 // LANES:(r0 + SUB_ROWS) // LANES, :] = jnp.concatenate(
            [cls[:, k * LANES:(k + 1) * LANES] for k in range(SUB_ROWS // LANES)], axis=0)

    def conformer(b):
        conformer_glu(b)
        conformer_conv(b, 0)
        conformer_conv(b, 1)
        conformer_out(b)

    def stage1(b):
        return [lambda: load(b), lambda: in_proj(b, 0, 2), lambda: in_proj(b, 2, 4), lambda: in_proj(b, 4, 5),
                lambda: in_proj(b, 5, 8)]

    def stage2(b):
        return [lambda: gating_mlp(b), lambda: conformer(b), lambda: pooling(b), lambda: short_conv(b)]

    def stage3(b):
        return [lambda: out_proj(b), lambda: router(b), lambda: route(b)]

    for t in range(n_blocks + 2):
        stages = [stage(t - lag) for lag, stage in ((2, stage3), (0, stage1), (1, stage2)) if 0 <= t - lag < n_blocks]
        for k in range(max(len(stage) for stage in stages)):
            for stage in stages:
                if k < len(stage):
                    stage[k]()

    carry_halo(blocks[-1], blocks[0], (0, 1, 4, 8))


def _plan_kernel(cls_ref, dest_ref, meta_ref):
    n = LANES
    cls = cls_ref[...]
    r = lax.broadcasted_iota(I32, (n, n), 0)
    c = lax.broadcasted_iota(I32, (n, n), 1)
    upper = jnp.where(r < c, 1.0, 0.0).astype(BF16)
    lower = jnp.where(c < r, 1.0, 0.0).astype(BF16)
    ones = jnp.ones((n, n), BF16)
    masks = [cls == k for k in range(N_CLASSES)]
    m_all = jnp.concatenate([jnp.where(mk, 1.0, 0.0) for mk in masks], axis=0).astype(BF16)
    within = _dot(m_all, upper)
    rowsum = _dot(m_all, ones)
    rs = jnp.zeros((n, n), F32)
    for k in range(N_CLASSES):
        rs = jnp.where(c == k, rowsum[k * n:(k + 1) * n], rs)
    rs_b = rs.astype(BF16)
    before = _dot(lower, rs_b)
    total = _dot(ones, rs_b)
    ntile = jnp.floor((total + (TMS - 1)) * (1.0 / TMS))
    tstart = _dot(ntile.astype(BF16), upper)
    base = before + tstart * TMS
    dest = jnp.zeros((n, n), F32)
    for k in range(N_CLASSES):
        dest = jnp.where(masks[k], within[k * n:(k + 1) * n] + base[:, k:k + 1], dest)
    dest_ref[...] = dest.astype(I32)

    tend_t = (tstart + ntile).T
    ntile_t = ntile.T
    tile = c.astype(F32)
    ended = jnp.where((r < N_CLASSES) & (tend_t <= tile), 1.0, 0.0)
    is_last = jnp.where((r < N_CLASSES) & (ntile_t > 0) & (tend_t - 1.0 == tile), 1.0, 0.0)
    tcls = jnp.minimum(jnp.sum(ended, axis=0, keepdims=True), N_CLASSES - 1.0)
    grp = (jnp.where(tcls >= N_PAIRS, 1.0, 0.0) + jnp.where(tcls >= 2 * N_PAIRS, 1.0, 0.0)
           + jnp.where(tcls >= 3 * N_PAIRS, 1.0, 0.0))
    pair = tcls - N_PAIRS * grp
    pa = jnp.where(pair >= 3, 1.0, 0.0) + jnp.where(pair >= 5, 1.0, 0.0)
    pb = jnp.where(pair == 0, 1.0, jnp.where((pair == 1) | (pair == 3), 2.0, 3.0))
    n_tiles = tstart[0:1, N_CLASSES:N_CLASSES + 1]
    tile_row = tile[0:1, :]
    active = jnp.where(tile_row < n_tiles, 1.0, 0.0)
    ragged = jnp.where((jnp.sum(is_last, axis=0, keepdims=True) > 0) | (tile_row >= n_tiles), 1.0, 0.0)
    meta = jnp.concatenate(
        [GROUP_SIZE * grp + pa, GROUP_SIZE * grp + pb, active, ragged, jnp.zeros((4, n), F32)], axis=0)
    meta_ref[...] = meta.astype(I32)


def _row_copy_wait(src, dst, sem):
    pltpu.make_async_copy(src, dst, sem).wait()


def _dispatch_kernel(dest_ref, meta_ref, x_ref, xs_ref, buf, zbuf, sems, zsem):
    i = pl.program_id(0)
    n_steps = pl.num_programs(0)
    tile_rows = TMS * ROW_TILES
    n_tiles = xs_ref.shape[0] // tile_rows

    @pl.when(i == 0)
    def _():
        zbuf[...] = jnp.zeros(zbuf.shape, F32)
        for k in range(n_tiles):
            @pl.when(meta_ref[3, k] == 1)
            def _():
                pltpu.make_async_copy(zbuf, xs_ref.at[pl.ds(k * tile_rows, tile_rows), :], zsem).start()
        for k in range(n_tiles):
            @pl.when(meta_ref[3, k] == 1)
            def _():
                _row_copy_wait(zbuf, xs_ref.at[pl.ds(k * tile_rows, tile_rows), :], zsem)

    for s in range(2):
        @pl.when(i > 0)
        def _():
            _row_copy_wait(buf.at[s], xs_ref.at[pl.ds(0, TD * ROW_TILES), :], sems.at[s])

        for j in range(ROW_TILES):
            buf[s, pl.ds(j, TD, stride=ROW_TILES), :] = x_ref[s * TD:(s + 1) * TD, j * LANES:(j + 1) * LANES]
        for t in range(TD):
            d = pl.multiple_of(dest_ref[(i * 2 + s) * TD + t] * ROW_TILES, ROW_TILES)
            pltpu.make_async_copy(buf.at[s, pl.ds(t * ROW_TILES, ROW_TILES), :],
                                  xs_ref.at[pl.ds(d, ROW_TILES), :], sems.at[s]).start(priority=t % 2)

    @pl.when(i == n_steps - 1)
    def _():
        for s in range(2):
            _row_copy_wait(buf.at[s], xs_ref.at[pl.ds(0, TD * ROW_TILES), :], sems.at[s])


def _combine_kernel(dest_ref, ys_ref, x_ref, ln_g_ref, ln_b_ref, out_ref, buf, sems):
    i = pl.program_id(0)
    n_steps = pl.num_programs(0)

    def issue(step, s):
        for t in range(TD):
            d = pl.multiple_of(dest_ref[(step * 2 + s) * TD + t] * ROW_TILES, ROW_TILES)
            pltpu.make_async_copy(ys_ref.at[pl.ds(d, ROW_TILES), :],
                                  buf.at[s, pl.ds(t * ROW_TILES, ROW_TILES), :], sems.at[s]).start(priority=t % 2)

    @pl.when(i == 0)
    def _():
        for s in range(2):
            issue(0, s)

    for s in range(2):
        _row_copy_wait(ys_ref.at[pl.ds(0, TD * ROW_TILES), :], buf.at[s], sems.at[s])
        f = jnp.concatenate([buf[s, pl.ds(j, TD, stride=ROW_TILES), :] for j in range(ROW_TILES)], axis=1)
        rows = slice(s * TD, (s + 1) * TD)
        out_ref[rows, :] = _layer_norm(ALPHA * x_ref[rows, :] + f, ln_g_ref[0], ln_b_ref[0])

        @pl.when(i + 1 < n_steps)
        def _():
            issue(i + 1, s)


def _moe_kernel(meta_ref, xs_ref, *refs):
    i = pl.program_id(0)
    w_refs, rwt_ref, ys_ref = refs[:-2], refs[-2], refs[-1]
    rows = TMS * ROW_TILES

    @pl.when(meta_ref[2, i * TILES_PER_STEP] == 0)
    def _():
        ys_ref[...] = jnp.zeros(ys_ref.shape, F32)

    @pl.when(meta_ref[2, i * TILES_PER_STEP] == 1)
    def _():
        st = [dict() for _ in range(TILES_PER_STEP)]

        def load(s):
            x = jnp.concatenate(
                [xs_ref[pl.ds(s * rows + j, TMS, stride=ROW_TILES), :] for j in range(ROW_TILES)], axis=1)
            st[s]["x"] = x
            st[s]["xb"] = x.astype(BF16)

        def up(s, e):
            wg_ref, wu_ref, _ = w_refs[6 * s + 3 * e:6 * s + 3 * e + 3]
            st[s]["g", e] = _dot(st[s]["xb"], wg_ref[0])
            st[s]["u", e] = _dot(st[s]["xb"], wu_ref[0])

        def act(s, e):
            g = st[s].pop(("g", e))
            st[s]["h", e] = ((g * _sigmoid(g)) * st[s].pop(("u", e))).astype(BF16)

        def down(s, e):
            st[s]["y", e] = _dot(st[s].pop(("h", e)), w_refs[6 * s + 3 * e + 2][0])

        def mix(s):
            tile = i * TILES_PER_STEP + s
            x = st[s].pop("x")
            la = jnp.sum(x * rwt_ref[pl.ds(meta_ref[0, tile], 1), :], axis=-1, keepdims=True)
            lb = jnp.sum(x * rwt_ref[pl.ds(meta_ref[1, tile], 1), :], axis=-1, keepdims=True)
            m = jnp.maximum(la, lb)
            pa = jnp.exp(la - m)
            pb = jnp.exp(lb - m)
            den = pa + pb
            f = (pa / den) * st[s].pop(("y", 0)) + (pb / den) * st[s].pop(("y", 1))
            for j in range(ROW_TILES):
                ys_ref[pl.ds(s * rows + j, TMS, stride=ROW_TILES), :] = f[:, j * LANES:(j + 1) * LANES]

        units = [(s, e) for s in range(TILES_PER_STEP) for e in range(2)]
        load(0)
        up(*units[0])
        for k, (s, e) in enumerate(units):
            if k + 1 < len(units):
                nxt = units[k + 1]
                if nxt[1] == 0:
                    load(nxt[0])
                up(*nxt)
            act(s, e)
            down(s, e)
            if e == 1:
                mix(s)


def _full(shape):
    return pl.BlockSpec(shape, lambda *_: (0,) * len(shape))


def _layer_block(shape, layer):
    return pl.BlockSpec((1,) + shape, lambda *_: (layer,) + (0,) * len(shape))


def _mixer_call(x, p, expert_w, layer):
    batch, seq, _ = x.shape
    steps = seq // TS
    slabs = batch * steps // N_EXPERTS
    assert batch * steps == slabs * N_EXPERTS

    def w_in(b, s):
        i = b * steps + s
        return (layer, i // slabs, i % slabs, 0)

    def w_out(b, s):
        i = b * steps + s
        return (i // slabs, i % slabs, 0)

    up_rows, down_rows = D_MODEL // slabs, D_EXPERT // slabs
    in_specs = [
        pl.BlockSpec((1, TS, D_MODEL), lambda b, s: (b, s, 0)),
        _layer_block((D_MODEL, 8 * D_GROUP), layer),
        _layer_block((1, D_GROUP), layer), _layer_block((1, D_GROUP), layer),
        _layer_block((N_HEADS * CHUNK, CHUNK), layer), _layer_block((CHUNK, D_GROUP), layer),
        _layer_block((HALO, D_GROUP), layer), _layer_block((1, D_GROUP), layer),
        _layer_block((1, D_GROUP), layer), _layer_block((1, D_GROUP), layer),
        _layer_block((D_GROUP, D_GROUP), layer),
        _layer_block((D_GROUP, D_GROUP), layer), _layer_block((1, D_GROUP), layer),
        _layer_block((SUBLANES, D_GROUP), layer),
        _layer_block((1, D_MODEL), layer), _layer_block((D_MODEL, D_MODEL), layer),
        _layer_block((1, D_MODEL), layer), _layer_block((1, D_MODEL), layer),
        _full((D_MODEL, 2 * LANES)), _full((1, LANES)),
        pl.BlockSpec((1, 1, up_rows, D_EXPERT), w_in),
        pl.BlockSpec((1, 1, up_rows, D_EXPERT), w_in),
        pl.BlockSpec((1, 1, down_rows, D_MODEL), w_in),
    ]
    out_specs = [
        pl.BlockSpec((1, TS, D_MODEL), lambda b, s: (b, s, 0)),
        pl.BlockSpec((1, TS // LANES, LANES), lambda b, s: (b * steps + s, 0, 0)),
        pl.BlockSpec((1, up_rows, D_EXPERT), w_out),
        pl.BlockSpec((1, up_rows, D_EXPERT), w_out),
        pl.BlockSpec((1, down_rows, D_MODEL), w_out),
    ]
    block_scratch = [
        pltpu.VMEM((HALO + SUB_ROWS, LANES), F32),
        pltpu.VMEM((HALO + SUB_ROWS, LANES), F32),
        pltpu.VMEM((SUB_ROWS, LANES), F32),
        pltpu.VMEM((SUB_ROWS, LANES), F32),
        pltpu.VMEM((HALO + SUB_ROWS, D_GROUP), F32),
        pltpu.VMEM((HALO + SUB_ROWS, D_GROUP), F32),
        pltpu.VMEM((HALO + SUB_ROWS, D_GROUP), F32),
        pltpu.VMEM((HALO + SUB_ROWS, D_GROUP), F32),
        pltpu.VMEM((SC_HALO + SUB_ROWS, D_GROUP), F32),
        pltpu.VMEM((SUB_ROWS, D_MODEL), BF16),
    ]
    scratch = block_scratch * (TS // SUB_ROWS)
    return pl.pallas_call(
        _mixer_kernel,
        grid=(batch, steps),
        in_specs=in_specs,
        out_specs=out_specs,
        out_shape=[jax.ShapeDtypeStruct((batch, seq, D_MODEL), F32),
                   jax.ShapeDtypeStruct((batch * steps, TS // LANES, LANES), I32),
                   jax.ShapeDtypeStruct((N_EXPERTS, D_MODEL, D_EXPERT), BF16),
                   jax.ShapeDtypeStruct((N_EXPERTS, D_MODEL, D_EXPERT), BF16),
                   jax.ShapeDtypeStruct((N_EXPERTS, D_EXPERT, D_MODEL), BF16)],
        scratch_shapes=scratch,
        compiler_params=pltpu.CompilerParams(
            dimension_semantics=("arbitrary", "arbitrary"), vmem_limit_bytes=VMEM_LIMIT),
        name="mixer",
    )(x, *p, *expert_w)


def _plan_call(cls2d):
    return pl.pallas_call(
        _plan_kernel,
        out_shape=[jax.ShapeDtypeStruct((LANES, LANES), I32), jax.ShapeDtypeStruct((SUBLANES, LANES), I32)],
        compiler_params=pltpu.CompilerParams(vmem_limit_bytes=VMEM_LIMIT),
        name="moe_plan",
    )(cls2d)


def _dispatch_call(dest, meta, x2d, n_sorted):
    n_tok = x2d.shape[0]
    return pl.pallas_call(
        _dispatch_kernel,
        grid_spec=pltpu.PrefetchScalarGridSpec(
            num_scalar_prefetch=2,
            grid=(n_tok // (2 * TD),),
            in_specs=[pl.BlockSpec((2 * TD, D_MODEL), lambda i, *_: (i, 0))],
            out_specs=pl.BlockSpec(memory_space=pl.ANY),
            scratch_shapes=[
                pltpu.VMEM((2, TD * ROW_TILES, LANES), F32),
                pltpu.VMEM((TMS * ROW_TILES, LANES), F32),
                pltpu.SemaphoreType.DMA((2,)),
                pltpu.SemaphoreType.DMA(()),
            ]),
        out_shape=jax.ShapeDtypeStruct((n_sorted * ROW_TILES, LANES), F32),
        compiler_params=pltpu.CompilerParams(
            dimension_semantics=("arbitrary",), vmem_limit_bytes=VMEM_LIMIT),
        name="moe_dispatch",
    )(dest, meta, x2d)


def _combine_call(dest, ys, x2d, ln_g, ln_b, layer):
    n_tok = x2d.shape[0]
    return pl.pallas_call(
        _combine_kernel,
        grid_spec=pltpu.PrefetchScalarGridSpec(
            num_scalar_prefetch=1,
            grid=(n_tok // (2 * TD),),
            in_specs=[pl.BlockSpec(memory_space=pl.ANY),
                      pl.BlockSpec((2 * TD, D_MODEL), lambda i, *_: (i, 0)),
                      _layer_block((1, D_MODEL), layer), _layer_block((1, D_MODEL), layer)],
            out_specs=pl.BlockSpec((2 * TD, D_MODEL), lambda i, *_: (i, 0)),
            scratch_shapes=[
                pltpu.VMEM((2, TD * ROW_TILES, LANES), F32),
                pltpu.SemaphoreType.DMA((2,)),
            ]),
        out_shape=jax.ShapeDtypeStruct((n_tok, D_MODEL), F32),
        compiler_params=pltpu.CompilerParams(
            dimension_semantics=("arbitrary",), vmem_limit_bytes=VMEM_LIMIT),
        name="moe_combine",
    )(dest, ys, x2d, ln_g, ln_b)


def _moe_call(meta, xs, wg, wu, wd, rwt, n_tiles):
    rows = TILES_PER_STEP * TMS * ROW_TILES
    w_specs, w_args = [], []
    for s in range(TILES_PER_STEP):
        for sel in (0, 1):
            def expert(i, meta, s=s, sel=sel):
                return (meta[sel, i * TILES_PER_STEP + s], 0, 0)
            w_specs += [pl.BlockSpec((1, D_MODEL, D_EXPERT), expert),
                        pl.BlockSpec((1, D_MODEL, D_EXPERT), expert),
                        pl.BlockSpec((1, D_EXPERT, D_MODEL), expert)]
            w_args += [wg, wu, wd]
    tile = pl.BlockSpec((rows, LANES), lambda i, meta: (i, 0))
    return pl.pallas_call(
        _moe_kernel,
        grid_spec=pltpu.PrefetchScalarGridSpec(
            num_scalar_prefetch=1,
            grid=(n_tiles // TILES_PER_STEP,),
            in_specs=[tile] + w_specs + [_full((N_EXPERTS, D_MODEL))],
            out_specs=tile),
        out_shape=jax.ShapeDtypeStruct(xs.shape, F32),
        compiler_params=pltpu.CompilerParams(
            dimension_semantics=("arbitrary",), vmem_limit_bytes=VMEM_LIMIT),
        name="moe_experts",
    )(meta, xs, *w_args, rwt)


def _block_diag(w):
    g, d, _ = w.shape
    eye = jnp.eye(g, dtype=w.dtype)
    return (eye[:, None, :, None] * w[:, :, None, :]).reshape(g * d, g * d)


def kernel(x, w_in, gm_ln_g, gm_ln_b, gm_w_s, gm_b_s, cf_dw_w, cf_dw_b, cf_ln_g, cf_ln_b, cf_pw,
           pool_w, pool_scale, sc_w, mix_norm_g, w_o, ln1_g, ln1_b, router_w, router_b,
           exp_w_gate, exp_w_up, exp_w_down, ln2_g, ln2_b):
    batch, seq, _ = x.shape
    n_tok = batch * seq
    assert n_tok == LANES * LANES, "the routing plan lays tokens out as one (128, 128) tile grid"
    assert seq % TS == 0 and TS % SUB_ROWS == 0 and SUB_ROWS % (2 * CONV_ROWS) == 0 and n_tok % (2 * TD) == 0
    n_tiles = n_tok // TMS + N_CLASSES
    assert n_tiles <= LANES and n_tiles % TILES_PER_STEP == 0
    per_layer = lambda a: a.reshape(DEPTH, 1, -1)
    rw = jnp.pad(router_w, ((0, 0), (0, LANES - N_EXPERTS)))
    rw_hi = rw.astype(BF16)
    rw_lo = (rw - rw_hi.astype(F32)).astype(BF16)
    rb = jnp.pad(router_b, (0, LANES - N_EXPERTS)).reshape(1, LANES)
    rwt = router_w.T
    params = (
        w_in.astype(BF16),
        per_layer(gm_ln_g), per_layer(gm_ln_b),
        gm_w_s.reshape(DEPTH, N_HEADS * CHUNK, CHUNK),
        jnp.repeat(gm_b_s.transpose(0, 2, 1), HEAD_DIM, axis=2),
        jnp.pad(cf_dw_w, ((0, 0), (0, HALO - CONF_K), (0, 0))),
        per_layer(cf_dw_b), per_layer(cf_ln_g), per_layer(cf_ln_b),
        cf_pw.astype(BF16),
        jax.vmap(_block_diag)(pool_w).astype(BF16), per_layer(pool_scale),
        jnp.pad(sc_w, ((0, 0), (0, SUBLANES - SHORT_K), (0, 0))),
        per_layer(mix_norm_g), w_o.astype(BF16),
        per_layer(ln1_g), per_layer(ln1_b),
        jnp.concatenate([rw_hi, rw_lo], axis=1), rb,
    )
    ln2_g3, ln2_b3 = per_layer(ln2_g), per_layer(ln2_b)
    for l in range(DEPTH):
        x1, cls, wg_b, wu_b, wd_b = _mixer_call(x, params, (exp_w_gate, exp_w_up, exp_w_down), l)
        x1 = x1.reshape(n_tok, D_MODEL)
        dest2d, meta = _plan_call(cls.reshape(LANES, LANES))
        dest = dest2d.reshape(n_tok)
        xs = _dispatch_call(dest, meta, x1, n_tiles * TMS)
        ys = _moe_call(meta, xs, wg_b, wu_b, wd_b, rwt, n_tiles)
        x = _combine_call(dest, ys, x1, ln2_g3, ln2_b3, l).reshape(batch, seq, D_MODEL)
    return x
```

```python
import jax
import jax.numpy as jnp
from jax import lax
from jax.experimental import pallas as pl
from jax.experimental.pallas import tpu as pltpu

D_MODEL = 1024
DEPTH = 2
D_GROUP = 256
N_HEADS = 4
HEAD_DIM = 64
CHUNK = 128
CONF_K = 31
SHORT_K = 3
N_EXPERTS = 16
N_GROUPS = 4
GROUP_SIZE = 4
N_PAIRS = 6
N_CLASSES = N_GROUPS * N_PAIRS
D_EXPERT = 512
ALPHA = (2 * DEPTH) ** 0.25
LN_EPS = 1e-5
RMS_EPS = 1e-6

LANES = 128
SUBLANES = 8
ROW_TILES = D_MODEL // LANES
HALO = 32
SC_HALO = SUBLANES
TS = 512
SUB_ROWS = 256
CONV_ROWS = 64
TMS = 256
TILES_PER_STEP = 2
TD = 256
VMEM_LIMIT = 56 * 1024 * 1024

BF16 = jnp.bfloat16
F32 = jnp.float32
I32 = jnp.int32


def _dot(a, b):
    return jnp.dot(a, b, preferred_element_type=F32)


def _layer_norm(x, g, b):
    mu = jnp.mean(x, axis=-1, keepdims=True)
    xc = x - mu
    var = jnp.mean(xc * xc, axis=-1, keepdims=True)
    return xc * lax.rsqrt(var + LN_EPS) * g + b


def _rms_norm(y, g):
    ms = jnp.mean(y * y, axis=-1, keepdims=True)
    return y * lax.rsqrt(ms + RMS_EPS) * g


def _sigmoid(x):
    return 1.0 / (1.0 + jnp.exp(-x))


def _route_class(sel):
    scores = []
    for g in range(N_GROUPS):
        v = sel[g * GROUP_SIZE:(g + 1) * GROUP_SIZE]
        best_pair = None
        for i in range(GROUP_SIZE):
            for j in range(i + 1, GROUP_SIZE):
                p = v[i] + v[j]
                best_pair = p if best_pair is None else jnp.maximum(best_pair, p)
        scores.append(best_pair)
    best = jnp.zeros(scores[0].shape, I32)
    best_score = scores[0]
    for g in range(1, N_GROUPS):
        better = scores[g] > best_score
        best = jnp.where(better, g, best)
        best_score = jnp.where(better, scores[g], best_score)
    v = []
    for j in range(GROUP_SIZE):
        out = sel[j]
        for g in range(1, N_GROUPS):
            out = jnp.where(best == g, sel[g * GROUP_SIZE + j], out)
        v.append(out)
    i0 = jnp.zeros_like(best)
    v0 = v[0]
    for j in range(1, GROUP_SIZE):
        better = v[j] > v0
        i0 = jnp.where(better, j, i0)
        v0 = jnp.where(better, v[j], v0)
    neg = jnp.full_like(v0, -jnp.inf)
    w = [jnp.where(i0 == j, neg, v[j]) for j in range(GROUP_SIZE)]
    i1 = jnp.zeros_like(best)
    v1 = w[0]
    for j in range(1, GROUP_SIZE):
        better = w[j] > v1
        i1 = jnp.where(better, j, i1)
        v1 = jnp.where(better, w[j], v1)
    a = jnp.minimum(i0, i1)
    b = jnp.maximum(i0, i1)
    pair = jnp.where(a == 0, b - 1, jnp.where(a == 1, b + 1, N_PAIRS - 1))
    return best * N_PAIRS + pair


def _mixer_kernel(x_ref, w_in_ref, vec_g_ref, vec_m_ref, gm_w_ref, gm_bs_ref, cf_w_ref, cf_pw_ref,
                  pool_w_ref, sc_w_ref, w_o_ref, rw_ref, rb_ref, wg_ref, wu_ref, wd_ref,
                  x1_ref, cls_ref, wgu_out, wd_out, *scratch):
    s_idx = pl.program_id(1)
    gm_g, gm_b, cf_b, cf_g, cf_beta, pool_scale = (vec_g_ref[0, k:k + 1, :] for k in range(6))
    mixg, ln_g, ln_b = (vec_m_ref[0, k:k + 1, :] for k in range(3))

    row = lax.broadcasted_iota(I32, (N_HEADS * CHUNK, CHUNK), 0)
    col = lax.broadcasted_iota(I32, (N_HEADS * CHUNK, CHUNK), 1)
    w_tril = jnp.where(col <= (row & (CHUNK - 1)), gm_w_ref[0], 0.0).astype(BF16)
    lane = lax.broadcasted_iota(I32, (CHUNK, D_GROUP), 1)
    lane_t = lax.broadcasted_iota(I32, (SUB_ROWS, D_GROUP), 1)
    win = jnp.where(lane_t < HEAD_DIM, 2,
                    jnp.where(lane_t < 2 * HEAD_DIM, 4, jnp.where(lane_t < 3 * HEAD_DIM, 8, 16)))
    bs = gm_bs_ref[0]

    n_blocks = TS // SUB_ROWS
    per_block = len(scratch) // n_blocks
    blocks = [scratch[i * per_block:(i + 1) * per_block] for i in range(n_blocks)]

    @pl.when(s_idx == 0)
    def _():
        hbuf0, hbuf1, _, _, pbuf, _, _, _, cbuf, _ = blocks[0]
        hbuf0[0:HALO, :] = jnp.zeros((HALO, LANES), F32)
        hbuf1[0:HALO, :] = jnp.zeros((HALO, LANES), F32)
        pbuf[0:HALO, :] = jnp.zeros((HALO, D_GROUP), F32)
        cbuf[0:SC_HALO, :] = jnp.zeros((SC_HALO, D_GROUP), F32)

    def carry_halo(src, dst, which):
        for i in which:
            rows = SC_HALO if i == 8 else HALO
            dst[i][0:rows, :] = src[i][SUB_ROWS:SUB_ROWS + rows, :]

    st = [dict() for _ in range(n_blocks)]

    def load(bi):
        x = x_ref[0, bi * SUB_ROWS:(bi + 1) * SUB_ROWS, :]
        st[bi]["x"] = x
        st[bi]["xb"] = x.astype(BF16)

    def in_proj(bi, lo, hi):
        z = _dot(st[bi]["xb"], w_in_ref[0, :, lo * D_GROUP:hi * D_GROUP])
        for j in range(lo, hi):
            st[bi][j] = z[:, (j - lo) * D_GROUP:(j - lo + 1) * D_GROUP]

    def gating_mlp(bi):
        yn_ref = blocks[bi][9]
        u = st[bi].pop(0)
        v = _layer_norm(st[bi].pop(1), gm_g, gm_b)
        for n in range(SUB_ROWS // CHUNK):
            crows = slice(n * CHUNK, (n + 1) * CHUNK)
            s_all = _dot(w_tril, v[crows, :].astype(BF16))
            s_sel = s_all[3 * CHUNK:4 * CHUNK]
            for h in (2, 1, 0):
                s_sel = jnp.where(lane < (h + 1) * HEAD_DIM, s_all[h * CHUNK:(h + 1) * CHUNK], s_sel)
            y1 = u[crows, :] * (s_sel + bs)
            yn_ref[crows, 0:D_GROUP] = _rms_norm(y1, mixg[:, 0:D_GROUP]).astype(BF16)

    def conformer_glu(bi):
        if bi > 0:
            carry_halo(blocks[bi - 1], blocks[bi], (0, 1))
        glu = st[bi].pop(2) * _sigmoid(st[bi].pop(3))
        for half in range(2):
            blocks[bi][half][HALO:HALO + SUB_ROWS, :] = glu[:, half * LANES:(half + 1) * LANES]

    def conformer_conv(bi, half):
        hb, co = blocks[bi][half], blocks[bi][2 + half]
        lanes = slice(half * LANES, (half + 1) * LANES)
        bias = cf_b[:, lanes]
        for q in range(SUB_ROWS // (2 * CONV_ROWS)):
            for parity in range(2):
                out0 = q * 2 * CONV_ROWS + parity
                base = out0 + HALO - (CONF_K - 1)
                acc = jnp.zeros((CONV_ROWS, LANES), F32) + bias
                for k in range(CONF_K):
                    acc = acc + cf_w_ref[0, k:k + 1, lanes] * hb[pl.ds(base + k, CONV_ROWS, stride=2), :]
                co[pl.ds(out0, CONV_ROWS, stride=2), :] = acc

    def conformer_out(bi):
        cbo0, cbo1, yn_ref = blocks[bi][2], blocks[bi][3], blocks[bi][9]
        hln = _layer_norm(jnp.concatenate([cbo0[...], cbo1[...]], axis=1), cf_g, cf_beta)
        y2 = _dot((hln * _sigmoid(hln)).astype(BF16), cf_pw_ref[0])
        yn_ref[:, D_GROUP:2 * D_GROUP] = _rms_norm(y2, mixg[:, D_GROUP:2 * D_GROUP]).astype(BF16)

    def pooling(bi):
        pbuf, s2buf, s4buf, s8buf = blocks[bi][4:8]
        yn_ref = blocks[bi][9]
        if bi > 0:
            carry_halo(blocks[bi - 1], blocks[bi], (4,))
        zc = st[bi].pop(4)
        pbuf[HALO:HALO + SUB_ROWS, :] = zc
        n2 = HALO + SUB_ROWS - 8
        s2buf[8:8 + n2, :] = pbuf[8:8 + n2, :] + pbuf[7:7 + n2, :]
        n4 = HALO + SUB_ROWS - 16
        s4buf[16:16 + n4, :] = s2buf[16:16 + n4, :] + s2buf[14:14 + n4, :]
        n8 = HALO + SUB_ROWS - 24
        s8buf[24:24 + n8, :] = s4buf[24:24 + n8, :] + s4buf[20:20 + n8, :]
        cur = slice(HALO, HALO + SUB_ROWS)
        s16 = s8buf[cur, :] + s8buf[HALO - 8:HALO - 8 + SUB_ROWS, :]
        pos1 = lax.broadcasted_iota(I32, (SUB_ROWS, D_GROUP), 0) + (s_idx * TS + bi * SUB_ROWS + 1)
        wsum = jnp.where(lane_t < HEAD_DIM, s2buf[cur, :],
                         jnp.where(lane_t < 2 * HEAD_DIM, s4buf[cur, :],
                                   jnp.where(lane_t < 3 * HEAD_DIM, s8buf[cur, :], s16)))
        count = jnp.minimum(pos1, win).astype(F32)
        pooled = wsum / count - zc
        y3 = _dot(pooled.astype(BF16), pool_w_ref[0]) * pool_scale
        yn_ref[:, 2 * D_GROUP:3 * D_GROUP] = _rms_norm(y3, mixg[:, 2 * D_GROUP:3 * D_GROUP]).astype(BF16)

    def short_conv(bi):
        cbuf, yn_ref = blocks[bi][8], blocks[bi][9]
        if bi > 0:
            carry_halo(blocks[bi - 1], blocks[bi], (8,))
        cbuf[SC_HALO:SC_HALO + SUB_ROWS, :] = st[bi].pop(6) * st[bi].pop(7)
        conv = jnp.zeros((SUB_ROWS, D_GROUP), F32)
        for k in range(SHORT_K):
            off = SC_HALO - (SHORT_K - 1) + k
            conv = conv + sc_w_ref[0, k:k + 1, :] * cbuf[off:off + SUB_ROWS, :]
        y4 = st[bi].pop(5) * conv
        yn_ref[:, 3 * D_GROUP:4 * D_GROUP] = _rms_norm(y4, mixg[:, 3 * D_GROUP:4 * D_GROUP]).astype(BF16)

    def out_proj(bi):
        m = _dot(blocks[bi][9][...], w_o_ref[0])
        x1 = _layer_norm(ALPHA * st[bi].pop("x") + m, ln_g, ln_b)
        x1_ref[0, bi * SUB_ROWS:(bi + 1) * SUB_ROWS, :] = x1
        st[bi]["x1"] = x1

    def router(bi):
        x1 = st[bi].pop("x1")
        hi = x1.astype(BF16)
        lo = (x1 - hi.astype(F32)).astype(BF16)
        hi_both = _dot(hi, rw_ref[...])
        logits = hi_both[:, 0:LANES] + hi_both[:, LANES:2 * LANES] + _dot(lo, rw_ref[:, 0:LANES])
        st[bi]["sel"] = logits + rb_ref[...]

    def route(bi):
        sel_t = st[bi].pop("sel").T
        cls = _route_class([sel_t[e:e + 1, :] for e in range(N_EXPERTS)])
        r0 = bi * SUB_ROWS
        cls_ref[0, r0 // LANES:(r0 + SUB_ROWS) // LANES, :] = jnp.concatenate(
            [cls[:, k * LANES:(k + 1) * LANES] for k in range(SUB_ROWS // LANES)], axis=0)

    def conformer(b):
        conformer_glu(b)
        conformer_conv(b, 0)
        conformer_conv(b, 1)
        conformer_out(b)

    def round_expert_weights():
        wgu_out[0, :, 0:D_EXPERT] = wg_ref[0, 0].astype(BF16)
        wgu_out[0, :, D_EXPERT:2 * D_EXPERT] = wu_ref[0, 0].astype(BF16)
        wd_out[0] = wd_ref[0, 0].astype(BF16)

    def stage1(b):
        pieces = [lambda: load(b), lambda: in_proj(b, 0, 2), lambda: in_proj(b, 2, 4), lambda: in_proj(b, 4, 5),
                  lambda: in_proj(b, 5, 8)]
        if b == 0:
            pieces.insert(2, round_expert_weights)
        return pieces

    def stage2(b):
        return [lambda: gating_mlp(b), lambda: conformer(b), lambda: pooling(b), lambda: short_conv(b)]

    def stage3(b):
        return [lambda: out_proj(b), lambda: router(b), lambda: route(b)]

    for t in range(n_blocks + 2):
        stages = [stage(t - lag) for lag, stage in ((2, stage3), (0, stage1), (1, stage2)) if 0 <= t - lag < n_blocks]
        for k in range(max(len(stage) for stage in stages)):
            for stage in stages:
                if k < len(stage):
                    stage[k]()

    carry_halo(blocks[-1], blocks[0], (0, 1, 4, 8))


def _plan_kernel(cls_ref, dest_ref, meta_ref):
    n = LANES
    cls = cls_ref[...]
    r = lax.broadcasted_iota(I32, (n, n), 0)
    c = lax.broadcasted_iota(I32, (n, n), 1)
    upper = jnp.where(r < c, 1.0, 0.0).astype(BF16)
    lower = jnp.where(c < r, 1.0, 0.0).astype(BF16)
    ones = jnp.ones((n, n), BF16)
    masks = [cls == k for k in range(N_CLASSES)]
    m_all = jnp.concatenate([jnp.where(mk, 1.0, 0.0) for mk in masks], axis=0).astype(BF16)
    within = _dot(m_all, upper)
    rowsum = _dot(m_all, ones)
    rs = jnp.zeros((n, n), F32)
    for k in range(N_CLASSES):
        rs = jnp.where(c == k, rowsum[k * n:(k + 1) * n], rs)
    rs_b = rs.astype(BF16)
    before = _dot(lower, rs_b)
    total = _dot(ones, rs_b)
    ntile = jnp.floor((total + (TMS - 1)) * (1.0 / TMS))
    tstart = _dot(ntile.astype(BF16), upper)
    base = before + tstart * TMS
    dest = jnp.zeros((n, n), F32)
    for k in range(N_CLASSES):
        dest = jnp.where(masks[k], within[k * n:(k + 1) * n] + base[:, k:k + 1], dest)
    dest_ref[...] = dest.astype(I32)

    tend_t = (tstart + ntile).T
    ntile_t = ntile.T
    tile = c.astype(F32)
    ended = jnp.where((r < N_CLASSES) & (tend_t <= tile), 1.0, 0.0)
    is_last = jnp.where((r < N_CLASSES) & (ntile_t > 0) & (tend_t - 1.0 == tile), 1.0, 0.0)
    tcls = jnp.minimum(jnp.sum(ended, axis=0, keepdims=True), N_CLASSES - 1.0)
    grp = (jnp.where(tcls >= N_PAIRS, 1.0, 0.0) + jnp.where(tcls >= 2 * N_PAIRS, 1.0, 0.0)
           + jnp.where(tcls >= 3 * N_PAIRS, 1.0, 0.0))
    pair = tcls - N_PAIRS * grp
    pa = jnp.where(pair >= 3, 1.0, 0.0) + jnp.where(pair >= 5, 1.0, 0.0)
    pb = jnp.where(pair == 0, 1.0, jnp.where((pair == 1) | (pair == 3), 2.0, 3.0))
    n_tiles = tstart[0:1, N_CLASSES:N_CLASSES + 1]
    tile_row = tile[0:1, :]
    active = jnp.where(tile_row < n_tiles, 1.0, 0.0)
    ragged = jnp.where((jnp.sum(is_last, axis=0, keepdims=True) > 0) | (tile_row >= n_tiles), 1.0, 0.0)
    meta = jnp.concatenate(
        [GROUP_SIZE * grp + pa, GROUP_SIZE * grp + pb, active, ragged, jnp.zeros((4, n), F32)], axis=0)
    meta_ref[...] = meta.astype(I32)


def _row_copy_wait(src, dst, sem):
    pltpu.make_async_copy(src, dst, sem).wait()


def _dispatch_kernel(dest_ref, meta_ref, x_ref, xs_ref, buf, zbuf, sems, zsem):
    i = pl.program_id(0)
    n_steps = pl.num_programs(0)
    tile_rows = TMS * ROW_TILES
    n_tiles = xs_ref.shape[0] // tile_rows

    @pl.when(i == 0)
    def _():
        zbuf[...] = jnp.zeros(zbuf.shape, F32)
        for k in range(n_tiles):
            @pl.when(meta_ref[3, k] == 1)
            def _():
                pltpu.make_async_copy(zbuf, xs_ref.at[pl.ds(k * tile_rows, tile_rows), :], zsem).start()
        for k in range(n_tiles):
            @pl.when(meta_ref[3, k] == 1)
            def _():
                _row_copy_wait(zbuf, xs_ref.at[pl.ds(k * tile_rows, tile_rows), :], zsem)

    for s in range(2):
        @pl.when(i > 0)
        def _():
            _row_copy_wait(buf.at[s], xs_ref.at[pl.ds(0, TD * ROW_TILES), :], sems.at[s])

        for j in range(ROW_TILES):
            buf[s, pl.ds(j, TD, stride=ROW_TILES), :] = x_ref[s * TD:(s + 1) * TD, j * LANES:(j + 1) * LANES]
        for t in range(TD):
            d = pl.multiple_of(dest_ref[(i * 2 + s) * TD + t] * ROW_TILES, ROW_TILES)
            pltpu.make_async_copy(buf.at[s, pl.ds(t * ROW_TILES, ROW_TILES), :],
                                  xs_ref.at[pl.ds(d, ROW_TILES), :], sems.at[s]).start(priority=t % 2)

    @pl.when(i == n_steps - 1)
    def _():
        for s in range(2):
            _row_copy_wait(buf.at[s], xs_ref.at[pl.ds(0, TD * ROW_TILES), :], sems.at[s])


def _combine_kernel(dest_ref, ys_ref, x_ref, vec_m_ref, out_ref, buf, sems):
    i = pl.program_id(0)
    n_steps = pl.num_programs(0)

    def issue(step, s):
        for t in range(TD):
            d = pl.multiple_of(dest_ref[(step * 2 + s) * TD + t] * ROW_TILES, ROW_TILES)
            pltpu.make_async_copy(ys_ref.at[pl.ds(d, ROW_TILES), :],
                                  buf.at[s, pl.ds(t * ROW_TILES, ROW_TILES), :], sems.at[s]).start(priority=t % 2)

    @pl.when(i == 0)
    def _():
        for s in range(2):
            issue(0, s)

    for s in range(2):
        _row_copy_wait(ys_ref.at[pl.ds(0, TD * ROW_TILES), :], buf.at[s], sems.at[s])
        f = jnp.concatenate([buf[s, pl.ds(j, TD, stride=ROW_TILES), :] for j in range(ROW_TILES)], axis=1)
        rows = slice(s * TD, (s + 1) * TD)
        out_ref[rows, :] = _layer_norm(ALPHA * x_ref[rows, :] + f, vec_m_ref[0, 3:4, :], vec_m_ref[0, 4:5, :])

        @pl.when(i + 1 < n_steps)
        def _():
            issue(i + 1, s)


def _moe_kernel(meta_ref, xs_ref, *refs):
    i = pl.program_id(0)
    w_refs, rwt_ref, ys_ref = refs[:-2], refs[-2], refs[-1]
    rows = TMS * ROW_TILES

    @pl.when(meta_ref[2, i * TILES_PER_STEP] == 0)
    def _():
        ys_ref[...] = jnp.zeros(ys_ref.shape, F32)

    @pl.when(meta_ref[2, i * TILES_PER_STEP] == 1)
    def _():
        st = [dict() for _ in range(TILES_PER_STEP)]

        def load(s):
            x = jnp.concatenate(
                [xs_ref[pl.ds(s * rows + j, TMS, stride=ROW_TILES), :] for j in range(ROW_TILES)], axis=1)
            st[s]["x"] = x
            st[s]["xb"] = x.astype(BF16)

        def up(s, e):
            gu = _dot(st[s]["xb"], w_refs[4 * s + 2 * e][0])
            st[s]["g", e] = gu[:, 0:D_EXPERT]
            st[s]["u", e] = gu[:, D_EXPERT:2 * D_EXPERT]

        def act(s, e):
            g = st[s].pop(("g", e))
            st[s]["h", e] = ((g * _sigmoid(g)) * st[s].pop(("u", e))).astype(BF16)

        def down(s, e):
            st[s]["y", e] = _dot(st[s].pop(("h", e)), w_refs[4 * s + 2 * e + 1][0])

        def mix(s):
            tile = i * TILES_PER_STEP + s
            x = st[s].pop("x")
            la = jnp.sum(x * rwt_ref[pl.ds(meta_ref[0, tile], 1), :], axis=-1, keepdims=True)
            lb = jnp.sum(x * rwt_ref[pl.ds(meta_ref[1, tile], 1), :], axis=-1, keepdims=True)
            m = jnp.maximum(la, lb)
            pa = jnp.exp(la - m)
            pb = jnp.exp(lb - m)
            den = pa + pb
            f = (pa / den) * st[s].pop(("y", 0)) + (pb / den) * st[s].pop(("y", 1))
            for j in range(ROW_TILES):
                ys_ref[pl.ds(s * rows + j, TMS, stride=ROW_TILES), :] = f[:, j * LANES:(j + 1) * LANES]

        units = [(s, e) for s in range(TILES_PER_STEP) for e in range(2)]
        load(0)
        up(*units[0])
        for k, (s, e) in enumerate(units):
            if k + 1 < len(units):
                nxt = units[k + 1]
                if nxt[1] == 0:
                    load(nxt[0])
                up(*nxt)
            act(s, e)
            down(s, e)
            if e == 1:
                mix(s)


def _full(shape):
    return pl.BlockSpec(shape, lambda *_: (0,) * len(shape))


def _layer_block(shape, layer):
    return pl.BlockSpec((1,) + shape, lambda *_: (layer,) + (0,) * len(shape))


def _mixer_call(x, p, expert_w, layer):
    batch, seq, _ = x.shape
    steps = seq // TS
    slabs = batch * steps // N_EXPERTS
    assert batch * steps == slabs * N_EXPERTS

    def w_in(b, s):
        i = b * steps + s
        return (layer, i // slabs, i % slabs, 0)

    def w_out(b, s):
        i = b * steps + s
        return (i // slabs, i % slabs, 0)

    up_rows, down_rows = D_MODEL // slabs, D_EXPERT // slabs
    in_specs = [
        pl.BlockSpec((1, TS, D_MODEL), lambda b, s: (b, s, 0)),
        _layer_block((D_MODEL, 8 * D_GROUP), layer),
        _layer_block((SUBLANES, D_GROUP), layer), _layer_block((SUBLANES, D_MODEL), layer),
        _layer_block((N_HEADS * CHUNK, CHUNK), layer), _layer_block((CHUNK, D_GROUP), layer),
        _layer_block((HALO, D_GROUP), layer),
        _layer_block((D_GROUP, D_GROUP), layer),
        _layer_block((D_GROUP, D_GROUP), layer),
        _layer_block((SUBLANES, D_GROUP), layer),
        _layer_block((D_MODEL, D_MODEL), layer),
        _full((D_MODEL, 2 * LANES)), _full((1, LANES)),
        pl.BlockSpec((1, 1, up_rows, D_EXPERT), w_in),
        pl.BlockSpec((1, 1, up_rows, D_EXPERT), w_in),
        pl.BlockSpec((1, 1, down_rows, D_MODEL), w_in),
    ]
    out_specs = [
        pl.BlockSpec((1, TS, D_MODEL), lambda b, s: (b, s, 0)),
        pl.BlockSpec((1, TS // LANES, LANES), lambda b, s: (b * steps + s, 0, 0)),
        pl.BlockSpec((1, up_rows, 2 * D_EXPERT), w_out),
        pl.BlockSpec((1, down_rows, D_MODEL), w_out),
    ]
    block_scratch = [
        pltpu.VMEM((HALO + SUB_ROWS, LANES), F32),
        pltpu.VMEM((HALO + SUB_ROWS, LANES), F32),
        pltpu.VMEM((SUB_ROWS, LANES), F32),
        pltpu.VMEM((SUB_ROWS, LANES), F32),
        pltpu.VMEM((HALO + SUB_ROWS, D_GROUP), F32),
        pltpu.VMEM((HALO + SUB_ROWS, D_GROUP), F32),
        pltpu.VMEM((HALO + SUB_ROWS, D_GROUP), F32),
        pltpu.VMEM((HALO + SUB_ROWS, D_GROUP), F32),
        pltpu.VMEM((SC_HALO + SUB_ROWS, D_GROUP), F32),
        pltpu.VMEM((SUB_ROWS, D_MODEL), BF16),
    ]
    scratch = block_scratch * (TS // SUB_ROWS)
    return pl.pallas_call(
        _mixer_kernel,
        grid=(batch, steps),
        in_specs=in_specs,
        out_specs=out_specs,
        out_shape=[jax.ShapeDtypeStruct((batch, seq, D_MODEL), F32),
                   jax.ShapeDtypeStruct((batch * steps, TS // LANES, LANES), I32),
                   jax.ShapeDtypeStruct((N_EXPERTS, D_MODEL, 2 * D_EXPERT), BF16),
                   jax.ShapeDtypeStruct((N_EXPERTS, D_EXPERT, D_MODEL), BF16)],
        scratch_shapes=scratch,
        compiler_params=pltpu.CompilerParams(
            dimension_semantics=("arbitrary", "arbitrary"), vmem_limit_bytes=VMEM_LIMIT),
        name="mixer",
    )(x, *p, *expert_w)


def _plan_call(cls2d):
    return pl.pallas_call(
        _plan_kernel,
        out_shape=[jax.ShapeDtypeStruct((LANES, LANES), I32), jax.ShapeDtypeStruct((SUBLANES, LANES), I32)],
        compiler_params=pltpu.CompilerParams(vmem_limit_bytes=VMEM_LIMIT),
        name="moe_plan",
    )(cls2d)


def _dispatch_call(dest, meta, x2d, n_sorted):
    n_tok = x2d.shape[0]
    return pl.pallas_call(
        _dispatch_kernel,
        grid_spec=pltpu.PrefetchScalarGridSpec(
            num_scalar_prefetch=2,
            grid=(n_tok // (2 * TD),),
            in_specs=[pl.BlockSpec((2 * TD, D_MODEL), lambda i, *_: (i, 0))],
            out_specs=pl.BlockSpec(memory_space=pl.ANY),
            scratch_shapes=[
                pltpu.VMEM((2, TD * ROW_TILES, LANES), F32),
                pltpu.VMEM((TMS * ROW_TILES, LANES), F32),
                pltpu.SemaphoreType.DMA((2,)),
                pltpu.SemaphoreType.DMA(()),
            ]),
        out_shape=jax.ShapeDtypeStruct((n_sorted * ROW_TILES, LANES), F32),
        compiler_params=pltpu.CompilerParams(
            dimension_semantics=("arbitrary",), vmem_limit_bytes=VMEM_LIMIT),
        name="moe_dispatch",
    )(dest, meta, x2d)


def _combine_call(dest, ys, x2d, vec_m, layer):
    n_tok = x2d.shape[0]
    return pl.pallas_call(
        _combine_kernel,
        grid_spec=pltpu.PrefetchScalarGridSpec(
            num_scalar_prefetch=1,
            grid=(n_tok // (2 * TD),),
            in_specs=[pl.BlockSpec(memory_space=pl.ANY),
                      pl.BlockSpec((2 * TD, D_MODEL), lambda i, *_: (i, 0)),
                      _layer_block((SUBLANES, D_MODEL), layer)],
            out_specs=pl.BlockSpec((2 * TD, D_MODEL), lambda i, *_: (i, 0)),
            scratch_shapes=[
                pltpu.VMEM((2, TD * ROW_TILES, LANES), F32),
                pltpu.SemaphoreType.DMA((2,)),
            ]),
        out_shape=jax.ShapeDtypeStruct((n_tok, D_MODEL), F32),
        compiler_params=pltpu.CompilerParams(
            dimension_semantics=("arbitrary",), vmem_limit_bytes=VMEM_LIMIT),
        name="moe_combine",
    )(dest, ys, x2d, vec_m)


def _moe_call(meta, xs, wgu, wd, rwt, n_tiles):
    rows = TILES_PER_STEP * TMS * ROW_TILES
    w_specs, w_args = [], []
    for s in range(TILES_PER_STEP):
        for sel in (0, 1):
            def expert(i, meta, s=s, sel=sel):
                return (meta[sel, i * TILES_PER_STEP + s], 0, 0)
            w_specs += [pl.BlockSpec((1, D_MODEL, 2 * D_EXPERT), expert),
                        pl.BlockSpec((1, D_EXPERT, D_MODEL), expert)]
            w_args += [wgu, wd]
    tile = pl.BlockSpec((rows, LANES), lambda i, meta: (i, 0))
    return pl.pallas_call(
        _moe_kernel,
        grid_spec=pltpu.PrefetchScalarGridSpec(
            num_scalar_prefetch=1,
            grid=(n_tiles // TILES_PER_STEP,),
            in_specs=[tile] + w_specs + [_full((N_EXPERTS, D_MODEL))],
            out_specs=tile),
        out_shape=jax.ShapeDtypeStruct(xs.shape, F32),
        compiler_params=pltpu.CompilerParams(
            dimension_semantics=("arbitrary",), vmem_limit_bytes=VMEM_LIMIT),
        name="moe_experts",
    )(meta, xs, *w_args, rwt)


def _block_diag(w):
    g, d, _ = w.shape
    eye = jnp.eye(g, dtype=w.dtype)
    return (eye[:, None, :, None] * w[:, :, None, :]).reshape(g * d, g * d)


def kernel(x, w_in, gm_ln_g, gm_ln_b, gm_w_s, gm_b_s, cf_dw_w, cf_dw_b, cf_ln_g, cf_ln_b, cf_pw,
           pool_w, pool_scale, sc_w, mix_norm_g, w_o, ln1_g, ln1_b, router_w, router_b,
           exp_w_gate, exp_w_up, exp_w_down, ln2_g, ln2_b):
    batch, seq, _ = x.shape
    n_tok = batch * seq
    assert n_tok == LANES * LANES, "the routing plan lays tokens out as one (128, 128) tile grid"
    assert seq % TS == 0 and TS % SUB_ROWS == 0 and SUB_ROWS % (2 * CONV_ROWS) == 0 and n_tok % (2 * TD) == 0
    n_tiles = n_tok // TMS + N_CLASSES
    assert n_tiles <= LANES and n_tiles % TILES_PER_STEP == 0
    def pack_rows(rows):
        slab = jnp.stack(rows, axis=1)
        return jnp.pad(slab, ((0, 0), (0, SUBLANES - len(rows)), (0, 0)))

    rw = jnp.pad(router_w, ((0, 0), (0, LANES - N_EXPERTS)))
    rw_hi = rw.astype(BF16)
    rw_lo = (rw - rw_hi.astype(F32)).astype(BF16)
    rb = jnp.pad(router_b, (0, LANES - N_EXPERTS)).reshape(1, LANES)
    rwt = router_w.T
    vec_m = pack_rows([mix_norm_g, ln1_g, ln1_b, ln2_g, ln2_b])
    params = (
        w_in.astype(BF16),
        pack_rows([gm_ln_g, gm_ln_b, cf_dw_b, cf_ln_g, cf_ln_b, pool_scale]), vec_m,
        gm_w_s.reshape(DEPTH, N_HEADS * CHUNK, CHUNK),
        jnp.repeat(gm_b_s.transpose(0, 2, 1), HEAD_DIM, axis=2),
        jnp.pad(cf_dw_w, ((0, 0), (0, HALO - CONF_K), (0, 0))),
        cf_pw.astype(BF16),
        jax.vmap(_block_diag)(pool_w).astype(BF16),
        jnp.pad(sc_w, ((0, 0), (0, SUBLANES - SHORT_K), (0, 0))),
        w_o.astype(BF16),
        jnp.concatenate([rw_hi, rw_lo], axis=1), rb,
    )
    for l in range(DEPTH):
        x1, cls, wgu_b, wd_b = _mixer_call(x, params, (exp_w_gate, exp_w_up, exp_w_down), l)
        x1 = x1.reshape(n_tok, D_MODEL)
        dest2d, meta = _plan_call(cls.reshape(LANES, LANES))
        dest = dest2d.reshape(n_tok)
        xs = _dispatch_call(dest, meta, x1, n_tiles * TMS)
        ys = _moe_call(meta, xs, wgu_b, wd_b, rwt, n_tiles)
        x = _combine_call(dest, ys, x1, vec_m, l).reshape(batch, seq, D_MODEL)
    return x
```

```python
import jax
import jax.numpy as jnp
from jax import lax
from jax.experimental import pallas as pl
from jax.experimental.pallas import tpu as pltpu

D_MODEL = 1024
DEPTH = 2
D_GROUP = 256
N_HEADS = 4
HEAD_DIM = 64
CHUNK = 128
CONF_K = 31
SHORT_K = 3
N_EXPERTS = 16
N_GROUPS = 4
GROUP_SIZE = 4
N_PAIRS = 6
N_CLASSES = N_GROUPS * N_PAIRS
D_EXPERT = 512
ALPHA = (2 * DEPTH) ** 0.25
LN_EPS = 1e-5
RMS_EPS = 1e-6

LANES = 128
SUBLANES = 8
ROW_TILES = D_MODEL // LANES
HALO = 32
SC_HALO = SUBLANES
TS = 512
SUB_ROWS = 256
CONV_ROWS = 64
TMS = 256
TILES_PER_STEP = 2
TD = 512
VMEM_LIMIT = 56 * 1024 * 1024

BF16 = jnp.bfloat16
F32 = jnp.float32
I32 = jnp.int32


def _dot(a, b):
    return jnp.dot(a, b, preferred_element_type=F32)


def _layer_norm(x, g, b):
    mu = jnp.mean(x, axis=-1, keepdims=True)
    xc = x - mu
    var = jnp.mean(xc * xc, axis=-1, keepdims=True)
    return xc * lax.rsqrt(var + LN_EPS) * g + b


def _rms_norm(y, g):
    ms = jnp.mean(y * y, axis=-1, keepdims=True)
    return y * lax.rsqrt(ms + RMS_EPS) * g


def _sigmoid(x):
    return 1.0 / (1.0 + jnp.exp(-x))


def _route_class(sel):
    scores = []
    for g in range(N_GROUPS):
        v = sel[g * GROUP_SIZE:(g + 1) * GROUP_SIZE]
        best_pair = None
        for i in range(GROUP_SIZE):
            for j in range(i + 1, GROUP_SIZE):
                p = v[i] + v[j]
                best_pair = p if best_pair is None else jnp.maximum(best_pair, p)
        scores.append(best_pair)
    best = jnp.zeros(scores[0].shape, I32)
    best_score = scores[0]
    for g in range(1, N_GROUPS):
        better = scores[g] > best_score
        best = jnp.where(better, g, best)
        best_score = jnp.where(better, scores[g], best_score)
    v = []
    for j in range(GROUP_SIZE):
        out = sel[j]
        for g in range(1, N_GROUPS):
            out = jnp.where(best == g, sel[g * GROUP_SIZE + j], out)
        v.append(out)
    i0 = jnp.zeros_like(best)
    v0 = v[0]
    for j in range(1, GROUP_SIZE):
        better = v[j] > v0
        i0 = jnp.where(better, j, i0)
        v0 = jnp.where(better, v[j], v0)
    neg = jnp.full_like(v0, -jnp.inf)
    w = [jnp.where(i0 == j, neg, v[j]) for j in range(GROUP_SIZE)]
    i1 = jnp.zeros_like(best)
    v1 = w[0]
    for j in range(1, GROUP_SIZE):
        better = w[j] > v1
        i1 = jnp.where(better, j, i1)
        v1 = jnp.where(better, w[j], v1)
    a = jnp.minimum(i0, i1)
    b = jnp.maximum(i0, i1)
    pair = jnp.where(a == 0, b - 1, jnp.where(a == 1, b + 1, N_PAIRS - 1))
    return best * N_PAIRS + pair


def _mixer_kernel(x_ref, w_in_ref, vec_g_ref, vec_m_ref, gm_w_ref, gm_bs_ref, cf_w_ref, cf_pw_ref,
                  pool_w_ref, sc_w_ref, w_o_ref, rw_ref, rb_ref, wg_ref, wu_ref, wd_ref,
                  x1_ref, cls_ref, wgu_out, wd_out, *scratch):
    s_idx = pl.program_id(1)
    gm_g, gm_b, cf_b, cf_g, cf_beta, pool_scale = (vec_g_ref[0, k:k + 1, :] for k in range(6))
    mixg, ln_g, ln_b = (vec_m_ref[0, k:k + 1, :] for k in range(3))

    row = lax.broadcasted_iota(I32, (N_HEADS * CHUNK, CHUNK), 0)
    col = lax.broadcasted_iota(I32, (N_HEADS * CHUNK, CHUNK), 1)
    w_tril = jnp.where(col <= (row & (CHUNK - 1)), gm_w_ref[0], 0.0).astype(BF16)
    lane = lax.broadcasted_iota(I32, (CHUNK, D_GROUP), 1)
    lane_t = lax.broadcasted_iota(I32, (SUB_ROWS, D_GROUP), 1)
    win = jnp.where(lane_t < HEAD_DIM, 2,
                    jnp.where(lane_t < 2 * HEAD_DIM, 4, jnp.where(lane_t < 3 * HEAD_DIM, 8, 16)))
    bs = gm_bs_ref[0]

    n_blocks = TS // SUB_ROWS
    per_block = len(scratch) // n_blocks
    blocks = [scratch[i * per_block:(i + 1) * per_block] for i in range(n_blocks)]

    @pl.when(s_idx == 0)
    def _():
        hbuf0, hbuf1, _, _, pbuf, _, _, _, cbuf, _ = blocks[0]
        hbuf0[0:HALO, :] = jnp.zeros((HALO, LANES), F32)
        hbuf1[0:HALO, :] = jnp.zeros((HALO, LANES), F32)
        pbuf[0:HALO, :] = jnp.zeros((HALO, D_GROUP), F32)
        cbuf[0:SC_HALO, :] = jnp.zeros((SC_HALO, D_GROUP), F32)

    def carry_halo(src, dst, which):
        for i in which:
            rows = SC_HALO if i == 8 else HALO
            dst[i][0:rows, :] = src[i][SUB_ROWS:SUB_ROWS + rows, :]

    st = [dict() for _ in range(n_blocks)]

    def load(bi):
        x = x_ref[0, bi * SUB_ROWS:(bi + 1) * SUB_ROWS, :]
        st[bi]["x"] = x
        st[bi]["xb"] = x.astype(BF16)

    def in_proj(bi, lo, hi):
        z = _dot(st[bi]["xb"], w_in_ref[0, :, lo * D_GROUP:hi * D_GROUP])
        for j in range(lo, hi):
            st[bi][j] = z[:, (j - lo) * D_GROUP:(j - lo + 1) * D_GROUP]

    def gating_mlp(bi):
        yn_ref = blocks[bi][9]
        u = st[bi].pop(0)
        v = _layer_norm(st[bi].pop(1), gm_g, gm_b)
        for n in range(SUB_ROWS // CHUNK):
            crows = slice(n * CHUNK, (n + 1) * CHUNK)
            s_all = _dot(w_tril, v[crows, :].astype(BF16))
            s_sel = s_all[3 * CHUNK:4 * CHUNK]
            for h in (2, 1, 0):
                s_sel = jnp.where(lane < (h + 1) * HEAD_DIM, s_all[h * CHUNK:(h + 1) * CHUNK], s_sel)
            y1 = u[crows, :] * (s_sel + bs)
            yn_ref[crows, 0:D_GROUP] = _rms_norm(y1, mixg[:, 0:D_GROUP]).astype(BF16)

    def conformer_glu(bi):
        if bi > 0:
            carry_halo(blocks[bi - 1], blocks[bi], (0, 1))
        glu = st[bi].pop(2) * _sigmoid(st[bi].pop(3))
        for half in range(2):
            blocks[bi][half][HALO:HALO + SUB_ROWS, :] = glu[:, half * LANES:(half + 1) * LANES]

    def conformer_conv(bi, half):
        hb, co = blocks[bi][half], blocks[bi][2 + half]
        lanes = slice(half * LANES, (half + 1) * LANES)
        bias = cf_b[:, lanes]
        for q in range(SUB_ROWS // (2 * CONV_ROWS)):
            for parity in range(2):
                out0 = q * 2 * CONV_ROWS + parity
                base = out0 + HALO - (CONF_K - 1)
                acc = jnp.zeros((CONV_ROWS, LANES), F32) + bias
                for k in range(CONF_K):
                    acc = acc + cf_w_ref[0, k:k + 1, lanes] * hb[pl.ds(base + k, CONV_ROWS, stride=2), :]
                co[pl.ds(out0, CONV_ROWS, stride=2), :] = acc

    def conformer_out(bi):
        cbo0, cbo1, yn_ref = blocks[bi][2], blocks[bi][3], blocks[bi][9]
        hln = _layer_norm(jnp.concatenate([cbo0[...], cbo1[...]], axis=1), cf_g, cf_beta)
        y2 = _dot((hln * _sigmoid(hln)).astype(BF16), cf_pw_ref[0])
        yn_ref[:, D_GROUP:2 * D_GROUP] = _rms_norm(y2, mixg[:, D_GROUP:2 * D_GROUP]).astype(BF16)

    def pooling(bi):
        pbuf, s2buf, s4buf, s8buf = blocks[bi][4:8]
        yn_ref = blocks[bi][9]
        if bi > 0:
            carry_halo(blocks[bi - 1], blocks[bi], (4,))
        zc = st[bi].pop(4)
        pbuf[HALO:HALO + SUB_ROWS, :] = zc
        n2 = HALO + SUB_ROWS - 8
        s2buf[8:8 + n2, :] = pbuf[8:8 + n2, :] + pbuf[7:7 + n2, :]
        n4 = HALO + SUB_ROWS - 16
        s4buf[16:16 + n4, :] = s2buf[16:16 + n4, :] + s2buf[14:14 + n4, :]
        n8 = HALO + SUB_ROWS - 24
        s8buf[24:24 + n8, :] = s4buf[24:24 + n8, :] + s4buf[20:20 + n8, :]
        cur = slice(HALO, HALO + SUB_ROWS)
        s16 = s8buf[cur, :] + s8buf[HALO - 8:HALO - 8 + SUB_ROWS, :]
        pos1 = lax.broadcasted_iota(I32, (SUB_ROWS, D_GROUP), 0) + (s_idx * TS + bi * SUB_ROWS + 1)
        wsum = jnp.where(lane_t < HEAD_DIM, s2buf[cur, :],
                         jnp.where(lane_t < 2 * HEAD_DIM, s4buf[cur, :],
                                   jnp.where(lane_t < 3 * HEAD_DIM, s8buf[cur, :], s16)))
        count = jnp.minimum(pos1, win).astype(F32)
        pooled = wsum / count - zc
        y3 = _dot(pooled.astype(BF16), pool_w_ref[0]) * pool_scale
        yn_ref[:, 2 * D_GROUP:3 * D_GROUP] = _rms_norm(y3, mixg[:, 2 * D_GROUP:3 * D_GROUP]).astype(BF16)

    def short_conv(bi):
        cbuf, yn_ref = blocks[bi][8], blocks[bi][9]
        if bi > 0:
            carry_halo(blocks[bi - 1], blocks[bi], (8,))
        cbuf[SC_HALO:SC_HALO + SUB_ROWS, :] = st[bi].pop(6) * st[bi].pop(7)
        conv = jnp.zeros((SUB_ROWS, D_GROUP), F32)
        for k in range(SHORT_K):
            off = SC_HALO - (SHORT_K - 1) + k
            conv = conv + sc_w_ref[0, k:k + 1, :] * cbuf[off:off + SUB_ROWS, :]
        y4 = st[bi].pop(5) * conv
        yn_ref[:, 3 * D_GROUP:4 * D_GROUP] = _rms_norm(y4, mixg[:, 3 * D_GROUP:4 * D_GROUP]).astype(BF16)

    def out_proj(bi):
        m = _dot(blocks[bi][9][...], w_o_ref[0])
        x1 = _layer_norm(ALPHA * st[bi].pop("x") + m, ln_g, ln_b)
        x1_ref[0, bi * SUB_ROWS:(bi + 1) * SUB_ROWS, :] = x1
        st[bi]["x1"] = x1

    def router(bi):
        x1 = st[bi].pop("x1")
        hi = x1.astype(BF16)
        lo = (x1 - hi.astype(F32)).astype(BF16)
        hi_both = _dot(hi, rw_ref[...])
        logits = hi_both[:, 0:LANES] + hi_both[:, LANES:2 * LANES] + _dot(lo, rw_ref[:, 0:LANES])
        st[bi]["sel"] = logits + rb_ref[...]

    def route(bi):
        sel_t = st[bi].pop("sel").T
        cls = _route_class([sel_t[e:e + 1, :] for e in range(N_EXPERTS)])
        r0 = bi * SUB_ROWS
        cls_ref[0, r0 // LANES:(r0 + SUB_ROWS) // LANES, :] = jnp.concatenate(
            [cls[:, k * LANES:(k + 1) * LANES] for k in range(SUB_ROWS // LANES)], axis=0)

    def conformer(b):
        conformer_glu(b)
        conformer_conv(b, 0)
        conformer_conv(b, 1)
        conformer_out(b)

    def round_expert_weights():
        wgu_out[0, :, 0:D_EXPERT] = wg_ref[0, 0].astype(BF16)
        wgu_out[0, :, D_EXPERT:2 * D_EXPERT] = wu_ref[0, 0].astype(BF16)
        wd_out[0] = wd_ref[0, 0].astype(BF16)

    def stage1(b):
        pieces = [lambda: load(b), lambda: in_proj(b, 0, 2), lambda: in_proj(b, 2, 4), lambda: in_proj(b, 4, 5),
                  lambda: in_proj(b, 5, 8)]
        if b == 0:
            pieces.insert(2, round_expert_weights)
        return pieces

    def stage2(b):
        return [lambda: gating_mlp(b), lambda: conformer(b), lambda: pooling(b), lambda: short_conv(b)]

    def stage3(b):
        return [lambda: out_proj(b), lambda: router(b), lambda: route(b)]

    for t in range(n_blocks + 2):
        stages = [stage(t - lag) for lag, stage in ((2, stage3), (0, stage1), (1, stage2)) if 0 <= t - lag < n_blocks]
        for k in range(max(len(stage) for stage in stages)):
            for stage in stages:
                if k < len(stage):
                    stage[k]()

    carry_halo(blocks[-1], blocks[0], (0, 1, 4, 8))


def _plan_kernel(cls_ref, dest_ref, meta_ref):
    n = LANES
    cls = cls_ref[...]
    r = lax.broadcasted_iota(I32, (n, n), 0)
    c = lax.broadcasted_iota(I32, (n, n), 1)
    upper = jnp.where(r < c, 1.0, 0.0).astype(BF16)
    lower = jnp.where(c < r, 1.0, 0.0).astype(BF16)
    ones = jnp.ones((n, n), BF16)
    masks = [cls == k for k in range(N_CLASSES)]
    m_all = jnp.concatenate([jnp.where(mk, 1.0, 0.0) for mk in masks], axis=0).astype(BF16)
    within = _dot(m_all, upper)
    rowsum = _dot(m_all, ones)
    rs = jnp.zeros((n, n), F32)
    for k in range(N_CLASSES):
        rs = jnp.where(c == k, rowsum[k * n:(k + 1) * n], rs)
    rs_b = rs.astype(BF16)
    before = _dot(lower, rs_b)
    total = _dot(ones, rs_b)
    ntile = jnp.floor((total + (TMS - 1)) * (1.0 / TMS))
    tstart = _dot(ntile.astype(BF16), upper)
    base = before + tstart * TMS
    dest = jnp.zeros((n, n), F32)
    for k in range(N_CLASSES):
        dest = jnp.where(masks[k], within[k * n:(k + 1) * n] + base[:, k:k + 1], dest)
    dest_ref[...] = dest.astype(I32)

    tend_t = (tstart + ntile).T
    ntile_t = ntile.T
    tile = c.astype(F32)
    ended = jnp.where((r < N_CLASSES) & (tend_t <= tile), 1.0, 0.0)
    is_last = jnp.where((r < N_CLASSES) & (ntile_t > 0) & (tend_t - 1.0 == tile), 1.0, 0.0)
    tcls = jnp.minimum(jnp.sum(ended, axis=0, keepdims=True), N_CLASSES - 1.0)
    grp = (jnp.where(tcls >= N_PAIRS, 1.0, 0.0) + jnp.where(tcls >= 2 * N_PAIRS, 1.0, 0.0)
           + jnp.where(tcls >= 3 * N_PAIRS, 1.0, 0.0))
    pair = tcls - N_PAIRS * grp
    pa = jnp.where(pair >= 3, 1.0, 0.0) + jnp.where(pair >= 5, 1.0, 0.0)
    pb = jnp.where(pair == 0, 1.0, jnp.where((pair == 1) | (pair == 3), 2.0, 3.0))
    n_tiles = tstart[0:1, N_CLASSES:N_CLASSES + 1]
    tile_row = tile[0:1, :]
    active = jnp.where(tile_row < n_tiles, 1.0, 0.0)
    ragged = jnp.where((jnp.sum(is_last, axis=0, keepdims=True) > 0) | (tile_row >= n_tiles), 1.0, 0.0)
    meta = jnp.concatenate(
        [GROUP_SIZE * grp + pa, GROUP_SIZE * grp + pb, active, ragged, jnp.zeros((4, n), F32)], axis=0)
    meta_ref[...] = meta.astype(I32)


def _row_copy_wait(src, dst, sem):
    pltpu.make_async_copy(src, dst, sem).wait()


def _dispatch_kernel(dest_ref, meta_ref, x_ref, xs_ref, buf, zbuf, sems, zsem):
    i = pl.program_id(0)
    n_steps = pl.num_programs(0)
    tile_rows = TMS * ROW_TILES
    n_tiles = xs_ref.shape[0] // tile_rows

    @pl.when(i == 0)
    def _():
        zbuf[...] = jnp.zeros(zbuf.shape, F32)
        for k in range(n_tiles):
            @pl.when(meta_ref[3, k] == 1)
            def _():
                pltpu.make_async_copy(zbuf, xs_ref.at[pl.ds(k * tile_rows, tile_rows), :], zsem).start()
        for k in range(n_tiles):
            @pl.when(meta_ref[3, k] == 1)
            def _():
                _row_copy_wait(zbuf, xs_ref.at[pl.ds(k * tile_rows, tile_rows), :], zsem)

    for s in range(2):
        @pl.when(i > 0)
        def _():
            _row_copy_wait(buf.at[s], xs_ref.at[pl.ds(0, TD * ROW_TILES), :], sems.at[s])

        for j in range(ROW_TILES):
            buf[s, pl.ds(j, TD, stride=ROW_TILES), :] = x_ref[s * TD:(s + 1) * TD, j * LANES:(j + 1) * LANES]
        for t in range(TD):
            d = pl.multiple_of(dest_ref[(i * 2 + s) * TD + t] * ROW_TILES, ROW_TILES)
            pltpu.make_async_copy(buf.at[s, pl.ds(t * ROW_TILES, ROW_TILES), :],
                                  xs_ref.at[pl.ds(d, ROW_TILES), :], sems.at[s]).start(priority=t % 2)

    @pl.when(i == n_steps - 1)
    def _():
        for s in range(2):
            _row_copy_wait(buf.at[s], xs_ref.at[pl.ds(0, TD * ROW_TILES), :], sems.at[s])


def _combine_kernel(dest_ref, ys_ref, x_ref, vec_m_ref, out_ref, buf, sems):
    i = pl.program_id(0)
    n_steps = pl.num_programs(0)

    def issue(step, s):
        for t in range(TD):
            d = pl.multiple_of(dest_ref[(step * 2 + s) * TD + t] * ROW_TILES, ROW_TILES)
            pltpu.make_async_copy(ys_ref.at[pl.ds(d, ROW_TILES), :],
                                  buf.at[s, pl.ds(t * ROW_TILES, ROW_TILES), :], sems.at[s]).start(priority=t % 2)

    @pl.when(i == 0)
    def _():
        for s in range(2):
            issue(0, s)

    for s in range(2):
        _row_copy_wait(ys_ref.at[pl.ds(0, TD * ROW_TILES), :], buf.at[s], sems.at[s])
        f = jnp.concatenate([buf[s, pl.ds(j, TD, stride=ROW_TILES), :] for j in range(ROW_TILES)], axis=1)
        rows = slice(s * TD, (s + 1) * TD)
        out_ref[rows, :] = _layer_norm(ALPHA * x_ref[rows, :] + f, vec_m_ref[0, 3:4, :], vec_m_ref[0, 4:5, :])

        @pl.when(i + 1 < n_steps)
        def _():
            issue(i + 1, s)


def _moe_kernel(meta_ref, xs_ref, *refs):
    i = pl.program_id(0)
    w_refs, rwt_ref, ys_ref = refs[:-2], refs[-2], refs[-1]
    rows = TMS * ROW_TILES

    @pl.when(meta_ref[2, i * TILES_PER_STEP] == 0)
    def _():
        ys_ref[...] = jnp.zeros(ys_ref.shape, F32)

    @pl.when(meta_ref[2, i * TILES_PER_STEP] == 1)
    def _():
        st = [dict() for _ in range(TILES_PER_STEP)]

        def load(s):
            x = jnp.concatenate(
                [xs_ref[pl.ds(s * rows + j, TMS, stride=ROW_TILES), :] for j in range(ROW_TILES)], axis=1)
            st[s]["x"] = x
            st[s]["xb"] = x.astype(BF16)

        def up(s, e):
            gu = _dot(st[s]["xb"], w_refs[4 * s + 2 * e][0])
            st[s]["g", e] = gu[:, 0:D_EXPERT]
            st[s]["u", e] = gu[:, D_EXPERT:2 * D_EXPERT]

        def act(s, e):
            g = st[s].pop(("g", e))
            st[s]["h", e] = ((g * _sigmoid(g)) * st[s].pop(("u", e))).astype(BF16)

        def down(s, e):
            st[s]["y", e] = _dot(st[s].pop(("h", e)), w_refs[4 * s + 2 * e + 1][0])

        def mix(s):
            tile = i * TILES_PER_STEP + s
            x = st[s].pop("x")
            la = jnp.sum(x * rwt_ref[pl.ds(meta_ref[0, tile], 1), :], axis=-1, keepdims=True)
            lb = jnp.sum(x * rwt_ref[pl.ds(meta_ref[1, tile], 1), :], axis=-1, keepdims=True)
            m = jnp.maximum(la, lb)
            pa = jnp.exp(la - m)
            pb = jnp.exp(lb - m)
            den = pa + pb
            f = (pa / den) * st[s].pop(("y", 0)) + (pb / den) * st[s].pop(("y", 1))
            for j in range(ROW_TILES):
                ys_ref[pl.ds(s * rows + j, TMS, stride=ROW_TILES), :] = f[:, j * LANES:(j + 1) * LANES]

        units = [(s, e) for s in range(TILES_PER_STEP) for e in range(2)]
        load(0)
        up(*units[0])
        for k, (s, e) in enumerate(units):
            if k + 1 < len(units):
                nxt = units[k + 1]
                if nxt[1] == 0:
                    load(nxt[0])
                up(*nxt)
            act(s, e)
            down(s, e)
            if e == 1:
                mix(s)


def _full(shape):
    return pl.BlockSpec(shape, lambda *_: (0,) * len(shape))


def _layer_block(shape, layer):
    return pl.BlockSpec((1,) + shape, lambda *_: (layer,) + (0,) * len(shape))


def _mixer_call(x, p, expert_w, layer):
    batch, seq, _ = x.shape
    steps = seq // TS
    slabs = batch * steps // N_EXPERTS
    assert batch * steps == slabs * N_EXPERTS

    def w_in(b, s):
        i = b * steps + s
        return (layer, i // slabs, i % slabs, 0)

    def w_out(b, s):
        i = b * steps + s
        return (i // slabs, i % slabs, 0)

    up_rows, down_rows = D_MODEL // slabs, D_EXPERT // slabs
    in_specs = [
        pl.BlockSpec((1, TS, D_MODEL), lambda b, s: (b, s, 0)),
        _layer_block((D_MODEL, 8 * D_GROUP), layer),
        _layer_block((SUBLANES, D_GROUP), layer), _layer_block((SUBLANES, D_MODEL), layer),
        _layer_block((N_HEADS * CHUNK, CHUNK), layer), _layer_block((CHUNK, D_GROUP), layer),
        _layer_block((HALO, D_GROUP), layer),
        _layer_block((D_GROUP, D_GROUP), layer),
        _layer_block((D_GROUP, D_GROUP), layer),
        _layer_block((SUBLANES, D_GROUP), layer),
        _layer_block((D_MODEL, D_MODEL), layer),
        _full((D_MODEL, 2 * LANES)), _full((1, LANES)),
        pl.BlockSpec((1, 1, up_rows, D_EXPERT), w_in),
        pl.BlockSpec((1, 1, up_rows, D_EXPERT), w_in),
        pl.BlockSpec((1, 1, down_rows, D_MODEL), w_in),
    ]
    out_specs = [
        pl.BlockSpec((1, TS, D_MODEL), lambda b, s: (b, s, 0)),
        pl.BlockSpec((1, TS // LANES, LANES), lambda b, s: (b * steps + s, 0, 0)),
        pl.BlockSpec((1, up_rows, 2 * D_EXPERT), w_out),
        pl.BlockSpec((1, down_rows, D_MODEL), w_out),
    ]
    block_scratch = [
        pltpu.VMEM((HALO + SUB_ROWS, LANES), F32),
        pltpu.VMEM((HALO + SUB_ROWS, LANES), F32),
        pltpu.VMEM((SUB_ROWS, LANES), F32),
        pltpu.VMEM((SUB_ROWS, LANES), F32),
        pltpu.VMEM((HALO + SUB_ROWS, D_GROUP), F32),
        pltpu.VMEM((HALO + SUB_ROWS, D_GROUP), F32),
        pltpu.VMEM((HALO + SUB_ROWS, D_GROUP), F32),
        pltpu.VMEM((HALO + SUB_ROWS, D_GROUP), F32),
        pltpu.VMEM((SC_HALO + SUB_ROWS, D_GROUP), F32),
        pltpu.VMEM((SUB_ROWS, D_MODEL), BF16),
    ]
    scratch = block_scratch * (TS // SUB_ROWS)
    return pl.pallas_call(
        _mixer_kernel,
        grid=(batch, steps),
        in_specs=in_specs,
        out_specs=out_specs,
        out_shape=[jax.ShapeDtypeStruct((batch, seq, D_MODEL), F32),
                   jax.ShapeDtypeStruct((batch * steps, TS // LANES, LANES), I32),
                   jax.ShapeDtypeStruct((N_EXPERTS, D_MODEL, 2 * D_EXPERT), BF16),
                   jax.ShapeDtypeStruct((N_EXPERTS, D_EXPERT, D_MODEL), BF16)],
        scratch_shapes=scratch,
        compiler_params=pltpu.CompilerParams(
            dimension_semantics=("arbitrary", "arbitrary"), vmem_limit_bytes=VMEM_LIMIT),
        name="mixer",
    )(x, *p, *expert_w)


def _plan_call(cls2d):
    return pl.pallas_call(
        _plan_kernel,
        out_shape=[jax.ShapeDtypeStruct((LANES, LANES), I32), jax.ShapeDtypeStruct((SUBLANES, LANES), I32)],
        compiler_params=pltpu.CompilerParams(vmem_limit_bytes=VMEM_LIMIT),
        name="moe_plan",
    )(cls2d)


def _dispatch_call(dest, meta, x2d, n_sorted):
    n_tok = x2d.shape[0]
    return pl.pallas_call(
        _dispatch_kernel,
        grid_spec=pltpu.PrefetchScalarGridSpec(
            num_scalar_prefetch=2,
            grid=(n_tok // (2 * TD),),
            in_specs=[pl.BlockSpec((2 * TD, D_MODEL), lambda i, *_: (i, 0))],
            out_specs=pl.BlockSpec(memory_space=pl.ANY),
            scratch_shapes=[
                pltpu.VMEM((2, TD * ROW_TILES, LANES), F32),
                pltpu.VMEM((TMS * ROW_TILES, LANES), F32),
                pltpu.SemaphoreType.DMA((2,)),
                pltpu.SemaphoreType.DMA(()),
            ]),
        out_shape=jax.ShapeDtypeStruct((n_sorted * ROW_TILES, LANES), F32),
        compiler_params=pltpu.CompilerParams(
            dimension_semantics=("arbitrary",), vmem_limit_bytes=VMEM_LIMIT),
        name="moe_dispatch",
    )(dest, meta, x2d)


def _combine_call(dest, ys, x2d, vec_m, layer):
    n_tok = x2d.shape[0]
    return pl.pallas_call(
        _combine_kernel,
        grid_spec=pltpu.PrefetchScalarGridSpec(
            num_scalar_prefetch=1,
            grid=(n_tok // (2 * TD),),
            in_specs=[pl.BlockSpec(memory_space=pl.ANY),
                      pl.BlockSpec((2 * TD, D_MODEL), lambda i, *_: (i, 0)),
                      _layer_block((SUBLANES, D_MODEL), layer)],
            out_specs=pl.BlockSpec((2 * TD, D_MODEL), lambda i, *_: (i, 0)),
            scratch_shapes=[
                pltpu.VMEM((2, TD * ROW_TILES, LANES), F32),
                pltpu.SemaphoreType.DMA((2,)),
            ]),
        out_shape=jax.ShapeDtypeStruct((n_tok, D_MODEL), F32),
        compiler_params=pltpu.CompilerParams(
            dimension_semantics=("arbitrary",), vmem_limit_bytes=VMEM_LIMIT),
        name="moe_combine",
    )(dest, ys, x2d, vec_m)


def _moe_call(meta, xs, wgu, wd, rwt, n_tiles):
    rows = TILES_PER_STEP * TMS * ROW_TILES
    w_specs, w_args = [], []
    for s in range(TILES_PER_STEP):
        for sel in (0, 1):
            def expert(i, meta, s=s, sel=sel):
                return (meta[sel, i * TILES_PER_STEP + s], 0, 0)
            w_specs += [pl.BlockSpec((1, D_MODEL, 2 * D_EXPERT), expert),
                        pl.BlockSpec((1, D_EXPERT, D_MODEL), expert)]
            w_args += [wgu, wd]
    tile = pl.BlockSpec((rows, LANES), lambda i, meta: (i, 0))
    return pl.pallas_call(
        _moe_kernel,
        grid_spec=pltpu.PrefetchScalarGridSpec(
            num_scalar_prefetch=1,
            grid=(n_tiles // TILES_PER_STEP,),
            in_specs=[tile] + w_specs + [_full((N_EXPERTS, D_MODEL))],
            out_specs=tile),
        out_shape=jax.ShapeDtypeStruct(xs.shape, F32),
        compiler_params=pltpu.CompilerParams(
            dimension_semantics=("arbitrary",), vmem_limit_bytes=VMEM_LIMIT),
        name="moe_experts",
    )(meta, xs, *w_args, rwt)


def _block_diag(w):
    g, d, _ = w.shape
    eye = jnp.eye(g, dtype=w.dtype)
    return (eye[:, None, :, None] * w[:, :, None, :]).reshape(g * d, g * d)


def kernel(x, w_in, gm_ln_g, gm_ln_b, gm_w_s, gm_b_s, cf_dw_w, cf_dw_b, cf_ln_g, cf_ln_b, cf_pw,
           pool_w, pool_scale, sc_w, mix_norm_g, w_o, ln1_g, ln1_b, router_w, router_b,
           exp_w_gate, exp_w_up, exp_w_down, ln2_g, ln2_b):
    batch, seq, _ = x.shape
    n_tok = batch * seq
    assert n_tok == LANES * LANES, "the routing plan lays tokens out as one (128, 128) tile grid"
    assert seq % TS == 0 and TS % SUB_ROWS == 0 and SUB_ROWS % (2 * CONV_ROWS) == 0 and n_tok % (2 * TD) == 0
    n_tiles = n_tok // TMS + N_CLASSES
    assert n_tiles <= LANES and n_tiles % TILES_PER_STEP == 0
    def pack_rows(rows):
        slab = jnp.stack(rows, axis=1)
        return jnp.pad(slab, ((0, 0), (0, SUBLANES - len(rows)), (0, 0)))

    rw = jnp.pad(router_w, ((0, 0), (0, LANES - N_EXPERTS)))
    rw_hi = rw.astype(BF16)
    rw_lo = (rw - rw_hi.astype(F32)).astype(BF16)
    rb = jnp.pad(router_b, (0, LANES - N_EXPERTS)).reshape(1, LANES)
    rwt = router_w.T
    vec_m = pack_rows([mix_norm_g, ln1_g, ln1_b, ln2_g, ln2_b])
    params = (
        w_in.astype(BF16),
        pack_rows([gm_ln_g, gm_ln_b, cf_dw_b, cf_ln_g, cf_ln_b, pool_scale]), vec_m,
        gm_w_s.reshape(DEPTH, N_HEADS * CHUNK, CHUNK),
        jnp.repeat(gm_b_s.transpose(0, 2, 1), HEAD_DIM, axis=2),
        jnp.pad(cf_dw_w, ((0, 0), (0, HALO - CONF_K), (0, 0))),
        cf_pw.astype(BF16),
        jax.vmap(_block_diag)(pool_w).astype(BF16),
        jnp.pad(sc_w, ((0, 0), (0, SUBLANES - SHORT_K), (0, 0))),
        w_o.astype(BF16),
        jnp.concatenate([rw_hi, rw_lo], axis=1), rb,
    )
    for l in range(DEPTH):
        x1, cls, wgu_b, wd_b = _mixer_call(x, params, (exp_w_gate, exp_w_up, exp_w_down), l)
        x1 = x1.reshape(n_tok, D_MODEL)
        dest2d, meta = _plan_call(cls.reshape(LANES, LANES))
        dest = dest2d.reshape(n_tok)
        xs = _dispatch_call(dest, meta, x1, n_tiles * TMS)
        ys = _moe_call(meta, xs, wgu_b, wd_b, rwt, n_tiles)
        x = _combine_call(dest, ys, x1, vec_m, l).reshape(batch, seq, D_MODEL)
    return x
```

```python
import jax
import jax.numpy as jnp
from jax import lax
from jax.experimental import pallas as pl
from jax.experimental.pallas import tpu as pltpu

D_MODEL = 1024
DEPTH = 2
D_GROUP = 256
N_HEADS = 4
HEAD_DIM = 64
CHUNK = 128
CONF_K = 31
SHORT_K = 3
N_EXPERTS = 16
N_GROUPS = 4
GROUP_SIZE = 4
N_PAIRS = 6
N_CLASSES = N_GROUPS * N_PAIRS
D_EXPERT = 512
ALPHA = (2 * DEPTH) ** 0.25
LN_EPS = 1e-5
RMS_EPS = 1e-6

LANES = 128
SUBLANES = 8
ROW_TILES = D_MODEL // LANES
HALO = 32
SC_HALO = SUBLANES
TS = 512
SUB_ROWS = 256
CONV_ROWS = 64
TMS = 256
TILES_PER_STEP = 2
TD = 1024
VMEM_LIMIT = 56 * 1024 * 1024

BF16 = jnp.bfloat16
F32 = jnp.float32
I32 = jnp.int32


def _dot(a, b):
    return jnp.dot(a, b, preferred_element_type=F32)


def _layer_norm(x, g, b):
    mu = jnp.mean(x, axis=-1, keepdims=True)
    xc = x - mu
    var = jnp.mean(xc * xc, axis=-1, keepdims=True)
    return xc * lax.rsqrt(var + LN_EPS) * g + b


def _rms_norm(y, g):
    ms = jnp.mean(y * y, axis=-1, keepdims=True)
    return y * lax.rsqrt(ms + RMS_EPS) * g


def _sigmoid(x):
    return 1.0 / (1.0 + jnp.exp(-x))


def _route_class(sel):
    scores = []
    for g in range(N_GROUPS):
        v = sel[g * GROUP_SIZE:(g + 1) * GROUP_SIZE]
        best_pair = None
        for i in range(GROUP_SIZE):
            for j in range(i + 1, GROUP_SIZE):
                p = v[i] + v[j]
                best_pair = p if best_pair is None else jnp.maximum(best_pair, p)
        scores.append(best_pair)
    best = jnp.zeros(scores[0].shape, I32)
    best_score = scores[0]
    for g in range(1, N_GROUPS):
        better = scores[g] > best_score
        best = jnp.where(better, g, best)
        best_score = jnp.where(better, scores[g], best_score)
    v = []
    for j in range(GROUP_SIZE):
        out = sel[j]
        for g in range(1, N_GROUPS):
            out = jnp.where(best == g, sel[g * GROUP_SIZE + j], out)
        v.append(out)
    i0 = jnp.zeros_like(best)
    v0 = v[0]
    for j in range(1, GROUP_SIZE):
        better = v[j] > v0
        i0 = jnp.where(better, j, i0)
        v0 = jnp.where(better, v[j], v0)
    neg = jnp.full_like(v0, -jnp.inf)
    w = [jnp.where(i0 == j, neg, v[j]) for j in range(GROUP_SIZE)]
    i1 = jnp.zeros_like(best)
    v1 = w[0]
    for j in range(1, GROUP_SIZE):
        better = w[j] > v1
        i1 = jnp.where(better, j, i1)
        v1 = jnp.where(better, w[j], v1)
    a = jnp.minimum(i0, i1)
    b = jnp.maximum(i0, i1)
    pair = jnp.where(a == 0, b - 1, jnp.where(a == 1, b + 1, N_PAIRS - 1))
    return best * N_PAIRS + pair


def _mixer_kernel(x_ref, w_in_ref, vec_g_ref, vec_m_ref, gm_w_ref, gm_bs_ref, cf_w_ref, cf_pw_ref,
                  pool_w_ref, sc_w_ref, w_o_ref, rw_ref, rb_ref, wg_ref, wu_ref, wd_ref,
                  x1_ref, cls_ref, wgu_out, wd_out, *scratch):
    s_idx = pl.program_id(1)
    gm_g, gm_b, cf_b, cf_g, cf_beta, pool_scale = (vec_g_ref[0, k:k + 1, :] for k in range(6))
    mixg, ln_g, ln_b = (vec_m_ref[0, k:k + 1, :] for k in range(3))

    row = lax.broadcasted_iota(I32, (N_HEADS * CHUNK, CHUNK), 0)
    col = lax.broadcasted_iota(I32, (N_HEADS * CHUNK, CHUNK), 1)
    w_tril = jnp.where(col <= (row & (CHUNK - 1)), gm_w_ref[0], 0.0).astype(BF16)
    lane = lax.broadcasted_iota(I32, (CHUNK, D_GROUP), 1)
    lane_t = lax.broadcasted_iota(I32, (SUB_ROWS, D_GROUP), 1)
    win = jnp.where(lane_t < HEAD_DIM, 2,
                    jnp.where(lane_t < 2 * HEAD_DIM, 4, jnp.where(lane_t < 3 * HEAD_DIM, 8, 16)))
    bs = gm_bs_ref[0]

    n_blocks = TS // SUB_ROWS
    per_block = len(scratch) // n_blocks
    blocks = [scratch[i * per_block:(i + 1) * per_block] for i in range(n_blocks)]

    @pl.when(s_idx == 0)
    def _():
        hbuf0, hbuf1, _, _, pbuf, _, _, _, cbuf, _ = blocks[0]
        hbuf0[0:HALO, :] = jnp.zeros((HALO, LANES), F32)
        hbuf1[0:HALO, :] = jnp.zeros((HALO, LANES), F32)
        pbuf[0:HALO, :] = jnp.zeros((HALO, D_GROUP), F32)
        cbuf[0:SC_HALO, :] = jnp.zeros((SC_HALO, D_GROUP), F32)

    def carry_halo(src, dst, which):
        for i in which:
            rows = SC_HALO if i == 8 else HALO
            dst[i][0:rows, :] = src[i][SUB_ROWS:SUB_ROWS + rows, :]

    st = [dict() for _ in range(n_blocks)]

    def load(bi):
        x = x_ref[0, bi * SUB_ROWS:(bi + 1) * SUB_ROWS, :]
        st[bi]["x"] = x
        st[bi]["xb"] = x.astype(BF16)

    def in_proj(bi, lo, hi):
        z = _dot(st[bi]["xb"], w_in_ref[0, :, lo * D_GROUP:hi * D_GROUP])
        for j in range(lo, hi):
            st[bi][j] = z[:, (j - lo) * D_GROUP:(j - lo + 1) * D_GROUP]

    def gating_mlp(bi):
        yn_ref = blocks[bi][9]
        u = st[bi].pop(0)
        v = _layer_norm(st[bi].pop(1), gm_g, gm_b)
        for n in range(SUB_ROWS // CHUNK):
            crows = slice(n * CHUNK, (n + 1) * CHUNK)
            s_all = _dot(w_tril, v[crows, :].astype(BF16))
            s_sel = s_all[3 * CHUNK:4 * CHUNK]
            for h in (2, 1, 0):
                s_sel = jnp.where(lane < (h + 1) * HEAD_DIM, s_all[h * CHUNK:(h + 1) * CHUNK], s_sel)
            y1 = u[crows, :] * (s_sel + bs)
            yn_ref[crows, 0:D_GROUP] = _rms_norm(y1, mixg[:, 0:D_GROUP]).astype(BF16)

    def conformer_glu(bi):
        if bi > 0:
            carry_halo(blocks[bi - 1], blocks[bi], (0, 1))
        glu = st[bi].pop(2) * _sigmoid(st[bi].pop(3))
        for half in range(2):
            blocks[bi][half][HALO:HALO + SUB_ROWS, :] = glu[:, half * LANES:(half + 1) * LANES]

    def conformer_conv(bi, half):
        hb, co = blocks[bi][half], blocks[bi][2 + half]
        lanes = slice(half * LANES, (half + 1) * LANES)
        bias = cf_b[:, lanes]
        for q in range(SUB_ROWS // (2 * CONV_ROWS)):
            for parity in range(2):
                out0 = q * 2 * CONV_ROWS + parity
                base = out0 + HALO - (CONF_K - 1)
                acc = jnp.zeros((CONV_ROWS, LANES), F32) + bias
                for k in range(CONF_K):
                    acc = acc + cf_w_ref[0, k:k + 1, lanes] * hb[pl.ds(base + k, CONV_ROWS, stride=2), :]
                co[pl.ds(out0, CONV_ROWS, stride=2), :] = acc

    def conformer_out(bi):
        cbo0, cbo1, yn_ref = blocks[bi][2], blocks[bi][3], blocks[bi][9]
        hln = _layer_norm(jnp.concatenate([cbo0[...], cbo1[...]], axis=1), cf_g, cf_beta)
        y2 = _dot((hln * _sigmoid(hln)).astype(BF16), cf_pw_ref[0])
        yn_ref[:, D_GROUP:2 * D_GROUP] = _rms_norm(y2, mixg[:, D_GROUP:2 * D_GROUP]).astype(BF16)

    def pooling(bi):
        pbuf, s2buf, s4buf, s8buf = blocks[bi][4:8]
        yn_ref = blocks[bi][9]
        if bi > 0:
            carry_halo(blocks[bi - 1], blocks[bi], (4,))
        zc = st[bi].pop(4)
        pbuf[HALO:HALO + SUB_ROWS, :] = zc
        n2 = HALO + SUB_ROWS - 8
        s2buf[8:8 + n2, :] = pbuf[8:8 + n2, :] + pbuf[7:7 + n2, :]
        n4 = HALO + SUB_ROWS - 16
        s4buf[16:16 + n4, :] = s2buf[16:16 + n4, :] + s2buf[14:14 + n4, :]
        n8 = HALO + SUB_ROWS - 24
        s8buf[24:24 + n8, :] = s4buf[24:24 + n8, :] + s4buf[20:20 + n8, :]
        cur = slice(HALO, HALO + SUB_ROWS)
        s16 = s8buf[cur, :] + s8buf[HALO - 8:HALO - 8 + SUB_ROWS, :]
        pos1 = lax.broadcasted_iota(I32, (SUB_ROWS, D_GROUP), 0) + (s_idx * TS + bi * SUB_ROWS + 1)
        wsum = jnp.where(lane_t < HEAD_DIM, s2buf[cur, :],
                         jnp.where(lane_t < 2 * HEAD_DIM, s4buf[cur, :],
                                   jnp.where(lane_t < 3 * HEAD_DIM, s8buf[cur, :], s16)))
        count = jnp.minimum(pos1, win).astype(F32)
        pooled = wsum / count - zc
        y3 = _dot(pooled.astype(BF16), pool_w_ref[0]) * pool_scale
        yn_ref[:, 2 * D_GROUP:3 * D_GROUP] = _rms_norm(y3, mixg[:, 2 * D_GROUP:3 * D_GROUP]).astype(BF16)

    def short_conv(bi):
        cbuf, yn_ref = blocks[bi][8], blocks[bi][9]
        if bi > 0:
            carry_halo(blocks[bi - 1], blocks[bi], (8,))
        cbuf[SC_HALO:SC_HALO + SUB_ROWS, :] = st[bi].pop(6) * st[bi].pop(7)
        conv = jnp.zeros((SUB_ROWS, D_GROUP), F32)
        for k in range(SHORT_K):
            off = SC_HALO - (SHORT_K - 1) + k
            conv = conv + sc_w_ref[0, k:k + 1, :] * cbuf[off:off + SUB_ROWS, :]
        y4 = st[bi].pop(5) * conv
        yn_ref[:, 3 * D_GROUP:4 * D_GROUP] = _rms_norm(y4, mixg[:, 3 * D_GROUP:4 * D_GROUP]).astype(BF16)

    def out_proj(bi):
        m = _dot(blocks[bi][9][...], w_o_ref[0])
        x1 = _layer_norm(ALPHA * st[bi].pop("x") + m, ln_g, ln_b)
        x1_ref[0, bi * SUB_ROWS:(bi + 1) * SUB_ROWS, :] = x1
        st[bi]["x1"] = x1

    def router(bi):
        x1 = st[bi].pop("x1")
        hi = x1.astype(BF16)
        lo = (x1 - hi.astype(F32)).astype(BF16)
        hi_both = _dot(hi, rw_ref[...])
        logits = hi_both[:, 0:LANES] + hi_both[:, LANES:2 * LANES] + _dot(lo, rw_ref[:, 0:LANES])
        st[bi]["sel"] = logits + rb_ref[...]

    def route(bi):
        sel_t = st[bi].pop("sel").T
        cls = _route_class([sel_t[e:e + 1, :] for e in range(N_EXPERTS)])
        r0 = bi * SUB_ROWS
        cls_ref[0, r0 // LANES:(r0 + SUB_ROWS) // LANES, :] = jnp.concatenate(
            [cls[:, k * LANES:(k + 1) * LANES] for k in range(SUB_ROWS // LANES)], axis=0)

    def conformer(b):
        conformer_glu(b)
        conformer_conv(b, 0)
        conformer_conv(b, 1)
        conformer_out(b)

    def round_expert_weights():
        wgu_out[0, :, 0:D_EXPERT] = wg_ref[0, 0].astype(BF16)
        wgu_out[0, :, D_EXPERT:2 * D_EXPERT] = wu_ref[0, 0].astype(BF16)
        wd_out[0] = wd_ref[0, 0].astype(BF16)

    def stage1(b):
        pieces = [lambda: load(b), lambda: in_proj(b, 0, 2), lambda: in_proj(b, 2, 4), lambda: in_proj(b, 4, 5),
                  lambda: in_proj(b, 5, 8)]
        if b == 0:
            pieces.insert(2, round_expert_weights)
        return pieces

    def stage2(b):
        return [lambda: gating_mlp(b), lambda: conformer(b), lambda: pooling(b), lambda: short_conv(b)]

    def stage3(b):
        return [lambda: out_proj(b), lambda: router(b), lambda: route(b)]

    for t in range(n_blocks + 2):
        stages = [stage(t - lag) for lag, stage in ((2, stage3), (0, stage1), (1, stage2)) if 0 <= t - lag < n_blocks]
        for k in range(max(len(stage) for stage in stages)):
            for stage in stages:
                if k < len(stage):
                    stage[k]()

    carry_halo(blocks[-1], blocks[0], (0, 1, 4, 8))


def _plan_kernel(cls_ref, dest_ref, meta_ref):
    n = LANES
    cls = cls_ref[...]
    r = lax.broadcasted_iota(I32, (n, n), 0)
    c = lax.broadcasted_iota(I32, (n, n), 1)
    upper = jnp.where(r < c, 1.0, 0.0).astype(BF16)
    lower = jnp.where(c < r, 1.0, 0.0).astype(BF16)
    ones = jnp.ones((n, n), BF16)
    masks = [cls == k for k in range(N_CLASSES)]
    m_all = jnp.concatenate([jnp.where(mk, 1.0, 0.0) for mk in masks], axis=0).astype(BF16)
    within = _dot(m_all, upper)
    rowsum = _dot(m_all, ones)
    rs = jnp.zeros((n, n), F32)
    for k in range(N_CLASSES):
        rs = jnp.where(c == k, rowsum[k * n:(k + 1) * n], rs)
    rs_b = rs.astype(BF16)
    before = _dot(lower, rs_b)
    total = _dot(ones, rs_b)
    ntile = jnp.floor((total + (TMS - 1)) * (1.0 / TMS))
    tstart = _dot(ntile.astype(BF16), upper)
    base = before + tstart * TMS
    dest = jnp.zeros((n, n), F32)
    for k in range(N_CLASSES):
        dest = jnp.where(masks[k], within[k * n:(k + 1) * n] + base[:, k:k + 1], dest)
    dest_ref[...] = dest.astype(I32)

    tend_t = (tstart + ntile).T
    ntile_t = ntile.T
    tile = c.astype(F32)
    ended = jnp.where((r < N_CLASSES) & (tend_t <= tile), 1.0, 0.0)
    is_last = jnp.where((r < N_CLASSES) & (ntile_t > 0) & (tend_t - 1.0 == tile), 1.0, 0.0)
    tcls = jnp.minimum(jnp.sum(ended, axis=0, keepdims=True), N_CLASSES - 1.0)
    grp = (jnp.where(tcls >= N_PAIRS, 1.0, 0.0) + jnp.where(tcls >= 2 * N_PAIRS, 1.0, 0.0)
           + jnp.where(tcls >= 3 * N_PAIRS, 1.0, 0.0))
    pair = tcls - N_PAIRS * grp
    pa = jnp.where(pair >= 3, 1.0, 0.0) + jnp.where(pair >= 5, 1.0, 0.0)
    pb = jnp.where(pair == 0, 1.0, jnp.where((pair == 1) | (pair == 3), 2.0, 3.0))
    n_tiles = tstart[0:1, N_CLASSES:N_CLASSES + 1]
    tile_row = tile[0:1, :]
    active = jnp.where(tile_row < n_tiles, 1.0, 0.0)
    ragged = jnp.where((jnp.sum(is_last, axis=0, keepdims=True) > 0) | (tile_row >= n_tiles), 1.0, 0.0)
    meta = jnp.concatenate(
        [GROUP_SIZE * grp + pa, GROUP_SIZE * grp + pb, active, ragged, jnp.zeros((4, n), F32)], axis=0)
    meta_ref[...] = meta.astype(I32)


def _row_copy_wait(src, dst, sem):
    pltpu.make_async_copy(src, dst, sem).wait()


def _dispatch_kernel(dest_ref, meta_ref, x_ref, xs_ref, buf, zbuf, sems, zsem):
    i = pl.program_id(0)
    n_steps = pl.num_programs(0)
    tile_rows = TMS * ROW_TILES
    n_tiles = xs_ref.shape[0] // tile_rows

    @pl.when(i == 0)
    def _():
        zbuf[...] = jnp.zeros(zbuf.shape, F32)
        for k in range(n_tiles):
            @pl.when(meta_ref[3, k] == 1)
            def _():
                pltpu.make_async_copy(zbuf, xs_ref.at[pl.ds(k * tile_rows, tile_rows), :], zsem).start()
        for k in range(n_tiles):
            @pl.when(meta_ref[3, k] == 1)
            def _():
                _row_copy_wait(zbuf, xs_ref.at[pl.ds(k * tile_rows, tile_rows), :], zsem)

    for s in range(2):
        @pl.when(i > 0)
        def _():
            _row_copy_wait(buf.at[s], xs_ref.at[pl.ds(0, TD * ROW_TILES), :], sems.at[s])

        for j in range(ROW_TILES):
            buf[s, pl.ds(j, TD, stride=ROW_TILES), :] = x_ref[s * TD:(s + 1) * TD, j * LANES:(j + 1) * LANES]
        for t in range(TD):
            d = pl.multiple_of(dest_ref[(i * 2 + s) * TD + t] * ROW_TILES, ROW_TILES)
            pltpu.make_async_copy(buf.at[s, pl.ds(t * ROW_TILES, ROW_TILES), :],
                                  xs_ref.at[pl.ds(d, ROW_TILES), :], sems.at[s]).start(priority=t % 2)

    @pl.when(i == n_steps - 1)
    def _():
        for s in range(2):
            _row_copy_wait(buf.at[s], xs_ref.at[pl.ds(0, TD * ROW_TILES), :], sems.at[s])


def _combine_kernel(dest_ref, ys_ref, x_ref, vec_m_ref, out_ref, buf, sems):
    i = pl.program_id(0)
    n_steps = pl.num_programs(0)

    def issue(step, s):
        for t in range(TD):
            d = pl.multiple_of(dest_ref[(step * 2 + s) * TD + t] * ROW_TILES, ROW_TILES)
            pltpu.make_async_copy(ys_ref.at[pl.ds(d, ROW_TILES), :],
                                  buf.at[s, pl.ds(t * ROW_TILES, ROW_TILES), :], sems.at[s]).start(priority=t % 2)

    @pl.when(i == 0)
    def _():
        for s in range(2):
            issue(0, s)

    for s in range(2):
        _row_copy_wait(ys_ref.at[pl.ds(0, TD * ROW_TILES), :], buf.at[s], sems.at[s])
        f = jnp.concatenate([buf[s, pl.ds(j, TD, stride=ROW_TILES), :] for j in range(ROW_TILES)], axis=1)
        rows = slice(s * TD, (s + 1) * TD)
        out_ref[rows, :] = _layer_norm(ALPHA * x_ref[rows, :] + f, vec_m_ref[0, 3:4, :], vec_m_ref[0, 4:5, :])

        @pl.when(i + 1 < n_steps)
        def _():
            issue(i + 1, s)


def _moe_kernel(meta_ref, xs_ref, *refs):
    i = pl.program_id(0)
    w_refs, rwt_ref, ys_ref = refs[:-2], refs[-2], refs[-1]
    rows = TMS * ROW_TILES

    @pl.when(meta_ref[2, i * TILES_PER_STEP] == 0)
    def _():
        ys_ref[...] = jnp.zeros(ys_ref.shape, F32)

    @pl.when(meta_ref[2, i * TILES_PER_STEP] == 1)
    def _():
        st = [dict() for _ in range(TILES_PER_STEP)]

        def load(s):
            x = jnp.concatenate(
                [xs_ref[pl.ds(s * rows + j, TMS, stride=ROW_TILES), :] for j in range(ROW_TILES)], axis=1)
            st[s]["x"] = x
            st[s]["xb"] = x.astype(BF16)

        def up(s, e):
            gu = _dot(st[s]["xb"], w_refs[4 * s + 2 * e][0])
            st[s]["g", e] = gu[:, 0:D_EXPERT]
            st[s]["u", e] = gu[:, D_EXPERT:2 * D_EXPERT]

        def act(s, e):
            g = st[s].pop(("g", e))
            st[s]["h", e] = ((g * _sigmoid(g)) * st[s].pop(("u", e))).astype(BF16)

        def down(s, e):
            st[s]["y", e] = _dot(st[s].pop(("h", e)), w_refs[4 * s + 2 * e + 1][0])

        def mix(s):
            tile = i * TILES_PER_STEP + s
            x = st[s].pop("x")
            la = jnp.sum(x * rwt_ref[pl.ds(meta_ref[0, tile], 1), :], axis=-1, keepdims=True)
            lb = jnp.sum(x * rwt_ref[pl.ds(meta_ref[1, tile], 1), :], axis=-1, keepdims=True)
            m = jnp.maximum(la, lb)
            pa = jnp.exp(la - m)
            pb = jnp.exp(lb - m)
            den = pa + pb
            f = (pa / den) * st[s].pop(("y", 0)) + (pb / den) * st[s].pop(("y", 1))
            for j in range(ROW_TILES):
                ys_ref[pl.ds(s * rows + j, TMS, stride=ROW_TILES), :] = f[:, j * LANES:(j + 1) * LANES]

        units = [(s, e) for s in range(TILES_PER_STEP) for e in range(2)]
        load(0)
        up(*units[0])
        for k, (s, e) in enumerate(units):
            if k + 1 < len(units):
                nxt = units[k + 1]
                if nxt[1] == 0:
                    load(nxt[0])
                up(*nxt)
            act(s, e)
            down(s, e)
            if e == 1:
                mix(s)


def _full(shape):
    return pl.BlockSpec(shape, lambda *_: (0,) * len(shape))


def _layer_block(shape, layer):
    return pl.BlockSpec((1,) + shape, lambda *_: (layer,) + (0,) * len(shape))


def _mixer_call(x, p, expert_w, layer):
    batch, seq, _ = x.shape
    steps = seq // TS
    slabs = batch * steps // N_EXPERTS
    assert batch * steps == slabs * N_EXPERTS

    def w_in(b, s):
        i = b * steps + s
        return (layer, i // slabs, i % slabs, 0)

    def w_out(b, s):
        i = b * steps + s
        return (i // slabs, i % slabs, 0)

    up_rows, down_rows = D_MODEL // slabs, D_EXPERT // slabs
    in_specs = [
        pl.BlockSpec((1, TS, D_MODEL), lambda b, s: (b, s, 0)),
        _layer_block((D_MODEL, 8 * D_GROUP), layer),
        _layer_block((SUBLANES, D_GROUP), layer), _layer_block((SUBLANES, D_MODEL), layer),
        _layer_block((N_HEADS * CHUNK, CHUNK), layer), _layer_block((CHUNK, D_GROUP), layer),
        _layer_block((HALO, D_GROUP), layer),
        _layer_block((D_GROUP, D_GROUP), layer),
        _layer_block((D_GROUP, D_GROUP), layer),
        _layer_block((SUBLANES, D_GROUP), layer),
        _layer_block((D_MODEL, D_MODEL), layer),
        _full((D_MODEL, 2 * LANES)), _full((1, LANES)),
        pl.BlockSpec((1, 1, up_rows, D_EXPERT), w_in),
        pl.BlockSpec((1, 1, up_rows, D_EXPERT), w_in),
        pl.BlockSpec((1, 1, down_rows, D_MODEL), w_in),
    ]
    out_specs = [
        pl.BlockSpec((1, TS, D_MODEL), lambda b, s: (b, s, 0)),
        pl.BlockSpec((1, TS // LANES, LANES), lambda b, s: (b * steps + s, 0, 0)),
        pl.BlockSpec((1, up_rows, 2 * D_EXPERT), w_out),
        pl.BlockSpec((1, down_rows, D_MODEL), w_out),
    ]
    block_scratch = [
        pltpu.VMEM((HALO + SUB_ROWS, LANES), F32),
        pltpu.VMEM((HALO + SUB_ROWS, LANES), F32),
        pltpu.VMEM((SUB_ROWS, LANES), F32),
        pltpu.VMEM((SUB_ROWS, LANES), F32),
        pltpu.VMEM((HALO + SUB_ROWS, D_GROUP), F32),
        pltpu.VMEM((HALO + SUB_ROWS, D_GROUP), F32),
        pltpu.VMEM((HALO + SUB_ROWS, D_GROUP), F32),
        pltpu.VMEM((HALO + SUB_ROWS, D_GROUP), F32),
        pltpu.VMEM((SC_HALO + SUB_ROWS, D_GROUP), F32),
        pltpu.VMEM((SUB_ROWS, D_MODEL), BF16),
    ]
    scratch = block_scratch * (TS // SUB_ROWS)
    return pl.pallas_call(
        _mixer_kernel,
        grid=(batch, steps),
        in_specs=in_specs,
        out_specs=out_specs,
        out_shape=[jax.ShapeDtypeStruct((batch, seq, D_MODEL), F32),
                   jax.ShapeDtypeStruct((batch * steps, TS // LANES, LANES), I32),
                   jax.ShapeDtypeStruct((N_EXPERTS, D_MODEL, 2 * D_EXPERT), BF16),
                   jax.ShapeDtypeStruct((N_EXPERTS, D_EXPERT, D_MODEL), BF16)],
        scratch_shapes=scratch,
        compiler_params=pltpu.CompilerParams(
            dimension_semantics=("arbitrary", "arbitrary"), vmem_limit_bytes=VMEM_LIMIT),
        name="mixer",
    )(x, *p, *expert_w)


def _plan_call(cls2d):
    return pl.pallas_call(
        _plan_kernel,
        out_shape=[jax.ShapeDtypeStruct((LANES, LANES), I32), jax.ShapeDtypeStruct((SUBLANES, LANES), I32)],
        compiler_params=pltpu.CompilerParams(vmem_limit_bytes=VMEM_LIMIT),
        name="moe_plan",
    )(cls2d)


def _dispatch_call(dest, meta, x2d, n_sorted):
    n_tok = x2d.shape[0]
    return pl.pallas_call(
        _dispatch_kernel,
        grid_spec=pltpu.PrefetchScalarGridSpec(
            num_scalar_prefetch=2,
            grid=(n_tok // (2 * TD),),
            in_specs=[pl.BlockSpec((2 * TD, D_MODEL), lambda i, *_: (i, 0))],
            out_specs=pl.BlockSpec(memory_space=pl.ANY),
            scratch_shapes=[
                pltpu.VMEM((2, TD * ROW_TILES, LANES), F32),
                pltpu.VMEM((TMS * ROW_TILES, LANES), F32),
                pltpu.SemaphoreType.DMA((2,)),
                pltpu.SemaphoreType.DMA(()),
            ]),
        out_shape=jax.ShapeDtypeStruct((n_sorted * ROW_TILES, LANES), F32),
        compiler_params=pltpu.CompilerParams(
            dimension_semantics=("arbitrary",), vmem_limit_bytes=VMEM_LIMIT),
        name="moe_dispatch",
    )(dest, meta, x2d)


def _combine_call(dest, ys, x2d, vec_m, layer):
    n_tok = x2d.shape[0]
    return pl.pallas_call(
        _combine_kernel,
        grid_spec=pltpu.PrefetchScalarGridSpec(
            num_scalar_prefetch=1,
            grid=(n_tok // (2 * TD),),
            in_specs=[pl.BlockSpec(memory_space=pl.ANY),
                      pl.BlockSpec((2 * TD, D_MODEL), lambda i, *_: (i, 0)),
                      _layer_block((SUBLANES, D_MODEL), layer)],
            out_specs=pl.BlockSpec((2 * TD, D_MODEL), lambda i, *_: (i, 0)),
            scratch_shapes=[
                pltpu.VMEM((2, TD * ROW_TILES, LANES), F32),
                pltpu.SemaphoreType.DMA((2,)),
            ]),
        out_shape=jax.ShapeDtypeStruct((n_tok, D_MODEL), F32),
        compiler_params=pltpu.CompilerParams(
            dimension_semantics=("arbitrary",), vmem_limit_bytes=VMEM_LIMIT),
        name="moe_combine",
    )(dest, ys, x2d, vec_m)


def _moe_call(meta, xs, wgu, wd, rwt, n_tiles):
    rows = TILES_PER_STEP * TMS * ROW_TILES
    w_specs, w_args = [], []
    for s in range(TILES_PER_STEP):
        for sel in (0, 1):
            def expert(i, meta, s=s, sel=sel):
                return (meta[sel, i * TILES_PER_STEP + s], 0, 0)
            w_specs += [pl.BlockSpec((1, D_MODEL, 2 * D_EXPERT), expert),
                        pl.BlockSpec((1, D_EXPERT, D_MODEL), expert)]
            w_args += [wgu, wd]
    tile = pl.BlockSpec((rows, LANES), lambda i, meta: (i, 0))
    return pl.pallas_call(
        _moe_kernel,
        grid_spec=pltpu.PrefetchScalarGridSpec(
            num_scalar_prefetch=1,
            grid=(n_tiles // TILES_PER_STEP,),
            in_specs=[tile] + w_specs + [_full((N_EXPERTS, D_MODEL))],
            out_specs=tile),
        out_shape=jax.ShapeDtypeStruct(xs.shape, F32),
        compiler_params=pltpu.CompilerParams(
            dimension_semantics=("arbitrary",), vmem_limit_bytes=VMEM_LIMIT),
        name="moe_experts",
    )(meta, xs, *w_args, rwt)


def _block_diag(w):
    g, d, _ = w.shape
    eye = jnp.eye(g, dtype=w.dtype)
    return (eye[:, None, :, None] * w[:, :, None, :]).reshape(g * d, g * d)


def kernel(x, w_in, gm_ln_g, gm_ln_b, gm_w_s, gm_b_s, cf_dw_w, cf_dw_b, cf_ln_g, cf_ln_b, cf_pw,
           pool_w, pool_scale, sc_w, mix_norm_g, w_o, ln1_g, ln1_b, router_w, router_b,
           exp_w_gate, exp_w_up, exp_w_down, ln2_g, ln2_b):
    batch, seq, _ = x.shape
    n_tok = batch * seq
    assert n_tok == LANES * LANES, "the routing plan lays tokens out as one (128, 128) tile grid"
    assert seq % TS == 0 and TS % SUB_ROWS == 0 and SUB_ROWS % (2 * CONV_ROWS) == 0 and n_tok % (2 * TD) == 0
    n_tiles = n_tok // TMS + N_CLASSES
    assert n_tiles <= LANES and n_tiles % TILES_PER_STEP == 0
    def pack_rows(rows):
        slab = jnp.stack(rows, axis=1)
        return jnp.pad(slab, ((0, 0), (0, SUBLANES - len(rows)), (0, 0)))

    rw = jnp.pad(router_w, ((0, 0), (0, LANES - N_EXPERTS)))
    rw_hi = rw.astype(BF16)
    rw_lo = (rw - rw_hi.astype(F32)).astype(BF16)
    rb = jnp.pad(router_b, (0, LANES - N_EXPERTS)).reshape(1, LANES)
    rwt = router_w.T
    vec_m = pack_rows([mix_norm_g, ln1_g, ln1_b, ln2_g, ln2_b])
    params = (
        w_in.astype(BF16),
        pack_rows([gm_ln_g, gm_ln_b, cf_dw_b, cf_ln_g, cf_ln_b, pool_scale]), vec_m,
        gm_w_s.reshape(DEPTH, N_HEADS * CHUNK, CHUNK),
        jnp.repeat(gm_b_s.transpose(0, 2, 1), HEAD_DIM, axis=2),
        jnp.pad(cf_dw_w, ((0, 0), (0, HALO - CONF_K), (0, 0))),
        cf_pw.astype(BF16),
        jax.vmap(_block_diag)(pool_w).astype(BF16),
        jnp.pad(sc_w, ((0, 0), (0, SUBLANES - SHORT_K), (0, 0))),
        w_o.astype(BF16),
        jnp.concatenate([rw_hi, rw_lo], axis=1), rb,
    )
    for l in range(DEPTH):
        x1, cls, wgu_b, wd_b = _mixer_call(x, params, (exp_w_gate, exp_w_up, exp_w_down), l)
        x1 = x1.reshape(n_tok, D_MODEL)
        dest2d, meta = _plan_call(cls.reshape(LANES, LANES))
        dest = dest2d.reshape(n_tok)
        xs = _dispatch_call(dest, meta, x1, n_tiles * TMS)
        ys = _moe_call(meta, xs, wgu_b, wd_b, rwt, n_tiles)
        x = _combine_call(dest, ys, x1, vec_m, l).reshape(batch, seq, D_MODEL)
    return x
```

```python
import jax
import jax.numpy as jnp
from jax import lax
from jax.experimental import pallas as pl
from jax.experimental.pallas import tpu as pltpu

D_MODEL = 1024
DEPTH = 2
D_GROUP = 256
N_HEADS = 4
HEAD_DIM = 64
CHUNK = 128
CONF_K = 31
SHORT_K = 3
N_EXPERTS = 16
N_GROUPS = 4
GROUP_SIZE = 4
N_PAIRS = 6
N_CLASSES = N_GROUPS * N_PAIRS
D_EXPERT = 512
ALPHA = (2 * DEPTH) ** 0.25
LN_EPS = 1e-5
RMS_EPS = 1e-6

LANES = 128
SUBLANES = 8
ROW_TILES = D_MODEL // LANES
HALO = 32
SC_HALO = SUBLANES
TS = 512
SUB_ROWS = 256
CONV_ROWS = 64
TMS = 256
TILES_PER_STEP = 2
TD = 512
VMEM_LIMIT = 56 * 1024 * 1024

BF16 = jnp.bfloat16
F32 = jnp.float32
I32 = jnp.int32


def _dot(a, b):
    return jnp.dot(a, b, preferred_element_type=F32)


def _layer_norm(x, g, b):
    mu = jnp.mean(x, axis=-1, keepdims=True)
    xc = x - mu
    var = jnp.mean(xc * xc, axis=-1, keepdims=True)
    return xc * lax.rsqrt(var + LN_EPS) * g + b


def _rms_norm(y, g):
    ms = jnp.mean(y * y, axis=-1, keepdims=True)
    return y * lax.rsqrt(ms + RMS_EPS) * g


def _sigmoid(x):
    return 1.0 / (1.0 + jnp.exp(-x))


def _route_class(sel):
    scores = []
    for g in range(N_GROUPS):
        v = sel[g * GROUP_SIZE:(g + 1) * GROUP_SIZE]
        best_pair = None
        for i in range(GROUP_SIZE):
            for j in range(i + 1, GROUP_SIZE):
                p = v[i] + v[j]
                best_pair = p if best_pair is None else jnp.maximum(best_pair, p)
        scores.append(best_pair)
    best = jnp.zeros(scores[0].shape, I32)
    best_score = scores[0]
    for g in range(1, N_GROUPS):
        better = scores[g] > best_score
        best = jnp.where(better, g, best)
        best_score = jnp.where(better, scores[g], best_score)
    v = []
    for j in range(GROUP_SIZE):
        out = sel[j]
        for g in range(1, N_GROUPS):
            out = jnp.where(best == g, sel[g * GROUP_SIZE + j], out)
        v.append(out)
    i0 = jnp.zeros_like(best)
    v0 = v[0]
    for j in range(1, GROUP_SIZE):
        better = v[j] > v0
        i0 = jnp.where(better, j, i0)
        v0 = jnp.where(better, v[j], v0)
    neg = jnp.full_like(v0, -jnp.inf)
    w = [jnp.where(i0 == j, neg, v[j]) for j in range(GROUP_SIZE)]
    i1 = jnp.zeros_like(best)
    v1 = w[0]
    for j in range(1, GROUP_SIZE):
        better = w[j] > v1
        i1 = jnp.where(better, j, i1)
        v1 = jnp.where(better, w[j], v1)
    a = jnp.minimum(i0, i1)
    b = jnp.maximum(i0, i1)
    pair = jnp.where(a == 0, b - 1, jnp.where(a == 1, b + 1, N_PAIRS - 1))
    return best * N_PAIRS + pair


def _mixer_kernel(x_ref, w_in_ref, vec_g_ref, vec_m_ref, gm_w_ref, gm_bs_ref, cf_w_ref, cf_pw_ref,
                  pool_w_ref, sc_w_ref, w_o_ref, rw_ref, rb_ref, wg_ref, wu_ref, wd_ref,
                  x1_ref, cls_ref, wgu_out, wd_out, xs_ref, *scratch):
    s_idx = pl.program_id(1)
    *scratch, zbuf, zsem = scratch
    step = pl.program_id(0) * pl.num_programs(1) + s_idx
    fill_rows = zbuf.shape[0]

    @pl.when(step == 0)
    def _():
        zbuf[...] = jnp.zeros(zbuf.shape, F32)

    fill = pltpu.make_async_copy(
        zbuf, xs_ref.at[pl.ds(pl.multiple_of(step * fill_rows, fill_rows), fill_rows), :], zsem)
    fill.start()
    gm_g, gm_b, cf_b, cf_g, cf_beta, pool_scale = (vec_g_ref[0, k:k + 1, :] for k in range(6))
    mixg, ln_g, ln_b = (vec_m_ref[0, k:k + 1, :] for k in range(3))

    row = lax.broadcasted_iota(I32, (N_HEADS * CHUNK, CHUNK), 0)
    col = lax.broadcasted_iota(I32, (N_HEADS * CHUNK, CHUNK), 1)
    w_tril = jnp.where(col <= (row & (CHUNK - 1)), gm_w_ref[0], 0.0).astype(BF16)
    lane = lax.broadcasted_iota(I32, (CHUNK, D_GROUP), 1)
    lane_t = lax.broadcasted_iota(I32, (SUB_ROWS, D_GROUP), 1)
    win = jnp.where(lane_t < HEAD_DIM, 2,
                    jnp.where(lane_t < 2 * HEAD_DIM, 4, jnp.where(lane_t < 3 * HEAD_DIM, 8, 16)))
    bs = gm_bs_ref[0]

    n_blocks = TS // SUB_ROWS
    per_block = len(scratch) // n_blocks
    blocks = [scratch[i * per_block:(i + 1) * per_block] for i in range(n_blocks)]

    @pl.when(s_idx == 0)
    def _():
        hbuf0, hbuf1, _, _, pbuf, _, _, _, cbuf, _ = blocks[0]
        hbuf0[0:HALO, :] = jnp.zeros((HALO, LANES), F32)
        hbuf1[0:HALO, :] = jnp.zeros((HALO, LANES), F32)
        pbuf[0:HALO, :] = jnp.zeros((HALO, D_GROUP), F32)
        cbuf[0:SC_HALO, :] = jnp.zeros((SC_HALO, D_GROUP), F32)

    def carry_halo(src, dst, which):
        for i in which:
            rows = SC_HALO if i == 8 else HALO
            dst[i][0:rows, :] = src[i][SUB_ROWS:SUB_ROWS + rows, :]

    st = [dict() for _ in range(n_blocks)]

    def load(bi):
        x = x_ref[0, bi * SUB_ROWS:(bi + 1) * SUB_ROWS, :]
        st[bi]["x"] = x
        st[bi]["xb"] = x.astype(BF16)

    def in_proj(bi, lo, hi):
        z = _dot(st[bi]["xb"], w_in_ref[0, :, lo * D_GROUP:hi * D_GROUP])
        for j in range(lo, hi):
            st[bi][j] = z[:, (j - lo) * D_GROUP:(j - lo + 1) * D_GROUP]

    def gating_mlp(bi):
        yn_ref = blocks[bi][9]
        u = st[bi].pop(0)
        v = _layer_norm(st[bi].pop(1), gm_g, gm_b)
        for n in range(SUB_ROWS // CHUNK):
            crows = slice(n * CHUNK, (n + 1) * CHUNK)
            s_all = _dot(w_tril, v[crows, :].astype(BF16))
            s_sel = s_all[3 * CHUNK:4 * CHUNK]
            for h in (2, 1, 0):
                s_sel = jnp.where(lane < (h + 1) * HEAD_DIM, s_all[h * CHUNK:(h + 1) * CHUNK], s_sel)
            y1 = u[crows, :] * (s_sel + bs)
            yn_ref[crows, 0:D_GROUP] = _rms_norm(y1, mixg[:, 0:D_GROUP]).astype(BF16)

    def conformer_glu(bi):
        if bi > 0:
            carry_halo(blocks[bi - 1], blocks[bi], (0, 1))
        glu = st[bi].pop(2) * _sigmoid(st[bi].pop(3))
        for half in range(2):
            blocks[bi][half][HALO:HALO + SUB_ROWS, :] = glu[:, half * LANES:(half + 1) * LANES]

    def conformer_conv(bi, half):
        hb, co = blocks[bi][half], blocks[bi][2 + half]
        lanes = slice(half * LANES, (half + 1) * LANES)
        bias = cf_b[:, lanes]
        for q in range(SUB_ROWS // (2 * CONV_ROWS)):
            for parity in range(2):
                out0 = q * 2 * CONV_ROWS + parity
                base = out0 + HALO - (CONF_K - 1)
                acc = jnp.zeros((CONV_ROWS, LANES), F32) + bias
                for k in range(CONF_K):
                    acc = acc + cf_w_ref[0, k:k + 1, lanes] * hb[pl.ds(base + k, CONV_ROWS, stride=2), :]
                co[pl.ds(out0, CONV_ROWS, stride=2), :] = acc

    def conformer_out(bi):
        cbo0, cbo1, yn_ref = blocks[bi][2], blocks[bi][3], blocks[bi][9]
        hln = _layer_norm(jnp.concatenate([cbo0[...], cbo1[...]], axis=1), cf_g, cf_beta)
        y2 = _dot((hln * _sigmoid(hln)).astype(BF16), cf_pw_ref[0])
        yn_ref[:, D_GROUP:2 * D_GROUP] = _rms_norm(y2, mixg[:, D_GROUP:2 * D_GROUP]).astype(BF16)

    def pooling(bi):
        pbuf, s2buf, s4buf, s8buf = blocks[bi][4:8]
        yn_ref = blocks[bi][9]
        if bi > 0:
            carry_halo(blocks[bi - 1], blocks[bi], (4,))
        zc = st[bi].pop(4)
        pbuf[HALO:HALO + SUB_ROWS, :] = zc
        n2 = HALO + SUB_ROWS - 8
        s2buf[8:8 + n2, :] = pbuf[8:8 + n2, :] + pbuf[7:7 + n2, :]
        n4 = HALO + SUB_ROWS - 16
        s4buf[16:16 + n4, :] = s2buf[16:16 + n4, :] + s2buf[14:14 + n4, :]
        n8 = HALO + SUB_ROWS - 24
        s8buf[24:24 + n8, :] = s4buf[24:24 + n8, :] + s4buf[20:20 + n8, :]
        cur = slice(HALO, HALO + SUB_ROWS)
        s16 = s8buf[cur, :] + s8buf[HALO - 8:HALO - 8 + SUB_ROWS, :]
        pos1 = lax.broadcasted_iota(I32, (SUB_ROWS, D_GROUP), 0) + (s_idx * TS + bi * SUB_ROWS + 1)
        wsum = jnp.where(lane_t < HEAD_DIM, s2buf[cur, :],
                         jnp.where(lane_t < 2 * HEAD_DIM, s4buf[cur, :],
                                   jnp.where(lane_t < 3 * HEAD_DIM, s8buf[cur, :], s16)))
        count = jnp.minimum(pos1, win).astype(F32)
        pooled = wsum / count - zc
        y3 = _dot(pooled.astype(BF16), pool_w_ref[0]) * pool_scale
        yn_ref[:, 2 * D_GROUP:3 * D_GROUP] = _rms_norm(y3, mixg[:, 2 * D_GROUP:3 * D_GROUP]).astype(BF16)

    def short_conv(bi):
        cbuf, yn_ref = blocks[bi][8], blocks[bi][9]
        if bi > 0:
            carry_halo(blocks[bi - 1], blocks[bi], (8,))
        cbuf[SC_HALO:SC_HALO + SUB_ROWS, :] = st[bi].pop(6) * st[bi].pop(7)
        conv = jnp.zeros((SUB_ROWS, D_GROUP), F32)
        for k in range(SHORT_K):
            off = SC_HALO - (SHORT_K - 1) + k
            conv = conv + sc_w_ref[0, k:k + 1, :] * cbuf[off:off + SUB_ROWS, :]
        y4 = st[bi].pop(5) * conv
        yn_ref[:, 3 * D_GROUP:4 * D_GROUP] = _rms_norm(y4, mixg[:, 3 * D_GROUP:4 * D_GROUP]).astype(BF16)

    def out_proj(bi):
        m = _dot(blocks[bi][9][...], w_o_ref[0])
        x1 = _layer_norm(ALPHA * st[bi].pop("x") + m, ln_g, ln_b)
        x1_ref[0, bi * SUB_ROWS:(bi + 1) * SUB_ROWS, :] = x1
        st[bi]["x1"] = x1

    def router(bi):
        x1 = st[bi].pop("x1")
        hi = x1.astype(BF16)
        lo = (x1 - hi.astype(F32)).astype(BF16)
        hi_both = _dot(hi, rw_ref[...])
        logits = hi_both[:, 0:LANES] + hi_both[:, LANES:2 * LANES] + _dot(lo, rw_ref[:, 0:LANES])
        st[bi]["sel"] = logits + rb_ref[...]

    def route(bi):
        sel_t = st[bi].pop("sel").T
        cls = _route_class([sel_t[e:e + 1, :] for e in range(N_EXPERTS)])
        r0 = bi * SUB_ROWS
        cls_ref[0, r0 // LANES:(r0 + SUB_ROWS) // LANES, :] = jnp.concatenate(
            [cls[:, k * LANES:(k + 1) * LANES] for k in range(SUB_ROWS // LANES)], axis=0)

    def conformer(b):
        conformer_glu(b)
        conformer_conv(b, 0)
        conformer_conv(b, 1)
        conformer_out(b)

    def round_expert_weights():
        wgu_out[0, :, 0:D_EXPERT] = wg_ref[0, 0].astype(BF16)
        wgu_out[0, :, D_EXPERT:2 * D_EXPERT] = wu_ref[0, 0].astype(BF16)
        wd_out[0] = wd_ref[0, 0].astype(BF16)

    def stage1(b):
        pieces = [lambda: load(b), lambda: in_proj(b, 0, 2), lambda: in_proj(b, 2, 4), lambda: in_proj(b, 4, 5),
                  lambda: in_proj(b, 5, 8)]
        if b == 0:
            pieces.insert(2, round_expert_weights)
        return pieces

    def stage2(b):
        return [lambda: gating_mlp(b), lambda: conformer(b), lambda: pooling(b), lambda: short_conv(b)]

    def stage3(b):
        return [lambda: out_proj(b), lambda: router(b), lambda: route(b)]

    for t in range(n_blocks + 2):
        stages = [stage(t - lag) for lag, stage in ((2, stage3), (0, stage1), (1, stage2)) if 0 <= t - lag < n_blocks]
        for k in range(max(len(stage) for stage in stages)):
            for stage in stages:
                if k < len(stage):
                    stage[k]()

    carry_halo(blocks[-1], blocks[0], (0, 1, 4, 8))
    fill.wait()


def _plan_kernel(cls_ref, dest_ref, meta_ref):
    n = LANES
    cls = cls_ref[...]
    r = lax.broadcasted_iota(I32, (n, n), 0)
    c = lax.broadcasted_iota(I32, (n, n), 1)
    upper = jnp.where(r < c, 1.0, 0.0).astype(BF16)
    lower = jnp.where(c < r, 1.0, 0.0).astype(BF16)
    ones = jnp.ones((n, n), BF16)
    masks = [cls == k for k in range(N_CLASSES)]
    m_all = jnp.concatenate([jnp.where(mk, 1.0, 0.0) for mk in masks], axis=0).astype(BF16)
    within = _dot(m_all, upper)
    rowsum = _dot(m_all, ones)
    rs = jnp.zeros((n, n), F32)
    for k in range(N_CLASSES):
        rs = jnp.where(c == k, rowsum[k * n:(k + 1) * n], rs)
    rs_b = rs.astype(BF16)
    before = _dot(lower, rs_b)
    total = _dot(ones, rs_b)
    ntile = jnp.floor((total + (TMS - 1)) * (1.0 / TMS))
    tstart = _dot(ntile.astype(BF16), upper)
    base = before + tstart * TMS
    dest = jnp.zeros((n, n), F32)
    for k in range(N_CLASSES):
        dest = jnp.where(masks[k], within[k * n:(k + 1) * n] + base[:, k:k + 1], dest)
    dest_ref[...] = dest.astype(I32)

    tend_t = (tstart + ntile).T
    tile = c.astype(F32)
    ended = jnp.where((r < N_CLASSES) & (tend_t <= tile), 1.0, 0.0)
    tcls = jnp.minimum(jnp.sum(ended, axis=0, keepdims=True), N_CLASSES - 1.0)
    grp = (jnp.where(tcls >= N_PAIRS, 1.0, 0.0) + jnp.where(tcls >= 2 * N_PAIRS, 1.0, 0.0)
           + jnp.where(tcls >= 3 * N_PAIRS, 1.0, 0.0))
    pair = tcls - N_PAIRS * grp
    pa = jnp.where(pair >= 3, 1.0, 0.0) + jnp.where(pair >= 5, 1.0, 0.0)
    pb = jnp.where(pair == 0, 1.0, jnp.where((pair == 1) | (pair == 3), 2.0, 3.0))
    n_tiles = tstart[0:1, N_CLASSES:N_CLASSES + 1]
    tile_row = tile[0:1, :]
    active = jnp.where(tile_row < n_tiles, 1.0, 0.0)
    meta = jnp.concatenate(
        [GROUP_SIZE * grp + pa, GROUP_SIZE * grp + pb, active, jnp.zeros((SUBLANES - 3, n), F32)], axis=0)
    meta_ref[...] = meta.astype(I32)


def _row_copy_wait(src, dst, sem):
    pltpu.make_async_copy(src, dst, sem).wait()


def _dispatch_kernel(dest_ref, x_ref, xs_zero_ref, xs_ref, buf, sems):
    del xs_zero_ref
    i = pl.program_id(0)
    n_steps = pl.num_programs(0)

    for s in range(2):
        @pl.when(i > 0)
        def _():
            _row_copy_wait(buf.at[s], xs_ref.at[pl.ds(0, TD * ROW_TILES), :], sems.at[s])

        for j in range(ROW_TILES):
            buf[s, pl.ds(j, TD, stride=ROW_TILES), :] = x_ref[s * TD:(s + 1) * TD, j * LANES:(j + 1) * LANES]
        for t in range(TD):
            d = pl.multiple_of(dest_ref[(i * 2 + s) * TD + t] * ROW_TILES, ROW_TILES)
            pltpu.make_async_copy(buf.at[s, pl.ds(t * ROW_TILES, ROW_TILES), :],
                                  xs_ref.at[pl.ds(d, ROW_TILES), :], sems.at[s]).start(priority=t % 2)

    @pl.when(i == n_steps - 1)
    def _():
        for s in range(2):
            _row_copy_wait(buf.at[s], xs_ref.at[pl.ds(0, TD * ROW_TILES), :], sems.at[s])


def _combine_kernel(dest_ref, ys_ref, x_ref, vec_m_ref, out_ref, buf, sems):
    i = pl.program_id(0)
    n_steps = pl.num_programs(0)

    def issue(step, s):
        for t in range(TD):
            d = pl.multiple_of(dest_ref[(step * 2 + s) * TD + t] * ROW_TILES, ROW_TILES)
            pltpu.make_async_copy(ys_ref.at[pl.ds(d, ROW_TILES), :],
                                  buf.at[s, pl.ds(t * ROW_TILES, ROW_TILES), :], sems.at[s]).start(priority=t % 2)

    @pl.when(i == 0)
    def _():
        for s in range(2):
            issue(0, s)

    for s in range(2):
        _row_copy_wait(ys_ref.at[pl.ds(0, TD * ROW_TILES), :], buf.at[s], sems.at[s])
        f = jnp.concatenate([buf[s, pl.ds(j, TD, stride=ROW_TILES), :] for j in range(ROW_TILES)], axis=1)
        rows = slice(s * TD, (s + 1) * TD)
        out_ref[rows, :] = _layer_norm(ALPHA * x_ref[rows, :] + f, vec_m_ref[0, 3:4, :], vec_m_ref[0, 4:5, :])

        @pl.when(i + 1 < n_steps)
        def _():
            issue(i + 1, s)


def _moe_kernel(meta_ref, xs_ref, *refs):
    i = pl.program_id(0)
    w_refs, rwt_ref, ys_ref = refs[:-2], refs[-2], refs[-1]
    rows = TMS * ROW_TILES

    @pl.when(meta_ref[2, i * TILES_PER_STEP] == 0)
    def _():
        ys_ref[...] = jnp.zeros(ys_ref.shape, F32)

    @pl.when(meta_ref[2, i * TILES_PER_STEP] == 1)
    def _():
        st = [dict() for _ in range(TILES_PER_STEP)]

        def load(s):
            x = jnp.concatenate(
                [xs_ref[pl.ds(s * rows + j, TMS, stride=ROW_TILES), :] for j in range(ROW_TILES)], axis=1)
            st[s]["x"] = x
            st[s]["xb"] = x.astype(BF16)

        def up(s, e):
            gu = _dot(st[s]["xb"], w_refs[4 * s + 2 * e][0])
            st[s]["g", e] = gu[:, 0:D_EXPERT]
            st[s]["u", e] = gu[:, D_EXPERT:2 * D_EXPERT]

        def act(s, e):
            g = st[s].pop(("g", e))
            st[s]["h", e] = ((g * _sigmoid(g)) * st[s].pop(("u", e))).astype(BF16)

        def down(s, e):
            st[s]["y", e] = _dot(st[s].pop(("h", e)), w_refs[4 * s + 2 * e + 1][0])

        def mix(s):
            tile = i * TILES_PER_STEP + s
            x = st[s].pop("x")
            la = jnp.sum(x * rwt_ref[pl.ds(meta_ref[0, tile], 1), :], axis=-1, keepdims=True)
            lb = jnp.sum(x * rwt_ref[pl.ds(meta_ref[1, tile], 1), :], axis=-1, keepdims=True)
            m = jnp.maximum(la, lb)
            pa = jnp.exp(la - m)
            pb = jnp.exp(lb - m)
            den = pa + pb
            f = (pa / den) * st[s].pop(("y", 0)) + (pb / den) * st[s].pop(("y", 1))
            for j in range(ROW_TILES):
                ys_ref[pl.ds(s * rows + j, TMS, stride=ROW_TILES), :] = f[:, j * LANES:(j + 1) * LANES]

        units = [(s, e) for s in range(TILES_PER_STEP) for e in range(2)]
        load(0)
        up(*units[0])
        for k, (s, e) in enumerate(units):
            if k + 1 < len(units):
                nxt = units[k + 1]
                if nxt[1] == 0:
                    load(nxt[0])
                up(*nxt)
            act(s, e)
            down(s, e)
            if e == 1:
                mix(s)


def _full(shape):
    return pl.BlockSpec(shape, lambda *_: (0,) * len(shape))


def _layer_block(shape, layer):
    return pl.BlockSpec((1,) + shape, lambda *_: (layer,) + (0,) * len(shape))


def _mixer_call(x, p, expert_w, n_sorted, layer):
    batch, seq, _ = x.shape
    steps = seq // TS
    fill_rows = n_sorted * ROW_TILES // (batch * steps)
    assert fill_rows * batch * steps == n_sorted * ROW_TILES and fill_rows % SUBLANES == 0
    slabs = batch * steps // N_EXPERTS
    assert batch * steps == slabs * N_EXPERTS

    def w_in(b, s):
        i = b * steps + s
        return (layer, i // slabs, i % slabs, 0)

    def w_out(b, s):
        i = b * steps + s
        return (i // slabs, i % slabs, 0)

    up_rows, down_rows = D_MODEL // slabs, D_EXPERT // slabs
    in_specs = [
        pl.BlockSpec((1, TS, D_MODEL), lambda b, s: (b, s, 0)),
        _layer_block((D_MODEL, 8 * D_GROUP), layer),
        _layer_block((SUBLANES, D_GROUP), layer), _layer_block((SUBLANES, D_MODEL), layer),
        _layer_block((N_HEADS * CHUNK, CHUNK), layer), _layer_block((CHUNK, D_GROUP), layer),
        _layer_block((HALO, D_GROUP), layer),
        _layer_block((D_GROUP, D_GROUP), layer),
        _layer_block((D_GROUP, D_GROUP), layer),
        _layer_block((SUBLANES, D_GROUP), layer),
        _layer_block((D_MODEL, D_MODEL), layer),
        _full((D_MODEL, 2 * LANES)), _full((1, LANES)),
        pl.BlockSpec((1, 1, up_rows, D_EXPERT), w_in),
        pl.BlockSpec((1, 1, up_rows, D_EXPERT), w_in),
        pl.BlockSpec((1, 1, down_rows, D_MODEL), w_in),
    ]
    out_specs = [
        pl.BlockSpec((1, TS, D_MODEL), lambda b, s: (b, s, 0)),
        pl.BlockSpec((1, TS // LANES, LANES), lambda b, s: (b * steps + s, 0, 0)),
        pl.BlockSpec((1, up_rows, 2 * D_EXPERT), w_out),
        pl.BlockSpec((1, down_rows, D_MODEL), w_out),
        pl.BlockSpec(memory_space=pl.ANY),
    ]
    block_scratch = [
        pltpu.VMEM((HALO + SUB_ROWS, LANES), F32),
        pltpu.VMEM((HALO + SUB_ROWS, LANES), F32),
        pltpu.VMEM((SUB_ROWS, LANES), F32),
        pltpu.VMEM((SUB_ROWS, LANES), F32),
        pltpu.VMEM((HALO + SUB_ROWS, D_GROUP), F32),
        pltpu.VMEM((HALO + SUB_ROWS, D_GROUP), F32),
        pltpu.VMEM((HALO + SUB_ROWS, D_GROUP), F32),
        pltpu.VMEM((HALO + SUB_ROWS, D_GROUP), F32),
        pltpu.VMEM((SC_HALO + SUB_ROWS, D_GROUP), F32),
        pltpu.VMEM((SUB_ROWS, D_MODEL), BF16),
    ]
    scratch = block_scratch * (TS // SUB_ROWS) + [pltpu.VMEM((fill_rows, LANES), F32), pltpu.SemaphoreType.DMA(())]
    return pl.pallas_call(
        _mixer_kernel,
        grid=(batch, steps),
        in_specs=in_specs,
        out_specs=out_specs,
        out_shape=[jax.ShapeDtypeStruct((batch, seq, D_MODEL), F32),
                   jax.ShapeDtypeStruct((batch * steps, TS // LANES, LANES), I32),
                   jax.ShapeDtypeStruct((N_EXPERTS, D_MODEL, 2 * D_EXPERT), BF16),
                   jax.ShapeDtypeStruct((N_EXPERTS, D_EXPERT, D_MODEL), BF16),
                   jax.ShapeDtypeStruct((n_sorted * ROW_TILES, LANES), F32)],
        scratch_shapes=scratch,
        compiler_params=pltpu.CompilerParams(
            dimension_semantics=("arbitrary", "arbitrary"), vmem_limit_bytes=VMEM_LIMIT),
        name="mixer",
    )(x, *p, *expert_w)


def _plan_call(cls2d):
    return pl.pallas_call(
        _plan_kernel,
        out_shape=[jax.ShapeDtypeStruct((LANES, LANES), I32), jax.ShapeDtypeStruct((SUBLANES, LANES), I32)],
        compiler_params=pltpu.CompilerParams(vmem_limit_bytes=VMEM_LIMIT),
        name="moe_plan",
    )(cls2d)


def _dispatch_call(dest, x2d, xs_zero):
    n_tok = x2d.shape[0]
    return pl.pallas_call(
        _dispatch_kernel,
        grid_spec=pltpu.PrefetchScalarGridSpec(
            num_scalar_prefetch=1,
            grid=(n_tok // (2 * TD),),
            in_specs=[pl.BlockSpec((2 * TD, D_MODEL), lambda i, *_: (i, 0)),
                      pl.BlockSpec(memory_space=pl.ANY)],
            out_specs=pl.BlockSpec(memory_space=pl.ANY),
            scratch_shapes=[
                pltpu.VMEM((2, TD * ROW_TILES, LANES), F32),
                pltpu.SemaphoreType.DMA((2,)),
            ]),
        out_shape=jax.ShapeDtypeStruct(xs_zero.shape, F32),
        input_output_aliases={2: 0},
        compiler_params=pltpu.CompilerParams(
            dimension_semantics=("arbitrary",), vmem_limit_bytes=VMEM_LIMIT),
        name="moe_dispatch",
    )(dest, x2d, xs_zero)


def _combine_call(dest, ys, x2d, vec_m, layer):
    n_tok = x2d.shape[0]
    return pl.pallas_call(
        _combine_kernel,
        grid_spec=pltpu.PrefetchScalarGridSpec(
            num_scalar_prefetch=1,
            grid=(n_tok // (2 * TD),),
            in_specs=[pl.BlockSpec(memory_space=pl.ANY),
                      pl.BlockSpec((2 * TD, D_MODEL), lambda i, *_: (i, 0)),
                      _layer_block((SUBLANES, D_MODEL), layer)],
            out_specs=pl.BlockSpec((2 * TD, D_MODEL), lambda i, *_: (i, 0)),
            scratch_shapes=[
                pltpu.VMEM((2, TD * ROW_TILES, LANES), F32),
                pltpu.SemaphoreType.DMA((2,)),
            ]),
        out_shape=jax.ShapeDtypeStruct((n_tok, D_MODEL), F32),
        compiler_params=pltpu.CompilerParams(
            dimension_semantics=("arbitrary",), vmem_limit_bytes=VMEM_LIMIT),
        name="moe_combine",
    )(dest, ys, x2d, vec_m)


def _moe_call(meta, xs, wgu, wd, rwt, n_tiles):
    rows = TILES_PER_STEP * TMS * ROW_TILES
    w_specs, w_args = [], []
    for s in range(TILES_PER_STEP):
        for sel in (0, 1):
            def expert(i, meta, s=s, sel=sel):
                return (meta[sel, i * TILES_PER_STEP + s], 0, 0)
            w_specs += [pl.BlockSpec((1, D_MODEL, 2 * D_EXPERT), expert),
                        pl.BlockSpec((1, D_EXPERT, D_MODEL), expert)]
            w_args += [wgu, wd]
    tile = pl.BlockSpec((rows, LANES), lambda i, meta: (i, 0))
    return pl.pallas_call(
        _moe_kernel,
        grid_spec=pltpu.PrefetchScalarGridSpec(
            num_scalar_prefetch=1,
            grid=(n_tiles // TILES_PER_STEP,),
            in_specs=[tile] + w_specs + [_full((N_EXPERTS, D_MODEL))],
            out_specs=tile),
        out_shape=jax.ShapeDtypeStruct(xs.shape, F32),
        compiler_params=pltpu.CompilerParams(
            dimension_semantics=("arbitrary",), vmem_limit_bytes=VMEM_LIMIT),
        name="moe_experts",
    )(meta, xs, *w_args, rwt)


def _block_diag(w):
    g, d, _ = w.shape
    eye = jnp.eye(g, dtype=w.dtype)
    return (eye[:, None, :, None] * w[:, :, None, :]).reshape(g * d, g * d)


def kernel(x, w_in, gm_ln_g, gm_ln_b, gm_w_s, gm_b_s, cf_dw_w, cf_dw_b, cf_ln_g, cf_ln_b, cf_pw,
           pool_w, pool_scale, sc_w, mix_norm_g, w_o, ln1_g, ln1_b, router_w, router_b,
           exp_w_gate, exp_w_up, exp_w_down, ln2_g, ln2_b):
    batch, seq, _ = x.shape
    n_tok = batch * seq
    assert n_tok == LANES * LANES, "the routing plan lays tokens out as one (128, 128) tile grid"
    assert seq % TS == 0 and TS % SUB_ROWS == 0 and SUB_ROWS % (2 * CONV_ROWS) == 0 and n_tok % (2 * TD) == 0
    n_tiles = n_tok // TMS + N_CLASSES
    assert n_tiles <= LANES and n_tiles % TILES_PER_STEP == 0
    def pack_rows(rows):
        slab = jnp.stack(rows, axis=1)
        return jnp.pad(slab, ((0, 0), (0, SUBLANES - len(rows)), (0, 0)))

    rw = jnp.pad(router_w, ((0, 0), (0, LANES - N_EXPERTS)))
    rw_hi = rw.astype(BF16)
    rw_lo = (rw - rw_hi.astype(F32)).astype(BF16)
    rb = jnp.pad(router_b, (0, LANES - N_EXPERTS)).reshape(1, LANES)
    rwt = router_w.T
    vec_m = pack_rows([mix_norm_g, ln1_g, ln1_b, ln2_g, ln2_b])
    params = (
        w_in.astype(BF16),
        pack_rows([gm_ln_g, gm_ln_b, cf_dw_b, cf_ln_g, cf_ln_b, pool_scale]), vec_m,
        gm_w_s.reshape(DEPTH, N_HEADS * CHUNK, CHUNK),
        jnp.repeat(gm_b_s.transpose(0, 2, 1), HEAD_DIM, axis=2),
        jnp.pad(cf_dw_w, ((0, 0), (0, HALO - CONF_K), (0, 0))),
        cf_pw.astype(BF16),
        jax.vmap(_block_diag)(pool_w).astype(BF16),
        jnp.pad(sc_w, ((0, 0), (0, SUBLANES - SHORT_K), (0, 0))),
        w_o.astype(BF16),
        jnp.concatenate([rw_hi, rw_lo], axis=1), rb,
    )
    for l in range(DEPTH):
        x1, cls, wgu_b, wd_b, xs_zero = _mixer_call(
            x, params, (exp_w_gate, exp_w_up, exp_w_down), n_tiles * TMS, l)
        x1 = x1.reshape(n_tok, D_MODEL)
        dest2d, meta = _plan_call(cls.reshape(LANES, LANES))
        dest = dest2d.reshape(n_tok)
        xs = _dispatch_call(dest, x1, xs_zero)
        ys = _moe_call(meta, xs, wgu_b, wd_b, rwt, n_tiles)
        x = _combine_call(dest, ys, x1, vec_m, l).reshape(batch, seq, D_MODEL)
    return x
```

```python
import jax
import jax.numpy as jnp
from jax import lax
from jax.experimental import pallas as pl
from jax.experimental.pallas import tpu as pltpu

D_MODEL = 1024
DEPTH = 2
D_GROUP = 256
N_HEADS = 4
HEAD_DIM = 64
CHUNK = 128
CONF_K = 31
SHORT_K = 3
N_EXPERTS = 16
N_GROUPS = 4
GROUP_SIZE = 4
N_PAIRS = 6
N_CLASSES = N_GROUPS * N_PAIRS
D_EXPERT = 512
ALPHA = (2 * DEPTH) ** 0.25
LN_EPS = 1e-5
RMS_EPS = 1e-6

LANES = 128
SUBLANES = 8
ROW_TILES = D_MODEL // LANES
HALO = 32
SC_HALO = SUBLANES
TS = 512
SUB_ROWS = 256
CONV_ROWS = 64
TMS = 256
TILES_PER_STEP = 2
TD = 512
V7X_VMEM_BYTES = 64 * 1024 * 1024
VMEM_LIMIT = V7X_VMEM_BYTES * 7 // 8

BF16 = jnp.bfloat16
F32 = jnp.float32
I32 = jnp.int32


def _dot(a, b):
    return jnp.dot(a, b, preferred_element_type=F32)


def _layer_norm(x, g, b):
    mu = jnp.mean(x, axis=-1, keepdims=True)
    xc = x - mu
    var = jnp.mean(xc * xc, axis=-1, keepdims=True)
    return xc * lax.rsqrt(var + LN_EPS) * g + b


def _rms_norm(y, g):
    ms = jnp.mean(y * y, axis=-1, keepdims=True)
    return y * lax.rsqrt(ms + RMS_EPS) * g


def _sigmoid(x):
    return 1.0 / (1.0 + jnp.exp(-x))


def _route_class(sel):
    scores = []
    for g in range(N_GROUPS):
        v = sel[g * GROUP_SIZE:(g + 1) * GROUP_SIZE]
        best_pair = None
        for i in range(GROUP_SIZE):
            for j in range(i + 1, GROUP_SIZE):
                p = v[i] + v[j]
                best_pair = p if best_pair is None else jnp.maximum(best_pair, p)
        scores.append(best_pair)
    best = jnp.zeros(scores[0].shape, I32)
    best_score = scores[0]
    for g in range(1, N_GROUPS):
        better = scores[g] > best_score
        best = jnp.where(better, g, best)
        best_score = jnp.where(better, scores[g], best_score)
    v = []
    for j in range(GROUP_SIZE):
        out = sel[j]
        for g in range(1, N_GROUPS):
            out = jnp.where(best == g, sel[g * GROUP_SIZE + j], out)
        v.append(out)
    i0 = jnp.zeros_like(best)
    v0 = v[0]
    for j in range(1, GROUP_SIZE):
        better = v[j] > v0
        i0 = jnp.where(better, j, i0)
        v0 = jnp.where(better, v[j], v0)
    neg = jnp.full_like(v0, -jnp.inf)
    w = [jnp.where(i0 == j, neg, v[j]) for j in range(GROUP_SIZE)]
    i1 = jnp.zeros_like(best)
    v1 = w[0]
    for j in range(1, GROUP_SIZE):
        better = w[j] > v1
        i1 = jnp.where(better, j, i1)
        v1 = jnp.where(better, w[j], v1)
    a = jnp.minimum(i0, i1)
    b = jnp.maximum(i0, i1)
    pair = jnp.where(a == 0, b - 1, jnp.where(a == 1, b + 1, N_PAIRS - 1))
    return best * N_PAIRS + pair


def _mixer_kernel(x_ref, w_in_ref, vec_g_ref, vec_m_ref, gm_w_ref, gm_bs_ref, cf_w_ref, cf_pw_ref,
                  pool_w_ref, sc_w_ref, w_o_ref, rw_ref, rb_ref, wg_ref, wu_ref, wd_ref,
                  x1_ref, cls_ref, wgu_out, wd_out, xs_ref, *scratch):
    s_idx = pl.program_id(1)
    *scratch, zbuf, zsem = scratch
    step = pl.program_id(0) * pl.num_programs(1) + s_idx
    fill_rows = zbuf.shape[0]

    @pl.when(step == 0)
    def _():
        zbuf[...] = jnp.zeros(zbuf.shape, F32)

    fill = pltpu.make_async_copy(
        zbuf, xs_ref.at[pl.ds(pl.multiple_of(step * fill_rows, fill_rows), fill_rows), :], zsem)
    fill.start()
    gm_g, gm_b, cf_b, cf_g, cf_beta, pool_scale = (vec_g_ref[0, k:k + 1, :] for k in range(6))
    mixg, ln_g, ln_b = (vec_m_ref[0, k:k + 1, :] for k in range(3))

    row = lax.broadcasted_iota(I32, (N_HEADS * CHUNK, CHUNK), 0)
    col = lax.broadcasted_iota(I32, (N_HEADS * CHUNK, CHUNK), 1)
    w_tril = jnp.where(col <= (row & (CHUNK - 1)), gm_w_ref[0], 0.0).astype(BF16)
    lane = lax.broadcasted_iota(I32, (CHUNK, D_GROUP), 1)
    lane_t = lax.broadcasted_iota(I32, (SUB_ROWS, D_GROUP), 1)
    win = jnp.where(lane_t < HEAD_DIM, 2,
                    jnp.where(lane_t < 2 * HEAD_DIM, 4, jnp.where(lane_t < 3 * HEAD_DIM, 8, 16)))
    bs = gm_bs_ref[0]

    n_blocks = TS // SUB_ROWS
    per_block = len(scratch) // n_blocks
    blocks = [scratch[i * per_block:(i + 1) * per_block] for i in range(n_blocks)]

    @pl.when(s_idx == 0)
    def _():
        hbuf0, hbuf1, _, _, pbuf, _, _, _, cbuf, _ = blocks[0]
        hbuf0[0:HALO, :] = jnp.zeros((HALO, LANES), F32)
        hbuf1[0:HALO, :] = jnp.zeros((HALO, LANES), F32)
        pbuf[0:HALO, :] = jnp.zeros((HALO, D_GROUP), F32)
        cbuf[0:SC_HALO, :] = jnp.zeros((SC_HALO, D_GROUP), F32)

    def carry_halo(src, dst, which):
        for i in which:
            rows = SC_HALO if i == 8 else HALO
            dst[i][0:rows, :] = src[i][SUB_ROWS:SUB_ROWS + rows, :]

    st = [dict() for _ in range(n_blocks)]

    def load(bi):
        x = x_ref[0, bi * SUB_ROWS:(bi + 1) * SUB_ROWS, :]
        st[bi]["x"] = x
        st[bi]["xb"] = x.astype(BF16)

    def in_proj(bi, lo, hi):
        z = _dot(st[bi]["xb"], w_in_ref[0, :, lo * D_GROUP:hi * D_GROUP])
        for j in range(lo, hi):
            st[bi][j] = z[:, (j - lo) * D_GROUP:(j - lo + 1) * D_GROUP]

    def gating_mlp(bi):
        yn_ref = blocks[bi][9]
        u = st[bi].pop(0)
        v = _layer_norm(st[bi].pop(1), gm_g, gm_b)
        for n in range(SUB_ROWS // CHUNK):
            crows = slice(n * CHUNK, (n + 1) * CHUNK)
            s_all = _dot(w_tril, v[crows, :].astype(BF16))
            s_sel = s_all[3 * CHUNK:4 * CHUNK]
            for h in (2, 1, 0):
                s_sel = jnp.where(lane < (h + 1) * HEAD_DIM, s_all[h * CHUNK:(h + 1) * CHUNK], s_sel)
            y1 = u[crows, :] * (s_sel + bs)
            yn_ref[crows, 0:D_GROUP] = _rms_norm(y1, mixg[:, 0:D_GROUP]).astype(BF16)

    def conformer_glu(bi):
        if bi > 0:
            carry_halo(blocks[bi - 1], blocks[bi], (0, 1))
        glu = st[bi].pop(2) * _sigmoid(st[bi].pop(3))
        for half in range(2):
            blocks[bi][half][HALO:HALO + SUB_ROWS, :] = glu[:, half * LANES:(half + 1) * LANES]

    def conformer_conv(bi, half):
        hb, co = blocks[bi][half], blocks[bi][2 + half]
        lanes = slice(half * LANES, (half + 1) * LANES)
        bias = cf_b[:, lanes]
        for q in range(SUB_ROWS // (2 * CONV_ROWS)):
            for parity in range(2):
                out0 = q * 2 * CONV_ROWS + parity
                base = out0 + HALO - (CONF_K - 1)
                acc = jnp.zeros((CONV_ROWS, LANES), F32) + bias
                for k in range(CONF_K):
                    acc = acc + cf_w_ref[0, k:k + 1, lanes] * hb[pl.ds(base + k, CONV_ROWS, stride=2), :]
                co[pl.ds(out0, CONV_ROWS, stride=2), :] = acc

    def conformer_out(bi):
        cbo0, cbo1, yn_ref = blocks[bi][2], blocks[bi][3], blocks[bi][9]
        hln = _layer_norm(jnp.concatenate([cbo0[...], cbo1[...]], axis=1), cf_g, cf_beta)
        y2 = _dot((hln * _sigmoid(hln)).astype(BF16), cf_pw_ref[0])
        yn_ref[:, D_GROUP:2 * D_GROUP] = _rms_norm(y2, mixg[:, D_GROUP:2 * D_GROUP]).astype(BF16)

    def pooling(bi):
        pbuf, s2buf, s4buf, s8buf = blocks[bi][4:8]
        yn_ref = blocks[bi][9]
        if bi > 0:
            carry_halo(blocks[bi - 1], blocks[bi], (4,))
        zc = st[bi].pop(4)
        pbuf[HALO:HALO + SUB_ROWS, :] = zc
        n2 = HALO + SUB_ROWS - 8
        s2buf[8:8 + n2, :] = pbuf[8:8 + n2, :] + pbuf[7:7 + n2, :]
        n4 = HALO + SUB_ROWS - 16
        s4buf[16:16 + n4, :] = s2buf[16:16 + n4, :] + s2buf[14:14 + n4, :]
        n8 = HALO + SUB_ROWS - 24
        s8buf[24:24 + n8, :] = s4buf[24:24 + n8, :] + s4buf[20:20 + n8, :]
        cur = slice(HALO, HALO + SUB_ROWS)
        s16 = s8buf[cur, :] + s8buf[HALO - 8:HALO - 8 + SUB_ROWS, :]
        pos1 = lax.broadcasted_iota(I32, (SUB_ROWS, D_GROUP), 0) + (s_idx * TS + bi * SUB_ROWS + 1)
        wsum = jnp.where(lane_t < HEAD_DIM, s2buf[cur, :],
                         jnp.where(lane_t < 2 * HEAD_DIM, s4buf[cur, :],
                                   jnp.where(lane_t < 3 * HEAD_DIM, s8buf[cur, :], s16)))
        count = jnp.minimum(pos1, win).astype(F32)
        pooled = wsum / count - zc
        y3 = _dot(pooled.astype(BF16), pool_w_ref[0]) * pool_scale
        yn_ref[:, 2 * D_GROUP:3 * D_GROUP] = _rms_norm(y3, mixg[:, 2 * D_GROUP:3 * D_GROUP]).astype(BF16)

    def short_conv(bi):
        cbuf, yn_ref = blocks[bi][8], blocks[bi][9]
        if bi > 0:
            carry_halo(blocks[bi - 1], blocks[bi], (8,))
        cbuf[SC_HALO:SC_HALO + SUB_ROWS, :] = st[bi].pop(6) * st[bi].pop(7)
        conv = jnp.zeros((SUB_ROWS, D_GROUP), F32)
        for k in range(SHORT_K):
            off = SC_HALO - (SHORT_K - 1) + k
            conv = conv + sc_w_ref[0, k:k + 1, :] * cbuf[off:off + SUB_ROWS, :]
        y4 = st[bi].pop(5) * conv
        yn_ref[:, 3 * D_GROUP:4 * D_GROUP] = _rms_norm(y4, mixg[:, 3 * D_GROUP:4 * D_GROUP]).astype(BF16)

    def out_proj(bi):
        m = _dot(blocks[bi][9][...], w_o_ref[0])
        x1 = _layer_norm(ALPHA * st[bi].pop("x") + m, ln_g, ln_b)
        x1_ref[0, bi * SUB_ROWS:(bi + 1) * SUB_ROWS, :] = x1
        st[bi]["x1"] = x1

    def router(bi):
        x1 = st[bi].pop("x1")
        hi = x1.astype(BF16)
        lo = (x1 - hi.astype(F32)).astype(BF16)
        hi_both = _dot(hi, rw_ref[...])
        logits = hi_both[:, 0:LANES] + hi_both[:, LANES:2 * LANES] + _dot(lo, rw_ref[:, 0:LANES])
        st[bi]["sel"] = logits + rb_ref[...]

    def route(bi):
        sel_t = st[bi].pop("sel").T
        cls = _route_class([sel_t[e:e + 1, :] for e in range(N_EXPERTS)])
        r0 = bi * SUB_ROWS
        cls_ref[0, r0 // LANES:(r0 + SUB_ROWS) // LANES, :] = jnp.concatenate(
            [cls[:, k * LANES:(k + 1) * LANES] for k in range(SUB_ROWS // LANES)], axis=0)

    def conformer(b):
        conformer_glu(b)
        conformer_conv(b, 0)
        conformer_conv(b, 1)
        conformer_out(b)

    def round_expert_weights():
        wgu_out[0, :, 0:D_EXPERT] = wg_ref[0, 0].astype(BF16)
        wgu_out[0, :, D_EXPERT:2 * D_EXPERT] = wu_ref[0, 0].astype(BF16)
        wd_out[0] = wd_ref[0, 0].astype(BF16)

    def stage1(b):
        pieces = [lambda: load(b), lambda: in_proj(b, 0, 2), lambda: in_proj(b, 2, 4), lambda: in_proj(b, 4, 5),
                  lambda: in_proj(b, 5, 8)]
        if b == 0:
            pieces.insert(2, round_expert_weights)
        return pieces

    def stage2(b):
        return [lambda: gating_mlp(b), lambda: conformer(b), lambda: pooling(b), lambda: short_conv(b)]

    def stage3(b):
        return [lambda: out_proj(b), lambda: router(b), lambda: route(b)]

    for t in range(n_blocks + 2):
        stages = [stage(t - lag) for lag, stage in ((2, stage3), (0, stage1), (1, stage2)) if 0 <= t - lag < n_blocks]
        for k in range(max(len(stage) for stage in stages)):
            for stage in stages:
                if k < len(stage):
                    stage[k]()

    carry_halo(blocks[-1], blocks[0], (0, 1, 4, 8))
    fill.wait()


def _plan_kernel(cls_ref, dest_ref, meta_ref):
    n = LANES
    cls = cls_ref[...]
    r = lax.broadcasted_iota(I32, (n, n), 0)
    c = lax.broadcasted_iota(I32, (n, n), 1)
    upper = jnp.where(r < c, 1.0, 0.0).astype(BF16)
    lower = jnp.where(c < r, 1.0, 0.0).astype(BF16)
    ones = jnp.ones((n, n), BF16)
    masks = [cls == k for k in range(N_CLASSES)]
    m_all = jnp.concatenate([jnp.where(mk, 1.0, 0.0) for mk in masks], axis=0).astype(BF16)
    within = _dot(m_all, upper)
    rowsum = _dot(m_all, ones)
    rs = jnp.zeros((n, n), F32)
    for k in range(N_CLASSES):
        rs = jnp.where(c == k, rowsum[k * n:(k + 1) * n], rs)
    rs_b = rs.astype(BF16)
    before = _dot(lower, rs_b)
    total = _dot(ones, rs_b)
    ntile = jnp.floor((total + (TMS - 1)) * (1.0 / TMS))
    tstart = _dot(ntile.astype(BF16), upper)
    base = before + tstart * TMS
    dest = jnp.zeros((n, n), F32)
    for k in range(N_CLASSES):
        dest = jnp.where(masks[k], within[k * n:(k + 1) * n] + base[:, k:k + 1], dest)
    dest_ref[...] = dest.astype(I32)

    tend_t = (tstart + ntile).T
    tile = c.astype(F32)
    ended = jnp.where((r < N_CLASSES) & (tend_t <= tile), 1.0, 0.0)
    tcls = jnp.minimum(jnp.sum(ended, axis=0, keepdims=True), N_CLASSES - 1.0)
    grp = (jnp.where(tcls >= N_PAIRS, 1.0, 0.0) + jnp.where(tcls >= 2 * N_PAIRS, 1.0, 0.0)
           + jnp.where(tcls >= 3 * N_PAIRS, 1.0, 0.0))
    pair = tcls - N_PAIRS * grp
    pa = jnp.where(pair >= 3, 1.0, 0.0) + jnp.where(pair >= 5, 1.0, 0.0)
    pb = jnp.where(pair == 0, 1.0, jnp.where((pair == 1) | (pair == 3), 2.0, 3.0))
    n_tiles = tstart[0:1, N_CLASSES:N_CLASSES + 1]
    tile_row = tile[0:1, :]
    active = jnp.where(tile_row < n_tiles, 1.0, 0.0)
    meta = jnp.concatenate(
        [GROUP_SIZE * grp + pa, GROUP_SIZE * grp + pb, active, jnp.zeros((SUBLANES - 3, n), F32)], axis=0)
    meta_ref[...] = meta.astype(I32)


def _row_copy_wait(src, dst, sem):
    pltpu.make_async_copy(src, dst, sem).wait()


def _dispatch_kernel(dest_ref, x_ref, xs_zero_ref, xs_ref, buf, sems):
    del xs_zero_ref
    i = pl.program_id(0)
    n_steps = pl.num_programs(0)

    for s in range(2):
        @pl.when(i > 0)
        def _():
            _row_copy_wait(buf.at[s], xs_ref.at[pl.ds(0, TD * ROW_TILES), :], sems.at[s])

        for j in range(ROW_TILES):
            buf[s, pl.ds(j, TD, stride=ROW_TILES), :] = x_ref[s * TD:(s + 1) * TD, j * LANES:(j + 1) * LANES]
        for t in range(TD):
            d = pl.multiple_of(dest_ref[(i * 2 + s) * TD + t] * ROW_TILES, ROW_TILES)
            pltpu.make_async_copy(buf.at[s, pl.ds(t * ROW_TILES, ROW_TILES), :],
                                  xs_ref.at[pl.ds(d, ROW_TILES), :], sems.at[s]).start(priority=t % 2)

    @pl.when(i == n_steps - 1)
    def _():
        for s in range(2):
            _row_copy_wait(buf.at[s], xs_ref.at[pl.ds(0, TD * ROW_TILES), :], sems.at[s])


def _combine_kernel(dest_ref, ys_ref, vec_m_ref, out_ref, buf, sems):
    i = pl.program_id(0)
    n_steps = pl.num_programs(0)

    def issue(step, s):
        for t in range(TD):
            d = pl.multiple_of(dest_ref[(step * 2 + s) * TD + t] * ROW_TILES, ROW_TILES)
            pltpu.make_async_copy(ys_ref.at[pl.ds(d, ROW_TILES), :],
                                  buf.at[s, pl.ds(t * ROW_TILES, ROW_TILES), :], sems.at[s]).start(priority=t % 2)

    @pl.when(i == 0)
    def _():
        for s in range(2):
            issue(0, s)

    for s in range(2):
        _row_copy_wait(ys_ref.at[pl.ds(0, TD * ROW_TILES), :], buf.at[s], sems.at[s])
        r = jnp.concatenate([buf[s, pl.ds(j, TD, stride=ROW_TILES), :] for j in range(ROW_TILES)], axis=1)
        out_ref[s * TD:(s + 1) * TD, :] = _layer_norm(r, vec_m_ref[0, 3:4, :], vec_m_ref[0, 4:5, :])

        @pl.when(i + 1 < n_steps)
        def _():
            issue(i + 1, s)


def _moe_kernel(meta_ref, xs_ref, *refs):
    i = pl.program_id(0)
    w_refs, rwt_ref, ys_ref = refs[:-2], refs[-2], refs[-1]
    rows = TMS * ROW_TILES

    @pl.when(meta_ref[2, i * TILES_PER_STEP] == 0)
    def _():
        ys_ref[...] = jnp.zeros(ys_ref.shape, F32)

    @pl.when(meta_ref[2, i * TILES_PER_STEP] == 1)
    def _():
        st = [dict() for _ in range(TILES_PER_STEP)]

        def load(s):
            x = jnp.concatenate(
                [xs_ref[pl.ds(s * rows + j, TMS, stride=ROW_TILES), :] for j in range(ROW_TILES)], axis=1)
            st[s]["x"] = x
            st[s]["xb"] = x.astype(BF16)

        def up(s, e):
            gu = _dot(st[s]["xb"], w_refs[4 * s + 2 * e][0])
            st[s]["g", e] = gu[:, 0:D_EXPERT]
            st[s]["u", e] = gu[:, D_EXPERT:2 * D_EXPERT]

        def act(s, e):
            g = st[s].pop(("g", e))
            st[s]["h", e] = ((g * _sigmoid(g)) * st[s].pop(("u", e))).astype(BF16)

        def down(s, e):
            st[s]["y", e] = _dot(st[s].pop(("h", e)), w_refs[4 * s + 2 * e + 1][0])

        def mix(s):
            tile = i * TILES_PER_STEP + s
            x = st[s].pop("x")
            la = jnp.sum(x * rwt_ref[pl.ds(meta_ref[0, tile], 1), :], axis=-1, keepdims=True)
            lb = jnp.sum(x * rwt_ref[pl.ds(meta_ref[1, tile], 1), :], axis=-1, keepdims=True)
            m = jnp.maximum(la, lb)
            pa = jnp.exp(la - m)
            pb = jnp.exp(lb - m)
            den = pa + pb
            r = ALPHA * x + ((pa / den) * st[s].pop(("y", 0)) + (pb / den) * st[s].pop(("y", 1)))
            for j in range(ROW_TILES):
                ys_ref[pl.ds(s * rows + j, TMS, stride=ROW_TILES), :] = r[:, j * LANES:(j + 1) * LANES]

        units = [(s, e) for s in range(TILES_PER_STEP) for e in range(2)]
        load(0)
        up(*units[0])
        for k, (s, e) in enumerate(units):
            if k + 1 < len(units):
                nxt = units[k + 1]
                if nxt[1] == 0:
                    load(nxt[0])
                up(*nxt)
            act(s, e)
            down(s, e)
            if e == 1:
                mix(s)


def _full(shape):
    return pl.BlockSpec(shape, lambda *_: (0,) * len(shape))


def _layer_block(shape, layer):
    return pl.BlockSpec((1,) + shape, lambda *_: (layer,) + (0,) * len(shape))


def _mixer_call(x, p, expert_w, n_sorted, layer):
    batch, seq, _ = x.shape
    steps = seq // TS
    fill_rows = n_sorted * ROW_TILES // (batch * steps)
    assert fill_rows * batch * steps == n_sorted * ROW_TILES and fill_rows % SUBLANES == 0
    slabs = batch * steps // N_EXPERTS
    assert batch * steps == slabs * N_EXPERTS

    def w_in(b, s):
        i = b * steps + s
        return (layer, i // slabs, i % slabs, 0)

    def w_out(b, s):
        i = b * steps + s
        return (i // slabs, i % slabs, 0)

    up_rows, down_rows = D_MODEL // slabs, D_EXPERT // slabs
    in_specs = [
        pl.BlockSpec((1, TS, D_MODEL), lambda b, s: (b, s, 0)),
        _layer_block((D_MODEL, 8 * D_GROUP), layer),
        _layer_block((SUBLANES, D_GROUP), layer), _layer_block((SUBLANES, D_MODEL), layer),
        _layer_block((N_HEADS * CHUNK, CHUNK), layer), _layer_block((CHUNK, D_GROUP), layer),
        _layer_block((HALO, D_GROUP), layer),
        _layer_block((D_GROUP, D_GROUP), layer),
        _layer_block((D_GROUP, D_GROUP), layer),
        _layer_block((SUBLANES, D_GROUP), layer),
        _layer_block((D_MODEL, D_MODEL), layer),
        _full((D_MODEL, 2 * LANES)), _full((1, LANES)),
        pl.BlockSpec((1, 1, up_rows, D_EXPERT), w_in),
        pl.BlockSpec((1, 1, up_rows, D_EXPERT), w_in),
        pl.BlockSpec((1, 1, down_rows, D_MODEL), w_in),
    ]
    out_specs = [
        pl.BlockSpec((1, TS, D_MODEL), lambda b, s: (b, s, 0)),
        pl.BlockSpec((1, TS // LANES, LANES), lambda b, s: (b * steps + s, 0, 0)),
        pl.BlockSpec((1, up_rows, 2 * D_EXPERT), w_out),
        pl.BlockSpec((1, down_rows, D_MODEL), w_out),
        pl.BlockSpec(memory_space=pl.ANY),
    ]
    block_scratch = [
        pltpu.VMEM((HALO + SUB_ROWS, LANES), F32),
        pltpu.VMEM((HALO + SUB_ROWS, LANES), F32),
        pltpu.VMEM((SUB_ROWS, LANES), F32),
        pltpu.VMEM((SUB_ROWS, LANES), F32),
        pltpu.VMEM((HALO + SUB_ROWS, D_GROUP), F32),
        pltpu.VMEM((HALO + SUB_ROWS, D_GROUP), F32),
        pltpu.VMEM((HALO + SUB_ROWS, D_GROUP), F32),
        pltpu.VMEM((HALO + SUB_ROWS, D_GROUP), F32),
        pltpu.VMEM((SC_HALO + SUB_ROWS, D_GROUP), F32),
        pltpu.VMEM((SUB_ROWS, D_MODEL), BF16),
    ]
    scratch = block_scratch * (TS // SUB_ROWS) + [pltpu.VMEM((fill_rows, LANES), F32), pltpu.SemaphoreType.DMA(())]
    return pl.pallas_call(
        _mixer_kernel,
        grid=(batch, steps),
        in_specs=in_specs,
        out_specs=out_specs,
        out_shape=[jax.ShapeDtypeStruct((batch, seq, D_MODEL), F32),
                   jax.ShapeDtypeStruct((batch * steps, TS // LANES, LANES), I32),
                   jax.ShapeDtypeStruct((N_EXPERTS, D_MODEL, 2 * D_EXPERT), BF16),
                   jax.ShapeDtypeStruct((N_EXPERTS, D_EXPERT, D_MODEL), BF16),
                   jax.ShapeDtypeStruct((n_sorted * ROW_TILES, LANES), F32)],
        scratch_shapes=scratch,
        compiler_params=pltpu.CompilerParams(
            dimension_semantics=("arbitrary", "arbitrary"), vmem_limit_bytes=VMEM_LIMIT),
        name="mixer",
    )(x, *p, *expert_w)


def _plan_call(cls2d):
    return pl.pallas_call(
        _plan_kernel,
        out_shape=[jax.ShapeDtypeStruct((LANES, LANES), I32), jax.ShapeDtypeStruct((SUBLANES, LANES), I32)],
        compiler_params=pltpu.CompilerParams(vmem_limit_bytes=VMEM_LIMIT),
        name="moe_plan",
    )(cls2d)


def _dispatch_call(dest, x2d, xs_zero):
    n_tok = x2d.shape[0]
    return pl.pallas_call(
        _dispatch_kernel,
        grid_spec=pltpu.PrefetchScalarGridSpec(
            num_scalar_prefetch=1,
            grid=(n_tok // (2 * TD),),
            in_specs=[pl.BlockSpec((2 * TD, D_MODEL), lambda i, *_: (i, 0)),
                      pl.BlockSpec(memory_space=pl.ANY)],
            out_specs=pl.BlockSpec(memory_space=pl.ANY),
            scratch_shapes=[
                pltpu.VMEM((2, TD * ROW_TILES, LANES), F32),
                pltpu.SemaphoreType.DMA((2,)),
            ]),
        out_shape=jax.ShapeDtypeStruct(xs_zero.shape, F32),
        input_output_aliases={2: 0},
        compiler_params=pltpu.CompilerParams(
            dimension_semantics=("arbitrary",), vmem_limit_bytes=VMEM_LIMIT),
        name="moe_dispatch",
    )(dest, x2d, xs_zero)


def _combine_call(dest, ys, vec_m, layer):
    n_tok = dest.shape[0]
    return pl.pallas_call(
        _combine_kernel,
        grid_spec=pltpu.PrefetchScalarGridSpec(
            num_scalar_prefetch=1,
            grid=(n_tok // (2 * TD),),
            in_specs=[pl.BlockSpec(memory_space=pl.ANY), _layer_block((SUBLANES, D_MODEL), layer)],
            out_specs=pl.BlockSpec((2 * TD, D_MODEL), lambda i, *_: (i, 0)),
            scratch_shapes=[
                pltpu.VMEM((2, TD * ROW_TILES, LANES), F32),
                pltpu.SemaphoreType.DMA((2,)),
            ]),
        out_shape=jax.ShapeDtypeStruct((n_tok, D_MODEL), F32),
        compiler_params=pltpu.CompilerParams(
            dimension_semantics=("arbitrary",), vmem_limit_bytes=VMEM_LIMIT),
        name="moe_combine",
    )(dest, ys, vec_m)


def _moe_call(meta, xs, wgu, wd, rwt, n_tiles):
    rows = TILES_PER_STEP * TMS * ROW_TILES
    w_specs, w_args = [], []
    for s in range(TILES_PER_STEP):
        for sel in (0, 1):
            def expert(i, meta, s=s, sel=sel):
                return (meta[sel, i * TILES_PER_STEP + s], 0, 0)
            w_specs += [pl.BlockSpec((1, D_MODEL, 2 * D_EXPERT), expert),
                        pl.BlockSpec((1, D_EXPERT, D_MODEL), expert)]
            w_args += [wgu, wd]
    tile = pl.BlockSpec((rows, LANES), lambda i, meta: (i, 0))
    return pl.pallas_call(
        _moe_kernel,
        grid_spec=pltpu.PrefetchScalarGridSpec(
            num_scalar_prefetch=1,
            grid=(n_tiles // TILES_PER_STEP,),
            in_specs=[tile] + w_specs + [_full((N_EXPERTS, D_MODEL))],
            out_specs=tile),
        out_shape=jax.ShapeDtypeStruct(xs.shape, F32),
        compiler_params=pltpu.CompilerParams(
            dimension_semantics=("arbitrary",), vmem_limit_bytes=VMEM_LIMIT),
        name="moe_experts",
    )(meta, xs, *w_args, rwt)


def _block_diag(w):
    g, d, _ = w.shape
    eye = jnp.eye(g, dtype=w.dtype)
    return (eye[:, None, :, None] * w[:, :, None, :]).reshape(g * d, g * d)


def kernel(x, w_in, gm_ln_g, gm_ln_b, gm_w_s, gm_b_s, cf_dw_w, cf_dw_b, cf_ln_g, cf_ln_b, cf_pw,
           pool_w, pool_scale, sc_w, mix_norm_g, w_o, ln1_g, ln1_b, router_w, router_b,
           exp_w_gate, exp_w_up, exp_w_down, ln2_g, ln2_b):
    batch, seq, _ = x.shape
    n_tok = batch * seq
    assert n_tok == LANES * LANES, "the routing plan lays tokens out as one (128, 128) tile grid"
    assert seq % TS == 0 and TS % SUB_ROWS == 0 and SUB_ROWS % (2 * CONV_ROWS) == 0 and n_tok % (2 * TD) == 0
    n_tiles = n_tok // TMS + N_CLASSES
    assert n_tiles <= LANES and n_tiles % TILES_PER_STEP == 0
    def pack_rows(rows):
        slab = jnp.stack(rows, axis=1)
        return jnp.pad(slab, ((0, 0), (0, SUBLANES - len(rows)), (0, 0)))

    rw = jnp.pad(router_w, ((0, 0), (0, LANES - N_EXPERTS)))
    rw_hi = rw.astype(BF16)
    rw_lo = (rw - rw_hi.astype(F32)).astype(BF16)
    rb = jnp.pad(router_b, (0, LANES - N_EXPERTS)).reshape(1, LANES)
    rwt = router_w.T
    vec_m = pack_rows([mix_norm_g, ln1_g, ln1_b, ln2_g, ln2_b])
    params = (
        w_in.astype(BF16),
        pack_rows([gm_ln_g, gm_ln_b, cf_dw_b, cf_ln_g, cf_ln_b, pool_scale]), vec_m,
        gm_w_s.reshape(DEPTH, N_HEADS * CHUNK, CHUNK),
        jnp.repeat(gm_b_s.transpose(0, 2, 1), HEAD_DIM, axis=2),
        jnp.pad(cf_dw_w, ((0, 0), (0, HALO - CONF_K), (0, 0))),
        cf_pw.astype(BF16),
        jax.vmap(_block_diag)(pool_w).astype(BF16),
        jnp.pad(sc_w, ((0, 0), (0, SUBLANES - SHORT_K), (0, 0))),
        w_o.astype(BF16),
        jnp.concatenate([rw_hi, rw_lo], axis=1), rb,
    )
    for l in range(DEPTH):
        x1, cls, wgu_b, wd_b, xs_zero = _mixer_call(
            x, params, (exp_w_gate, exp_w_up, exp_w_down), n_tiles * TMS, l)
        x1 = x1.reshape(n_tok, D_MODEL)
        dest2d, meta = _plan_call(cls.reshape(LANES, LANES))
        dest = dest2d.reshape(n_tok)
        xs = _dispatch_call(dest, x1, xs_zero)
        ys = _moe_call(meta, xs, wgu_b, wd_b, rwt, n_tiles)
        x = _combine_call(dest, ys, vec_m, l).reshape(batch, seq, D_MODEL)
    return x
```

```python
import jax
import jax.numpy as jnp
from jax import lax
from jax.experimental import pallas as pl
from jax.experimental.pallas import tpu as pltpu

D_MODEL = 1024
DEPTH = 2
D_GROUP = 256
N_HEADS = 4
HEAD_DIM = 64
CHUNK = 128
CONF_K = 31
SHORT_K = 3
N_EXPERTS = 16
N_GROUPS = 4
GROUP_SIZE = 4
N_PAIRS = 6
N_CLASSES = N_GROUPS * N_PAIRS
D_EXPERT = 512
ALPHA = (2 * DEPTH) ** 0.25
LN_EPS = 1e-5
RMS_EPS = 1e-6

LANES = 128
SUBLANES = 8
ROW_TILES = D_MODEL // LANES
HALO = 32
SC_HALO = SUBLANES
TS = 512
SUB_ROWS = 256
CONV_ROWS = 64
TMS = 256
TILES_PER_STEP = 3
TD = 512
V7X_VMEM_BYTES = 64 * 1024 * 1024
VMEM_LIMIT = V7X_VMEM_BYTES * 7 // 8

BF16 = jnp.bfloat16
F32 = jnp.float32
I32 = jnp.int32


def _dot(a, b):
    return jnp.dot(a, b, preferred_element_type=F32)


def _layer_norm(x, g, b):
    mu = jnp.mean(x, axis=-1, keepdims=True)
    xc = x - mu
    var = jnp.mean(xc * xc, axis=-1, keepdims=True)
    return xc * lax.rsqrt(var + LN_EPS) * g + b


def _rms_norm(y, g):
    ms = jnp.mean(y * y, axis=-1, keepdims=True)
    return y * lax.rsqrt(ms + RMS_EPS) * g


def _sigmoid(x):
    return 1.0 / (1.0 + jnp.exp(-x))


def _route_class(sel):
    scores = []
    for g in range(N_GROUPS):
        v = sel[g * GROUP_SIZE:(g + 1) * GROUP_SIZE]
        best_pair = None
        for i in range(GROUP_SIZE):
            for j in range(i + 1, GROUP_SIZE):
                p = v[i] + v[j]
                best_pair = p if best_pair is None else jnp.maximum(best_pair, p)
        scores.append(best_pair)
    best = jnp.zeros(scores[0].shape, I32)
    best_score = scores[0]
    for g in range(1, N_GROUPS):
        better = scores[g] > best_score
        best = jnp.where(better, g, best)
        best_score = jnp.where(better, scores[g], best_score)
    v = []
    for j in range(GROUP_SIZE):
        out = sel[j]
        for g in range(1, N_GROUPS):
            out = jnp.where(best == g, sel[g * GROUP_SIZE + j], out)
        v.append(out)
    i0 = jnp.zeros_like(best)
    v0 = v[0]
    for j in range(1, GROUP_SIZE):
        better = v[j] > v0
        i0 = jnp.where(better, j, i0)
        v0 = jnp.where(better, v[j], v0)
    neg = jnp.full_like(v0, -jnp.inf)
    w = [jnp.where(i0 == j, neg, v[j]) for j in range(GROUP_SIZE)]
    i1 = jnp.zeros_like(best)
    v1 = w[0]
    for j in range(1, GROUP_SIZE):
        better = w[j] > v1
        i1 = jnp.where(better, j, i1)
        v1 = jnp.where(better, w[j], v1)
    a = jnp.minimum(i0, i1)
    b = jnp.maximum(i0, i1)
    pair = jnp.where(a == 0, b - 1, jnp.where(a == 1, b + 1, N_PAIRS - 1))
    return best * N_PAIRS + pair


def _mixer_kernel(x_ref, w_in_ref, vec_g_ref, vec_m_ref, gm_w_ref, gm_bs_ref, cf_w_ref, cf_pw_ref,
                  pool_w_ref, sc_w_ref, w_o_ref, rw_ref, rb_ref, wg_ref, wu_ref, wd_ref,
                  x1_ref, cls_ref, wgu_out, wd_out, xs_ref, *scratch):
    s_idx = pl.program_id(1)
    *scratch, zbuf, zsem = scratch
    step = pl.program_id(0) * pl.num_programs(1) + s_idx
    fill_rows = zbuf.shape[0]

    @pl.when(step == 0)
    def _():
        zbuf[...] = jnp.zeros(zbuf.shape, F32)

    fill = pltpu.make_async_copy(
        zbuf, xs_ref.at[pl.ds(pl.multiple_of(step * fill_rows, fill_rows), fill_rows), :], zsem)
    fill.start()
    gm_g, gm_b, cf_b, cf_g, cf_beta, pool_scale = (vec_g_ref[0, k:k + 1, :] for k in range(6))
    mixg, ln_g, ln_b = (vec_m_ref[0, k:k + 1, :] for k in range(3))

    row = lax.broadcasted_iota(I32, (N_HEADS * CHUNK, CHUNK), 0)
    col = lax.broadcasted_iota(I32, (N_HEADS * CHUNK, CHUNK), 1)
    w_tril = jnp.where(col <= (row & (CHUNK - 1)), gm_w_ref[0], 0.0).astype(BF16)
    lane = lax.broadcasted_iota(I32, (CHUNK, D_GROUP), 1)
    lane_t = lax.broadcasted_iota(I32, (SUB_ROWS, D_GROUP), 1)
    win = jnp.where(lane_t < HEAD_DIM, 2,
                    jnp.where(lane_t < 2 * HEAD_DIM, 4, jnp.where(lane_t < 3 * HEAD_DIM, 8, 16)))
    bs = gm_bs_ref[0]

    n_blocks = TS // SUB_ROWS
    per_block = len(scratch) // n_blocks
    blocks = [scratch[i * per_block:(i + 1) * per_block] for i in range(n_blocks)]

    @pl.when(s_idx == 0)
    def _():
        hbuf0, hbuf1, _, _, pbuf, _, _, _, cbuf, _ = blocks[0]
        hbuf0[0:HALO, :] = jnp.zeros((HALO, LANES), F32)
        hbuf1[0:HALO, :] = jnp.zeros((HALO, LANES), F32)
        pbuf[0:HALO, :] = jnp.zeros((HALO, D_GROUP), F32)
        cbuf[0:SC_HALO, :] = jnp.zeros((SC_HALO, D_GROUP), F32)

    def carry_halo(src, dst, which):
        for i in which:
            rows = SC_HALO if i == 8 else HALO
            dst[i][0:rows, :] = src[i][SUB_ROWS:SUB_ROWS + rows, :]

    st = [dict() for _ in range(n_blocks)]

    def load(bi):
        x = x_ref[0, bi * SUB_ROWS:(bi + 1) * SUB_ROWS, :]
        st[bi]["x"] = x
        st[bi]["xb"] = x.astype(BF16)

    def in_proj(bi, lo, hi):
        z = _dot(st[bi]["xb"], w_in_ref[0, :, lo * D_GROUP:hi * D_GROUP])
        for j in range(lo, hi):
            st[bi][j] = z[:, (j - lo) * D_GROUP:(j - lo + 1) * D_GROUP]

    def gating_mlp(bi):
        yn_ref = blocks[bi][9]
        u = st[bi].pop(0)
        v = _layer_norm(st[bi].pop(1), gm_g, gm_b)
        for n in range(SUB_ROWS // CHUNK):
            crows = slice(n * CHUNK, (n + 1) * CHUNK)
            s_all = _dot(w_tril, v[crows, :].astype(BF16))
            s_sel = s_all[3 * CHUNK:4 * CHUNK]
            for h in (2, 1, 0):
                s_sel = jnp.where(lane < (h + 1) * HEAD_DIM, s_all[h * CHUNK:(h + 1) * CHUNK], s_sel)
            y1 = u[crows, :] * (s_sel + bs)
            yn_ref[crows, 0:D_GROUP] = _rms_norm(y1, mixg[:, 0:D_GROUP]).astype(BF16)

    def conformer_glu(bi):
        if bi > 0:
            carry_halo(blocks[bi - 1], blocks[bi], (0, 1))
        glu = st[bi].pop(2) * _sigmoid(st[bi].pop(3))
        for half in range(2):
            blocks[bi][half][HALO:HALO + SUB_ROWS, :] = glu[:, half * LANES:(half + 1) * LANES]

    def conformer_conv(bi, half):
        hb, co = blocks[bi][half], blocks[bi][2 + half]
        lanes = slice(half * LANES, (half + 1) * LANES)
        bias = cf_b[:, lanes]
        for q in range(SUB_ROWS // (2 * CONV_ROWS)):
            for parity in range(2):
                out0 = q * 2 * CONV_ROWS + parity
                base = out0 + HALO - (CONF_K - 1)
                acc = jnp.zeros((CONV_ROWS, LANES), F32) + bias
                for k in range(CONF_K):
                    acc = acc + cf_w_ref[0, k:k + 1, lanes] * hb[pl.ds(base + k, CONV_ROWS, stride=2), :]
                co[pl.ds(out0, CONV_ROWS, stride=2), :] = acc

    def conformer_out(bi):
        cbo0, cbo1, yn_ref = blocks[bi][2], blocks[bi][3], blocks[bi][9]
        hln = _layer_norm(jnp.concatenate([cbo0[...], cbo1[...]], axis=1), cf_g, cf_beta)
        y2 = _dot((hln * _sigmoid(hln)).astype(BF16), cf_pw_ref[0])
        yn_ref[:, D_GROUP:2 * D_GROUP] = _rms_norm(y2, mixg[:, D_GROUP:2 * D_GROUP]).astype(BF16)

    def pooling(bi):
        pbuf, s2buf, s4buf, s8buf = blocks[bi][4:8]
        yn_ref = blocks[bi][9]
        if bi > 0:
            carry_halo(blocks[bi - 1], blocks[bi], (4,))
        zc = st[bi].pop(4)
        pbuf[HALO:HALO + SUB_ROWS, :] = zc
        n2 = HALO + SUB_ROWS - 8
        s2buf[8:8 + n2, :] = pbuf[8:8 + n2, :] + pbuf[7:7 + n2, :]
        n4 = HALO + SUB_ROWS - 16
        s4buf[16:16 + n4, :] = s2buf[16:16 + n4, :] + s2buf[14:14 + n4, :]
        n8 = HALO + SUB_ROWS - 24
        s8buf[24:24 + n8, :] = s4buf[24:24 + n8, :] + s4buf[20:20 + n8, :]
        cur = slice(HALO, HALO + SUB_ROWS)
        s16 = s8buf[cur, :] + s8buf[HALO - 8:HALO - 8 + SUB_ROWS, :]
        pos1 = lax.broadcasted_iota(I32, (SUB_ROWS, D_GROUP), 0) + (s_idx * TS + bi * SUB_ROWS + 1)
        wsum = jnp.where(lane_t < HEAD_DIM, s2buf[cur, :],
                         jnp.where(lane_t < 2 * HEAD_DIM, s4buf[cur, :],
                                   jnp.where(lane_t < 3 * HEAD_DIM, s8buf[cur, :], s16)))
        count = jnp.minimum(pos1, win).astype(F32)
        pooled = wsum / count - zc
        y3 = _dot(pooled.astype(BF16), pool_w_ref[0]) * pool_scale
        yn_ref[:, 2 * D_GROUP:3 * D_GROUP] = _rms_norm(y3, mixg[:, 2 * D_GROUP:3 * D_GROUP]).astype(BF16)

    def short_conv(bi):
        cbuf, yn_ref = blocks[bi][8], blocks[bi][9]
        if bi > 0:
            carry_halo(blocks[bi - 1], blocks[bi], (8,))
        cbuf[SC_HALO:SC_HALO + SUB_ROWS, :] = st[bi].pop(6) * st[bi].pop(7)
        conv = jnp.zeros((SUB_ROWS, D_GROUP), F32)
        for k in range(SHORT_K):
            off = SC_HALO - (SHORT_K - 1) + k
            conv = conv + sc_w_ref[0, k:k + 1, :] * cbuf[off:off + SUB_ROWS, :]
        y4 = st[bi].pop(5) * conv
        yn_ref[:, 3 * D_GROUP:4 * D_GROUP] = _rms_norm(y4, mixg[:, 3 * D_GROUP:4 * D_GROUP]).astype(BF16)

    def out_proj(bi):
        m = _dot(blocks[bi][9][...], w_o_ref[0])
        x1 = _layer_norm(ALPHA * st[bi].pop("x") + m, ln_g, ln_b)
        x1_ref[0, bi * SUB_ROWS:(bi + 1) * SUB_ROWS, :] = x1
        st[bi]["x1"] = x1

    def router(bi):
        x1 = st[bi].pop("x1")
        hi = x1.astype(BF16)
        lo = (x1 - hi.astype(F32)).astype(BF16)
        hi_both = _dot(hi, rw_ref[...])
        logits = hi_both[:, 0:LANES] + hi_both[:, LANES:2 * LANES] + _dot(lo, rw_ref[:, 0:LANES])
        st[bi]["sel"] = logits + rb_ref[...]

    def route(bi):
        sel_t = st[bi].pop("sel").T
        cls = _route_class([sel_t[e:e + 1, :] for e in range(N_EXPERTS)])
        r0 = bi * SUB_ROWS
        cls_ref[0, r0 // LANES:(r0 + SUB_ROWS) // LANES, :] = jnp.concatenate(
            [cls[:, k * LANES:(k + 1) * LANES] for k in range(SUB_ROWS // LANES)], axis=0)

    def conformer(b):
        conformer_glu(b)
        conformer_conv(b, 0)
        conformer_conv(b, 1)
        conformer_out(b)

    def round_expert_weights():
        wgu_out[0, :, 0:D_EXPERT] = wg_ref[0, 0].astype(BF16)
        wgu_out[0, :, D_EXPERT:2 * D_EXPERT] = wu_ref[0, 0].astype(BF16)
        wd_out[0] = wd_ref[0, 0].astype(BF16)

    def stage1(b):
        pieces = [lambda: load(b), lambda: in_proj(b, 0, 2), lambda: in_proj(b, 2, 4), lambda: in_proj(b, 4, 5),
                  lambda: in_proj(b, 5, 8)]
        if b == 0:
            pieces.insert(2, round_expert_weights)
        return pieces

    def stage2(b):
        return [lambda: gating_mlp(b), lambda: conformer(b), lambda: pooling(b), lambda: short_conv(b)]

    def stage3(b):
        return [lambda: out_proj(b), lambda: router(b), lambda: route(b)]

    for t in range(n_blocks + 2):
        stages = [stage(t - lag) for lag, stage in ((2, stage3), (0, stage1), (1, stage2)) if 0 <= t - lag < n_blocks]
        for k in range(max(len(stage) for stage in stages)):
            for stage in stages:
                if k < len(stage):
                    stage[k]()

    carry_halo(blocks[-1], blocks[0], (0, 1, 4, 8))
    fill.wait()


def _plan_kernel(cls_ref, dest_ref, meta_ref):
    n = LANES
    cls = cls_ref[...]
    r = lax.broadcasted_iota(I32, (n, n), 0)
    c = lax.broadcasted_iota(I32, (n, n), 1)
    upper = jnp.where(r < c, 1.0, 0.0).astype(BF16)
    lower = jnp.where(c < r, 1.0, 0.0).astype(BF16)
    ones = jnp.ones((n, n), BF16)
    masks = [cls == k for k in range(N_CLASSES)]
    m_all = jnp.concatenate([jnp.where(mk, 1.0, 0.0) for mk in masks], axis=0).astype(BF16)
    within = _dot(m_all, upper)
    rowsum = _dot(m_all, ones)
    rs = jnp.zeros((n, n), F32)
    for k in range(N_CLASSES):
        rs = jnp.where(c == k, rowsum[k * n:(k + 1) * n], rs)
    rs_b = rs.astype(BF16)
    before = _dot(lower, rs_b)
    total = _dot(ones, rs_b)
    ntile = jnp.floor((total + (TMS - 1)) * (1.0 / TMS))
    tstart = _dot(ntile.astype(BF16), upper)
    base = before + tstart * TMS
    dest = jnp.zeros((n, n), F32)
    for k in range(N_CLASSES):
        dest = jnp.where(masks[k], within[k * n:(k + 1) * n] + base[:, k:k + 1], dest)
    dest_ref[...] = dest.astype(I32)

    tend_t = (tstart + ntile).T
    tile = c.astype(F32)
    ended = jnp.where((r < N_CLASSES) & (tend_t <= tile), 1.0, 0.0)
    tcls = jnp.minimum(jnp.sum(ended, axis=0, keepdims=True), N_CLASSES - 1.0)
    grp = (jnp.where(tcls >= N_PAIRS, 1.0, 0.0) + jnp.where(tcls >= 2 * N_PAIRS, 1.0, 0.0)
           + jnp.where(tcls >= 3 * N_PAIRS, 1.0, 0.0))
    pair = tcls - N_PAIRS * grp
    pa = jnp.where(pair >= 3, 1.0, 0.0) + jnp.where(pair >= 5, 1.0, 0.0)
    pb = jnp.where(pair == 0, 1.0, jnp.where((pair == 1) | (pair == 3), 2.0, 3.0))
    n_tiles = tstart[0:1, N_CLASSES:N_CLASSES + 1]
    tile_row = tile[0:1, :]
    active = jnp.where(tile_row < n_tiles, 1.0, 0.0)
    meta = jnp.concatenate(
        [GROUP_SIZE * grp + pa, GROUP_SIZE * grp + pb, active, jnp.zeros((SUBLANES - 3, n), F32)], axis=0)
    meta_ref[...] = meta.astype(I32)


def _row_copy_wait(src, dst, sem):
    pltpu.make_async_copy(src, dst, sem).wait()


def _dispatch_kernel(dest_ref, x_ref, xs_zero_ref, xs_ref, buf, sems):
    del xs_zero_ref
    i = pl.program_id(0)
    n_steps = pl.num_programs(0)

    for s in range(2):
        @pl.when(i > 0)
        def _():
            _row_copy_wait(buf.at[s], xs_ref.at[pl.ds(0, TD * ROW_TILES), :], sems.at[s])

        for j in range(ROW_TILES):
            buf[s, pl.ds(j, TD, stride=ROW_TILES), :] = x_ref[s * TD:(s + 1) * TD, j * LANES:(j + 1) * LANES]
        for t in range(TD):
            d = pl.multiple_of(dest_ref[(i * 2 + s) * TD + t] * ROW_TILES, ROW_TILES)
            pltpu.make_async_copy(buf.at[s, pl.ds(t * ROW_TILES, ROW_TILES), :],
                                  xs_ref.at[pl.ds(d, ROW_TILES), :], sems.at[s]).start(priority=t % 2)

    @pl.when(i == n_steps - 1)
    def _():
        for s in range(2):
            _row_copy_wait(buf.at[s], xs_ref.at[pl.ds(0, TD * ROW_TILES), :], sems.at[s])


def _combine_kernel(dest_ref, ys_ref, vec_m_ref, out_ref, buf, sems):
    i = pl.program_id(0)
    n_steps = pl.num_programs(0)

    def issue(step, s):
        for t in range(TD):
            d = pl.multiple_of(dest_ref[(step * 2 + s) * TD + t] * ROW_TILES, ROW_TILES)
            pltpu.make_async_copy(ys_ref.at[pl.ds(d, ROW_TILES), :],
                                  buf.at[s, pl.ds(t * ROW_TILES, ROW_TILES), :], sems.at[s]).start(priority=t % 2)

    @pl.when(i == 0)
    def _():
        for s in range(2):
            issue(0, s)

    for s in range(2):
        _row_copy_wait(ys_ref.at[pl.ds(0, TD * ROW_TILES), :], buf.at[s], sems.at[s])
        r = jnp.concatenate([buf[s, pl.ds(j, TD, stride=ROW_TILES), :] for j in range(ROW_TILES)], axis=1)
        out_ref[s * TD:(s + 1) * TD, :] = _layer_norm(r, vec_m_ref[0, 3:4, :], vec_m_ref[0, 4:5, :])

        @pl.when(i + 1 < n_steps)
        def _():
            issue(i + 1, s)


def _moe_kernel(meta_ref, xs_ref, *refs):
    i = pl.program_id(0)
    w_refs, rwt_ref, ys_ref = refs[:-2], refs[-2], refs[-1]
    rows = TMS * ROW_TILES

    @pl.when(meta_ref[2, i * TILES_PER_STEP] == 0)
    def _():
        ys_ref[...] = jnp.zeros(ys_ref.shape, F32)

    @pl.when(meta_ref[2, i * TILES_PER_STEP] == 1)
    def _():
        st = [dict() for _ in range(TILES_PER_STEP)]

        def load(s):
            x = jnp.concatenate(
                [xs_ref[pl.ds(s * rows + j, TMS, stride=ROW_TILES), :] for j in range(ROW_TILES)], axis=1)
            st[s]["x"] = x
            st[s]["xb"] = x.astype(BF16)

        def up(s, e):
            gu = _dot(st[s]["xb"], w_refs[4 * s + 2 * e][0])
            st[s]["g", e] = gu[:, 0:D_EXPERT]
            st[s]["u", e] = gu[:, D_EXPERT:2 * D_EXPERT]

        def act(s, e):
            g = st[s].pop(("g", e))
            st[s]["h", e] = ((g * _sigmoid(g)) * st[s].pop(("u", e))).astype(BF16)

        def down(s, e):
            st[s]["y", e] = _dot(st[s].pop(("h", e)), w_refs[4 * s + 2 * e + 1][0])

        def mix(s):
            tile = i * TILES_PER_STEP + s
            x = st[s].pop("x")
            la = jnp.sum(x * rwt_ref[pl.ds(meta_ref[0, tile], 1), :], axis=-1, keepdims=True)
            lb = jnp.sum(x * rwt_ref[pl.ds(meta_ref[1, tile], 1), :], axis=-1, keepdims=True)
            m = jnp.maximum(la, lb)
            pa = jnp.exp(la - m)
            pb = jnp.exp(lb - m)
            den = pa + pb
            r = ALPHA * x + ((pa / den) * st[s].pop(("y", 0)) + (pb / den) * st[s].pop(("y", 1)))
            for j in range(ROW_TILES):
                ys_ref[pl.ds(s * rows + j, TMS, stride=ROW_TILES), :] = r[:, j * LANES:(j + 1) * LANES]

        units = [(s, e) for s in range(TILES_PER_STEP) for e in range(2)]
        load(0)
        up(*units[0])
        for k, (s, e) in enumerate(units):
            if k + 1 < len(units):
                nxt = units[k + 1]
                if nxt[1] == 0:
                    load(nxt[0])
                up(*nxt)
            act(s, e)
            down(s, e)
            if e == 1:
                mix(s)


def _full(shape):
    return pl.BlockSpec(shape, lambda *_: (0,) * len(shape))


def _layer_block(shape, layer):
    return pl.BlockSpec((1,) + shape, lambda *_: (layer,) + (0,) * len(shape))


def _mixer_call(x, p, expert_w, n_sorted, layer):
    batch, seq, _ = x.shape
    steps = seq // TS
    fill_rows = n_sorted * ROW_TILES // (batch * steps)
    assert fill_rows * batch * steps == n_sorted * ROW_TILES and fill_rows % SUBLANES == 0
    slabs = batch * steps // N_EXPERTS
    assert batch * steps == slabs * N_EXPERTS

    def w_in(b, s):
        i = b * steps + s
        return (layer, i // slabs, i % slabs, 0)

    def w_out(b, s):
        i = b * steps + s
        return (i // slabs, i % slabs, 0)

    up_rows, down_rows = D_MODEL // slabs, D_EXPERT // slabs
    in_specs = [
        pl.BlockSpec((1, TS, D_MODEL), lambda b, s: (b, s, 0)),
        _layer_block((D_MODEL, 8 * D_GROUP), layer),
        _layer_block((SUBLANES, D_GROUP), layer), _layer_block((SUBLANES, D_MODEL), layer),
        _layer_block((N_HEADS * CHUNK, CHUNK), layer), _layer_block((CHUNK, D_GROUP), layer),
        _layer_block((HALO, D_GROUP), layer),
        _layer_block((D_GROUP, D_GROUP), layer),
        _layer_block((D_GROUP, D_GROUP), layer),
        _layer_block((SUBLANES, D_GROUP), layer),
        _layer_block((D_MODEL, D_MODEL), layer),
        _full((D_MODEL, 2 * LANES)), _full((1, LANES)),
        pl.BlockSpec((1, 1, up_rows, D_EXPERT), w_in),
        pl.BlockSpec((1, 1, up_rows, D_EXPERT), w_in),
        pl.BlockSpec((1, 1, down_rows, D_MODEL), w_in),
    ]
    out_specs = [
        pl.BlockSpec((1, TS, D_MODEL), lambda b, s: (b, s, 0)),
        pl.BlockSpec((1, TS // LANES, LANES), lambda b, s: (b * steps + s, 0, 0)),
        pl.BlockSpec((1, up_rows, 2 * D_EXPERT), w_out),
        pl.BlockSpec((1, down_rows, D_MODEL), w_out),
        pl.BlockSpec(memory_space=pl.ANY),
    ]
    block_scratch = [
        pltpu.VMEM((HALO + SUB_ROWS, LANES), F32),
        pltpu.VMEM((HALO + SUB_ROWS, LANES), F32),
        pltpu.VMEM((SUB_ROWS, LANES), F32),
        pltpu.VMEM((SUB_ROWS, LANES), F32),
        pltpu.VMEM((HALO + SUB_ROWS, D_GROUP), F32),
        pltpu.VMEM((HALO + SUB_ROWS, D_GROUP), F32),
        pltpu.VMEM((HALO + SUB_ROWS, D_GROUP), F32),
        pltpu.VMEM((HALO + SUB_ROWS, D_GROUP), F32),
        pltpu.VMEM((SC_HALO + SUB_ROWS, D_GROUP), F32),
        pltpu.VMEM((SUB_ROWS, D_MODEL), BF16),
    ]
    scratch = block_scratch * (TS // SUB_ROWS) + [pltpu.VMEM((fill_rows, LANES), F32), pltpu.SemaphoreType.DMA(())]
    return pl.pallas_call(
        _mixer_kernel,
        grid=(batch, steps),
        in_specs=in_specs,
        out_specs=out_specs,
        out_shape=[jax.ShapeDtypeStruct((batch, seq, D_MODEL), F32),
                   jax.ShapeDtypeStruct((batch * steps, TS // LANES, LANES), I32),
                   jax.ShapeDtypeStruct((N_EXPERTS, D_MODEL, 2 * D_EXPERT), BF16),
                   jax.ShapeDtypeStruct((N_EXPERTS, D_EXPERT, D_MODEL), BF16),
                   jax.ShapeDtypeStruct((n_sorted * ROW_TILES, LANES), F32)],
        scratch_shapes=scratch,
        compiler_params=pltpu.CompilerParams(
            dimension_semantics=("arbitrary", "arbitrary"), vmem_limit_bytes=VMEM_LIMIT),
        name="mixer",
    )(x, *p, *expert_w)


def _plan_call(cls2d):
    return pl.pallas_call(
        _plan_kernel,
        out_shape=[jax.ShapeDtypeStruct((LANES, LANES), I32), jax.ShapeDtypeStruct((SUBLANES, LANES), I32)],
        compiler_params=pltpu.CompilerParams(vmem_limit_bytes=VMEM_LIMIT),
        name="moe_plan",
    )(cls2d)


def _dispatch_call(dest, x2d, xs_zero):
    n_tok = x2d.shape[0]
    return pl.pallas_call(
        _dispatch_kernel,
        grid_spec=pltpu.PrefetchScalarGridSpec(
            num_scalar_prefetch=1,
            grid=(n_tok // (2 * TD),),
            in_specs=[pl.BlockSpec((2 * TD, D_MODEL), lambda i, *_: (i, 0)),
                      pl.BlockSpec(memory_space=pl.ANY)],
            out_specs=pl.BlockSpec(memory_space=pl.ANY),
            scratch_shapes=[
                pltpu.VMEM((2, TD * ROW_TILES, LANES), F32),
                pltpu.SemaphoreType.DMA((2,)),
            ]),
        out_shape=jax.ShapeDtypeStruct(xs_zero.shape, F32),
        input_output_aliases={2: 0},
        compiler_params=pltpu.CompilerParams(
            dimension_semantics=("arbitrary",), vmem_limit_bytes=VMEM_LIMIT),
        name="moe_dispatch",
    )(dest, x2d, xs_zero)


def _combine_call(dest, ys, vec_m, layer):
    n_tok = dest.shape[0]
    return pl.pallas_call(
        _combine_kernel,
        grid_spec=pltpu.PrefetchScalarGridSpec(
            num_scalar_prefetch=1,
            grid=(n_tok // (2 * TD),),
            in_specs=[pl.BlockSpec(memory_space=pl.ANY), _layer_block((SUBLANES, D_MODEL), layer)],
            out_specs=pl.BlockSpec((2 * TD, D_MODEL), lambda i, *_: (i, 0)),
            scratch_shapes=[
                pltpu.VMEM((2, TD * ROW_TILES, LANES), F32),
                pltpu.SemaphoreType.DMA((2,)),
            ]),
        out_shape=jax.ShapeDtypeStruct((n_tok, D_MODEL), F32),
        compiler_params=pltpu.CompilerParams(
            dimension_semantics=("arbitrary",), vmem_limit_bytes=VMEM_LIMIT),
        name="moe_combine",
    )(dest, ys, vec_m)


def _moe_call(meta, xs, wgu, wd, rwt, n_tiles):
    rows = TILES_PER_STEP * TMS * ROW_TILES
    w_specs, w_args = [], []
    for s in range(TILES_PER_STEP):
        for sel in (0, 1):
            def expert(i, meta, s=s, sel=sel):
                return (meta[sel, i * TILES_PER_STEP + s], 0, 0)
            w_specs += [pl.BlockSpec((1, D_MODEL, 2 * D_EXPERT), expert),
                        pl.BlockSpec((1, D_EXPERT, D_MODEL), expert)]
            w_args += [wgu, wd]
    tile = pl.BlockSpec((rows, LANES), lambda i, meta: (i, 0))
    return pl.pallas_call(
        _moe_kernel,
        grid_spec=pltpu.PrefetchScalarGridSpec(
            num_scalar_prefetch=1,
            grid=(n_tiles // TILES_PER_STEP,),
            in_specs=[tile] + w_specs + [_full((N_EXPERTS, D_MODEL))],
            out_specs=tile),
        out_shape=jax.ShapeDtypeStruct(xs.shape, F32),
        compiler_params=pltpu.CompilerParams(
            dimension_semantics=("arbitrary",), vmem_limit_bytes=VMEM_LIMIT),
        name="moe_experts",
    )(meta, xs, *w_args, rwt)


def _block_diag(w):
    g, d, _ = w.shape
    eye = jnp.eye(g, dtype=w.dtype)
    return (eye[:, None, :, None] * w[:, :, None, :]).reshape(g * d, g * d)


def kernel(x, w_in, gm_ln_g, gm_ln_b, gm_w_s, gm_b_s, cf_dw_w, cf_dw_b, cf_ln_g, cf_ln_b, cf_pw,
           pool_w, pool_scale, sc_w, mix_norm_g, w_o, ln1_g, ln1_b, router_w, router_b,
           exp_w_gate, exp_w_up, exp_w_down, ln2_g, ln2_b):
    batch, seq, _ = x.shape
    n_tok = batch * seq
    assert n_tok == LANES * LANES, "the routing plan lays tokens out as one (128, 128) tile grid"
    assert seq % TS == 0 and TS % SUB_ROWS == 0 and SUB_ROWS % (2 * CONV_ROWS) == 0 and n_tok % (2 * TD) == 0
    n_tiles = n_tok // TMS + N_CLASSES
    n_tiles = -(-n_tiles // TILES_PER_STEP) * TILES_PER_STEP
    assert n_tiles <= LANES and n_tiles % TILES_PER_STEP == 0
    def pack_rows(rows):
        slab = jnp.stack(rows, axis=1)
        return jnp.pad(slab, ((0, 0), (0, SUBLANES - len(rows)), (0, 0)))

    rw = jnp.pad(router_w, ((0, 0), (0, LANES - N_EXPERTS)))
    rw_hi = rw.astype(BF16)
    rw_lo = (rw - rw_hi.astype(F32)).astype(BF16)
    rb = jnp.pad(router_b, (0, LANES - N_EXPERTS)).reshape(1, LANES)
    rwt = router_w.T
    vec_m = pack_rows([mix_norm_g, ln1_g, ln1_b, ln2_g, ln2_b])
    params = (
        w_in.astype(BF16),
        pack_rows([gm_ln_g, gm_ln_b, cf_dw_b, cf_ln_g, cf_ln_b, pool_scale]), vec_m,
        gm_w_s.reshape(DEPTH, N_HEADS * CHUNK, CHUNK),
        jnp.repeat(gm_b_s.transpose(0, 2, 1), HEAD_DIM, axis=2),
        jnp.pad(cf_dw_w, ((0, 0), (0, HALO - CONF_K), (0, 0))),
        cf_pw.astype(BF16),
        jax.vmap(_block_diag)(pool_w).astype(BF16),
        jnp.pad(sc_w, ((0, 0), (0, SUBLANES - SHORT_K), (0, 0))),
        w_o.astype(BF16),
        jnp.concatenate([rw_hi, rw_lo], axis=1), rb,
    )
    for l in range(DEPTH):
        x1, cls, wgu_b, wd_b, xs_zero = _mixer_call(
            x, params, (exp_w_gate, exp_w_up, exp_w_down), n_tiles * TMS, l)
        x1 = x1.reshape(n_tok, D_MODEL)
        dest2d, meta = _plan_call(cls.reshape(LANES, LANES))
        dest = dest2d.reshape(n_tok)
        xs = _dispatch_call(dest, x1, xs_zero)
        ys = _moe_call(meta, xs, wgu_b, wd_b, rwt, n_tiles)
        x = _combine_call(dest, ys, vec_m, l).reshape(batch, seq, D_MODEL)
    return x
```

```python
import jax
import jax.numpy as jnp
from jax import lax
from jax.experimental import pallas as pl
from jax.experimental.pallas import tpu as pltpu

D_MODEL = 1024
DEPTH = 2
D_GROUP = 256
N_HEADS = 4
HEAD_DIM = 64
CHUNK = 128
CONF_K = 31
SHORT_K = 3
N_EXPERTS = 16
N_GROUPS = 4
GROUP_SIZE = 4
N_PAIRS = 6
N_CLASSES = N_GROUPS * N_PAIRS
D_EXPERT = 512
ALPHA = (2 * DEPTH) ** 0.25
LN_EPS = 1e-5
RMS_EPS = 1e-6

LANES = 128
SUBLANES = 8
ROW_TILES = D_MODEL // LANES
HALO = 32
SC_HALO = SUBLANES
TS = 512
SUB_ROWS = 256
CONV_ROWS = 64
TMS = 256
TILES_PER_STEP = 2
TD = 512
V7X_VMEM_BYTES = 64 * 1024 * 1024
VMEM_LIMIT = V7X_VMEM_BYTES * 7 // 8

BF16 = jnp.bfloat16
F32 = jnp.float32
I32 = jnp.int32


def _dot(a, b):
    return jnp.dot(a, b, preferred_element_type=F32)


def _layer_norm(x, g, b):
    mu = jnp.mean(x, axis=-1, keepdims=True)
    xc = x - mu
    var = jnp.mean(xc * xc, axis=-1, keepdims=True)
    return xc * lax.rsqrt(var + LN_EPS) * g + b


def _rms_norm(y, g):
    ms = jnp.mean(y * y, axis=-1, keepdims=True)
    return y * lax.rsqrt(ms + RMS_EPS) * g


def _sigmoid(x):
    return 1.0 / (1.0 + jnp.exp(-x))


def _route_class(sel):
    scores = []
    for g in range(N_GROUPS):
        v = sel[g * GROUP_SIZE:(g + 1) * GROUP_SIZE]
        best_pair = None
        for i in range(GROUP_SIZE):
            for j in range(i + 1, GROUP_SIZE):
                p = v[i] + v[j]
                best_pair = p if best_pair is None else jnp.maximum(best_pair, p)
        scores.append(best_pair)
    best = jnp.zeros(scores[0].shape, I32)
    best_score = scores[0]
    for g in range(1, N_GROUPS):
        better = scores[g] > best_score
        best = jnp.where(better, g, best)
        best_score = jnp.where(better, scores[g], best_score)
    v = []
    for j in range(GROUP_SIZE):
        out = sel[j]
        for g in range(1, N_GROUPS):
            out = jnp.where(best == g, sel[g * GROUP_SIZE + j], out)
        v.append(out)
    i0 = jnp.zeros_like(best)
    v0 = v[0]
    for j in range(1, GROUP_SIZE):
        better = v[j] > v0
        i0 = jnp.where(better, j, i0)
        v0 = jnp.where(better, v[j], v0)
    neg = jnp.full_like(v0, -jnp.inf)
    w = [jnp.where(i0 == j, neg, v[j]) for j in range(GROUP_SIZE)]
    i1 = jnp.zeros_like(best)
    v1 = w[0]
    for j in range(1, GROUP_SIZE):
        better = w[j] > v1
        i1 = jnp.where(better, j, i1)
        v1 = jnp.where(better, w[j], v1)
    a = jnp.minimum(i0, i1)
    b = jnp.maximum(i0, i1)
    pair = jnp.where(a == 0, b - 1, jnp.where(a == 1, b + 1, N_PAIRS - 1))
    return best * N_PAIRS + pair


def _mixer_kernel(x_ref, w_in_ref, vec_g_ref, vec_m_ref, gm_w_ref, gm_bs_ref, cf_w_ref, cf_pw_ref,
                  pool_w_ref, sc_w_ref, w_o_ref, rw_ref, rb_ref, wg_ref, wu_ref, wd_ref,
                  x1_ref, cls_ref, wgu_out, wd_out, *scratch):
    s_idx = pl.program_id(1)
    gm_g, gm_b, cf_b, cf_g, cf_beta, pool_scale = (vec_g_ref[0, k:k + 1, :] for k in range(6))
    mixg, ln_g, ln_b = (vec_m_ref[0, k:k + 1, :] for k in range(3))

    row = lax.broadcasted_iota(I32, (N_HEADS * CHUNK, CHUNK), 0)
    col = lax.broadcasted_iota(I32, (N_HEADS * CHUNK, CHUNK), 1)
    w_tril = jnp.where(col <= (row & (CHUNK - 1)), gm_w_ref[0], 0.0).astype(BF16)
    lane = lax.broadcasted_iota(I32, (CHUNK, D_GROUP), 1)
    lane_t = lax.broadcasted_iota(I32, (SUB_ROWS, D_GROUP), 1)
    win = jnp.where(lane_t < HEAD_DIM, 2,
                    jnp.where(lane_t < 2 * HEAD_DIM, 4, jnp.where(lane_t < 3 * HEAD_DIM, 8, 16)))
    bs = gm_bs_ref[0]

    n_blocks = TS // SUB_ROWS
    per_block = len(scratch) // n_blocks
    blocks = [scratch[i * per_block:(i + 1) * per_block] for i in range(n_blocks)]

    @pl.when(s_idx == 0)
    def _():
        hbuf0, hbuf1, _, _, pbuf, _, _, _, cbuf, _ = blocks[0]
        hbuf0[0:HALO, :] = jnp.zeros((HALO, LANES), F32)
        hbuf1[0:HALO, :] = jnp.zeros((HALO, LANES), F32)
        pbuf[0:HALO, :] = jnp.zeros((HALO, D_GROUP), F32)
        cbuf[0:SC_HALO, :] = jnp.zeros((SC_HALO, D_GROUP), F32)

    def carry_halo(src, dst, which):
        for i in which:
            rows = SC_HALO if i == 8 else HALO
            dst[i][0:rows, :] = src[i][SUB_ROWS:SUB_ROWS + rows, :]

    st = [dict() for _ in range(n_blocks)]

    def load(bi):
        x = x_ref[0, bi * SUB_ROWS:(bi + 1) * SUB_ROWS, :]
        st[bi]["x"] = x
        st[bi]["xb"] = x.astype(BF16)

    def in_proj(bi, lo, hi):
        z = _dot(st[bi]["xb"], w_in_ref[0, :, lo * D_GROUP:hi * D_GROUP])
        for j in range(lo, hi):
            st[bi][j] = z[:, (j - lo) * D_GROUP:(j - lo + 1) * D_GROUP]

    def gating_mlp(bi):
        yn_ref = blocks[bi][9]
        u = st[bi].pop(0)
        v = _layer_norm(st[bi].pop(1), gm_g, gm_b)
        for n in range(SUB_ROWS // CHUNK):
            crows = slice(n * CHUNK, (n + 1) * CHUNK)
            s_all = _dot(w_tril, v[crows, :].astype(BF16))
            s_sel = s_all[3 * CHUNK:4 * CHUNK]
            for h in (2, 1, 0):
                s_sel = jnp.where(lane < (h + 1) * HEAD_DIM, s_all[h * CHUNK:(h + 1) * CHUNK], s_sel)
            y1 = u[crows, :] * (s_sel + bs)
            yn_ref[crows, 0:D_GROUP] = _rms_norm(y1, mixg[:, 0:D_GROUP]).astype(BF16)

    def conformer_glu(bi):
        if bi > 0:
            carry_halo(blocks[bi - 1], blocks[bi], (0, 1))
        glu = st[bi].pop(2) * _sigmoid(st[bi].pop(3))
        for half in range(2):
            blocks[bi][half][HALO:HALO + SUB_ROWS, :] = glu[:, half * LANES:(half + 1) * LANES]

    def conformer_conv(bi, half):
        hb, co = blocks[bi][half], blocks[bi][2 + half]
        lanes = slice(half * LANES, (half + 1) * LANES)
        bias = cf_b[:, lanes]
        for q in range(SUB_ROWS // (2 * CONV_ROWS)):
            for parity in range(2):
                out0 = q * 2 * CONV_ROWS + parity
                base = out0 + HALO - (CONF_K - 1)
                acc = jnp.zeros((CONV_ROWS, LANES), F32) + bias
                for k in range(CONF_K):
                    acc = acc + cf_w_ref[0, k:k + 1, lanes] * hb[pl.ds(base + k, CONV_ROWS, stride=2), :]
                co[pl.ds(out0, CONV_ROWS, stride=2), :] = acc

    def conformer_out(bi):
        cbo0, cbo1, yn_ref = blocks[bi][2], blocks[bi][3], blocks[bi][9]
        hln = _layer_norm(jnp.concatenate([cbo0[...], cbo1[...]], axis=1), cf_g, cf_beta)
        y2 = _dot((hln * _sigmoid(hln)).astype(BF16), cf_pw_ref[0])
        yn_ref[:, D_GROUP:2 * D_GROUP] = _rms_norm(y2, mixg[:, D_GROUP:2 * D_GROUP]).astype(BF16)

    def pooling(bi):
        pbuf, s2buf, s4buf, s8buf = blocks[bi][4:8]
        yn_ref = blocks[bi][9]
        if bi > 0:
            carry_halo(blocks[bi - 1], blocks[bi], (4,))
        zc = st[bi].pop(4)
        pbuf[HALO:HALO + SUB_ROWS, :] = zc
        n2 = HALO + SUB_ROWS - 8
        s2buf[8:8 + n2, :] = pbuf[8:8 + n2, :] + pbuf[7:7 + n2, :]
        n4 = HALO + SUB_ROWS - 16
        s4buf[16:16 + n4, :] = s2buf[16:16 + n4, :] + s2buf[14:14 + n4, :]
        n8 = HALO + SUB_ROWS - 24
        s8buf[24:24 + n8, :] = s4buf[24:24 + n8, :] + s4buf[20:20 + n8, :]
        cur = slice(HALO, HALO + SUB_ROWS)
        s16 = s8buf[cur, :] + s8buf[HALO - 8:HALO - 8 + SUB_ROWS, :]
        pos1 = lax.broadcasted_iota(I32, (SUB_ROWS, D_GROUP), 0) + (s_idx * TS + bi * SUB_ROWS + 1)
        wsum = jnp.where(lane_t < HEAD_DIM, s2buf[cur, :],
                         jnp.where(lane_t < 2 * HEAD_DIM, s4buf[cur, :],
                                   jnp.where(lane_t < 3 * HEAD_DIM, s8buf[cur, :], s16)))
        count = jnp.minimum(pos1, win).astype(F32)
        pooled = wsum / count - zc
        y3 = _dot(pooled.astype(BF16), pool_w_ref[0]) * pool_scale
        yn_ref[:, 2 * D_GROUP:3 * D_GROUP] = _rms_norm(y3, mixg[:, 2 * D_GROUP:3 * D_GROUP]).astype(BF16)

    def short_conv(bi):
        cbuf, yn_ref = blocks[bi][8], blocks[bi][9]
        if bi > 0:
            carry_halo(blocks[bi - 1], blocks[bi], (8,))
        cbuf[SC_HALO:SC_HALO + SUB_ROWS, :] = st[bi].pop(6) * st[bi].pop(7)
        conv = jnp.zeros((SUB_ROWS, D_GROUP), F32)
        for k in range(SHORT_K):
            off = SC_HALO - (SHORT_K - 1) + k
            conv = conv + sc_w_ref[0, k:k + 1, :] * cbuf[off:off + SUB_ROWS, :]
        y4 = st[bi].pop(5) * conv
        yn_ref[:, 3 * D_GROUP:4 * D_GROUP] = _rms_norm(y4, mixg[:, 3 * D_GROUP:4 * D_GROUP]).astype(BF16)

    def out_proj(bi):
        m = _dot(blocks[bi][9][...], w_o_ref[0])
        x1 = _layer_norm(ALPHA * st[bi].pop("x") + m, ln_g, ln_b)
        for j in range(ROW_TILES):
            x1_ref[pl.ds(bi * SUB_ROWS * ROW_TILES + j, SUB_ROWS, stride=ROW_TILES), :] = x1[:, j * LANES:(j + 1) * LANES]
        st[bi]["x1"] = x1

    def router(bi):
        x1 = st[bi].pop("x1")
        hi = x1.astype(BF16)
        lo = (x1 - hi.astype(F32)).astype(BF16)
        hi_both = _dot(hi, rw_ref[...])
        logits = hi_both[:, 0:LANES] + hi_both[:, LANES:2 * LANES] + _dot(lo, rw_ref[:, 0:LANES])
        st[bi]["sel"] = logits + rb_ref[...]

    def route(bi):
        sel_t = st[bi].pop("sel").T
        cls = _route_class([sel_t[e:e + 1, :] for e in range(N_EXPERTS)])
        r0 = bi * SUB_ROWS
        cls_ref[0, r0 // LANES:(r0 + SUB_ROWS) // LANES, :] = jnp.concatenate(
            [cls[:, k * LANES:(k + 1) * LANES] for k in range(SUB_ROWS // LANES)], axis=0)

    def conformer(b):
        conformer_glu(b)
        conformer_conv(b, 0)
        conformer_conv(b, 1)
        conformer_out(b)

    def round_expert_weights():
        wgu_out[0, :, 0:D_EXPERT] = wg_ref[0, 0].astype(BF16)
        wgu_out[0, :, D_EXPERT:2 * D_EXPERT] = wu_ref[0, 0].astype(BF16)
        wd_out[0] = wd_ref[0, 0].astype(BF16)

    def stage1(b):
        pieces = [lambda: load(b), lambda: in_proj(b, 0, 2), lambda: in_proj(b, 2, 4), lambda: in_proj(b, 4, 5),
                  lambda: in_proj(b, 5, 8)]
        if b == 0:
            pieces.insert(2, round_expert_weights)
        return pieces

    def stage2(b):
        return [lambda: gating_mlp(b), lambda: conformer(b), lambda: pooling(b), lambda: short_conv(b)]

    def stage3(b):
        return [lambda: out_proj(b), lambda: router(b), lambda: route(b)]

    for t in range(n_blocks + 2):
        stages = [stage(t - lag) for lag, stage in ((2, stage3), (0, stage1), (1, stage2)) if 0 <= t - lag < n_blocks]
        for k in range(max(len(stage) for stage in stages)):
            for stage in stages:
                if k < len(stage):
                    stage[k]()

    carry_halo(blocks[-1], blocks[0], (0, 1, 4, 8))


def _plan_kernel(cls_ref, dest_ref, meta_ref):
    n = LANES
    cls = cls_ref[...]
    r = lax.broadcasted_iota(I32, (n, n), 0)
    c = lax.broadcasted_iota(I32, (n, n), 1)
    upper = jnp.where(r < c, 1.0, 0.0).astype(BF16)
    lower = jnp.where(c < r, 1.0, 0.0).astype(BF16)
    ones = jnp.ones((n, n), BF16)
    masks = [cls == k for k in range(N_CLASSES)]
    m_all = jnp.concatenate([jnp.where(mk, 1.0, 0.0) for mk in masks], axis=0).astype(BF16)
    within = _dot(m_all, upper)
    rowsum = _dot(m_all, ones)
    rs = jnp.zeros((n, n), F32)
    for k in range(N_CLASSES):
        rs = jnp.where(c == k, rowsum[k * n:(k + 1) * n], rs)
    rs_b = rs.astype(BF16)
    before = _dot(lower, rs_b)
    total = _dot(ones, rs_b)
    ntile = jnp.floor((total + (TMS - 1)) * (1.0 / TMS))
    tstart = _dot(ntile.astype(BF16), upper)
    base = before + tstart * TMS
    dest = jnp.zeros((n, n), F32)
    for k in range(N_CLASSES):
        dest = jnp.where(masks[k], within[k * n:(k + 1) * n] + base[:, k:k + 1], dest)
    dest_ref[...] = dest.astype(I32)

    tend_t = (tstart + ntile).T
    tile = c.astype(F32)
    ended = jnp.where((r < N_CLASSES) & (tend_t <= tile), 1.0, 0.0)
    tcls = jnp.minimum(jnp.sum(ended, axis=0, keepdims=True), N_CLASSES - 1.0)
    grp = (jnp.where(tcls >= N_PAIRS, 1.0, 0.0) + jnp.where(tcls >= 2 * N_PAIRS, 1.0, 0.0)
           + jnp.where(tcls >= 3 * N_PAIRS, 1.0, 0.0))
    pair = tcls - N_PAIRS * grp
    pa = jnp.where(pair >= 3, 1.0, 0.0) + jnp.where(pair >= 5, 1.0, 0.0)
    pb = jnp.where(pair == 0, 1.0, jnp.where((pair == 1) | (pair == 3), 2.0, 3.0))
    n_tiles = tstart[0:1, N_CLASSES:N_CLASSES + 1]
    tile_row = tile[0:1, :]
    active = jnp.where(tile_row < n_tiles, 1.0, 0.0)
    left = jnp.where(r.astype(F32) == tcls, total.T - (tile - tstart.T) * TMS, 0.0)
    held = jnp.clip(jnp.sum(left, axis=0, keepdims=True), 0.0, float(TMS)) * active
    meta = jnp.concatenate(
        [GROUP_SIZE * grp + pa, GROUP_SIZE * grp + pb, active, held, jnp.zeros((SUBLANES - 4, n), F32)], axis=0)
    meta_ref[...] = meta.astype(I32)


def _row_copy_wait(src, dst, sem):
    pltpu.make_async_copy(src, dst, sem).wait()


def _source_kernel(dest_ref, meta_ref, src_ref):
    n_tok = dest_ref.shape[0]
    n_tiles = src_ref.shape[0] // TMS

    def pad_tile(k, carry):
        def pad(s, carry):
            src_ref[k * TMS + s] = 0
            return carry
        return lax.fori_loop(meta_ref[3, k], TMS, pad, carry)

    def place(t, carry):
        src_ref[dest_ref[t]] = t
        return carry

    lax.fori_loop(0, n_tiles, pad_tile, 0)
    lax.fori_loop(0, n_tok, place, 0, unroll=16)


def _combine_kernel(dest_ref, ys_ref, vec_m_ref, out_ref, buf, sems):
    i = pl.program_id(0)
    n_steps = pl.num_programs(0)

    def issue(step, s):
        for t in range(TD):
            d = pl.multiple_of(dest_ref[(step * 2 + s) * TD + t] * ROW_TILES, ROW_TILES)
            pltpu.make_async_copy(ys_ref.at[pl.ds(d, ROW_TILES), :],
                                  buf.at[s, pl.ds(t * ROW_TILES, ROW_TILES), :], sems.at[s]).start(priority=t % 2)

    @pl.when(i == 0)
    def _():
        for s in range(2):
            issue(0, s)

    for s in range(2):
        _row_copy_wait(ys_ref.at[pl.ds(0, TD * ROW_TILES), :], buf.at[s], sems.at[s])
        r = jnp.concatenate([buf[s, pl.ds(j, TD, stride=ROW_TILES), :] for j in range(ROW_TILES)], axis=1)
        out_ref[s * TD:(s + 1) * TD, :] = _layer_norm(r, vec_m_ref[0, 3:4, :], vec_m_ref[0, 4:5, :])

        @pl.when(i + 1 < n_steps)
        def _():
            issue(i + 1, s)


def _moe_kernel(meta_ref, src_ref, x_ref, *refs):
    i = pl.program_id(0)
    n_steps = pl.num_programs(0)
    w_refs, rwt_ref, ys_ref, buf, sems = refs[:-4], refs[-4], refs[-3], refs[-2], refs[-1]
    rows = TMS * ROW_TILES
    step_tokens = TILES_PER_STEP * TMS
    slot = i % 2

    def gather(step, slot, lo, hi):
        for t in range(lo, hi):
            row = pl.multiple_of(src_ref[step * step_tokens + t] * ROW_TILES, ROW_TILES)
            pltpu.make_async_copy(x_ref.at[pl.ds(row, ROW_TILES), :],
                                  buf.at[slot, pl.ds(t * ROW_TILES, ROW_TILES), :],
                                  sems.at[slot]).start(priority=t % 2)

    def gather_wait(slot):
        _row_copy_wait(x_ref.at[pl.ds(0, step_tokens * ROW_TILES), :], buf.at[slot], sems.at[slot])

    @pl.when(i == 0)
    def _():
        gather(0, 0, 0, step_tokens)

    @pl.when(meta_ref[2, i * TILES_PER_STEP] == 0)
    def _():
        ys_ref[...] = jnp.zeros(ys_ref.shape, F32)

        @pl.when(meta_ref[2, jnp.maximum(i - 1, 0) * TILES_PER_STEP] == 1)
        def _():
            gather_wait(slot)

    @pl.when(meta_ref[2, i * TILES_PER_STEP] == 1)
    def _():
        gather_wait(slot)
        nxt = jnp.minimum(i + 1, n_steps - 1)
        st = [dict() for _ in range(TILES_PER_STEP)]

        def load(s):
            x = jnp.concatenate(
                [buf[slot, pl.ds(s * rows + j, TMS, stride=ROW_TILES), :] for j in range(ROW_TILES)], axis=1)
            st[s]["x"] = x
            st[s]["xb"] = x.astype(BF16)

        def up(s, e):
            gu = _dot(st[s]["xb"], w_refs[4 * s + 2 * e][0])
            st[s]["g", e] = gu[:, 0:D_EXPERT]
            st[s]["u", e] = gu[:, D_EXPERT:2 * D_EXPERT]

        def act(s, e):
            g = st[s].pop(("g", e))
            st[s]["h", e] = ((g * _sigmoid(g)) * st[s].pop(("u", e))).astype(BF16)

        def down(s, e):
            st[s]["y", e] = _dot(st[s].pop(("h", e)), w_refs[4 * s + 2 * e + 1][0])

        def mix(s):
            tile = i * TILES_PER_STEP + s
            x = st[s].pop("x")
            la = jnp.sum(x * rwt_ref[pl.ds(meta_ref[0, tile], 1), :], axis=-1, keepdims=True)
            lb = jnp.sum(x * rwt_ref[pl.ds(meta_ref[1, tile], 1), :], axis=-1, keepdims=True)
            m = jnp.maximum(la, lb)
            pa = jnp.exp(la - m)
            pb = jnp.exp(lb - m)
            den = pa + pb
            r = ALPHA * x + ((pa / den) * st[s].pop(("y", 0)) + (pb / den) * st[s].pop(("y", 1)))
            for j in range(ROW_TILES):
                ys_ref[pl.ds(s * rows + j, TMS, stride=ROW_TILES), :] = r[:, j * LANES:(j + 1) * LANES]

        units = [(s, e) for s in range(TILES_PER_STEP) for e in range(2)]
        piece = step_tokens // len(units)
        load(0)
        up(*units[0])
        for k, (s, e) in enumerate(units):
            if k + 1 < len(units):
                nxt_unit = units[k + 1]
                if nxt_unit[1] == 0:
                    load(nxt_unit[0])
                up(*nxt_unit)
            act(s, e)
            gather(nxt, 1 - slot, k * piece, (k + 1) * piece)
            down(s, e)
            if e == 1:
                mix(s)

        @pl.when(i == n_steps - 1)
        def _():
            gather_wait(1 - slot)


def _full(shape):
    return pl.BlockSpec(shape, lambda *_: (0,) * len(shape))


def _layer_block(shape, layer):
    return pl.BlockSpec((1,) + shape, lambda *_: (layer,) + (0,) * len(shape))


def _mixer_call(x, p, expert_w, layer):
    batch, seq, _ = x.shape
    steps = seq // TS
    slabs = batch * steps // N_EXPERTS
    assert batch * steps == slabs * N_EXPERTS

    def w_in(b, s):
        i = b * steps + s
        return (layer, i // slabs, i % slabs, 0)

    def w_out(b, s):
        i = b * steps + s
        return (i // slabs, i % slabs, 0)

    up_rows, down_rows = D_MODEL // slabs, D_EXPERT // slabs
    in_specs = [
        pl.BlockSpec((1, TS, D_MODEL), lambda b, s: (b, s, 0)),
        _layer_block((D_MODEL, 8 * D_GROUP), layer),
        _layer_block((SUBLANES, D_GROUP), layer), _layer_block((SUBLANES, D_MODEL), layer),
        _layer_block((N_HEADS * CHUNK, CHUNK), layer), _layer_block((CHUNK, D_GROUP), layer),
        _layer_block((HALO, D_GROUP), layer),
        _layer_block((D_GROUP, D_GROUP), layer),
        _layer_block((D_GROUP, D_GROUP), layer),
        _layer_block((SUBLANES, D_GROUP), layer),
        _layer_block((D_MODEL, D_MODEL), layer),
        _full((D_MODEL, 2 * LANES)), _full((1, LANES)),
        pl.BlockSpec((1, 1, up_rows, D_EXPERT), w_in),
        pl.BlockSpec((1, 1, up_rows, D_EXPERT), w_in),
        pl.BlockSpec((1, 1, down_rows, D_MODEL), w_in),
    ]
    out_specs = [
        pl.BlockSpec((TS * ROW_TILES, LANES), lambda b, s: (b * steps + s, 0)),
        pl.BlockSpec((1, TS // LANES, LANES), lambda b, s: (b * steps + s, 0, 0)),
        pl.BlockSpec((1, up_rows, 2 * D_EXPERT), w_out),
        pl.BlockSpec((1, down_rows, D_MODEL), w_out),
    ]
    block_scratch = [
        pltpu.VMEM((HALO + SUB_ROWS, LANES), F32),
        pltpu.VMEM((HALO + SUB_ROWS, LANES), F32),
        pltpu.VMEM((SUB_ROWS, LANES), F32),
        pltpu.VMEM((SUB_ROWS, LANES), F32),
        pltpu.VMEM((HALO + SUB_ROWS, D_GROUP), F32),
        pltpu.VMEM((HALO + SUB_ROWS, D_GROUP), F32),
        pltpu.VMEM((HALO + SUB_ROWS, D_GROUP), F32),
        pltpu.VMEM((HALO + SUB_ROWS, D_GROUP), F32),
        pltpu.VMEM((SC_HALO + SUB_ROWS, D_GROUP), F32),
        pltpu.VMEM((SUB_ROWS, D_MODEL), BF16),
    ]
    scratch = block_scratch * (TS // SUB_ROWS)
    return pl.pallas_call(
        _mixer_kernel,
        grid=(batch, steps),
        in_specs=in_specs,
        out_specs=out_specs,
        out_shape=[jax.ShapeDtypeStruct((batch * seq * ROW_TILES, LANES), F32),
                   jax.ShapeDtypeStruct((batch * steps, TS // LANES, LANES), I32),
                   jax.ShapeDtypeStruct((N_EXPERTS, D_MODEL, 2 * D_EXPERT), BF16),
                   jax.ShapeDtypeStruct((N_EXPERTS, D_EXPERT, D_MODEL), BF16)],
        scratch_shapes=scratch,
        compiler_params=pltpu.CompilerParams(
            dimension_semantics=("arbitrary", "arbitrary"), vmem_limit_bytes=VMEM_LIMIT),
        name="mixer",
    )(x, *p, *expert_w)


def _plan_call(cls2d):
    return pl.pallas_call(
        _plan_kernel,
        out_shape=[jax.ShapeDtypeStruct((LANES, LANES), I32), jax.ShapeDtypeStruct((SUBLANES, LANES), I32)],
        compiler_params=pltpu.CompilerParams(vmem_limit_bytes=VMEM_LIMIT),
        name="moe_plan",
    )(cls2d)


def _source_call(dest, meta, n_sorted):
    smem = pl.BlockSpec(memory_space=pltpu.SMEM)
    return pl.pallas_call(
        _source_kernel,
        in_specs=[smem, smem],
        out_specs=smem,
        out_shape=jax.ShapeDtypeStruct((n_sorted,), I32),
        name="moe_source",
    )(dest, meta)


def _combine_call(dest, ys, vec_m, layer):
    n_tok = dest.shape[0]
    return pl.pallas_call(
        _combine_kernel,
        grid_spec=pltpu.PrefetchScalarGridSpec(
            num_scalar_prefetch=1,
            grid=(n_tok // (2 * TD),),
            in_specs=[pl.BlockSpec(memory_space=pl.ANY), _layer_block((SUBLANES, D_MODEL), layer)],
            out_specs=pl.BlockSpec((2 * TD, D_MODEL), lambda i, *_: (i, 0)),
            scratch_shapes=[
                pltpu.VMEM((2, TD * ROW_TILES, LANES), F32),
                pltpu.SemaphoreType.DMA((2,)),
            ]),
        out_shape=jax.ShapeDtypeStruct((n_tok, D_MODEL), F32),
        compiler_params=pltpu.CompilerParams(
            dimension_semantics=("arbitrary",), vmem_limit_bytes=VMEM_LIMIT),
        name="moe_combine",
    )(dest, ys, vec_m)


def _moe_call(meta, src, x_rows, wgu, wd, rwt, n_tiles):
    rows = TILES_PER_STEP * TMS * ROW_TILES
    w_specs, w_args = [], []
    for s in range(TILES_PER_STEP):
        for sel in (0, 1):
            def expert(i, meta, src, s=s, sel=sel):
                return (meta[sel, i * TILES_PER_STEP + s], 0, 0)
            w_specs += [pl.BlockSpec((1, D_MODEL, 2 * D_EXPERT), expert),
                        pl.BlockSpec((1, D_EXPERT, D_MODEL), expert)]
            w_args += [wgu, wd]
    return pl.pallas_call(
        _moe_kernel,
        grid_spec=pltpu.PrefetchScalarGridSpec(
            num_scalar_prefetch=2,
            grid=(n_tiles // TILES_PER_STEP,),
            in_specs=[pl.BlockSpec(memory_space=pl.ANY)] + w_specs + [_full((N_EXPERTS, D_MODEL))],
            out_specs=pl.BlockSpec((rows, LANES), lambda i, *_: (i, 0)),
            scratch_shapes=[
                pltpu.VMEM((2, rows, LANES), F32),
                pltpu.SemaphoreType.DMA((2,)),
            ]),
        out_shape=jax.ShapeDtypeStruct((n_tiles * TMS * ROW_TILES, LANES), F32),
        compiler_params=pltpu.CompilerParams(
            dimension_semantics=("arbitrary",), vmem_limit_bytes=VMEM_LIMIT),
        name="moe_experts",
    )(meta, src, x_rows, *w_args, rwt)


def _block_diag(w):
    g, d, _ = w.shape
    eye = jnp.eye(g, dtype=w.dtype)
    return (eye[:, None, :, None] * w[:, :, None, :]).reshape(g * d, g * d)


def kernel(x, w_in, gm_ln_g, gm_ln_b, gm_w_s, gm_b_s, cf_dw_w, cf_dw_b, cf_ln_g, cf_ln_b, cf_pw,
           pool_w, pool_scale, sc_w, mix_norm_g, w_o, ln1_g, ln1_b, router_w, router_b,
           exp_w_gate, exp_w_up, exp_w_down, ln2_g, ln2_b):
    batch, seq, _ = x.shape
    n_tok = batch * seq
    assert n_tok == LANES * LANES, "the routing plan lays tokens out as one (128, 128) tile grid"
    assert seq % TS == 0 and TS % SUB_ROWS == 0 and SUB_ROWS % (2 * CONV_ROWS) == 0 and n_tok % (2 * TD) == 0
    n_tiles = n_tok // TMS + N_CLASSES
    assert n_tiles <= LANES and n_tiles % TILES_PER_STEP == 0
    def pack_rows(rows):
        slab = jnp.stack(rows, axis=1)
        return jnp.pad(slab, ((0, 0), (0, SUBLANES - len(rows)), (0, 0)))

    rw = jnp.pad(router_w, ((0, 0), (0, LANES - N_EXPERTS)))
    rw_hi = rw.astype(BF16)
    rw_lo = (rw - rw_hi.astype(F32)).astype(BF16)
    rb = jnp.pad(router_b, (0, LANES - N_EXPERTS)).reshape(1, LANES)
    rwt = router_w.T
    vec_m = pack_rows([mix_norm_g, ln1_g, ln1_b, ln2_g, ln2_b])
    params = (
        w_in.astype(BF16),
        pack_rows([gm_ln_g, gm_ln_b, cf_dw_b, cf_ln_g, cf_ln_b, pool_scale]), vec_m,
        gm_w_s.reshape(DEPTH, N_HEADS * CHUNK, CHUNK),
        jnp.repeat(gm_b_s.transpose(0, 2, 1), HEAD_DIM, axis=2),
        jnp.pad(cf_dw_w, ((0, 0), (0, HALO - CONF_K), (0, 0))),
        cf_pw.astype(BF16),
        jax.vmap(_block_diag)(pool_w).astype(BF16),
        jnp.pad(sc_w, ((0, 0), (0, SUBLANES - SHORT_K), (0, 0))),
        w_o.astype(BF16),
        jnp.concatenate([rw_hi, rw_lo], axis=1), rb,
    )
    for l in range(DEPTH):
        x1, cls, wgu_b, wd_b = _mixer_call(x, params, (exp_w_gate, exp_w_up, exp_w_down), l)
        dest2d, meta = _plan_call(cls.reshape(LANES, LANES))
        dest = dest2d.reshape(n_tok)
        src = _source_call(dest, meta, n_tiles * TMS)
        ys = _moe_call(meta, src, x1, wgu_b, wd_b, rwt, n_tiles)
        x = _combine_call(dest, ys, vec_m, l).reshape(batch, seq, D_MODEL)
    return x
```

```python
import jax
import jax.numpy as jnp
from jax import lax
from jax.experimental import pallas as pl
from jax.experimental.pallas import tpu as pltpu

D_MODEL = 1024
DEPTH = 2
D_GROUP = 256
N_HEADS = 4
HEAD_DIM = 64
CHUNK = 128
CONF_K = 31
SHORT_K = 3
N_EXPERTS = 16
N_GROUPS = 4
GROUP_SIZE = 4
N_PAIRS = 6
N_CLASSES = N_GROUPS * N_PAIRS
D_EXPERT = 512
ALPHA = (2 * DEPTH) ** 0.25
LN_EPS = 1e-5
RMS_EPS = 1e-6

LANES = 128
SUBLANES = 8
ROW_TILES = D_MODEL // LANES
HALO = 32
SC_HALO = SUBLANES
TS = 512
SUB_ROWS = 256
CONV_ROWS = 64
TMS = 256
TILES_PER_STEP = 2
TD = 512
V7X_VMEM_BYTES = 64 * 1024 * 1024
VMEM_LIMIT = V7X_VMEM_BYTES * 7 // 8

BF16 = jnp.bfloat16
F32 = jnp.float32
I32 = jnp.int32


def _dot(a, b):
    return jnp.dot(a, b, preferred_element_type=F32)


def _layer_norm(x, g, b):
    mu = jnp.mean(x, axis=-1, keepdims=True)
    xc = x - mu
    var = jnp.mean(xc * xc, axis=-1, keepdims=True)
    return xc * lax.rsqrt(var + LN_EPS) * g + b


def _rms_norm(y, g):
    ms = jnp.mean(y * y, axis=-1, keepdims=True)
    return y * lax.rsqrt(ms + RMS_EPS) * g


def _sigmoid(x):
    return 1.0 / (1.0 + jnp.exp(-x))


def _route_class(sel):
    scores = []
    for g in range(N_GROUPS):
        v = sel[g * GROUP_SIZE:(g + 1) * GROUP_SIZE]
        best_pair = None
        for i in range(GROUP_SIZE):
            for j in range(i + 1, GROUP_SIZE):
                p = v[i] + v[j]
                best_pair = p if best_pair is None else jnp.maximum(best_pair, p)
        scores.append(best_pair)
    best = jnp.zeros(scores[0].shape, I32)
    best_score = scores[0]
    for g in range(1, N_GROUPS):
        better = scores[g] > best_score
        best = jnp.where(better, g, best)
        best_score = jnp.where(better, scores[g], best_score)
    v = []
    for j in range(GROUP_SIZE):
        out = sel[j]
        for g in range(1, N_GROUPS):
            out = jnp.where(best == g, sel[g * GROUP_SIZE + j], out)
        v.append(out)
    i0 = jnp.zeros_like(best)
    v0 = v[0]
    for j in range(1, GROUP_SIZE):
        better = v[j] > v0
        i0 = jnp.where(better, j, i0)
        v0 = jnp.where(better, v[j], v0)
    neg = jnp.full_like(v0, -jnp.inf)
    w = [jnp.where(i0 == j, neg, v[j]) for j in range(GROUP_SIZE)]
    i1 = jnp.zeros_like(best)
    v1 = w[0]
    for j in range(1, GROUP_SIZE):
        better = w[j] > v1
        i1 = jnp.where(better, j, i1)
        v1 = jnp.where(better, w[j], v1)
    a = jnp.minimum(i0, i1)
    b = jnp.maximum(i0, i1)
    pair = jnp.where(a == 0, b - 1, jnp.where(a == 1, b + 1, N_PAIRS - 1))
    return best * N_PAIRS + pair


def _mixer_kernel(x_ref, w_in_ref, gm_g_ref, gm_b_ref, cf_b_ref, cf_g_ref, cf_beta_ref, pool_scale_ref,
                  mixg_ref, ln_g_ref, ln_b_ref, gm_w_ref, gm_bs_ref, cf_w_ref, cf_pw_ref,
                  pool_w_ref, sc_w_ref, w_o_ref, rw_ref, rb_ref, wg_ref, wu_ref, wd_ref,
                  x1_ref, cls_ref, wgu_out, wd_out, xs_ref, *scratch):
    s_idx = pl.program_id(1)
    *scratch, zbuf, zsem = scratch
    step = pl.program_id(0) * pl.num_programs(1) + s_idx
    fill_rows = zbuf.shape[0]

    @pl.when(step == 0)
    def _():
        zbuf[...] = jnp.zeros(zbuf.shape, F32)

    fill = pltpu.make_async_copy(
        zbuf, xs_ref.at[pl.ds(pl.multiple_of(step * fill_rows, fill_rows), fill_rows), :], zsem)
    fill.start()
    gm_g, gm_b, cf_b, cf_g, cf_beta, pool_scale = (
        ref[0] for ref in (gm_g_ref, gm_b_ref, cf_b_ref, cf_g_ref, cf_beta_ref, pool_scale_ref))
    mixg, ln_g, ln_b = mixg_ref[0], ln_g_ref[0], ln_b_ref[0]

    row = lax.broadcasted_iota(I32, (N_HEADS * CHUNK, CHUNK), 0)
    col = lax.broadcasted_iota(I32, (N_HEADS * CHUNK, CHUNK), 1)
    w_tril = jnp.where(col <= (row & (CHUNK - 1)), gm_w_ref[0], 0.0).astype(BF16)
    lane = lax.broadcasted_iota(I32, (CHUNK, D_GROUP), 1)
    lane_t = lax.broadcasted_iota(I32, (SUB_ROWS, D_GROUP), 1)
    win = jnp.where(lane_t < HEAD_DIM, 2,
                    jnp.where(lane_t < 2 * HEAD_DIM, 4, jnp.where(lane_t < 3 * HEAD_DIM, 8, 16)))
    bs = gm_bs_ref[0]

    n_blocks = TS // SUB_ROWS
    per_block = len(scratch) // n_blocks
    blocks = [scratch[i * per_block:(i + 1) * per_block] for i in range(n_blocks)]

    @pl.when(s_idx == 0)
    def _():
        hbuf0, hbuf1, _, _, pbuf, _, _, _, cbuf, _ = blocks[0]
        hbuf0[0:HALO, :] = jnp.zeros((HALO, LANES), F32)
        hbuf1[0:HALO, :] = jnp.zeros((HALO, LANES), F32)
        pbuf[0:HALO, :] = jnp.zeros((HALO, D_GROUP), F32)
        cbuf[0:SC_HALO, :] = jnp.zeros((SC_HALO, D_GROUP), F32)

    def carry_halo(src, dst, which):
        for i in which:
            rows = SC_HALO if i == 8 else HALO
            dst[i][0:rows, :] = src[i][SUB_ROWS:SUB_ROWS + rows, :]

    st = [dict() for _ in range(n_blocks)]

    def load(bi):
        x = x_ref[0, bi * SUB_ROWS:(bi + 1) * SUB_ROWS, :]
        st[bi]["x"] = x
        st[bi]["xb"] = x.astype(BF16)

    def in_proj(bi, lo, hi):
        z = _dot(st[bi]["xb"], w_in_ref[0, :, lo * D_GROUP:hi * D_GROUP])
        for j in range(lo, hi):
            st[bi][j] = z[:, (j - lo) * D_GROUP:(j - lo + 1) * D_GROUP]

    def gating_mlp(bi):
        yn_ref = blocks[bi][9]
        u = st[bi].pop(0)
        v = _layer_norm(st[bi].pop(1), gm_g, gm_b)
        for n in range(SUB_ROWS // CHUNK):
            crows = slice(n * CHUNK, (n + 1) * CHUNK)
            s_all = _dot(w_tril, v[crows, :].astype(BF16))
            s_sel = s_all[3 * CHUNK:4 * CHUNK]
            for h in (2, 1, 0):
                s_sel = jnp.where(lane < (h + 1) * HEAD_DIM, s_all[h * CHUNK:(h + 1) * CHUNK], s_sel)
            y1 = u[crows, :] * (s_sel + bs)
            yn_ref[crows, 0:D_GROUP] = _rms_norm(y1, mixg[:, 0:D_GROUP]).astype(BF16)

    def conformer_glu(bi):
        if bi > 0:
            carry_halo(blocks[bi - 1], blocks[bi], (0, 1))
        glu = st[bi].pop(2) * _sigmoid(st[bi].pop(3))
        for half in range(2):
            blocks[bi][half][HALO:HALO + SUB_ROWS, :] = glu[:, half * LANES:(half + 1) * LANES]

    def conformer_conv(bi, half):
        hb, co = blocks[bi][half], blocks[bi][2 + half]
        lanes = slice(half * LANES, (half + 1) * LANES)
        bias = cf_b[:, lanes]
        for q in range(SUB_ROWS // (2 * CONV_ROWS)):
            for parity in range(2):
                out0 = q * 2 * CONV_ROWS + parity
                base = out0 + HALO - (CONF_K - 1)
                acc = jnp.zeros((CONV_ROWS, LANES), F32) + bias
                for k in range(CONF_K):
                    acc = acc + cf_w_ref[0, k:k + 1, lanes] * hb[pl.ds(base + k, CONV_ROWS, stride=2), :]
                co[pl.ds(out0, CONV_ROWS, stride=2), :] = acc

    def conformer_out(bi):
        cbo0, cbo1, yn_ref = blocks[bi][2], blocks[bi][3], blocks[bi][9]
        hln = _layer_norm(jnp.concatenate([cbo0[...], cbo1[...]], axis=1), cf_g, cf_beta)
        y2 = _dot((hln * _sigmoid(hln)).astype(BF16), cf_pw_ref[0])
        yn_ref[:, D_GROUP:2 * D_GROUP] = _rms_norm(y2, mixg[:, D_GROUP:2 * D_GROUP]).astype(BF16)

    def pooling(bi):
        pbuf, s2buf, s4buf, s8buf = blocks[bi][4:8]
        yn_ref = blocks[bi][9]
        if bi > 0:
            carry_halo(blocks[bi - 1], blocks[bi], (4,))
        zc = st[bi].pop(4)
        pbuf[HALO:HALO + SUB_ROWS, :] = zc
        n2 = HALO + SUB_ROWS - 8
        s2buf[8:8 + n2, :] = pbuf[8:8 + n2, :] + pbuf[7:7 + n2, :]
        n4 = HALO + SUB_ROWS - 16
        s4buf[16:16 + n4, :] = s2buf[16:16 + n4, :] + s2buf[14:14 + n4, :]
        n8 = HALO + SUB_ROWS - 24
        s8buf[24:24 + n8, :] = s4buf[24:24 + n8, :] + s4buf[20:20 + n8, :]
        cur = slice(HALO, HALO + SUB_ROWS)
        s16 = s8buf[cur, :] + s8buf[HALO - 8:HALO - 8 + SUB_ROWS, :]
        pos1 = lax.broadcasted_iota(I32, (SUB_ROWS, D_GROUP), 0) + (s_idx * TS + bi * SUB_ROWS + 1)
        wsum = jnp.where(lane_t < HEAD_DIM, s2buf[cur, :],
                         jnp.where(lane_t < 2 * HEAD_DIM, s4buf[cur, :],
                                   jnp.where(lane_t < 3 * HEAD_DIM, s8buf[cur, :], s16)))
        count = jnp.minimum(pos1, win).astype(F32)
        pooled = wsum / count - zc
        y3 = _dot(pooled.astype(BF16), pool_w_ref[0]) * pool_scale
        yn_ref[:, 2 * D_GROUP:3 * D_GROUP] = _rms_norm(y3, mixg[:, 2 * D_GROUP:3 * D_GROUP]).astype(BF16)

    def short_conv(bi):
        cbuf, yn_ref = blocks[bi][8], blocks[bi][9]
        if bi > 0:
            carry_halo(blocks[bi - 1], blocks[bi], (8,))
        cbuf[SC_HALO:SC_HALO + SUB_ROWS, :] = st[bi].pop(6) * st[bi].pop(7)
        conv = jnp.zeros((SUB_ROWS, D_GROUP), F32)
        for k in range(SHORT_K):
            off = SC_HALO - (SHORT_K - 1) + k
            conv = conv + sc_w_ref[0, k:k + 1, :] * cbuf[off:off + SUB_ROWS, :]
        y4 = st[bi].pop(5) * conv
        yn_ref[:, 3 * D_GROUP:4 * D_GROUP] = _rms_norm(y4, mixg[:, 3 * D_GROUP:4 * D_GROUP]).astype(BF16)

    def out_proj(bi):
        m = _dot(blocks[bi][9][...], w_o_ref[0])
        x1 = _layer_norm(ALPHA * st[bi].pop("x") + m, ln_g, ln_b)
        x1_ref[0, bi * SUB_ROWS:(bi + 1) * SUB_ROWS, :] = x1
        st[bi]["x1"] = x1

    def router(bi):
        x1 = st[bi].pop("x1")
        hi = x1.astype(BF16)
        lo = (x1 - hi.astype(F32)).astype(BF16)
        hi_both = _dot(hi, rw_ref[...])
        logits = hi_both[:, 0:LANES] + hi_both[:, LANES:2 * LANES] + _dot(lo, rw_ref[:, 0:LANES])
        st[bi]["sel"] = logits + rb_ref[...]

    def route(bi):
        sel_t = st[bi].pop("sel").T
        cls = _route_class([sel_t[e:e + 1, :] for e in range(N_EXPERTS)])
        r0 = bi * SUB_ROWS
        cls_ref[0, r0 // LANES:(r0 + SUB_ROWS) // LANES, :] = jnp.concatenate(
            [cls[:, k * LANES:(k + 1) * LANES] for k in range(SUB_ROWS // LANES)], axis=0)

    def conformer(b):
        conformer_glu(b)
        conformer_conv(b, 0)
        conformer_conv(b, 1)
        conformer_out(b)

    def round_expert_weights():
        wgu_out[0, :, 0:D_EXPERT] = wg_ref[0, 0].astype(BF16)
        wgu_out[0, :, D_EXPERT:2 * D_EXPERT] = wu_ref[0, 0].astype(BF16)
        wd_out[0] = wd_ref[0, 0].astype(BF16)

    def stage1(b):
        pieces = [lambda: load(b), lambda: in_proj(b, 0, 2), lambda: in_proj(b, 2, 4), lambda: in_proj(b, 4, 5),
                  lambda: in_proj(b, 5, 8)]
        if b == 0:
            pieces.insert(2, round_expert_weights)
        return pieces

    def stage2(b):
        return [lambda: gating_mlp(b), lambda: conformer(b), lambda: pooling(b), lambda: short_conv(b)]

    def stage3(b):
        return [lambda: out_proj(b), lambda: router(b), lambda: route(b)]

    for t in range(n_blocks + 2):
        stages = [stage(t - lag) for lag, stage in ((2, stage3), (0, stage1), (1, stage2)) if 0 <= t - lag < n_blocks]
        for k in range(max(len(stage) for stage in stages)):
            for stage in stages:
                if k < len(stage):
                    stage[k]()

    carry_halo(blocks[-1], blocks[0], (0, 1, 4, 8))
    fill.wait()


def _plan_kernel(cls_ref, dest_ref, meta_ref):
    n = LANES
    cls = cls_ref[...]
    r = lax.broadcasted_iota(I32, (n, n), 0)
    c = lax.broadcasted_iota(I32, (n, n), 1)
    upper = jnp.where(r < c, 1.0, 0.0).astype(BF16)
    lower = jnp.where(c < r, 1.0, 0.0).astype(BF16)
    ones = jnp.ones((n, n), BF16)
    masks = [cls == k for k in range(N_CLASSES)]
    m_all = jnp.concatenate([jnp.where(mk, 1.0, 0.0) for mk in masks], axis=0).astype(BF16)
    within = _dot(m_all, upper)
    rowsum = _dot(m_all, ones)
    rs = jnp.zeros((n, n), F32)
    for k in range(N_CLASSES):
        rs = jnp.where(c == k, rowsum[k * n:(k + 1) * n], rs)
    rs_b = rs.astype(BF16)
    before = _dot(lower, rs_b)
    total = _dot(ones, rs_b)
    ntile = jnp.floor((total + (TMS - 1)) * (1.0 / TMS))
    tstart = _dot(ntile.astype(BF16), upper)
    base = before + tstart * TMS
    dest = jnp.zeros((n, n), F32)
    for k in range(N_CLASSES):
        dest = jnp.where(masks[k], within[k * n:(k + 1) * n] + base[:, k:k + 1], dest)
    dest_ref[...] = dest.astype(I32)

    tend_t = (tstart + ntile).T
    tile = c.astype(F32)
    ended = jnp.where((r < N_CLASSES) & (tend_t <= tile), 1.0, 0.0)
    tcls = jnp.minimum(jnp.sum(ended, axis=0, keepdims=True), N_CLASSES - 1.0)
    grp = (jnp.where(tcls >= N_PAIRS, 1.0, 0.0) + jnp.where(tcls >= 2 * N_PAIRS, 1.0, 0.0)
           + jnp.where(tcls >= 3 * N_PAIRS, 1.0, 0.0))
    pair = tcls - N_PAIRS * grp
    pa = jnp.where(pair >= 3, 1.0, 0.0) + jnp.where(pair >= 5, 1.0, 0.0)
    pb = jnp.where(pair == 0, 1.0, jnp.where((pair == 1) | (pair == 3), 2.0, 3.0))
    n_tiles = tstart[0:1, N_CLASSES:N_CLASSES + 1]
    tile_row = tile[0:1, :]
    active = jnp.where(tile_row < n_tiles, 1.0, 0.0)
    meta = jnp.concatenate(
        [GROUP_SIZE * grp + pa, GROUP_SIZE * grp + pb, active, jnp.zeros((SUBLANES - 3, n), F32)], axis=0)
    meta_ref[...] = meta.astype(I32)


def _row_copy_wait(src, dst, sem):
    pltpu.make_async_copy(src, dst, sem).wait()


def _dispatch_kernel(dest_ref, x_ref, xs_zero_ref, xs_ref, buf, sems):
    del xs_zero_ref
    i = pl.program_id(0)
    n_steps = pl.num_programs(0)

    for s in range(2):
        @pl.when(i > 0)
        def _():
            _row_copy_wait(buf.at[s], xs_ref.at[pl.ds(0, TD * ROW_TILES), :], sems.at[s])

        for j in range(ROW_TILES):
            buf[s, pl.ds(j, TD, stride=ROW_TILES), :] = x_ref[s * TD:(s + 1) * TD, j * LANES:(j + 1) * LANES]
        for t in range(TD):
            d = pl.multiple_of(dest_ref[(i * 2 + s) * TD + t] * ROW_TILES, ROW_TILES)
            pltpu.make_async_copy(buf.at[s, pl.ds(t * ROW_TILES, ROW_TILES), :],
                                  xs_ref.at[pl.ds(d, ROW_TILES), :], sems.at[s]).start(priority=t % 2)

    @pl.when(i == n_steps - 1)
    def _():
        for s in range(2):
            _row_copy_wait(buf.at[s], xs_ref.at[pl.ds(0, TD * ROW_TILES), :], sems.at[s])


def _combine_kernel(dest_ref, ys_ref, ln_g_ref, ln_b_ref, out_ref, buf, sems):
    i = pl.program_id(0)
    n_steps = pl.num_programs(0)

    def issue(step, s):
        for t in range(TD):
            d = pl.multiple_of(dest_ref[(step * 2 + s) * TD + t] * ROW_TILES, ROW_TILES)
            pltpu.make_async_copy(ys_ref.at[pl.ds(d, ROW_TILES), :],
                                  buf.at[s, pl.ds(t * ROW_TILES, ROW_TILES), :], sems.at[s]).start(priority=t % 2)

    @pl.when(i == 0)
    def _():
        for s in range(2):
            issue(0, s)

    for s in range(2):
        _row_copy_wait(ys_ref.at[pl.ds(0, TD * ROW_TILES), :], buf.at[s], sems.at[s])
        r = jnp.concatenate([buf[s, pl.ds(j, TD, stride=ROW_TILES), :] for j in range(ROW_TILES)], axis=1)
        out_ref[s * TD:(s + 1) * TD, :] = _layer_norm(r, ln_g_ref[0], ln_b_ref[0])

        @pl.when(i + 1 < n_steps)
        def _():
            issue(i + 1, s)


def _moe_kernel(meta_ref, xs_ref, *refs):
    i = pl.program_id(0)
    w_refs, rwt_ref, ys_ref = refs[:-2], refs[-2], refs[-1]
    rows = TMS * ROW_TILES

    @pl.when(meta_ref[2, i * TILES_PER_STEP] == 0)
    def _():
        ys_ref[...] = jnp.zeros(ys_ref.shape, F32)

    @pl.when(meta_ref[2, i * TILES_PER_STEP] == 1)
    def _():
        st = [dict() for _ in range(TILES_PER_STEP)]

        def load(s):
            x = jnp.concatenate(
                [xs_ref[pl.ds(s * rows + j, TMS, stride=ROW_TILES), :] for j in range(ROW_TILES)], axis=1)
            st[s]["x"] = x
            st[s]["xb"] = x.astype(BF16)

        def up(s, e):
            gu = _dot(st[s]["xb"], w_refs[4 * s + 2 * e][0])
            st[s]["g", e] = gu[:, 0:D_EXPERT]
            st[s]["u", e] = gu[:, D_EXPERT:2 * D_EXPERT]

        def act(s, e):
            g = st[s].pop(("g", e))
            st[s]["h", e] = ((g * _sigmoid(g)) * st[s].pop(("u", e))).astype(BF16)

        def down(s, e):
            st[s]["y", e] = _dot(st[s].pop(("h", e)), w_refs[4 * s + 2 * e + 1][0])

        def mix(s):
            tile = i * TILES_PER_STEP + s
            x = st[s].pop("x")
            la = jnp.sum(x * rwt_ref[pl.ds(meta_ref[0, tile], 1), :], axis=-1, keepdims=True)
            lb = jnp.sum(x * rwt_ref[pl.ds(meta_ref[1, tile], 1), :], axis=-1, keepdims=True)
            m = jnp.maximum(la, lb)
            pa = jnp.exp(la - m)
            pb = jnp.exp(lb - m)
            den = pa + pb
            r = ALPHA * x + ((pa / den) * st[s].pop(("y", 0)) + (pb / den) * st[s].pop(("y", 1)))
            for j in range(ROW_TILES):
                ys_ref[pl.ds(s * rows + j, TMS, stride=ROW_TILES), :] = r[:, j * LANES:(j + 1) * LANES]

        units = [(s, e) for s in range(TILES_PER_STEP) for e in range(2)]
        load(0)
        up(*units[0])
        for k, (s, e) in enumerate(units):
            if k + 1 < len(units):
                nxt = units[k + 1]
                if nxt[1] == 0:
                    load(nxt[0])
                up(*nxt)
            act(s, e)
            down(s, e)
            if e == 1:
                mix(s)


def _full(shape):
    return pl.BlockSpec(shape, lambda *_: (0,) * len(shape))


def _layer_block(shape, layer):
    return pl.BlockSpec((1,) + shape, lambda *_: (layer,) + (0,) * len(shape))


def _mixer_call(x, p, expert_w, n_sorted, layer):
    batch, seq, _ = x.shape
    steps = seq // TS
    fill_rows = n_sorted * ROW_TILES // (batch * steps)
    assert fill_rows * batch * steps == n_sorted * ROW_TILES and fill_rows % SUBLANES == 0
    slabs = batch * steps // N_EXPERTS
    assert batch * steps == slabs * N_EXPERTS

    def w_in(b, s):
        i = b * steps + s
        return (layer, i // slabs, i % slabs, 0)

    def w_out(b, s):
        i = b * steps + s
        return (i // slabs, i % slabs, 0)

    up_rows, down_rows = D_MODEL // slabs, D_EXPERT // slabs
    in_specs = [
        pl.BlockSpec((1, TS, D_MODEL), lambda b, s: (b, s, 0)),
        _layer_block((D_MODEL, 8 * D_GROUP), layer),
        *[_layer_block((1, D_GROUP), layer)] * 6, *[_layer_block((1, D_MODEL), layer)] * 3,
        _layer_block((N_HEADS * CHUNK, CHUNK), layer), _layer_block((CHUNK, D_GROUP), layer),
        _layer_block((HALO, D_GROUP), layer),
        _layer_block((D_GROUP, D_GROUP), layer),
        _layer_block((D_GROUP, D_GROUP), layer),
        _layer_block((SUBLANES, D_GROUP), layer),
        _layer_block((D_MODEL, D_MODEL), layer),
        _full((D_MODEL, 2 * LANES)), _full((1, LANES)),
        pl.BlockSpec((1, 1, up_rows, D_EXPERT), w_in),
        pl.BlockSpec((1, 1, up_rows, D_EXPERT), w_in),
        pl.BlockSpec((1, 1, down_rows, D_MODEL), w_in),
    ]
    out_specs = [
        pl.BlockSpec((1, TS, D_MODEL), lambda b, s: (b, s, 0)),
        pl.BlockSpec((1, TS // LANES, LANES), lambda b, s: (b * steps + s, 0, 0)),
        pl.BlockSpec((1, up_rows, 2 * D_EXPERT), w_out),
        pl.BlockSpec((1, down_rows, D_MODEL), w_out),
        pl.BlockSpec(memory_space=pl.ANY),
    ]
    block_scratch = [
        pltpu.VMEM((HALO + SUB_ROWS, LANES), F32),
        pltpu.VMEM((HALO + SUB_ROWS, LANES), F32),
        pltpu.VMEM((SUB_ROWS, LANES), F32),
        pltpu.VMEM((SUB_ROWS, LANES), F32),
        pltpu.VMEM((HALO + SUB_ROWS, D_GROUP), F32),
        pltpu.VMEM((HALO + SUB_ROWS, D_GROUP), F32),
        pltpu.VMEM((HALO + SUB_ROWS, D_GROUP), F32),
        pltpu.VMEM((HALO + SUB_ROWS, D_GROUP), F32),
        pltpu.VMEM((SC_HALO + SUB_ROWS, D_GROUP), F32),
        pltpu.VMEM((SUB_ROWS, D_MODEL), BF16),
    ]
    scratch = block_scratch * (TS // SUB_ROWS) + [pltpu.VMEM((fill_rows, LANES), F32), pltpu.SemaphoreType.DMA(())]
    return pl.pallas_call(
        _mixer_kernel,
        grid=(batch, steps),
        in_specs=in_specs,
        out_specs=out_specs,
        out_shape=[jax.ShapeDtypeStruct((batch, seq, D_MODEL), F32),
                   jax.ShapeDtypeStruct((batch * steps, TS // LANES, LANES), I32),
                   jax.ShapeDtypeStruct((N_EXPERTS, D_MODEL, 2 * D_EXPERT), BF16),
                   jax.ShapeDtypeStruct((N_EXPERTS, D_EXPERT, D_MODEL), BF16),
                   jax.ShapeDtypeStruct((n_sorted * ROW_TILES, LANES), F32)],
        scratch_shapes=scratch,
        compiler_params=pltpu.CompilerParams(
            dimension_semantics=("arbitrary", "arbitrary"), vmem_limit_bytes=VMEM_LIMIT),
        name="mixer",
    )(x, *p, *expert_w)


def _plan_call(cls2d):
    return pl.pallas_call(
        _plan_kernel,
        out_shape=[jax.ShapeDtypeStruct((LANES, LANES), I32), jax.ShapeDtypeStruct((SUBLANES, LANES), I32)],
        compiler_params=pltpu.CompilerParams(vmem_limit_bytes=VMEM_LIMIT),
        name="moe_plan",
    )(cls2d)


def _dispatch_call(dest, x2d, xs_zero):
    n_tok = x2d.shape[0]
    return pl.pallas_call(
        _dispatch_kernel,
        grid_spec=pltpu.PrefetchScalarGridSpec(
            num_scalar_prefetch=1,
            grid=(n_tok // (2 * TD),),
            in_specs=[pl.BlockSpec((2 * TD, D_MODEL), lambda i, *_: (i, 0)),
                      pl.BlockSpec(memory_space=pl.ANY)],
            out_specs=pl.BlockSpec(memory_space=pl.ANY),
            scratch_shapes=[
                pltpu.VMEM((2, TD * ROW_TILES, LANES), F32),
                pltpu.SemaphoreType.DMA((2,)),
            ]),
        out_shape=jax.ShapeDtypeStruct(xs_zero.shape, F32),
        input_output_aliases={2: 0},
        compiler_params=pltpu.CompilerParams(
            dimension_semantics=("arbitrary",), vmem_limit_bytes=VMEM_LIMIT),
        name="moe_dispatch",
    )(dest, x2d, xs_zero)


def _combine_call(dest, ys, ln_g, ln_b, layer):
    n_tok = dest.shape[0]
    return pl.pallas_call(
        _combine_kernel,
        grid_spec=pltpu.PrefetchScalarGridSpec(
            num_scalar_prefetch=1,
            grid=(n_tok // (2 * TD),),
            in_specs=[pl.BlockSpec(memory_space=pl.ANY)] + [_layer_block((1, D_MODEL), layer)] * 2,
            out_specs=pl.BlockSpec((2 * TD, D_MODEL), lambda i, *_: (i, 0)),
            scratch_shapes=[
                pltpu.VMEM((2, TD * ROW_TILES, LANES), F32),
                pltpu.SemaphoreType.DMA((2,)),
            ]),
        out_shape=jax.ShapeDtypeStruct((n_tok, D_MODEL), F32),
        compiler_params=pltpu.CompilerParams(
            dimension_semantics=("arbitrary",), vmem_limit_bytes=VMEM_LIMIT),
        name="moe_combine",
    )(dest, ys, ln_g, ln_b)


def _moe_call(meta, xs, wgu, wd, rwt, n_tiles):
    rows = TILES_PER_STEP * TMS * ROW_TILES
    w_specs, w_args = [], []
    for s in range(TILES_PER_STEP):
        for sel in (0, 1):
            def expert(i, meta, s=s, sel=sel):
                return (meta[sel, i * TILES_PER_STEP + s], 0, 0)
            w_specs += [pl.BlockSpec((1, D_MODEL, 2 * D_EXPERT), expert),
                        pl.BlockSpec((1, D_EXPERT, D_MODEL), expert)]
            w_args += [wgu, wd]
    tile = pl.BlockSpec((rows, LANES), lambda i, meta: (i, 0))
    return pl.pallas_call(
        _moe_kernel,
        grid_spec=pltpu.PrefetchScalarGridSpec(
            num_scalar_prefetch=1,
            grid=(n_tiles // TILES_PER_STEP,),
            in_specs=[tile] + w_specs + [_full((N_EXPERTS, D_MODEL))],
            out_specs=tile),
        out_shape=jax.ShapeDtypeStruct(xs.shape, F32),
        compiler_params=pltpu.CompilerParams(
            dimension_semantics=("arbitrary",), vmem_limit_bytes=VMEM_LIMIT),
        name="moe_experts",
    )(meta, xs, *w_args, rwt)


def _block_diag(w):
    g, d, _ = w.shape
    eye = jnp.eye(g, dtype=w.dtype)
    return (eye[:, None, :, None] * w[:, :, None, :]).reshape(g * d, g * d)


def kernel(x, w_in, gm_ln_g, gm_ln_b, gm_w_s, gm_b_s, cf_dw_w, cf_dw_b, cf_ln_g, cf_ln_b, cf_pw,
           pool_w, pool_scale, sc_w, mix_norm_g, w_o, ln1_g, ln1_b, router_w, router_b,
           exp_w_gate, exp_w_up, exp_w_down, ln2_g, ln2_b):
    batch, seq, _ = x.shape
    n_tok = batch * seq
    assert n_tok == LANES * LANES, "the routing plan lays tokens out as one (128, 128) tile grid"
    assert seq % TS == 0 and TS % SUB_ROWS == 0 and SUB_ROWS % (2 * CONV_ROWS) == 0 and n_tok % (2 * TD) == 0
    n_tiles = n_tok // TMS + N_CLASSES
    assert n_tiles <= LANES and n_tiles % TILES_PER_STEP == 0
    def rows(*vectors):
        return [v.reshape(DEPTH, 1, -1) for v in vectors]

    rw = jnp.pad(router_w, ((0, 0), (0, LANES - N_EXPERTS)))
    rw_hi = rw.astype(BF16)
    rw_lo = (rw - rw_hi.astype(F32)).astype(BF16)
    rb = jnp.pad(router_b, (0, LANES - N_EXPERTS)).reshape(1, LANES)
    rwt = router_w.T
    ln2 = rows(ln2_g, ln2_b)
    params = (
        w_in.astype(BF16),
        *rows(gm_ln_g, gm_ln_b, cf_dw_b, cf_ln_g, cf_ln_b, pool_scale), *rows(mix_norm_g, ln1_g, ln1_b),
        gm_w_s.reshape(DEPTH, N_HEADS * CHUNK, CHUNK),
        jnp.repeat(gm_b_s.transpose(0, 2, 1), HEAD_DIM, axis=2),
        jnp.pad(cf_dw_w, ((0, 0), (0, HALO - CONF_K), (0, 0))),
        cf_pw.astype(BF16),
        jax.vmap(_block_diag)(pool_w).astype(BF16),
        jnp.pad(sc_w, ((0, 0), (0, SUBLANES - SHORT_K), (0, 0))),
        w_o.astype(BF16),
        jnp.concatenate([rw_hi, rw_lo], axis=1), rb,
    )
    for l in range(DEPTH):
        x1, cls, wgu_b, wd_b, xs_zero = _mixer_call(
            x, params, (exp_w_gate, exp_w_up, exp_w_down), n_tiles * TMS, l)
        x1 = x1.reshape(n_tok, D_MODEL)
        dest2d, meta = _plan_call(cls.reshape(LANES, LANES))
        dest = dest2d.reshape(n_tok)
        xs = _dispatch_call(dest, x1, xs_zero)
        ys = _moe_call(meta, xs, wgu_b, wd_b, rwt, n_tiles)
        x = _combine_call(dest, ys, *ln2, l).reshape(batch, seq, D_MODEL)
    return x
```

```python
import jax
import jax.numpy as jnp
from jax import lax
from jax.experimental import pallas as pl
from jax.experimental.pallas import tpu as pltpu

D_MODEL = 1024
DEPTH = 2
D_GROUP = 256
N_HEADS = 4
HEAD_DIM = 64
CHUNK = 128
CONF_K = 31
SHORT_K = 3
N_EXPERTS = 16
N_GROUPS = 4
GROUP_SIZE = 4
N_PAIRS = 6
N_CLASSES = N_GROUPS * N_PAIRS
D_EXPERT = 512
ALPHA = (2 * DEPTH) ** 0.25
LN_EPS = 1e-5
RMS_EPS = 1e-6

LANES = 128
SUBLANES = 8
ROW_TILES = D_MODEL // LANES
HALO = 32
SC_HALO = SUBLANES
TS = 512
SUB_ROWS = 256
CONV_ROWS = 64
TMS = 256
TILES_PER_STEP = 2
DEST_BATCH = 16
TD = 512
V7X_VMEM_BYTES = 64 * 1024 * 1024
VMEM_LIMIT = V7X_VMEM_BYTES * 7 // 8

BF16 = jnp.bfloat16
F32 = jnp.float32
I32 = jnp.int32


def _dot(a, b):
    return jnp.dot(a, b, preferred_element_type=F32)


def _layer_norm(x, g, b):
    mu = jnp.mean(x, axis=-1, keepdims=True)
    xc = x - mu
    var = jnp.mean(xc * xc, axis=-1, keepdims=True)
    return xc * lax.rsqrt(var + LN_EPS) * g + b


def _rms_norm(y, g):
    ms = jnp.mean(y * y, axis=-1, keepdims=True)
    return y * lax.rsqrt(ms + RMS_EPS) * g


def _sigmoid(x):
    return 1.0 / (1.0 + jnp.exp(-x))


def _route_class(sel):
    scores = []
    for g in range(N_GROUPS):
        v = sel[g * GROUP_SIZE:(g + 1) * GROUP_SIZE]
        best_pair = None
        for i in range(GROUP_SIZE):
            for j in range(i + 1, GROUP_SIZE):
                p = v[i] + v[j]
                best_pair = p if best_pair is None else jnp.maximum(best_pair, p)
        scores.append(best_pair)
    best = jnp.zeros(scores[0].shape, I32)
    best_score = scores[0]
    for g in range(1, N_GROUPS):
        better = scores[g] > best_score
        best = jnp.where(better, g, best)
        best_score = jnp.where(better, scores[g], best_score)
    v = []
    for j in range(GROUP_SIZE):
        out = sel[j]
        for g in range(1, N_GROUPS):
            out = jnp.where(best == g, sel[g * GROUP_SIZE + j], out)
        v.append(out)
    i0 = jnp.zeros_like(best)
    v0 = v[0]
    for j in range(1, GROUP_SIZE):
        better = v[j] > v0
        i0 = jnp.where(better, j, i0)
        v0 = jnp.where(better, v[j], v0)
    neg = jnp.full_like(v0, -jnp.inf)
    w = [jnp.where(i0 == j, neg, v[j]) for j in range(GROUP_SIZE)]
    i1 = jnp.zeros_like(best)
    v1 = w[0]
    for j in range(1, GROUP_SIZE):
        better = w[j] > v1
        i1 = jnp.where(better, j, i1)
        v1 = jnp.where(better, w[j], v1)
    a = jnp.minimum(i0, i1)
    b = jnp.maximum(i0, i1)
    pair = jnp.where(a == 0, b - 1, jnp.where(a == 1, b + 1, N_PAIRS - 1))
    return best * N_PAIRS + pair


def _mixer_kernel(x_ref, w_in_ref, vec_g_ref, vec_m_ref, gm_w_ref, gm_bs_ref, cf_w_ref, cf_pw_ref,
                  pool_w_ref, sc_w_ref, w_o_ref, rw_ref, rb_ref, wg_ref, wu_ref, wd_ref,
                  x1_ref, cls_ref, wgu_out, wd_out, xs_ref, *scratch):
    s_idx = pl.program_id(1)
    *scratch, zbuf, zsem = scratch
    step = pl.program_id(0) * pl.num_programs(1) + s_idx
    fill_rows = zbuf.shape[0]

    @pl.when(step == 0)
    def _():
        zbuf[...] = jnp.zeros(zbuf.shape, F32)

    fill = pltpu.make_async_copy(
        zbuf, xs_ref.at[pl.ds(pl.multiple_of(step * fill_rows, fill_rows), fill_rows), :], zsem)
    fill.start()
    gm_g, gm_b, cf_b, cf_g, cf_beta, pool_scale = (vec_g_ref[0, k:k + 1, :] for k in range(6))
    mixg, ln_g, ln_b = (vec_m_ref[0, k:k + 1, :] for k in range(3))

    row = lax.broadcasted_iota(I32, (N_HEADS * CHUNK, CHUNK), 0)
    col = lax.broadcasted_iota(I32, (N_HEADS * CHUNK, CHUNK), 1)
    w_tril = jnp.where(col <= (row & (CHUNK - 1)), gm_w_ref[0], 0.0).astype(BF16)
    lane = lax.broadcasted_iota(I32, (CHUNK, D_GROUP), 1)
    lane_t = lax.broadcasted_iota(I32, (SUB_ROWS, D_GROUP), 1)
    win = jnp.where(lane_t < HEAD_DIM, 2,
                    jnp.where(lane_t < 2 * HEAD_DIM, 4, jnp.where(lane_t < 3 * HEAD_DIM, 8, 16)))
    bs = gm_bs_ref[0]

    n_blocks = TS // SUB_ROWS
    per_block = len(scratch) // n_blocks
    blocks = [scratch[i * per_block:(i + 1) * per_block] for i in range(n_blocks)]

    @pl.when(s_idx == 0)
    def _():
        hbuf0, hbuf1, _, _, pbuf, _, _, _, cbuf, _ = blocks[0]
        hbuf0[0:HALO, :] = jnp.zeros((HALO, LANES), F32)
        hbuf1[0:HALO, :] = jnp.zeros((HALO, LANES), F32)
        pbuf[0:HALO, :] = jnp.zeros((HALO, D_GROUP), F32)
        cbuf[0:SC_HALO, :] = jnp.zeros((SC_HALO, D_GROUP), F32)

    def carry_halo(src, dst, which):
        for i in which:
            rows = SC_HALO if i == 8 else HALO
            dst[i][0:rows, :] = src[i][SUB_ROWS:SUB_ROWS + rows, :]

    st = [dict() for _ in range(n_blocks)]

    def load(bi):
        x = x_ref[0, bi * SUB_ROWS:(bi + 1) * SUB_ROWS, :]
        st[bi]["x"] = x
        st[bi]["xb"] = x.astype(BF16)

    def in_proj(bi, lo, hi):
        z = _dot(st[bi]["xb"], w_in_ref[0, :, lo * D_GROUP:hi * D_GROUP])
        for j in range(lo, hi):
            st[bi][j] = z[:, (j - lo) * D_GROUP:(j - lo + 1) * D_GROUP]

    def gating_mlp(bi):
        yn_ref = blocks[bi][9]
        u = st[bi].pop(0)
        v = _layer_norm(st[bi].pop(1), gm_g, gm_b)
        for n in range(SUB_ROWS // CHUNK):
            crows = slice(n * CHUNK, (n + 1) * CHUNK)
            s_all = _dot(w_tril, v[crows, :].astype(BF16))
            s_sel = s_all[3 * CHUNK:4 * CHUNK]
            for h in (2, 1, 0):
                s_sel = jnp.where(lane < (h + 1) * HEAD_DIM, s_all[h * CHUNK:(h + 1) * CHUNK], s_sel)
            y1 = u[crows, :] * (s_sel + bs)
            yn_ref[crows, 0:D_GROUP] = _rms_norm(y1, mixg[:, 0:D_GROUP]).astype(BF16)

    def conformer_glu(bi):
        if bi > 0:
            carry_halo(blocks[bi - 1], blocks[bi], (0, 1))
        glu = st[bi].pop(2) * _sigmoid(st[bi].pop(3))
        for half in range(2):
            blocks[bi][half][HALO:HALO + SUB_ROWS, :] = glu[:, half * LANES:(half + 1) * LANES]

    def conformer_conv(bi, half):
        hb, co = blocks[bi][half], blocks[bi][2 + half]
        lanes = slice(half * LANES, (half + 1) * LANES)
        bias = cf_b[:, lanes]
        for q in range(SUB_ROWS // (2 * CONV_ROWS)):
            for parity in range(2):
                out0 = q * 2 * CONV_ROWS + parity
                base = out0 + HALO - (CONF_K - 1)
                acc = jnp.zeros((CONV_ROWS, LANES), F32) + bias
                for k in range(CONF_K):
                    acc = acc + cf_w_ref[0, k:k + 1, lanes] * hb[pl.ds(base + k, CONV_ROWS, stride=2), :]
                co[pl.ds(out0, CONV_ROWS, stride=2), :] = acc

    def conformer_out(bi):
        cbo0, cbo1, yn_ref = blocks[bi][2], blocks[bi][3], blocks[bi][9]
        hln = _layer_norm(jnp.concatenate([cbo0[...], cbo1[...]], axis=1), cf_g, cf_beta)
        y2 = _dot((hln * _sigmoid(hln)).astype(BF16), cf_pw_ref[0])
        yn_ref[:, D_GROUP:2 * D_GROUP] = _rms_norm(y2, mixg[:, D_GROUP:2 * D_GROUP]).astype(BF16)

    def pooling(bi):
        pbuf, s2buf, s4buf, s8buf = blocks[bi][4:8]
        yn_ref = blocks[bi][9]
        if bi > 0:
            carry_halo(blocks[bi - 1], blocks[bi], (4,))
        zc = st[bi].pop(4)
        pbuf[HALO:HALO + SUB_ROWS, :] = zc
        n2 = HALO + SUB_ROWS - 8
        s2buf[8:8 + n2, :] = pbuf[8:8 + n2, :] + pbuf[7:7 + n2, :]
        n4 = HALO + SUB_ROWS - 16
        s4buf[16:16 + n4, :] = s2buf[16:16 + n4, :] + s2buf[14:14 + n4, :]
        n8 = HALO + SUB_ROWS - 24
        s8buf[24:24 + n8, :] = s4buf[24:24 + n8, :] + s4buf[20:20 + n8, :]
        cur = slice(HALO, HALO + SUB_ROWS)
        s16 = s8buf[cur, :] + s8buf[HALO - 8:HALO - 8 + SUB_ROWS, :]
        pos1 = lax.broadcasted_iota(I32, (SUB_ROWS, D_GROUP), 0) + (s_idx * TS + bi * SUB_ROWS + 1)
        wsum = jnp.where(lane_t < HEAD_DIM, s2buf[cur, :],
                         jnp.where(lane_t < 2 * HEAD_DIM, s4buf[cur, :],
                                   jnp.where(lane_t < 3 * HEAD_DIM, s8buf[cur, :], s16)))
        count = jnp.minimum(pos1, win).astype(F32)
        pooled = wsum / count - zc
        y3 = _dot(pooled.astype(BF16), pool_w_ref[0]) * pool_scale
        yn_ref[:, 2 * D_GROUP:3 * D_GROUP] = _rms_norm(y3, mixg[:, 2 * D_GROUP:3 * D_GROUP]).astype(BF16)

    def short_conv(bi):
        cbuf, yn_ref = blocks[bi][8], blocks[bi][9]
        if bi > 0:
            carry_halo(blocks[bi - 1], blocks[bi], (8,))
        cbuf[SC_HALO:SC_HALO + SUB_ROWS, :] = st[bi].pop(6) * st[bi].pop(7)
        conv = jnp.zeros((SUB_ROWS, D_GROUP), F32)
        for k in range(SHORT_K):
            off = SC_HALO - (SHORT_K - 1) + k
            conv = conv + sc_w_ref[0, k:k + 1, :] * cbuf[off:off + SUB_ROWS, :]
        y4 = st[bi].pop(5) * conv
        yn_ref[:, 3 * D_GROUP:4 * D_GROUP] = _rms_norm(y4, mixg[:, 3 * D_GROUP:4 * D_GROUP]).astype(BF16)

    def out_proj(bi):
        m = _dot(blocks[bi][9][...], w_o_ref[0])
        x1 = _layer_norm(ALPHA * st[bi].pop("x") + m, ln_g, ln_b)
        for j in range(ROW_TILES):
            x1_ref[pl.ds(bi * SUB_ROWS * ROW_TILES + j, SUB_ROWS, stride=ROW_TILES), :] = x1[:, j * LANES:(j + 1) * LANES]
        st[bi]["x1"] = x1

    def router(bi):
        x1 = st[bi].pop("x1")
        hi = x1.astype(BF16)
        lo = (x1 - hi.astype(F32)).astype(BF16)
        hi_both = _dot(hi, rw_ref[...])
        logits = hi_both[:, 0:LANES] + hi_both[:, LANES:2 * LANES] + _dot(lo, rw_ref[:, 0:LANES])
        st[bi]["sel"] = logits + rb_ref[...]

    def route(bi):
        sel_t = st[bi].pop("sel").T
        cls = _route_class([sel_t[e:e + 1, :] for e in range(N_EXPERTS)])
        r0 = bi * SUB_ROWS
        cls_ref[0, r0 // LANES:(r0 + SUB_ROWS) // LANES, :] = jnp.concatenate(
            [cls[:, k * LANES:(k + 1) * LANES] for k in range(SUB_ROWS // LANES)], axis=0)

    def conformer(b):
        conformer_glu(b)
        conformer_conv(b, 0)
        conformer_conv(b, 1)
        conformer_out(b)

    def round_expert_weights():
        wgu_out[0, :, 0:D_EXPERT] = wg_ref[0, 0].astype(BF16)
        wgu_out[0, :, D_EXPERT:2 * D_EXPERT] = wu_ref[0, 0].astype(BF16)
        wd_out[0] = wd_ref[0, 0].astype(BF16)

    def stage1(b):
        pieces = [lambda: load(b), lambda: in_proj(b, 0, 2), lambda: in_proj(b, 2, 4), lambda: in_proj(b, 4, 5),
                  lambda: in_proj(b, 5, 8)]
        if b == 0:
            pieces.insert(2, round_expert_weights)
        return pieces

    def stage2(b):
        return [lambda: gating_mlp(b), lambda: conformer(b), lambda: pooling(b), lambda: short_conv(b)]

    def stage3(b):
        return [lambda: out_proj(b), lambda: router(b), lambda: route(b)]

    for t in range(n_blocks + 2):
        stages = [stage(t - lag) for lag, stage in ((2, stage3), (0, stage1), (1, stage2)) if 0 <= t - lag < n_blocks]
        for k in range(max(len(stage) for stage in stages)):
            for stage in stages:
                if k < len(stage):
                    stage[k]()

    carry_halo(blocks[-1], blocks[0], (0, 1, 4, 8))
    fill.wait()


def _plan_kernel(cls_ref, dest_ref, meta_ref):
    n = LANES
    cls = cls_ref[...]
    r = lax.broadcasted_iota(I32, (n, n), 0)
    c = lax.broadcasted_iota(I32, (n, n), 1)
    upper = jnp.where(r < c, 1.0, 0.0).astype(BF16)
    lower = jnp.where(c < r, 1.0, 0.0).astype(BF16)
    ones = jnp.ones((n, n), BF16)
    masks = [cls == k for k in range(N_CLASSES)]
    m_all = jnp.concatenate([jnp.where(mk, 1.0, 0.0) for mk in masks], axis=0).astype(BF16)
    within = _dot(m_all, upper)
    rowsum = _dot(m_all, ones)
    rs = jnp.zeros((n, n), F32)
    for k in range(N_CLASSES):
        rs = jnp.where(c == k, rowsum[k * n:(k + 1) * n], rs)
    rs_b = rs.astype(BF16)
    before = _dot(lower, rs_b)
    total = _dot(ones, rs_b)
    ntile = jnp.floor((total + (TMS - 1)) * (1.0 / TMS))
    tstart = _dot(ntile.astype(BF16), upper)
    base = before + tstart * TMS
    dest = jnp.zeros((n, n), F32)
    for k in range(N_CLASSES):
        dest = jnp.where(masks[k], within[k * n:(k + 1) * n] + base[:, k:k + 1], dest)
    dest_ref[...] = dest.astype(I32)

    tend_t = (tstart + ntile).T
    tile = c.astype(F32)
    ended = jnp.where((r < N_CLASSES) & (tend_t <= tile), 1.0, 0.0)
    tcls = jnp.minimum(jnp.sum(ended, axis=0, keepdims=True), N_CLASSES - 1.0)
    grp = (jnp.where(tcls >= N_PAIRS, 1.0, 0.0) + jnp.where(tcls >= 2 * N_PAIRS, 1.0, 0.0)
           + jnp.where(tcls >= 3 * N_PAIRS, 1.0, 0.0))
    pair = tcls - N_PAIRS * grp
    pa = jnp.where(pair >= 3, 1.0, 0.0) + jnp.where(pair >= 5, 1.0, 0.0)
    pb = jnp.where(pair == 0, 1.0, jnp.where((pair == 1) | (pair == 3), 2.0, 3.0))
    n_tiles = tstart[0:1, N_CLASSES:N_CLASSES + 1]
    tile_row = tile[0:1, :]
    active = jnp.where(tile_row < n_tiles, 1.0, 0.0)
    meta = jnp.concatenate(
        [GROUP_SIZE * grp + pa, GROUP_SIZE * grp + pb, active, jnp.zeros((SUBLANES - 3, n), F32)], axis=0)
    meta_ref[...] = meta.astype(I32)


def _row_copy_wait(src, dst, sem):
    pltpu.make_async_copy(src, dst, sem).wait()


def _dispatch_kernel(dest_ref, x_ref, xs_zero_ref, xs_ref, sem):
    del xs_zero_ref
    i = pl.program_id(0)
    n_steps = pl.num_programs(0)

    first = i * (2 * TD)
    for t0 in range(0, 2 * TD, DEST_BATCH):
        dests = [dest_ref[first + t0 + k] for k in range(DEST_BATCH)]
        for k, dest in enumerate(dests):
            src_row = pl.multiple_of((first + t0 + k) * ROW_TILES, ROW_TILES)
            dst_row = pl.multiple_of(dest * ROW_TILES, ROW_TILES)
            pltpu.make_async_copy(x_ref.at[pl.ds(src_row, ROW_TILES), :],
                                  xs_ref.at[pl.ds(dst_row, ROW_TILES), :], sem).start(priority=k % 2)

    def wait_one_step():
        span = pl.ds(0, 2 * TD * ROW_TILES)
        _row_copy_wait(x_ref.at[span, :], xs_ref.at[span, :], sem)

    pl.when(i > 0)(wait_one_step)
    pl.when(i == n_steps - 1)(wait_one_step)


def _combine_kernel(dest_ref, ys_ref, vec_m_ref, out_ref, buf, sems):
    i = pl.program_id(0)
    n_steps = pl.num_programs(0)

    def issue(step, s):
        for t in range(TD):
            d = pl.multiple_of(dest_ref[(step * 2 + s) * TD + t] * ROW_TILES, ROW_TILES)
            pltpu.make_async_copy(ys_ref.at[pl.ds(d, ROW_TILES), :],
                                  buf.at[s, pl.ds(t * ROW_TILES, ROW_TILES), :], sems.at[s]).start(priority=t % 2)

    @pl.when(i == 0)
    def _():
        for s in range(2):
            issue(0, s)

    for s in range(2):
        _row_copy_wait(ys_ref.at[pl.ds(0, TD * ROW_TILES), :], buf.at[s], sems.at[s])
        r = jnp.concatenate([buf[s, pl.ds(j, TD, stride=ROW_TILES), :] for j in range(ROW_TILES)], axis=1)
        out_ref[s * TD:(s + 1) * TD, :] = _layer_norm(r, vec_m_ref[0, 3:4, :], vec_m_ref[0, 4:5, :])

        @pl.when(i + 1 < n_steps)
        def _():
            issue(i + 1, s)


def _moe_kernel(meta_ref, xs_ref, *refs):
    i = pl.program_id(0)
    w_refs, rwt_ref, ys_ref = refs[:-2], refs[-2], refs[-1]
    rows = TMS * ROW_TILES

    @pl.when(meta_ref[2, i * TILES_PER_STEP] == 0)
    def _():
        ys_ref[...] = jnp.zeros(ys_ref.shape, F32)

    @pl.when(meta_ref[2, i * TILES_PER_STEP] == 1)
    def _():
        st = [dict() for _ in range(TILES_PER_STEP)]

        def load(s):
            x = jnp.concatenate(
                [xs_ref[pl.ds(s * rows + j, TMS, stride=ROW_TILES), :] for j in range(ROW_TILES)], axis=1)
            st[s]["x"] = x
            st[s]["xb"] = x.astype(BF16)

        def up(s, e):
            gu = _dot(st[s]["xb"], w_refs[4 * s + 2 * e][0])
            st[s]["g", e] = gu[:, 0:D_EXPERT]
            st[s]["u", e] = gu[:, D_EXPERT:2 * D_EXPERT]

        def act(s, e):
            g = st[s].pop(("g", e))
            st[s]["h", e] = ((g * _sigmoid(g)) * st[s].pop(("u", e))).astype(BF16)

        def down(s, e):
            st[s]["y", e] = _dot(st[s].pop(("h", e)), w_refs[4 * s + 2 * e + 1][0])

        def mix(s):
            tile = i * TILES_PER_STEP + s
            x = st[s].pop("x")
            la = jnp.sum(x * rwt_ref[pl.ds(meta_ref[0, tile], 1), :], axis=-1, keepdims=True)
            lb = jnp.sum(x * rwt_ref[pl.ds(meta_ref[1, tile], 1), :], axis=-1, keepdims=True)
            m = jnp.maximum(la, lb)
            pa = jnp.exp(la - m)
            pb = jnp.exp(lb - m)
            den = pa + pb
            r = ALPHA * x + ((pa / den) * st[s].pop(("y", 0)) + (pb / den) * st[s].pop(("y", 1)))
            for j in range(ROW_TILES):
                ys_ref[pl.ds(s * rows + j, TMS, stride=ROW_TILES), :] = r[:, j * LANES:(j + 1) * LANES]

        units = [(s, e) for s in range(TILES_PER_STEP) for e in range(2)]
        load(0)
        up(*units[0])
        for k, (s, e) in enumerate(units):
            if k + 1 < len(units):
                nxt = units[k + 1]
                if nxt[1] == 0:
                    load(nxt[0])
                up(*nxt)
            act(s, e)
            down(s, e)
            if e == 1:
                mix(s)


def _full(shape):
    return pl.BlockSpec(shape, lambda *_: (0,) * len(shape))


def _layer_block(shape, layer):
    return pl.BlockSpec((1,) + shape, lambda *_: (layer,) + (0,) * len(shape))


def _mixer_call(x, p, expert_w, n_sorted, layer):
    batch, seq, _ = x.shape
    steps = seq // TS
    fill_rows = n_sorted * ROW_TILES // (batch * steps)
    assert fill_rows * batch * steps == n_sorted * ROW_TILES and fill_rows % SUBLANES == 0
    slabs = batch * steps // N_EXPERTS
    assert batch * steps == slabs * N_EXPERTS

    def w_in(b, s):
        i = b * steps + s
        return (layer, i // slabs, i % slabs, 0)

    def w_out(b, s):
        i = b * steps + s
        return (i // slabs, i % slabs, 0)

    up_rows, down_rows = D_MODEL // slabs, D_EXPERT // slabs
    in_specs = [
        pl.BlockSpec((1, TS, D_MODEL), lambda b, s: (b, s, 0)),
        _layer_block((D_MODEL, 8 * D_GROUP), layer),
        _layer_block((SUBLANES, D_GROUP), layer), _layer_block((SUBLANES, D_MODEL), layer),
        _layer_block((N_HEADS * CHUNK, CHUNK), layer), _layer_block((CHUNK, D_GROUP), layer),
        _layer_block((HALO, D_GROUP), layer),
        _layer_block((D_GROUP, D_GROUP), layer),
        _layer_block((D_GROUP, D_GROUP), layer),
        _layer_block((SUBLANES, D_GROUP), layer),
        _layer_block((D_MODEL, D_MODEL), layer),
        _full((D_MODEL, 2 * LANES)), _full((1, LANES)),
        pl.BlockSpec((1, 1, up_rows, D_EXPERT), w_in),
        pl.BlockSpec((1, 1, up_rows, D_EXPERT), w_in),
        pl.BlockSpec((1, 1, down_rows, D_MODEL), w_in),
    ]
    out_specs = [
        pl.BlockSpec((TS * ROW_TILES, LANES), lambda b, s: (b * steps + s, 0)),
        pl.BlockSpec((1, TS // LANES, LANES), lambda b, s: (b * steps + s, 0, 0)),
        pl.BlockSpec((1, up_rows, 2 * D_EXPERT), w_out),
        pl.BlockSpec((1, down_rows, D_MODEL), w_out),
        pl.BlockSpec(memory_space=pl.ANY),
    ]
    block_scratch = [
        pltpu.VMEM((HALO + SUB_ROWS, LANES), F32),
        pltpu.VMEM((HALO + SUB_ROWS, LANES), F32),
        pltpu.VMEM((SUB_ROWS, LANES), F32),
        pltpu.VMEM((SUB_ROWS, LANES), F32),
        pltpu.VMEM((HALO + SUB_ROWS, D_GROUP), F32),
        pltpu.VMEM((HALO + SUB_ROWS, D_GROUP), F32),
        pltpu.VMEM((HALO + SUB_ROWS, D_GROUP), F32),
        pltpu.VMEM((HALO + SUB_ROWS, D_GROUP), F32),
        pltpu.VMEM((SC_HALO + SUB_ROWS, D_GROUP), F32),
        pltpu.VMEM((SUB_ROWS, D_MODEL), BF16),
    ]
    scratch = block_scratch * (TS // SUB_ROWS) + [pltpu.VMEM((fill_rows, LANES), F32), pltpu.SemaphoreType.DMA(())]
    return pl.pallas_call(
        _mixer_kernel,
        grid=(batch, steps),
        in_specs=in_specs,
        out_specs=out_specs,
        out_shape=[jax.ShapeDtypeStruct((batch * seq * ROW_TILES, LANES), F32),
                   jax.ShapeDtypeStruct((batch * steps, TS // LANES, LANES), I32),
                   jax.ShapeDtypeStruct((N_EXPERTS, D_MODEL, 2 * D_EXPERT), BF16),
                   jax.ShapeDtypeStruct((N_EXPERTS, D_EXPERT, D_MODEL), BF16),
                   jax.ShapeDtypeStruct((n_sorted * ROW_TILES, LANES), F32)],
        scratch_shapes=scratch,
        compiler_params=pltpu.CompilerParams(
            dimension_semantics=("arbitrary", "arbitrary"), vmem_limit_bytes=VMEM_LIMIT),
        name="mixer",
    )(x, *p, *expert_w)


def _plan_call(cls2d):
    return pl.pallas_call(
        _plan_kernel,
        out_shape=[jax.ShapeDtypeStruct((LANES, LANES), I32), jax.ShapeDtypeStruct((SUBLANES, LANES), I32)],
        compiler_params=pltpu.CompilerParams(vmem_limit_bytes=VMEM_LIMIT),
        name="moe_plan",
    )(cls2d)


def _dispatch_call(dest, x_rows, xs_zero):
    n_tok = dest.shape[0]
    return pl.pallas_call(
        _dispatch_kernel,
        grid_spec=pltpu.PrefetchScalarGridSpec(
            num_scalar_prefetch=1,
            grid=(n_tok // (2 * TD),),
            in_specs=[pl.BlockSpec(memory_space=pl.ANY), pl.BlockSpec(memory_space=pl.ANY)],
            out_specs=pl.BlockSpec(memory_space=pl.ANY),
            scratch_shapes=[pltpu.SemaphoreType.DMA(())]),
        out_shape=jax.ShapeDtypeStruct(xs_zero.shape, F32),
        input_output_aliases={2: 0},
        compiler_params=pltpu.CompilerParams(
            dimension_semantics=("arbitrary",), vmem_limit_bytes=VMEM_LIMIT),
        name="moe_dispatch",
    )(dest, x_rows, xs_zero)


def _combine_call(dest, ys, vec_m, layer):
    n_tok = dest.shape[0]
    return pl.pallas_call(
        _combine_kernel,
        grid_spec=pltpu.PrefetchScalarGridSpec(
            num_scalar_prefetch=1,
            grid=(n_tok // (2 * TD),),
            in_specs=[pl.BlockSpec(memory_space=pl.ANY), _layer_block((SUBLANES, D_MODEL), layer)],
            out_specs=pl.BlockSpec((2 * TD, D_MODEL), lambda i, *_: (i, 0)),
            scratch_shapes=[
                pltpu.VMEM((2, TD * ROW_TILES, LANES), F32),
                pltpu.SemaphoreType.DMA((2,)),
            ]),
        out_shape=jax.ShapeDtypeStruct((n_tok, D_MODEL), F32),
        compiler_params=pltpu.CompilerParams(
            dimension_semantics=("arbitrary",), vmem_limit_bytes=VMEM_LIMIT),
        name="moe_combine",
    )(dest, ys, vec_m)


def _moe_call(meta, xs, wgu, wd, rwt, n_tiles):
    rows = TILES_PER_STEP * TMS * ROW_TILES
    w_specs, w_args = [], []
    for s in range(TILES_PER_STEP):
        for sel in (0, 1):
            def expert(i, meta, s=s, sel=sel):
                return (meta[sel, i * TILES_PER_STEP + s], 0, 0)
            w_specs += [pl.BlockSpec((1, D_MODEL, 2 * D_EXPERT), expert),
                        pl.BlockSpec((1, D_EXPERT, D_MODEL), expert)]
            w_args += [wgu, wd]
    tile = pl.BlockSpec((rows, LANES), lambda i, meta: (i, 0))
    return pl.pallas_call(
        _moe_kernel,
        grid_spec=pltpu.PrefetchScalarGridSpec(
            num_scalar_prefetch=1,
            grid=(n_tiles // TILES_PER_STEP,),
            in_specs=[tile] + w_specs + [_full((N_EXPERTS, D_MODEL))],
            out_specs=tile),
        out_shape=jax.ShapeDtypeStruct(xs.shape, F32),
        compiler_params=pltpu.CompilerParams(
            dimension_semantics=("arbitrary",), vmem_limit_bytes=VMEM_LIMIT),
        name="moe_experts",
    )(meta, xs, *w_args, rwt)


def _block_diag(w):
    g, d, _ = w.shape
    eye = jnp.eye(g, dtype=w.dtype)
    return (eye[:, None, :, None] * w[:, :, None, :]).reshape(g * d, g * d)


def kernel(x, w_in, gm_ln_g, gm_ln_b, gm_w_s, gm_b_s, cf_dw_w, cf_dw_b, cf_ln_g, cf_ln_b, cf_pw,
           pool_w, pool_scale, sc_w, mix_norm_g, w_o, ln1_g, ln1_b, router_w, router_b,
           exp_w_gate, exp_w_up, exp_w_down, ln2_g, ln2_b):
    batch, seq, _ = x.shape
    n_tok = batch * seq
    assert n_tok == LANES * LANES, "the routing plan lays tokens out as one (128, 128) tile grid"
    assert seq % TS == 0 and TS % SUB_ROWS == 0 and SUB_ROWS % (2 * CONV_ROWS) == 0 and n_tok % (2 * TD) == 0
    n_tiles = n_tok // TMS + N_CLASSES
    assert n_tiles <= LANES and n_tiles % TILES_PER_STEP == 0
    def pack_rows(rows):
        slab = jnp.stack(rows, axis=1)
        return jnp.pad(slab, ((0, 0), (0, SUBLANES - len(rows)), (0, 0)))

    rw = jnp.pad(router_w, ((0, 0), (0, LANES - N_EXPERTS)))
    rw_hi = rw.astype(BF16)
    rw_lo = (rw - rw_hi.astype(F32)).astype(BF16)
    rb = jnp.pad(router_b, (0, LANES - N_EXPERTS)).reshape(1, LANES)
    rwt = router_w.T
    vec_m = pack_rows([mix_norm_g, ln1_g, ln1_b, ln2_g, ln2_b])
    params = (
        w_in.astype(BF16),
        pack_rows([gm_ln_g, gm_ln_b, cf_dw_b, cf_ln_g, cf_ln_b, pool_scale]), vec_m,
        gm_w_s.reshape(DEPTH, N_HEADS * CHUNK, CHUNK),
        jnp.repeat(gm_b_s.transpose(0, 2, 1), HEAD_DIM, axis=2),
        jnp.pad(cf_dw_w, ((0, 0), (0, HALO - CONF_K), (0, 0))),
        cf_pw.astype(BF16),
        jax.vmap(_block_diag)(pool_w).astype(BF16),
        jnp.pad(sc_w, ((0, 0), (0, SUBLANES - SHORT_K), (0, 0))),
        w_o.astype(BF16),
        jnp.concatenate([rw_hi, rw_lo], axis=1), rb,
    )
    for l in range(DEPTH):
        x1, cls, wgu_b, wd_b, xs_zero = _mixer_call(
            x, params, (exp_w_gate, exp_w_up, exp_w_down), n_tiles * TMS, l)
        dest2d, meta = _plan_call(cls.reshape(LANES, LANES))
        dest = dest2d.reshape(n_tok)
        xs = _dispatch_call(dest, x1, xs_zero)
        ys = _moe_call(meta, xs, wgu_b, wd_b, rwt, n_tiles)
        x = _combine_call(dest, ys, vec_m, l).reshape(batch, seq, D_MODEL)
    return x
```

```python
import jax
import jax.numpy as jnp
from jax import lax
from jax.experimental import pallas as pl
from jax.experimental.pallas import tpu as pltpu

D_MODEL = 1024
DEPTH = 2
D_GROUP = 256
N_HEADS = 4
HEAD_DIM = 64
CHUNK = 128
CONF_K = 31
SHORT_K = 3
N_EXPERTS = 16
N_GROUPS = 4
GROUP_SIZE = 4
N_PAIRS = 6
N_CLASSES = N_GROUPS * N_PAIRS
D_EXPERT = 512
ALPHA = (2 * DEPTH) ** 0.25
LN_EPS = 1e-5
RMS_EPS = 1e-6

LANES = 128
SUBLANES = 8
ROW_TILES = D_MODEL // LANES
HALO = 32
SC_HALO = SUBLANES
TS = 512
SUB_ROWS = 256
CONV_ROWS = 64
TMS = 256
TILES_PER_STEP = 2
XS_SLOTS = 3
TD = 512
V7X_VMEM_BYTES = 64 * 1024 * 1024
VMEM_LIMIT = V7X_VMEM_BYTES * 7 // 8

BF16 = jnp.bfloat16
F32 = jnp.float32
I32 = jnp.int32


def _dot(a, b):
    return jnp.dot(a, b, preferred_element_type=F32)


def _layer_norm(x, g, b):
    mu = jnp.mean(x, axis=-1, keepdims=True)
    xc = x - mu
    var = jnp.mean(xc * xc, axis=-1, keepdims=True)
    return xc * lax.rsqrt(var + LN_EPS) * g + b


def _rms_norm(y, g):
    ms = jnp.mean(y * y, axis=-1, keepdims=True)
    return y * lax.rsqrt(ms + RMS_EPS) * g


def _sigmoid(x):
    return 1.0 / (1.0 + jnp.exp(-x))


def _route_class(sel):
    scores = []
    for g in range(N_GROUPS):
        v = sel[g * GROUP_SIZE:(g + 1) * GROUP_SIZE]
        best_pair = None
        for i in range(GROUP_SIZE):
            for j in range(i + 1, GROUP_SIZE):
                p = v[i] + v[j]
                best_pair = p if best_pair is None else jnp.maximum(best_pair, p)
        scores.append(best_pair)
    best = jnp.zeros(scores[0].shape, I32)
    best_score = scores[0]
    for g in range(1, N_GROUPS):
        better = scores[g] > best_score
        best = jnp.where(better, g, best)
        best_score = jnp.where(better, scores[g], best_score)
    v = []
    for j in range(GROUP_SIZE):
        out = sel[j]
        for g in range(1, N_GROUPS):
            out = jnp.where(best == g, sel[g * GROUP_SIZE + j], out)
        v.append(out)
    i0 = jnp.zeros_like(best)
    v0 = v[0]
    for j in range(1, GROUP_SIZE):
        better = v[j] > v0
        i0 = jnp.where(better, j, i0)
        v0 = jnp.where(better, v[j], v0)
    neg = jnp.full_like(v0, -jnp.inf)
    w = [jnp.where(i0 == j, neg, v[j]) for j in range(GROUP_SIZE)]
    i1 = jnp.zeros_like(best)
    v1 = w[0]
    for j in range(1, GROUP_SIZE):
        better = w[j] > v1
        i1 = jnp.where(better, j, i1)
        v1 = jnp.where(better, w[j], v1)
    a = jnp.minimum(i0, i1)
    b = jnp.maximum(i0, i1)
    pair = jnp.where(a == 0, b - 1, jnp.where(a == 1, b + 1, N_PAIRS - 1))
    return best * N_PAIRS + pair


def _mixer_kernel(x_ref, w_in_ref, vec_g_ref, vec_m_ref, gm_w_ref, gm_bs_ref, cf_w_ref, cf_pw_ref,
                  pool_w_ref, sc_w_ref, w_o_ref, rw_ref, rb_ref, wg_ref, wu_ref, wd_ref,
                  x1_ref, cls_ref, wgu_out, wd_out, xs_ref, *scratch):
    s_idx = pl.program_id(1)
    *scratch, zbuf, zsem = scratch
    step = pl.program_id(0) * pl.num_programs(1) + s_idx
    fill_rows = zbuf.shape[0]

    @pl.when(step == 0)
    def _():
        zbuf[...] = jnp.zeros(zbuf.shape, F32)

    fill = pltpu.make_async_copy(
        zbuf, xs_ref.at[pl.ds(pl.multiple_of(step * fill_rows, fill_rows), fill_rows), :], zsem)
    fill.start()
    gm_g, gm_b, cf_b, cf_g, cf_beta, pool_scale = (vec_g_ref[0, k:k + 1, :] for k in range(6))
    mixg, ln_g, ln_b = (vec_m_ref[0, k:k + 1, :] for k in range(3))

    row = lax.broadcasted_iota(I32, (N_HEADS * CHUNK, CHUNK), 0)
    col = lax.broadcasted_iota(I32, (N_HEADS * CHUNK, CHUNK), 1)
    w_tril = jnp.where(col <= (row & (CHUNK - 1)), gm_w_ref[0], 0.0).astype(BF16)
    lane = lax.broadcasted_iota(I32, (CHUNK, D_GROUP), 1)
    lane_t = lax.broadcasted_iota(I32, (SUB_ROWS, D_GROUP), 1)
    win = jnp.where(lane_t < HEAD_DIM, 2,
                    jnp.where(lane_t < 2 * HEAD_DIM, 4, jnp.where(lane_t < 3 * HEAD_DIM, 8, 16)))
    bs = gm_bs_ref[0]

    n_blocks = TS // SUB_ROWS
    per_block = len(scratch) // n_blocks
    blocks = [scratch[i * per_block:(i + 1) * per_block] for i in range(n_blocks)]

    @pl.when(s_idx == 0)
    def _():
        hbuf0, hbuf1, _, _, pbuf, _, _, _, cbuf, _ = blocks[0]
        hbuf0[0:HALO, :] = jnp.zeros((HALO, LANES), F32)
        hbuf1[0:HALO, :] = jnp.zeros((HALO, LANES), F32)
        pbuf[0:HALO, :] = jnp.zeros((HALO, D_GROUP), F32)
        cbuf[0:SC_HALO, :] = jnp.zeros((SC_HALO, D_GROUP), F32)

    def carry_halo(src, dst, which):
        for i in which:
            rows = SC_HALO if i == 8 else HALO
            dst[i][0:rows, :] = src[i][SUB_ROWS:SUB_ROWS + rows, :]

    st = [dict() for _ in range(n_blocks)]

    def load(bi):
        x = x_ref[0, bi * SUB_ROWS:(bi + 1) * SUB_ROWS, :]
        st[bi]["x"] = x
        st[bi]["xb"] = x.astype(BF16)

    def in_proj(bi, lo, hi):
        z = _dot(st[bi]["xb"], w_in_ref[0, :, lo * D_GROUP:hi * D_GROUP])
        for j in range(lo, hi):
            st[bi][j] = z[:, (j - lo) * D_GROUP:(j - lo + 1) * D_GROUP]

    def gating_mlp(bi):
        yn_ref = blocks[bi][9]
        u = st[bi].pop(0)
        v = _layer_norm(st[bi].pop(1), gm_g, gm_b)
        for n in range(SUB_ROWS // CHUNK):
            crows = slice(n * CHUNK, (n + 1) * CHUNK)
            s_all = _dot(w_tril, v[crows, :].astype(BF16))
            s_sel = s_all[3 * CHUNK:4 * CHUNK]
            for h in (2, 1, 0):
                s_sel = jnp.where(lane < (h + 1) * HEAD_DIM, s_all[h * CHUNK:(h + 1) * CHUNK], s_sel)
            y1 = u[crows, :] * (s_sel + bs)
            yn_ref[crows, 0:D_GROUP] = _rms_norm(y1, mixg[:, 0:D_GROUP]).astype(BF16)

    def conformer_glu(bi):
        if bi > 0:
            carry_halo(blocks[bi - 1], blocks[bi], (0, 1))
        glu = st[bi].pop(2) * _sigmoid(st[bi].pop(3))
        for half in range(2):
            blocks[bi][half][HALO:HALO + SUB_ROWS, :] = glu[:, half * LANES:(half + 1) * LANES]

    def conformer_conv(bi, half):
        hb, co = blocks[bi][half], blocks[bi][2 + half]
        lanes = slice(half * LANES, (half + 1) * LANES)
        bias = cf_b[:, lanes]
        for q in range(SUB_ROWS // (2 * CONV_ROWS)):
            for parity in range(2):
                out0 = q * 2 * CONV_ROWS + parity
                base = out0 + HALO - (CONF_K - 1)
                acc = jnp.zeros((CONV_ROWS, LANES), F32) + bias
                for k in range(CONF_K):
                    acc = acc + cf_w_ref[0, k:k + 1, lanes] * hb[pl.ds(base + k, CONV_ROWS, stride=2), :]
                co[pl.ds(out0, CONV_ROWS, stride=2), :] = acc

    def conformer_out(bi):
        cbo0, cbo1, yn_ref = blocks[bi][2], blocks[bi][3], blocks[bi][9]
        hln = _layer_norm(jnp.concatenate([cbo0[...], cbo1[...]], axis=1), cf_g, cf_beta)
        y2 = _dot((hln * _sigmoid(hln)).astype(BF16), cf_pw_ref[0])
        yn_ref[:, D_GROUP:2 * D_GROUP] = _rms_norm(y2, mixg[:, D_GROUP:2 * D_GROUP]).astype(BF16)

    def pooling(bi):
        pbuf, s2buf, s4buf, s8buf = blocks[bi][4:8]
        yn_ref = blocks[bi][9]
        if bi > 0:
            carry_halo(blocks[bi - 1], blocks[bi], (4,))
        zc = st[bi].pop(4)
        pbuf[HALO:HALO + SUB_ROWS, :] = zc
        n2 = HALO + SUB_ROWS - 8
        s2buf[8:8 + n2, :] = pbuf[8:8 + n2, :] + pbuf[7:7 + n2, :]
        n4 = HALO + SUB_ROWS - 16
        s4buf[16:16 + n4, :] = s2buf[16:16 + n4, :] + s2buf[14:14 + n4, :]
        n8 = HALO + SUB_ROWS - 24
        s8buf[24:24 + n8, :] = s4buf[24:24 + n8, :] + s4buf[20:20 + n8, :]
        cur = slice(HALO, HALO + SUB_ROWS)
        s16 = s8buf[cur, :] + s8buf[HALO - 8:HALO - 8 + SUB_ROWS, :]
        pos1 = lax.broadcasted_iota(I32, (SUB_ROWS, D_GROUP), 0) + (s_idx * TS + bi * SUB_ROWS + 1)
        wsum = jnp.where(lane_t < HEAD_DIM, s2buf[cur, :],
                         jnp.where(lane_t < 2 * HEAD_DIM, s4buf[cur, :],
                                   jnp.where(lane_t < 3 * HEAD_DIM, s8buf[cur, :], s16)))
        count = jnp.minimum(pos1, win).astype(F32)
        pooled = wsum / count - zc
        y3 = _dot(pooled.astype(BF16), pool_w_ref[0]) * pool_scale
        yn_ref[:, 2 * D_GROUP:3 * D_GROUP] = _rms_norm(y3, mixg[:, 2 * D_GROUP:3 * D_GROUP]).astype(BF16)

    def short_conv(bi):
        cbuf, yn_ref = blocks[bi][8], blocks[bi][9]
        if bi > 0:
            carry_halo(blocks[bi - 1], blocks[bi], (8,))
        cbuf[SC_HALO:SC_HALO + SUB_ROWS, :] = st[bi].pop(6) * st[bi].pop(7)
        conv = jnp.zeros((SUB_ROWS, D_GROUP), F32)
        for k in range(SHORT_K):
            off = SC_HALO - (SHORT_K - 1) + k
            conv = conv + sc_w_ref[0, k:k + 1, :] * cbuf[off:off + SUB_ROWS, :]
        y4 = st[bi].pop(5) * conv
        yn_ref[:, 3 * D_GROUP:4 * D_GROUP] = _rms_norm(y4, mixg[:, 3 * D_GROUP:4 * D_GROUP]).astype(BF16)

    def out_proj(bi):
        m = _dot(blocks[bi][9][...], w_o_ref[0])
        x1 = _layer_norm(ALPHA * st[bi].pop("x") + m, ln_g, ln_b)
        x1_ref[0, bi * SUB_ROWS:(bi + 1) * SUB_ROWS, :] = x1
        st[bi]["x1"] = x1

    def router(bi):
        x1 = st[bi].pop("x1")
        hi = x1.astype(BF16)
        lo = (x1 - hi.astype(F32)).astype(BF16)
        hi_both = _dot(hi, rw_ref[...])
        logits = hi_both[:, 0:LANES] + hi_both[:, LANES:2 * LANES] + _dot(lo, rw_ref[:, 0:LANES])
        st[bi]["sel"] = logits + rb_ref[...]

    def route(bi):
        sel_t = st[bi].pop("sel").T
        cls = _route_class([sel_t[e:e + 1, :] for e in range(N_EXPERTS)])
        r0 = bi * SUB_ROWS
        cls_ref[0, r0 // LANES:(r0 + SUB_ROWS) // LANES, :] = jnp.concatenate(
            [cls[:, k * LANES:(k + 1) * LANES] for k in range(SUB_ROWS // LANES)], axis=0)

    def conformer(b):
        conformer_glu(b)
        conformer_conv(b, 0)
        conformer_conv(b, 1)
        conformer_out(b)

    def round_expert_weights():
        wgu_out[0, :, 0:D_EXPERT] = wg_ref[0, 0].astype(BF16)
        wgu_out[0, :, D_EXPERT:2 * D_EXPERT] = wu_ref[0, 0].astype(BF16)
        wd_out[0] = wd_ref[0, 0].astype(BF16)

    def stage1(b):
        pieces = [lambda: load(b), lambda: in_proj(b, 0, 2), lambda: in_proj(b, 2, 4), lambda: in_proj(b, 4, 5),
                  lambda: in_proj(b, 5, 8)]
        if b == 0:
            pieces.insert(2, round_expert_weights)
        return pieces

    def stage2(b):
        return [lambda: gating_mlp(b), lambda: conformer(b), lambda: pooling(b), lambda: short_conv(b)]

    def stage3(b):
        return [lambda: out_proj(b), lambda: router(b), lambda: route(b)]

    for t in range(n_blocks + 2):
        stages = [stage(t - lag) for lag, stage in ((2, stage3), (0, stage1), (1, stage2)) if 0 <= t - lag < n_blocks]
        for k in range(max(len(stage) for stage in stages)):
            for stage in stages:
                if k < len(stage):
                    stage[k]()

    carry_halo(blocks[-1], blocks[0], (0, 1, 4, 8))
    fill.wait()


def _plan_kernel(cls_ref, dest_ref, meta_ref):
    n = LANES
    cls = cls_ref[...]
    r = lax.broadcasted_iota(I32, (n, n), 0)
    c = lax.broadcasted_iota(I32, (n, n), 1)
    upper = jnp.where(r < c, 1.0, 0.0).astype(BF16)
    lower = jnp.where(c < r, 1.0, 0.0).astype(BF16)
    ones = jnp.ones((n, n), BF16)
    masks = [cls == k for k in range(N_CLASSES)]
    m_all = jnp.concatenate([jnp.where(mk, 1.0, 0.0) for mk in masks], axis=0).astype(BF16)
    within = _dot(m_all, upper)
    rowsum = _dot(m_all, ones)
    rs = jnp.zeros((n, n), F32)
    for k in range(N_CLASSES):
        rs = jnp.where(c == k, rowsum[k * n:(k + 1) * n], rs)
    rs_b = rs.astype(BF16)
    before = _dot(lower, rs_b)
    total = _dot(ones, rs_b)
    ntile = jnp.floor((total + (TMS - 1)) * (1.0 / TMS))
    tstart = _dot(ntile.astype(BF16), upper)
    base = before + tstart * TMS
    dest = jnp.zeros((n, n), F32)
    for k in range(N_CLASSES):
        dest = jnp.where(masks[k], within[k * n:(k + 1) * n] + base[:, k:k + 1], dest)
    dest_ref[...] = dest.astype(I32)

    tend_t = (tstart + ntile).T
    tile = c.astype(F32)
    ended = jnp.where((r < N_CLASSES) & (tend_t <= tile), 1.0, 0.0)
    tcls = jnp.minimum(jnp.sum(ended, axis=0, keepdims=True), N_CLASSES - 1.0)
    grp = (jnp.where(tcls >= N_PAIRS, 1.0, 0.0) + jnp.where(tcls >= 2 * N_PAIRS, 1.0, 0.0)
           + jnp.where(tcls >= 3 * N_PAIRS, 1.0, 0.0))
    pair = tcls - N_PAIRS * grp
    pa = jnp.where(pair >= 3, 1.0, 0.0) + jnp.where(pair >= 5, 1.0, 0.0)
    pb = jnp.where(pair == 0, 1.0, jnp.where((pair == 1) | (pair == 3), 2.0, 3.0))
    n_tiles = tstart[0:1, N_CLASSES:N_CLASSES + 1]
    tile_row = tile[0:1, :]
    active = jnp.where(tile_row < n_tiles, 1.0, 0.0)
    meta = jnp.concatenate(
        [GROUP_SIZE * grp + pa, GROUP_SIZE * grp + pb, active, jnp.zeros((SUBLANES - 3, n), F32)], axis=0)
    meta_ref[...] = meta.astype(I32)


def _row_copy_wait(src, dst, sem):
    pltpu.make_async_copy(src, dst, sem).wait()


def _dispatch_kernel(dest_ref, x_ref, xs_zero_ref, xs_ref, buf, sems):
    del xs_zero_ref
    i = pl.program_id(0)
    n_steps = pl.num_programs(0)

    for s in range(2):
        @pl.when(i > 0)
        def _():
            _row_copy_wait(buf.at[s], xs_ref.at[pl.ds(0, TD * ROW_TILES), :], sems.at[s])

        for j in range(ROW_TILES):
            buf[s, pl.ds(j, TD, stride=ROW_TILES), :] = x_ref[s * TD:(s + 1) * TD, j * LANES:(j + 1) * LANES]
        for t in range(TD):
            d = pl.multiple_of(dest_ref[(i * 2 + s) * TD + t] * ROW_TILES, ROW_TILES)
            pltpu.make_async_copy(buf.at[s, pl.ds(t * ROW_TILES, ROW_TILES), :],
                                  xs_ref.at[pl.ds(d, ROW_TILES), :], sems.at[s]).start(priority=t % 2)

    @pl.when(i == n_steps - 1)
    def _():
        for s in range(2):
            _row_copy_wait(buf.at[s], xs_ref.at[pl.ds(0, TD * ROW_TILES), :], sems.at[s])


def _combine_kernel(dest_ref, ys_ref, vec_m_ref, out_ref, buf, sems):
    i = pl.program_id(0)
    n_steps = pl.num_programs(0)

    def issue(step, s):
        for t in range(TD):
            d = pl.multiple_of(dest_ref[(step * 2 + s) * TD + t] * ROW_TILES, ROW_TILES)
            pltpu.make_async_copy(ys_ref.at[pl.ds(d, ROW_TILES), :],
                                  buf.at[s, pl.ds(t * ROW_TILES, ROW_TILES), :], sems.at[s]).start(priority=t % 2)

    @pl.when(i == 0)
    def _():
        for s in range(2):
            issue(0, s)

    for s in range(2):
        _row_copy_wait(ys_ref.at[pl.ds(0, TD * ROW_TILES), :], buf.at[s], sems.at[s])
        r = jnp.concatenate([buf[s, pl.ds(j, TD, stride=ROW_TILES), :] for j in range(ROW_TILES)], axis=1)
        out_ref[s * TD:(s + 1) * TD, :] = _layer_norm(r, vec_m_ref[0, 3:4, :], vec_m_ref[0, 4:5, :])

        @pl.when(i + 1 < n_steps)
        def _():
            issue(i + 1, s)


def _moe_kernel(meta_ref, xs_ref, *refs):
    i = pl.program_id(0)
    n_steps = pl.num_programs(0)
    w_refs, rwt_ref, ys_ref, xbuf, xsems = refs[:-4], refs[-4], refs[-3], refs[-2], refs[-1]
    rows = TMS * ROW_TILES
    step_rows = TILES_PER_STEP * rows

    def fetch(step):
        slot = step % XS_SLOTS
        start = pl.multiple_of(step * step_rows, step_rows)
        return pltpu.make_async_copy(xs_ref.at[pl.ds(start, step_rows), :], xbuf.at[slot], xsems.at[slot])

    @pl.when(i == 0)
    def _():
        for ahead in range(XS_SLOTS - 1):
            fetch(ahead).start()

    @pl.when(i + XS_SLOTS - 1 < n_steps)
    def _():
        fetch(i + XS_SLOTS - 1).start()

    fetch(i).wait()
    x_rows = xbuf.at[i % XS_SLOTS]

    @pl.when(meta_ref[2, i * TILES_PER_STEP] == 0)
    def _():
        ys_ref[...] = jnp.zeros(ys_ref.shape, F32)

    @pl.when(meta_ref[2, i * TILES_PER_STEP] == 1)
    def _():
        st = [dict() for _ in range(TILES_PER_STEP)]

        def load(s):
            x = jnp.concatenate(
                [x_rows[pl.ds(s * rows + j, TMS, stride=ROW_TILES), :] for j in range(ROW_TILES)], axis=1)
            st[s]["x"] = x
            st[s]["xb"] = x.astype(BF16)

        def up(s, e):
            gu = _dot(st[s]["xb"], w_refs[4 * s + 2 * e][0])
            st[s]["g", e] = gu[:, 0:D_EXPERT]
            st[s]["u", e] = gu[:, D_EXPERT:2 * D_EXPERT]

        def act(s, e):
            g = st[s].pop(("g", e))
            st[s]["h", e] = ((g * _sigmoid(g)) * st[s].pop(("u", e))).astype(BF16)

        def down(s, e):
            st[s]["y", e] = _dot(st[s].pop(("h", e)), w_refs[4 * s + 2 * e + 1][0])

        def mix(s):
            tile = i * TILES_PER_STEP + s
            x = st[s].pop("x")
            la = jnp.sum(x * rwt_ref[pl.ds(meta_ref[0, tile], 1), :], axis=-1, keepdims=True)
            lb = jnp.sum(x * rwt_ref[pl.ds(meta_ref[1, tile], 1), :], axis=-1, keepdims=True)
            m = jnp.maximum(la, lb)
            pa = jnp.exp(la - m)
            pb = jnp.exp(lb - m)
            den = pa + pb
            r = ALPHA * x + ((pa / den) * st[s].pop(("y", 0)) + (pb / den) * st[s].pop(("y", 1)))
            for j in range(ROW_TILES):
                ys_ref[pl.ds(s * rows + j, TMS, stride=ROW_TILES), :] = r[:, j * LANES:(j + 1) * LANES]

        units = [(s, e) for s in range(TILES_PER_STEP) for e in range(2)]
        load(0)
        up(*units[0])
        for k, (s, e) in enumerate(units):
            if k + 1 < len(units):
                nxt = units[k + 1]
                if nxt[1] == 0:
                    load(nxt[0])
                up(*nxt)
            act(s, e)
            down(s, e)
            if e == 1:
                mix(s)


def _full(shape):
    return pl.BlockSpec(shape, lambda *_: (0,) * len(shape))


def _layer_block(shape, layer):
    return pl.BlockSpec((1,) + shape, lambda *_: (layer,) + (0,) * len(shape))


def _mixer_call(x, p, expert_w, n_sorted, layer):
    batch, seq, _ = x.shape
    steps = seq // TS
    fill_rows = n_sorted * ROW_TILES // (batch * steps)
    assert fill_rows * batch * steps == n_sorted * ROW_TILES and fill_rows % SUBLANES == 0
    slabs = batch * steps // N_EXPERTS
    assert batch * steps == slabs * N_EXPERTS

    def w_in(b, s):
        i = b * steps + s
        return (layer, i // slabs, i % slabs, 0)

    def w_out(b, s):
        i = b * steps + s
        return (i // slabs, i % slabs, 0)

    up_rows, down_rows = D_MODEL // slabs, D_EXPERT // slabs
    in_specs = [
        pl.BlockSpec((1, TS, D_MODEL), lambda b, s: (b, s, 0)),
        _layer_block((D_MODEL, 8 * D_GROUP), layer),
        _layer_block((SUBLANES, D_GROUP), layer), _layer_block((SUBLANES, D_MODEL), layer),
        _layer_block((N_HEADS * CHUNK, CHUNK), layer), _layer_block((CHUNK, D_GROUP), layer),
        _layer_block((HALO, D_GROUP), layer),
        _layer_block((D_GROUP, D_GROUP), layer),
        _layer_block((D_GROUP, D_GROUP), layer),
        _layer_block((SUBLANES, D_GROUP), layer),
        _layer_block((D_MODEL, D_MODEL), layer),
        _full((D_MODEL, 2 * LANES)), _full((1, LANES)),
        pl.BlockSpec((1, 1, up_rows, D_EXPERT), w_in),
        pl.BlockSpec((1, 1, up_rows, D_EXPERT), w_in),
        pl.BlockSpec((1, 1, down_rows, D_MODEL), w_in),
    ]
    out_specs = [
        pl.BlockSpec((1, TS, D_MODEL), lambda b, s: (b, s, 0)),
        pl.BlockSpec((1, TS // LANES, LANES), lambda b, s: (b * steps + s, 0, 0)),
        pl.BlockSpec((1, up_rows, 2 * D_EXPERT), w_out),
        pl.BlockSpec((1, down_rows, D_MODEL), w_out),
        pl.BlockSpec(memory_space=pl.ANY),
    ]
    block_scratch = [
        pltpu.VMEM((HALO + SUB_ROWS, LANES), F32),
        pltpu.VMEM((HALO + SUB_ROWS, LANES), F32),
        pltpu.VMEM((SUB_ROWS, LANES), F32),
        pltpu.VMEM((SUB_ROWS, LANES), F32),
        pltpu.VMEM((HALO + SUB_ROWS, D_GROUP), F32),
        pltpu.VMEM((HALO + SUB_ROWS, D_GROUP), F32),
        pltpu.VMEM((HALO + SUB_ROWS, D_GROUP), F32),
        pltpu.VMEM((HALO + SUB_ROWS, D_GROUP), F32),
        pltpu.VMEM((SC_HALO + SUB_ROWS, D_GROUP), F32),
        pltpu.VMEM((SUB_ROWS, D_MODEL), BF16),
    ]
    scratch = block_scratch * (TS // SUB_ROWS) + [pltpu.VMEM((fill_rows, LANES), F32), pltpu.SemaphoreType.DMA(())]
    return pl.pallas_call(
        _mixer_kernel,
        grid=(batch, steps),
        in_specs=in_specs,
        out_specs=out_specs,
        out_shape=[jax.ShapeDtypeStruct((batch, seq, D_MODEL), F32),
                   jax.ShapeDtypeStruct((batch * steps, TS // LANES, LANES), I32),
                   jax.ShapeDtypeStruct((N_EXPERTS, D_MODEL, 2 * D_EXPERT), BF16),
                   jax.ShapeDtypeStruct((N_EXPERTS, D_EXPERT, D_MODEL), BF16),
                   jax.ShapeDtypeStruct((n_sorted * ROW_TILES, LANES), F32)],
        scratch_shapes=scratch,
        compiler_params=pltpu.CompilerParams(
            dimension_semantics=("arbitrary", "arbitrary"), vmem_limit_bytes=VMEM_LIMIT),
        name="mixer",
    )(x, *p, *expert_w)


def _plan_call(cls2d):
    return pl.pallas_call(
        _plan_kernel,
        out_shape=[jax.ShapeDtypeStruct((LANES, LANES), I32), jax.ShapeDtypeStruct((SUBLANES, LANES), I32)],
        compiler_params=pltpu.CompilerParams(vmem_limit_bytes=VMEM_LIMIT),
        name="moe_plan",
    )(cls2d)


def _dispatch_call(dest, x2d, xs_zero):
    n_tok = x2d.shape[0]
    return pl.pallas_call(
        _dispatch_kernel,
        grid_spec=pltpu.PrefetchScalarGridSpec(
            num_scalar_prefetch=1,
            grid=(n_tok // (2 * TD),),
            in_specs=[pl.BlockSpec((2 * TD, D_MODEL), lambda i, *_: (i, 0)),
                      pl.BlockSpec(memory_space=pl.ANY)],
            out_specs=pl.BlockSpec(memory_space=pl.ANY),
            scratch_shapes=[
                pltpu.VMEM((2, TD * ROW_TILES, LANES), F32),
                pltpu.SemaphoreType.DMA((2,)),
            ]),
        out_shape=jax.ShapeDtypeStruct(xs_zero.shape, F32),
        input_output_aliases={2: 0},
        compiler_params=pltpu.CompilerParams(
            dimension_semantics=("arbitrary",), vmem_limit_bytes=VMEM_LIMIT),
        name="moe_dispatch",
    )(dest, x2d, xs_zero)


def _combine_call(dest, ys, vec_m, layer):
    n_tok = dest.shape[0]
    return pl.pallas_call(
        _combine_kernel,
        grid_spec=pltpu.PrefetchScalarGridSpec(
            num_scalar_prefetch=1,
            grid=(n_tok // (2 * TD),),
            in_specs=[pl.BlockSpec(memory_space=pl.ANY), _layer_block((SUBLANES, D_MODEL), layer)],
            out_specs=pl.BlockSpec((2 * TD, D_MODEL), lambda i, *_: (i, 0)),
            scratch_shapes=[
                pltpu.VMEM((2, TD * ROW_TILES, LANES), F32),
                pltpu.SemaphoreType.DMA((2,)),
            ]),
        out_shape=jax.ShapeDtypeStruct((n_tok, D_MODEL), F32),
        compiler_params=pltpu.CompilerParams(
            dimension_semantics=("arbitrary",), vmem_limit_bytes=VMEM_LIMIT),
        name="moe_combine",
    )(dest, ys, vec_m)


def _moe_call(meta, xs, wgu, wd, rwt, n_tiles):
    rows = TILES_PER_STEP * TMS * ROW_TILES
    w_specs, w_args = [], []
    for s in range(TILES_PER_STEP):
        for sel in (0, 1):
            def expert(i, meta, s=s, sel=sel):
                return (meta[sel, i * TILES_PER_STEP + s], 0, 0)
            w_specs += [pl.BlockSpec((1, D_MODEL, 2 * D_EXPERT), expert),
                        pl.BlockSpec((1, D_EXPERT, D_MODEL), expert)]
            w_args += [wgu, wd]
    tile = pl.BlockSpec((rows, LANES), lambda i, meta: (i, 0))
    return pl.pallas_call(
        _moe_kernel,
        grid_spec=pltpu.PrefetchScalarGridSpec(
            num_scalar_prefetch=1,
            grid=(n_tiles // TILES_PER_STEP,),
            in_specs=[pl.BlockSpec(memory_space=pl.ANY)] + w_specs + [_full((N_EXPERTS, D_MODEL))],
            out_specs=tile,
            scratch_shapes=[pltpu.VMEM((XS_SLOTS, rows, LANES), F32), pltpu.SemaphoreType.DMA((XS_SLOTS,))]),
        out_shape=jax.ShapeDtypeStruct(xs.shape, F32),
        compiler_params=pltpu.CompilerParams(
            dimension_semantics=("arbitrary",), vmem_limit_bytes=VMEM_LIMIT),
        name="moe_experts",
    )(meta, xs, *w_args, rwt)


def _block_diag(w):
    g, d, _ = w.shape
    eye = jnp.eye(g, dtype=w.dtype)
    return (eye[:, None, :, None] * w[:, :, None, :]).reshape(g * d, g * d)


def kernel(x, w_in, gm_ln_g, gm_ln_b, gm_w_s, gm_b_s, cf_dw_w, cf_dw_b, cf_ln_g, cf_ln_b, cf_pw,
           pool_w, pool_scale, sc_w, mix_norm_g, w_o, ln1_g, ln1_b, router_w, router_b,
           exp_w_gate, exp_w_up, exp_w_down, ln2_g, ln2_b):
    batch, seq, _ = x.shape
    n_tok = batch * seq
    assert n_tok == LANES * LANES, "the routing plan lays tokens out as one (128, 128) tile grid"
    assert seq % TS == 0 and TS % SUB_ROWS == 0 and SUB_ROWS % (2 * CONV_ROWS) == 0 and n_tok % (2 * TD) == 0
    n_tiles = n_tok // TMS + N_CLASSES
    assert n_tiles <= LANES and n_tiles % TILES_PER_STEP == 0 and n_tiles // TILES_PER_STEP >= XS_SLOTS - 1
    def pack_rows(rows):
        slab = jnp.stack(rows, axis=1)
        return jnp.pad(slab, ((0, 0), (0, SUBLANES - len(rows)), (0, 0)))

    rw = jnp.pad(router_w, ((0, 0), (0, LANES - N_EXPERTS)))
    rw_hi = rw.astype(BF16)
    rw_lo = (rw - rw_hi.astype(F32)).astype(BF16)
    rb = jnp.pad(router_b, (0, LANES - N_EXPERTS)).reshape(1, LANES)
    rwt = router_w.T
    vec_m = pack_rows([mix_norm_g, ln1_g, ln1_b, ln2_g, ln2_b])
    params = (
        w_in.astype(BF16),
        pack_rows([gm_ln_g, gm_ln_b, cf_dw_b, cf_ln_g, cf_ln_b, pool_scale]), vec_m,
        gm_w_s.reshape(DEPTH, N_HEADS * CHUNK, CHUNK),
        jnp.repeat(gm_b_s.transpose(0, 2, 1), HEAD_DIM, axis=2),
        jnp.pad(cf_dw_w, ((0, 0), (0, HALO - CONF_K), (0, 0))),
        cf_pw.astype(BF16),
        jax.vmap(_block_diag)(pool_w).astype(BF16),
        jnp.pad(sc_w, ((0, 0), (0, SUBLANES - SHORT_K), (0, 0))),
        w_o.astype(BF16),
        jnp.concatenate([rw_hi, rw_lo], axis=1), rb,
    )
    for l in range(DEPTH):
        x1, cls, wgu_b, wd_b, xs_zero = _mixer_call(
            x, params, (exp_w_gate, exp_w_up, exp_w_down), n_tiles * TMS, l)
        x1 = x1.reshape(n_tok, D_MODEL)
        dest2d, meta = _plan_call(cls.reshape(LANES, LANES))
        dest = dest2d.reshape(n_tok)
        xs = _dispatch_call(dest, x1, xs_zero)
        ys = _moe_call(meta, xs, wgu_b, wd_b, rwt, n_tiles)
        x = _combine_call(dest, ys, vec_m, l).reshape(batch, seq, D_MODEL)
    return x
```

```python
import jax
import jax.numpy as jnp
from jax import lax
from jax.experimental import pallas as pl
from jax.experimental.pallas import tpu as pltpu

D_MODEL = 1024
DEPTH = 2
D_GROUP = 256
N_HEADS = 4
HEAD_DIM = 64
CHUNK = 128
CONF_K = 31
SHORT_K = 3
N_EXPERTS = 16
N_GROUPS = 4
GROUP_SIZE = 4
N_PAIRS = 6
N_CLASSES = N_GROUPS * N_PAIRS
D_EXPERT = 512
ALPHA = (2 * DEPTH) ** 0.25
LN_EPS = 1e-5
RMS_EPS = 1e-6

LANES = 128
SUBLANES = 8
ROW_TILES = D_MODEL // LANES
HALO = 32
SC_HALO = SUBLANES
TS = 512
SUB_ROWS = 256
CONV_ROWS = 64
TMS = 256
TILES_PER_STEP = 2
ACT_COLS = 256
XS_SLOTS = 3
TD = 512
V7X_VMEM_BYTES = 64 * 1024 * 1024
VMEM_LIMIT = V7X_VMEM_BYTES * 7 // 8

BF16 = jnp.bfloat16
F32 = jnp.float32
I32 = jnp.int32


def _dot(a, b):
    return jnp.dot(a, b, preferred_element_type=F32)


def _layer_norm(x, g, b):
    mu = jnp.mean(x, axis=-1, keepdims=True)
    xc = x - mu
    var = jnp.mean(xc * xc, axis=-1, keepdims=True)
    return xc * lax.rsqrt(var + LN_EPS) * g + b


def _rms_norm(y, g):
    ms = jnp.mean(y * y, axis=-1, keepdims=True)
    return y * lax.rsqrt(ms + RMS_EPS) * g


def _sigmoid(x):
    return 1.0 / (1.0 + jnp.exp(-x))


def _route_class(sel):
    scores = []
    for g in range(N_GROUPS):
        v = sel[g * GROUP_SIZE:(g + 1) * GROUP_SIZE]
        best_pair = None
        for i in range(GROUP_SIZE):
            for j in range(i + 1, GROUP_SIZE):
                p = v[i] + v[j]
                best_pair = p if best_pair is None else jnp.maximum(best_pair, p)
        scores.append(best_pair)
    best = jnp.zeros(scores[0].shape, I32)
    best_score = scores[0]
    for g in range(1, N_GROUPS):
        better = scores[g] > best_score
        best = jnp.where(better, g, best)
        best_score = jnp.where(better, scores[g], best_score)
    v = []
    for j in range(GROUP_SIZE):
        out = sel[j]
        for g in range(1, N_GROUPS):
            out = jnp.where(best == g, sel[g * GROUP_SIZE + j], out)
        v.append(out)
    i0 = jnp.zeros_like(best)
    v0 = v[0]
    for j in range(1, GROUP_SIZE):
        better = v[j] > v0
        i0 = jnp.where(better, j, i0)
        v0 = jnp.where(better, v[j], v0)
    neg = jnp.full_like(v0, -jnp.inf)
    w = [jnp.where(i0 == j, neg, v[j]) for j in range(GROUP_SIZE)]
    i1 = jnp.zeros_like(best)
    v1 = w[0]
    for j in range(1, GROUP_SIZE):
        better = w[j] > v1
        i1 = jnp.where(better, j, i1)
        v1 = jnp.where(better, w[j], v1)
    a = jnp.minimum(i0, i1)
    b = jnp.maximum(i0, i1)
    pair = jnp.where(a == 0, b - 1, jnp.where(a == 1, b + 1, N_PAIRS - 1))
    return best * N_PAIRS + pair


def _mixer_kernel(x_ref, w_in_ref, vec_g_ref, vec_m_ref, gm_w_ref, gm_bs_ref, cf_w_ref, cf_pw_ref,
                  pool_w_ref, sc_w_ref, w_o_ref, rw_ref, rb_ref, wg_ref, wu_ref, wd_ref,
                  x1_ref, cls_ref, wgu_out, wd_out, xs_ref, *scratch):
    s_idx = pl.program_id(1)
    *scratch, zbuf, zsem = scratch
    step = pl.program_id(0) * pl.num_programs(1) + s_idx
    fill_rows = zbuf.shape[0]

    @pl.when(step == 0)
    def _():
        zbuf[...] = jnp.zeros(zbuf.shape, F32)

    fill = pltpu.make_async_copy(
        zbuf, xs_ref.at[pl.ds(pl.multiple_of(step * fill_rows, fill_rows), fill_rows), :], zsem)
    fill.start()
    gm_g, gm_b, cf_b, cf_g, cf_beta, pool_scale = (vec_g_ref[0, k:k + 1, :] for k in range(6))
    mixg, ln_g, ln_b = (vec_m_ref[0, k:k + 1, :] for k in range(3))

    row = lax.broadcasted_iota(I32, (N_HEADS * CHUNK, CHUNK), 0)
    col = lax.broadcasted_iota(I32, (N_HEADS * CHUNK, CHUNK), 1)
    w_tril = jnp.where(col <= (row & (CHUNK - 1)), gm_w_ref[0], 0.0).astype(BF16)
    lane = lax.broadcasted_iota(I32, (CHUNK, D_GROUP), 1)
    lane_t = lax.broadcasted_iota(I32, (SUB_ROWS, D_GROUP), 1)
    win = jnp.where(lane_t < HEAD_DIM, 2,
                    jnp.where(lane_t < 2 * HEAD_DIM, 4, jnp.where(lane_t < 3 * HEAD_DIM, 8, 16)))
    bs = gm_bs_ref[0]

    n_blocks = TS // SUB_ROWS
    per_block = len(scratch) // n_blocks
    blocks = [scratch[i * per_block:(i + 1) * per_block] for i in range(n_blocks)]

    @pl.when(s_idx == 0)
    def _():
        hbuf0, hbuf1, _, _, pbuf, _, _, _, cbuf, _ = blocks[0]
        hbuf0[0:HALO, :] = jnp.zeros((HALO, LANES), F32)
        hbuf1[0:HALO, :] = jnp.zeros((HALO, LANES), F32)
        pbuf[0:HALO, :] = jnp.zeros((HALO, D_GROUP), F32)
        cbuf[0:SC_HALO, :] = jnp.zeros((SC_HALO, D_GROUP), F32)

    def carry_halo(src, dst, which):
        for i in which:
            rows = SC_HALO if i == 8 else HALO
            dst[i][0:rows, :] = src[i][SUB_ROWS:SUB_ROWS + rows, :]

    st = [dict() for _ in range(n_blocks)]

    def load(bi):
        x = x_ref[0, bi * SUB_ROWS:(bi + 1) * SUB_ROWS, :]
        st[bi]["x"] = x
        st[bi]["xb"] = x.astype(BF16)

    def in_proj(bi, lo, hi):
        z = _dot(st[bi]["xb"], w_in_ref[0, :, lo * D_GROUP:hi * D_GROUP])
        for j in range(lo, hi):
            st[bi][j] = z[:, (j - lo) * D_GROUP:(j - lo + 1) * D_GROUP]

    def gating_mlp(bi):
        yn_ref = blocks[bi][9]
        u = st[bi].pop(0)
        v = _layer_norm(st[bi].pop(1), gm_g, gm_b)
        for n in range(SUB_ROWS // CHUNK):
            crows = slice(n * CHUNK, (n + 1) * CHUNK)
            s_all = _dot(w_tril, v[crows, :].astype(BF16))
            s_sel = s_all[3 * CHUNK:4 * CHUNK]
            for h in (2, 1, 0):
                s_sel = jnp.where(lane < (h + 1) * HEAD_DIM, s_all[h * CHUNK:(h + 1) * CHUNK], s_sel)
            y1 = u[crows, :] * (s_sel + bs)
            yn_ref[crows, 0:D_GROUP] = _rms_norm(y1, mixg[:, 0:D_GROUP]).astype(BF16)

    def conformer_glu(bi):
        if bi > 0:
            carry_halo(blocks[bi - 1], blocks[bi], (0, 1))
        glu = st[bi].pop(2) * _sigmoid(st[bi].pop(3))
        for half in range(2):
            blocks[bi][half][HALO:HALO + SUB_ROWS, :] = glu[:, half * LANES:(half + 1) * LANES]

    def conformer_conv(bi, half):
        hb, co = blocks[bi][half], blocks[bi][2 + half]
        lanes = slice(half * LANES, (half + 1) * LANES)
        bias = cf_b[:, lanes]
        for q in range(SUB_ROWS // (2 * CONV_ROWS)):
            for parity in range(2):
                out0 = q * 2 * CONV_ROWS + parity
                base = out0 + HALO - (CONF_K - 1)
                acc = jnp.zeros((CONV_ROWS, LANES), F32) + bias
                for k in range(CONF_K):
                    acc = acc + cf_w_ref[0, k:k + 1, lanes] * hb[pl.ds(base + k, CONV_ROWS, stride=2), :]
                co[pl.ds(out0, CONV_ROWS, stride=2), :] = acc

    def conformer_out(bi):
        cbo0, cbo1, yn_ref = blocks[bi][2], blocks[bi][3], blocks[bi][9]
        hln = _layer_norm(jnp.concatenate([cbo0[...], cbo1[...]], axis=1), cf_g, cf_beta)
        y2 = _dot((hln * _sigmoid(hln)).astype(BF16), cf_pw_ref[0])
        yn_ref[:, D_GROUP:2 * D_GROUP] = _rms_norm(y2, mixg[:, D_GROUP:2 * D_GROUP]).astype(BF16)

    def pooling(bi):
        pbuf, s2buf, s4buf, s8buf = blocks[bi][4:8]
        yn_ref = blocks[bi][9]
        if bi > 0:
            carry_halo(blocks[bi - 1], blocks[bi], (4,))
        zc = st[bi].pop(4)
        pbuf[HALO:HALO + SUB_ROWS, :] = zc
        n2 = HALO + SUB_ROWS - 8
        s2buf[8:8 + n2, :] = pbuf[8:8 + n2, :] + pbuf[7:7 + n2, :]
        n4 = HALO + SUB_ROWS - 16
        s4buf[16:16 + n4, :] = s2buf[16:16 + n4, :] + s2buf[14:14 + n4, :]
        n8 = HALO + SUB_ROWS - 24
        s8buf[24:24 + n8, :] = s4buf[24:24 + n8, :] + s4buf[20:20 + n8, :]
        cur = slice(HALO, HALO + SUB_ROWS)
        s16 = s8buf[cur, :] + s8buf[HALO - 8:HALO - 8 + SUB_ROWS, :]
        pos1 = lax.broadcasted_iota(I32, (SUB_ROWS, D_GROUP), 0) + (s_idx * TS + bi * SUB_ROWS + 1)
        wsum = jnp.where(lane_t < HEAD_DIM, s2buf[cur, :],
                         jnp.where(lane_t < 2 * HEAD_DIM, s4buf[cur, :],
                                   jnp.where(lane_t < 3 * HEAD_DIM, s8buf[cur, :], s16)))
        count = jnp.minimum(pos1, win).astype(F32)
        pooled = wsum / count - zc
        y3 = _dot(pooled.astype(BF16), pool_w_ref[0]) * pool_scale
        yn_ref[:, 2 * D_GROUP:3 * D_GROUP] = _rms_norm(y3, mixg[:, 2 * D_GROUP:3 * D_GROUP]).astype(BF16)

    def short_conv(bi):
        cbuf, yn_ref = blocks[bi][8], blocks[bi][9]
        if bi > 0:
            carry_halo(blocks[bi - 1], blocks[bi], (8,))
        cbuf[SC_HALO:SC_HALO + SUB_ROWS, :] = st[bi].pop(6) * st[bi].pop(7)
        conv = jnp.zeros((SUB_ROWS, D_GROUP), F32)
        for k in range(SHORT_K):
            off = SC_HALO - (SHORT_K - 1) + k
            conv = conv + sc_w_ref[0, k:k + 1, :] * cbuf[off:off + SUB_ROWS, :]
        y4 = st[bi].pop(5) * conv
        yn_ref[:, 3 * D_GROUP:4 * D_GROUP] = _rms_norm(y4, mixg[:, 3 * D_GROUP:4 * D_GROUP]).astype(BF16)

    def out_proj(bi):
        m = _dot(blocks[bi][9][...], w_o_ref[0])
        x1 = _layer_norm(ALPHA * st[bi].pop("x") + m, ln_g, ln_b)
        x1_ref[0, bi * SUB_ROWS:(bi + 1) * SUB_ROWS, :] = x1
        st[bi]["x1"] = x1

    def router(bi):
        x1 = st[bi].pop("x1")
        hi = x1.astype(BF16)
        lo = (x1 - hi.astype(F32)).astype(BF16)
        hi_both = _dot(hi, rw_ref[...])
        logits = hi_both[:, 0:LANES] + hi_both[:, LANES:2 * LANES] + _dot(lo, rw_ref[:, 0:LANES])
        st[bi]["sel"] = logits + rb_ref[...]

    def route(bi):
        sel_t = st[bi].pop("sel").T
        cls = _route_class([sel_t[e:e + 1, :] for e in range(N_EXPERTS)])
        r0 = bi * SUB_ROWS
        cls_ref[0, r0 // LANES:(r0 + SUB_ROWS) // LANES, :] = jnp.concatenate(
            [cls[:, k * LANES:(k + 1) * LANES] for k in range(SUB_ROWS // LANES)], axis=0)

    def conformer(b):
        conformer_glu(b)
        conformer_conv(b, 0)
        conformer_conv(b, 1)
        conformer_out(b)

    def round_expert_weights():
        wgu_out[0, :, 0:D_EXPERT] = wg_ref[0, 0].astype(BF16)
        wgu_out[0, :, D_EXPERT:2 * D_EXPERT] = wu_ref[0, 0].astype(BF16)
        wd_out[0] = wd_ref[0, 0].astype(BF16)

    def stage1(b):
        pieces = [lambda: load(b), lambda: in_proj(b, 0, 2), lambda: in_proj(b, 2, 4), lambda: in_proj(b, 4, 5),
                  lambda: in_proj(b, 5, 8)]
        if b == 0:
            pieces.insert(2, round_expert_weights)
        return pieces

    def stage2(b):
        return [lambda: gating_mlp(b), lambda: conformer(b), lambda: pooling(b), lambda: short_conv(b)]

    def stage3(b):
        return [lambda: out_proj(b), lambda: router(b), lambda: route(b)]

    for t in range(n_blocks + 2):
        stages = [stage(t - lag) for lag, stage in ((2, stage3), (0, stage1), (1, stage2)) if 0 <= t - lag < n_blocks]
        for k in range(max(len(stage) for stage in stages)):
            for stage in stages:
                if k < len(stage):
                    stage[k]()

    carry_halo(blocks[-1], blocks[0], (0, 1, 4, 8))
    fill.wait()


def _plan_kernel(cls_ref, dest_ref, meta_ref):
    n = LANES
    cls = cls_ref[...]
    r = lax.broadcasted_iota(I32, (n, n), 0)
    c = lax.broadcasted_iota(I32, (n, n), 1)
    upper = jnp.where(r < c, 1.0, 0.0).astype(BF16)
    lower = jnp.where(c < r, 1.0, 0.0).astype(BF16)
    ones = jnp.ones((n, n), BF16)
    masks = [cls == k for k in range(N_CLASSES)]
    m_all = jnp.concatenate([jnp.where(mk, 1.0, 0.0) for mk in masks], axis=0).astype(BF16)
    within = _dot(m_all, upper)
    rowsum = _dot(m_all, ones)
    rs = jnp.zeros((n, n), F32)
    for k in range(N_CLASSES):
        rs = jnp.where(c == k, rowsum[k * n:(k + 1) * n], rs)
    rs_b = rs.astype(BF16)
    before = _dot(lower, rs_b)
    total = _dot(ones, rs_b)
    ntile = jnp.floor((total + (TMS - 1)) * (1.0 / TMS))
    tstart = _dot(ntile.astype(BF16), upper)
    base = before + tstart * TMS
    dest = jnp.zeros((n, n), F32)
    for k in range(N_CLASSES):
        dest = jnp.where(masks[k], within[k * n:(k + 1) * n] + base[:, k:k + 1], dest)
    dest_ref[...] = dest.astype(I32)

    tend_t = (tstart + ntile).T
    tile = c.astype(F32)
    ended = jnp.where((r < N_CLASSES) & (tend_t <= tile), 1.0, 0.0)
    tcls = jnp.minimum(jnp.sum(ended, axis=0, keepdims=True), N_CLASSES - 1.0)
    grp = (jnp.where(tcls >= N_PAIRS, 1.0, 0.0) + jnp.where(tcls >= 2 * N_PAIRS, 1.0, 0.0)
           + jnp.where(tcls >= 3 * N_PAIRS, 1.0, 0.0))
    pair = tcls - N_PAIRS * grp
    pa = jnp.where(pair >= 3, 1.0, 0.0) + jnp.where(pair >= 5, 1.0, 0.0)
    pb = jnp.where(pair == 0, 1.0, jnp.where((pair == 1) | (pair == 3), 2.0, 3.0))
    n_tiles = tstart[0:1, N_CLASSES:N_CLASSES + 1]
    tile_row = tile[0:1, :]
    active = jnp.where(tile_row < n_tiles, 1.0, 0.0)
    meta = jnp.concatenate(
        [GROUP_SIZE * grp + pa, GROUP_SIZE * grp + pb, active, jnp.zeros((SUBLANES - 3, n), F32)], axis=0)
    meta_ref[...] = meta.astype(I32)


def _row_copy_wait(src, dst, sem):
    pltpu.make_async_copy(src, dst, sem).wait()


def _dispatch_kernel(dest_ref, x_ref, xs_zero_ref, xs_ref, buf, sems):
    del xs_zero_ref
    i = pl.program_id(0)
    n_steps = pl.num_programs(0)

    for s in range(2):
        @pl.when(i > 0)
        def _():
            _row_copy_wait(buf.at[s], xs_ref.at[pl.ds(0, TD * ROW_TILES), :], sems.at[s])

        for j in range(ROW_TILES):
            buf[s, pl.ds(j, TD, stride=ROW_TILES), :] = x_ref[s * TD:(s + 1) * TD, j * LANES:(j + 1) * LANES]
        for t in range(TD):
            d = pl.multiple_of(dest_ref[(i * 2 + s) * TD + t] * ROW_TILES, ROW_TILES)
            pltpu.make_async_copy(buf.at[s, pl.ds(t * ROW_TILES, ROW_TILES), :],
                                  xs_ref.at[pl.ds(d, ROW_TILES), :], sems.at[s]).start(priority=t % 2)

    @pl.when(i == n_steps - 1)
    def _():
        for s in range(2):
            _row_copy_wait(buf.at[s], xs_ref.at[pl.ds(0, TD * ROW_TILES), :], sems.at[s])


def _combine_kernel(dest_ref, ys_ref, vec_m_ref, out_ref, buf, sems):
    i = pl.program_id(0)
    n_steps = pl.num_programs(0)

    def issue(step, s):
        for t in range(TD):
            d = pl.multiple_of(dest_ref[(step * 2 + s) * TD + t] * ROW_TILES, ROW_TILES)
            pltpu.make_async_copy(ys_ref.at[pl.ds(d, ROW_TILES), :],
                                  buf.at[s, pl.ds(t * ROW_TILES, ROW_TILES), :], sems.at[s]).start(priority=t % 2)

    @pl.when(i == 0)
    def _():
        for s in range(2):
            issue(0, s)

    for s in range(2):
        _row_copy_wait(ys_ref.at[pl.ds(0, TD * ROW_TILES), :], buf.at[s], sems.at[s])
        r = jnp.concatenate([buf[s, pl.ds(j, TD, stride=ROW_TILES), :] for j in range(ROW_TILES)], axis=1)
        out_ref[s * TD:(s + 1) * TD, :] = _layer_norm(r, vec_m_ref[0, 3:4, :], vec_m_ref[0, 4:5, :])

        @pl.when(i + 1 < n_steps)
        def _():
            issue(i + 1, s)


def _moe_kernel(meta_ref, xs_ref, *refs):
    i = pl.program_id(0)
    n_steps = pl.num_programs(0)
    w_refs, rwt_ref, ys_ref, xbuf, xsems = refs[:-4], refs[-4], refs[-3], refs[-2], refs[-1]
    rows = TMS * ROW_TILES
    step_rows = TILES_PER_STEP * rows

    def fetch(step):
        slot = step % XS_SLOTS
        start = pl.multiple_of(step * step_rows, step_rows)
        return pltpu.make_async_copy(xs_ref.at[pl.ds(start, step_rows), :], xbuf.at[slot], xsems.at[slot])

    @pl.when(i == 0)
    def _():
        for ahead in range(XS_SLOTS - 1):
            fetch(ahead).start()

    @pl.when(i + XS_SLOTS - 1 < n_steps)
    def _():
        fetch(i + XS_SLOTS - 1).start()

    fetch(i).wait()
    x_rows = xbuf.at[i % XS_SLOTS]

    @pl.when(meta_ref[2, i * TILES_PER_STEP] == 0)
    def _():
        ys_ref[...] = jnp.zeros(ys_ref.shape, F32)

    @pl.when(meta_ref[2, i * TILES_PER_STEP] == 1)
    def _():
        st = [dict() for _ in range(TILES_PER_STEP)]

        def load(s):
            x = jnp.concatenate(
                [x_rows[pl.ds(s * rows + j, TMS, stride=ROW_TILES), :] for j in range(ROW_TILES)], axis=1)
            st[s]["x"] = x
            st[s]["xb"] = x.astype(BF16)

        def up(s, e, c):
            w, lo = w_refs[4 * s + 2 * e], c * ACT_COLS
            st[s]["g", e, c] = _dot(st[s]["xb"], w[0, :, lo:lo + ACT_COLS])
            st[s]["u", e, c] = _dot(st[s]["xb"], w[0, :, D_EXPERT + lo:D_EXPERT + lo + ACT_COLS])

        def act(s, e, c):
            g = st[s].pop(("g", e, c))
            st[s]["h", e, c] = ((g * _sigmoid(g)) * st[s].pop(("u", e, c))).astype(BF16)

        def down(s, e):
            h = jnp.concatenate([st[s].pop(("h", e, c)) for c in range(D_EXPERT // ACT_COLS)], axis=1)
            st[s]["y", e] = _dot(h, w_refs[4 * s + 2 * e + 1][0])

        def mix(s):
            tile = i * TILES_PER_STEP + s
            x = st[s].pop("x")
            la = jnp.sum(x * rwt_ref[pl.ds(meta_ref[0, tile], 1), :], axis=-1, keepdims=True)
            lb = jnp.sum(x * rwt_ref[pl.ds(meta_ref[1, tile], 1), :], axis=-1, keepdims=True)
            m = jnp.maximum(la, lb)
            pa = jnp.exp(la - m)
            pb = jnp.exp(lb - m)
            den = pa + pb
            r = ALPHA * x + ((pa / den) * st[s].pop(("y", 0)) + (pb / den) * st[s].pop(("y", 1)))
            for j in range(ROW_TILES):
                ys_ref[pl.ds(s * rows + j, TMS, stride=ROW_TILES), :] = r[:, j * LANES:(j + 1) * LANES]

        units = [(s, e) for s in range(TILES_PER_STEP) for e in range(2)]
        chunks = range(D_EXPERT // ACT_COLS)
        load(0)
        for c in chunks:
            up(*units[0], c)
        for k, (s, e) in enumerate(units):
            nxt = units[k + 1] if k + 1 < len(units) else None
            if nxt is not None and nxt[1] == 0:
                load(nxt[0])
            for c in chunks:
                if nxt is not None:
                    up(*nxt, c)
                act(s, e, c)
            down(s, e)
            if e == 1:
                mix(s)


def _full(shape):
    return pl.BlockSpec(shape, lambda *_: (0,) * len(shape))


def _layer_block(shape, layer):
    return pl.BlockSpec((1,) + shape, lambda *_: (layer,) + (0,) * len(shape))


def _mixer_call(x, p, expert_w, n_sorted, layer):
    batch, seq, _ = x.shape
    steps = seq // TS
    fill_rows = n_sorted * ROW_TILES // (batch * steps)
    assert fill_rows * batch * steps == n_sorted * ROW_TILES and fill_rows % SUBLANES == 0
    slabs = batch * steps // N_EXPERTS
    assert batch * steps == slabs * N_EXPERTS

    def w_in(b, s):
        i = b * steps + s
        return (layer, i // slabs, i % slabs, 0)

    def w_out(b, s):
        i = b * steps + s
        return (i // slabs, i % slabs, 0)

    up_rows, down_rows = D_MODEL // slabs, D_EXPERT // slabs
    in_specs = [
        pl.BlockSpec((1, TS, D_MODEL), lambda b, s: (b, s, 0)),
        _layer_block((D_MODEL, 8 * D_GROUP), layer),
        _layer_block((SUBLANES, D_GROUP), layer), _layer_block((SUBLANES, D_MODEL), layer),
        _layer_block((N_HEADS * CHUNK, CHUNK), layer), _layer_block((CHUNK, D_GROUP), layer),
        _layer_block((HALO, D_GROUP), layer),
        _layer_block((D_GROUP, D_GROUP), layer),
        _layer_block((D_GROUP, D_GROUP), layer),
        _layer_block((SUBLANES, D_GROUP), layer),
        _layer_block((D_MODEL, D_MODEL), layer),
        _full((D_MODEL, 2 * LANES)), _full((1, LANES)),
        pl.BlockSpec((1, 1, up_rows, D_EXPERT), w_in),
        pl.BlockSpec((1, 1, up_rows, D_EXPERT), w_in),
        pl.BlockSpec((1, 1, down_rows, D_MODEL), w_in),
    ]
    out_specs = [
        pl.BlockSpec((1, TS, D_MODEL), lambda b, s: (b, s, 0)),
        pl.BlockSpec((1, TS // LANES, LANES), lambda b, s: (b * steps + s, 0, 0)),
        pl.BlockSpec((1, up_rows, 2 * D_EXPERT), w_out),
        pl.BlockSpec((1, down_rows, D_MODEL), w_out),
        pl.BlockSpec(memory_space=pl.ANY),
    ]
    block_scratch = [
        pltpu.VMEM((HALO + SUB_ROWS, LANES), F32),
        pltpu.VMEM((HALO + SUB_ROWS, LANES), F32),
        pltpu.VMEM((SUB_ROWS, LANES), F32),
        pltpu.VMEM((SUB_ROWS, LANES), F32),
        pltpu.VMEM((HALO + SUB_ROWS, D_GROUP), F32),
        pltpu.VMEM((HALO + SUB_ROWS, D_GROUP), F32),
        pltpu.VMEM((HALO + SUB_ROWS, D_GROUP), F32),
        pltpu.VMEM((HALO + SUB_ROWS, D_GROUP), F32),
        pltpu.VMEM((SC_HALO + SUB_ROWS, D_GROUP), F32),
        pltpu.VMEM((SUB_ROWS, D_MODEL), BF16),
    ]
    scratch = block_scratch * (TS // SUB_ROWS) + [pltpu.VMEM((fill_rows, LANES), F32), pltpu.SemaphoreType.DMA(())]
    return pl.pallas_call(
        _mixer_kernel,
        grid=(batch, steps),
        in_specs=in_specs,
        out_specs=out_specs,
        out_shape=[jax.ShapeDtypeStruct((batch, seq, D_MODEL), F32),
                   jax.ShapeDtypeStruct((batch * steps, TS // LANES, LANES), I32),
                   jax.ShapeDtypeStruct((N_EXPERTS, D_MODEL, 2 * D_EXPERT), BF16),
                   jax.ShapeDtypeStruct((N_EXPERTS, D_EXPERT, D_MODEL), BF16),
                   jax.ShapeDtypeStruct((n_sorted * ROW_TILES, LANES), F32)],
        scratch_shapes=scratch,
        compiler_params=pltpu.CompilerParams(
            dimension_semantics=("arbitrary", "arbitrary"), vmem_limit_bytes=VMEM_LIMIT),
        name="mixer",
    )(x, *p, *expert_w)


def _plan_call(cls2d):
    return pl.pallas_call(
        _plan_kernel,
        out_shape=[jax.ShapeDtypeStruct((LANES, LANES), I32), jax.ShapeDtypeStruct((SUBLANES, LANES), I32)],
        compiler_params=pltpu.CompilerParams(vmem_limit_bytes=VMEM_LIMIT),
        name="moe_plan",
    )(cls2d)


def _dispatch_call(dest, x2d, xs_zero):
    n_tok = x2d.shape[0]
    return pl.pallas_call(
        _dispatch_kernel,
        grid_spec=pltpu.PrefetchScalarGridSpec(
            num_scalar_prefetch=1,
            grid=(n_tok // (2 * TD),),
            in_specs=[pl.BlockSpec((2 * TD, D_MODEL), lambda i, *_: (i, 0)),
                      pl.BlockSpec(memory_space=pl.ANY)],
            out_specs=pl.BlockSpec(memory_space=pl.ANY),
            scratch_shapes=[
                pltpu.VMEM((2, TD * ROW_TILES, LANES), F32),
                pltpu.SemaphoreType.DMA((2,)),
            ]),
        out_shape=jax.ShapeDtypeStruct(xs_zero.shape, F32),
        input_output_aliases={2: 0},
        compiler_params=pltpu.CompilerParams(
            dimension_semantics=("arbitrary",), vmem_limit_bytes=VMEM_LIMIT),
        name="moe_dispatch",
    )(dest, x2d, xs_zero)


def _combine_call(dest, ys, vec_m, layer):
    n_tok = dest.shape[0]
    return pl.pallas_call(
        _combine_kernel,
        grid_spec=pltpu.PrefetchScalarGridSpec(
            num_scalar_prefetch=1,
            grid=(n_tok // (2 * TD),),
            in_specs=[pl.BlockSpec(memory_space=pl.ANY), _layer_block((SUBLANES, D_MODEL), layer)],
            out_specs=pl.BlockSpec((2 * TD, D_MODEL), lambda i, *_: (i, 0)),
            scratch_shapes=[
                pltpu.VMEM((2, TD * ROW_TILES, LANES), F32),
                pltpu.SemaphoreType.DMA((2,)),
            ]),
        out_shape=jax.ShapeDtypeStruct((n_tok, D_MODEL), F32),
        compiler_params=pltpu.CompilerParams(
            dimension_semantics=("arbitrary",), vmem_limit_bytes=VMEM_LIMIT),
        name="moe_combine",
    )(dest, ys, vec_m)


def _moe_call(meta, xs, wgu, wd, rwt, n_tiles):
    rows = TILES_PER_STEP * TMS * ROW_TILES
    w_specs, w_args = [], []
    for s in range(TILES_PER_STEP):
        for sel in (0, 1):
            def expert(i, meta, s=s, sel=sel):
                return (meta[sel, i * TILES_PER_STEP + s], 0, 0)
            w_specs += [pl.BlockSpec((1, D_MODEL, 2 * D_EXPERT), expert),
                        pl.BlockSpec((1, D_EXPERT, D_MODEL), expert)]
            w_args += [wgu, wd]
    tile = pl.BlockSpec((rows, LANES), lambda i, meta: (i, 0))
    return pl.pallas_call(
        _moe_kernel,
        grid_spec=pltpu.PrefetchScalarGridSpec(
            num_scalar_prefetch=1,
            grid=(n_tiles // TILES_PER_STEP,),
            in_specs=[pl.BlockSpec(memory_space=pl.ANY)] + w_specs + [_full((N_EXPERTS, D_MODEL))],
            out_specs=tile,
            scratch_shapes=[pltpu.VMEM((XS_SLOTS, rows, LANES), F32), pltpu.SemaphoreType.DMA((XS_SLOTS,))]),
        out_shape=jax.ShapeDtypeStruct(xs.shape, F32),
        compiler_params=pltpu.CompilerParams(
            dimension_semantics=("arbitrary",), vmem_limit_bytes=VMEM_LIMIT),
        name="moe_experts",
    )(meta, xs, *w_args, rwt)


def _block_diag(w):
    g, d, _ = w.shape
    eye = jnp.eye(g, dtype=w.dtype)
    return (eye[:, None, :, None] * w[:, :, None, :]).reshape(g * d, g * d)


def kernel(x, w_in, gm_ln_g, gm_ln_b, gm_w_s, gm_b_s, cf_dw_w, cf_dw_b, cf_ln_g, cf_ln_b, cf_pw,
           pool_w, pool_scale, sc_w, mix_norm_g, w_o, ln1_g, ln1_b, router_w, router_b,
           exp_w_gate, exp_w_up, exp_w_down, ln2_g, ln2_b):
    batch, seq, _ = x.shape
    n_tok = batch * seq
    assert n_tok == LANES * LANES, "the routing plan lays tokens out as one (128, 128) tile grid"
    assert seq % TS == 0 and TS % SUB_ROWS == 0 and SUB_ROWS % (2 * CONV_ROWS) == 0 and n_tok % (2 * TD) == 0
    n_tiles = n_tok // TMS + N_CLASSES
    assert n_tiles <= LANES and n_tiles % TILES_PER_STEP == 0 and n_tiles // TILES_PER_STEP >= XS_SLOTS - 1
    def pack_rows(rows):
        slab = jnp.stack(rows, axis=1)
        return jnp.pad(slab, ((0, 0), (0, SUBLANES - len(rows)), (0, 0)))

    rw = jnp.pad(router_w, ((0, 0), (0, LANES - N_EXPERTS)))
    rw_hi = rw.astype(BF16)
    rw_lo = (rw - rw_hi.astype(F32)).astype(BF16)
    rb = jnp.pad(router_b, (0, LANES - N_EXPERTS)).reshape(1, LANES)
    rwt = router_w.T
    vec_m = pack_rows([mix_norm_g, ln1_g, ln1_b, ln2_g, ln2_b])
    params = (
        w_in.astype(BF16),
        pack_rows([gm_ln_g, gm_ln_b, cf_dw_b, cf_ln_g, cf_ln_b, pool_scale]), vec_m,
        gm_w_s.reshape(DEPTH, N_HEADS * CHUNK, CHUNK),
        jnp.repeat(gm_b_s.transpose(0, 2, 1), HEAD_DIM, axis=2),
        jnp.pad(cf_dw_w, ((0, 0), (0, HALO - CONF_K), (0, 0))),
        cf_pw.astype(BF16),
        jax.vmap(_block_diag)(pool_w).astype(BF16),
        jnp.pad(sc_w, ((0, 0), (0, SUBLANES - SHORT_K), (0, 0))),
        w_o.astype(BF16),
        jnp.concatenate([rw_hi, rw_lo], axis=1), rb,
    )
    for l in range(DEPTH):
        x1, cls, wgu_b, wd_b, xs_zero = _mixer_call(
            x, params, (exp_w_gate, exp_w_up, exp_w_down), n_tiles * TMS, l)
        x1 = x1.reshape(n_tok, D_MODEL)
        dest2d, meta = _plan_call(cls.reshape(LANES, LANES))
        dest = dest2d.reshape(n_tok)
        xs = _dispatch_call(dest, x1, xs_zero)
        ys = _moe_call(meta, xs, wgu_b, wd_b, rwt, n_tiles)
        x = _combine_call(dest, ys, vec_m, l).reshape(batch, seq, D_MODEL)
    return x
```
